```python
import math
import jax, jax.numpy as jnp
from jax import lax
import numpy as np

D_MODEL = 1024
BATCH = 32
SEQ = 2048
DEPTH = 4

GRID_W = 64
CTX_LEN = 256
N_MIXERS = 3
N_A = (DEPTH + 2) // 3
N_B = (DEPTH + 1) // 3
N_C = DEPTH // 3
NORM_EPS = 1e-6
ROPE_BASE = 10000.0
Q_BLOCK = 128
NEG_INF = -1e30
MOD_GAIN = 0.5

A_HEADS = 8
A_Q_RANK = 512
A_KV_RANK = 256
A_NOPE = 128
A_ROPE = 64
A_VDIM = 128
A_WIDTH = A_HEADS * A_VDIM
A_IN = A_Q_RANK + A_KV_RANK + A_ROPE + A_WIDTH

B_HEADS = 16
B_KV_HEADS = 4
B_GROUP = B_HEADS // B_KV_HEADS
B_HDIM = 64
B_WINDOW = 128
B_BLOCK = 128
B_WIDTH = B_HEADS * B_HDIM
B_KVW = B_KV_HEADS * B_HDIM
B_IN = B_WIDTH + 2 * B_KVW + B_WIDTH

C_WIDTH = D_MODEL
C_ORDER = 2
C_SHORT = 3
C_BANDS = 16
C_EMB = 1 + 2 * C_BANDS
C_FFN = 64
C_TARGET = 1e-2
C_FAST = 0.3
C_SLOW = 1.5
C_MIN_DECAY = math.log(C_TARGET) / C_SLOW
C_MAX_DECAY = math.log(C_TARGET) / C_FAST
C_FILTER_GAIN = 0.1
C_IN = (C_ORDER + 1) * C_WIDTH + C_WIDTH

kernel_name = 'hybrid_mla_swa_hyena_prefix_dit'


def rmsnorm(x, g):
    xf = x.astype(jnp.float32)
    y = xf * lax.rsqrt(jnp.mean(xf * xf, axis=-1, keepdims=True) + NORM_EPS)
    return (y * g.astype(jnp.float32)).astype(x.dtype)


def axial_rope(n, rot_dim):
    rows = n // GRID_W
    row = jnp.repeat(jnp.arange(rows, dtype=jnp.float32), GRID_W)
    col = jnp.tile(jnp.arange(GRID_W, dtype=jnp.float32), rows)
    per_axis = rot_dim // 2
    inv = ROPE_BASE ** (-jnp.arange(0, per_axis, 2, dtype=jnp.float32) / per_axis)
    ang = jnp.concatenate([row[:, None] * inv, col[:, None] * inv], axis=-1)
    return jnp.cos(ang), jnp.sin(ang)


def apply_rope(x, cos, sin):
    half = x.shape[-1] // 2
    x1 = x[..., :half].astype(jnp.float32)
    x2 = x[..., half:].astype(jnp.float32)
    return jnp.concatenate([x1 * cos - x2 * sin, x1 * sin + x2 * cos], axis=-1).astype(x.dtype)


def dense_attend(q, k, v, scale):
    s = jnp.einsum('bqhd,bkhd->bhqk', q, k).astype(jnp.float32) * scale
    p = jax.nn.softmax(s, axis=-1).astype(v.dtype)
    return jnp.einsum('bhqk,bkhd->bqhd', p, v)


def blockwise_attend(q, k, v, scale):
    b, n, h, d = q.shape
    qb = q.reshape(b, n // Q_BLOCK, Q_BLOCK, h, d).swapaxes(0, 1)
    ob = lax.map(lambda qi: dense_attend(qi, k, v, scale), qb)
    return ob.swapaxes(0, 1).reshape(b, n, h, v.shape[-1])


def sink_attend(scores, values, sink_l):
    sink_col = jnp.broadcast_to(sink_l[None, :, :, None, None], scores[0].shape[:-1] + (1,))
    p = jax.nn.softmax(jnp.concatenate(scores + [sink_col], axis=-1), axis=-1)
    offset = 0
    parts = []
    for s, v in zip(scores, values):
        kk = s.shape[-1]
        parts.append(jnp.einsum('bhgqk,bkhd->bqhgd', p[..., offset:offset + kk].astype(v.dtype), v))
        offset += kk
    out = parts[0]
    for part in parts[1:]:
        out = out + part
    return out


def mla_mixer(hx, hc, w_in, g_q, w_q, g_kv, w_kv, w_out, with_ctx_out):
    b, n, _ = hx.shape
    cos, sin = axial_rope(n, A_ROPE)
    split_at = [A_Q_RANK, A_Q_RANK + A_KV_RANK, A_Q_RANK + A_KV_RANK + A_ROPE]
    scale = (A_NOPE + A_ROPE) ** -0.5

    def queries(cq, rope):
        bb, m, _ = cq.shape
        q = (rmsnorm(cq, g_q) @ w_q).reshape(bb, m, A_HEADS, A_NOPE + A_ROPE)
        if rope:
            q = jnp.concatenate([q[..., :A_NOPE], apply_rope(q[..., A_NOPE:], cos[:, None, :], sin[:, None, :])], axis=-1)
        return q

    def keys_values(ckv, kr, rope):
        bb, m, _ = ckv.shape
        kv = (rmsnorm(ckv, g_kv) @ w_kv).reshape(bb, m, A_HEADS, A_NOPE + A_VDIM)
        if rope:
            kr = apply_rope(kr, cos, sin)
        kr = jnp.broadcast_to(kr[:, :, None, :], (bb, m, A_HEADS, A_ROPE))
        return jnp.concatenate([kv[..., :A_NOPE], kr], axis=-1), kv[..., A_NOPE:]

    cq_x, ckv_x, kr_x, gate_x = jnp.split(hx @ w_in, split_at, axis=-1)
    q_x = queries(cq_x, True)
    k_x, v_x = keys_values(ckv_x, kr_x, True)
    if with_ctx_out:
        cq_c, ckv_c, kr_c, gate_c = jnp.split(hc @ w_in, split_at, axis=-1)
    else:
        ckv_c, kr_c = jnp.split(hc @ w_in[:, split_at[0]:split_at[2]], [A_KV_RANK], axis=-1)
    k_c, v_c = keys_values(ckv_c, kr_c, False)

    k_all = jnp.concatenate([k_c, k_x], axis=1)
    v_all = jnp.concatenate([v_c, v_x], axis=1)
    o_x = blockwise_attend(q_x, k_all, v_all, scale)
    y_x = (o_x.reshape(b, n, A_WIDTH) * jax.nn.silu(gate_x)) @ w_out
    if not with_ctx_out:
        return y_x, None
    q_c = queries(cq_c, False)
    o_c = dense_attend(q_c, k_c, v_c, scale)
    y_c = (o_c.reshape(b, hc.shape[1], A_WIDTH) * jax.nn.silu(gate_c)) @ w_out
    return y_x, y_c


def swa_mixer(hx, hc, w_in, sink, w_out, with_ctx_out):
    b, n, _ = hx.shape
    cos, sin = axial_rope(n, B_HDIM)
    scale = B_HDIM ** -0.5
    sink_l = sink.astype(jnp.float32).reshape(B_KV_HEADS, B_GROUP)

    def project(h):
        m = h.shape[1]
        q, k, v, gate = jnp.split(h @ w_in, [B_WIDTH, B_WIDTH + B_KVW, B_WIDTH + 2 * B_KVW], axis=-1)
        return (q.reshape(b, m, B_KV_HEADS, B_GROUP, B_HDIM), k.reshape(b, m, B_KV_HEADS, B_HDIM),
                v.reshape(b, m, B_KV_HEADS, B_HDIM), gate)

    q_x, k_x, v_x, gate_x = project(hx)
    q_c, k_c, v_c, gate_c = project(hc)
    q_x = apply_rope(q_x, cos[:, None, None, :], sin[:, None, None, :])
    k_x = apply_rope(k_x, cos[:, None, :], sin[:, None, :])

    pad = ((0, 0), (B_BLOCK, B_BLOCK), (0, 0), (0, 0))
    kp = jnp.pad(k_x, pad)
    vp = jnp.pad(v_x, pad)
    n_blocks = n // B_BLOCK
    qb = q_x.reshape(b, n_blocks, B_BLOCK, B_KV_HEADS, B_GROUP, B_HDIM).swapaxes(0, 1)

    def band_block(args):
        i, qi = args
        start = i * B_BLOCK
        kb = lax.dynamic_slice_in_dim(kp, start, 3 * B_BLOCK, axis=1)
        vb = lax.dynamic_slice_in_dim(vp, start, 3 * B_BLOCK, axis=1)
        qpos = start + jnp.arange(B_BLOCK)
        kpos = start - B_BLOCK + jnp.arange(3 * B_BLOCK)
        mask = (jnp.abs(qpos[:, None] - kpos[None, :]) <= B_WINDOW) & (kpos[None, :] >= 0) & (kpos[None, :] < n)
        s_band = jnp.einsum('bqhgd,bkhd->bhgqk', qi, kb).astype(jnp.float32) * scale
        s_band = jnp.where(mask, s_band, NEG_INF)
        s_ctx = jnp.einsum('bqhgd,bkhd->bhgqk', qi, k_c).astype(jnp.float32) * scale
        return sink_attend([s_ctx, s_band], [v_c, vb], sink_l)

    ob = lax.map(band_block, (jnp.arange(n_blocks), qb))
    o_x = ob.swapaxes(0, 1).reshape(b, n, B_WIDTH)
    y_x = (o_x * jax.nn.silu(gate_x)) @ w_out
    if not with_ctx_out:
        return y_x, None
    s_cc = jnp.einsum('bqhgd,bkhd->bhgqk', q_c, k_c).astype(jnp.float32) * scale
    o_c = sink_attend([s_cc], [v_c], sink_l).reshape(b, hc.shape[1], B_WIDTH)
    y_c = (o_c * jax.nn.silu(gate_c)) @ w_out
    return y_x, y_c


def short_conv(u, w, bias):
    n = u.shape[1]
    p = C_SHORT // 2
    up = jnp.pad(u, ((0, 0), (p, p), (0, 0)))
    y = bias
    for j in range(C_SHORT):
        y = y + up[:, j:j + n] * w[j]
    return y


def hyena_filters(n, f_w1, f_b1, f_freq, f_w2, f_b2, f_w3):
    t = jnp.linspace(0.0, 1.0, n, dtype=jnp.float32)[:, None]
    w = (2.0 * math.pi / n) * jnp.arange(n, dtype=jnp.float32)[:, None]
    bands = jnp.linspace(1e-4, C_BANDS - 1, C_BANDS, dtype=jnp.float32)[None, :]
    z = jnp.concatenate([t, jnp.cos(bands * w), -jnp.sin(bands * w)], axis=-1)
    freq = f_freq.astype(jnp.float32)
    h = jnp.sin(freq * (z @ f_w1 + f_b1))
    h = jnp.sin(freq * (h @ f_w2 + f_b2))
    h = (h @ f_w3).astype(jnp.float32).reshape(n, 2, C_ORDER, C_WIDTH)
    deltas = jnp.abs(jnp.linspace(C_MIN_DECAY, C_MAX_DECAY, C_WIDTH, dtype=jnp.float32))
    return h * jnp.exp(-t[:, :, None, None] * deltas)


def bidir_long_conv(u, h_fwd, h_bwd, bias):
    n = u.shape[1]
    lag0 = h_fwd[:1] + h_bwd[:1]
    k = jnp.concatenate([lag0, h_fwd[1:], jnp.zeros((1, C_WIDTH), jnp.float32), h_bwd[:0:-1]], axis=0)
    uf = jnp.fft.rfft(u.astype(jnp.float32), n=2 * n, axis=1)
    kf = jnp.fft.rfft(k, axis=0)
    y = jnp.fft.irfft(uf * kf[None], n=2 * n, axis=1)[:, :n]
    return (y + u.astype(jnp.float32) * bias.astype(jnp.float32)).astype(u.dtype)


def hyena_mixer(hx, hc, w_in, conv_w, conv_b, f_w1, f_b1, f_freq, f_w2, f_b2, f_w3, filt_bias, w_out, with_ctx_out):
    def run(h):
        n = h.shape[1]
        p = h @ w_in
        u = short_conv(p[..., :(C_ORDER + 1) * C_WIDTH], conv_w, conv_b)
        gate = p[..., (C_ORDER + 1) * C_WIDTH:]
        v, *gx = jnp.split(u, C_ORDER + 1, axis=-1)
        filt = hyena_filters(n, f_w1, f_b1, f_freq, f_w2, f_b2, f_w3)
        z = v
        for o in range(C_ORDER):
            z = gx[o] * bidir_long_conv(z, filt[:, 0, o], filt[:, 1, o], filt_bias[o])
        return (z * jax.nn.silu(gate)) @ w_out
    y_x = run(hx)
    y_c = run(hc) if with_ctx_out else None
    return y_x, y_c


def setup_inputs(seed: int = 0) -> dict:
    key = jax.random.key(seed)
    ks = iter(jax.random.split(key, 40))
    f32 = jnp.float32

    def nrm(shape, fan_in, gain=1.0):
        return jax.random.normal(next(ks), shape, f32) * (gain * fan_in ** -0.5)

    def gains(shape):
        return 1.0 + 0.01 * jax.random.normal(next(ks), shape, f32)

    def small(shape, s=0.01):
        return s * jax.random.normal(next(ks), shape, f32)

    D = D_MODEL
    return {
        'x': jax.random.normal(next(ks), (BATCH, SEQ, D), f32),
        'c': jax.random.normal(next(ks), (BATCH, D), f32),
        'ctx': jax.random.normal(next(ks), (BATCH, CTX_LEN, D), f32),
        'c_ctx': jax.random.normal(next(ks), (D,), f32),
        'w_mod': nrm((DEPTH, D, 3 * D), D, MOD_GAIN),
        'b_mod': small((DEPTH, 3 * D)),
        'g_pre': gains((DEPTH, D)),
        'g_post': gains((DEPTH, D)),
        'a_w_in': nrm((N_A, D, A_IN), D),
        'a_g_q': gains((N_A, A_Q_RANK)),
        'a_w_q': nrm((N_A, A_Q_RANK, A_HEADS * (A_NOPE + A_ROPE)), A_Q_RANK),
        'a_g_kv': gains((N_A, A_KV_RANK)),
        'a_w_kv': nrm((N_A, A_KV_RANK, A_HEADS * (A_NOPE + A_VDIM)), A_KV_RANK),
        'a_w_out': nrm((N_A, A_WIDTH, D), A_WIDTH),
        'b_w_in': nrm((N_B, D, B_IN), D),
        'b_sink': small((N_B, B_HEADS), 0.5),
        'b_w_out': nrm((N_B, B_WIDTH, D), B_WIDTH),
        'c_w_in': nrm((N_C, D, C_IN), D),
        'c_conv_w': nrm((N_C, C_SHORT, (C_ORDER + 1) * C_WIDTH), C_SHORT),
        'c_conv_b': small((N_C, (C_ORDER + 1) * C_WIDTH)),
        'c_f_w1': nrm((N_C, C_EMB, C_FFN), C_EMB),
        'c_f_b1': small((N_C, C_FFN), 0.1),
        'c_f_freq': gains((N_C, C_FFN)),
        'c_f_w2': nrm((N_C, C_FFN, C_FFN), C_FFN),
        'c_f_b2': small((N_C, C_FFN), 0.1),
        'c_f_w3': nrm((N_C, C_FFN, 2 * C_ORDER * C_WIDTH), C_FFN, C_FILTER_GAIN),
        'c_filt_bias': small((N_C, C_ORDER, C_WIDTH), 0.1),
        'c_w_out': nrm((N_C, C_WIDTH, D), C_WIDTH),
    }


def reference(x, c, ctx, c_ctx, w_mod, b_mod, g_pre, g_post,
              a_w_in, a_g_q, a_w_q, a_g_kv, a_w_kv, a_w_out,
              b_w_in, b_sink, b_w_out,
              c_w_in, c_conv_w, c_conv_b, c_f_w1, c_f_b1, c_f_freq, c_f_w2, c_f_b2, c_f_w3, c_filt_bias, c_w_out):
    cond_x = jax.nn.silu(c)
    cond_c = jax.nn.silu(c_ctx)
    for layer in range(DEPTH):
        kind = layer % N_MIXERS
        j = layer // N_MIXERS
        with_ctx_out = layer < DEPTH - 1
        mx = cond_x @ w_mod[layer] + b_mod[layer]
        mc = cond_c @ w_mod[layer] + b_mod[layer]
        shift_x, scale_x, gate_x = jnp.split(mx[:, None, :], 3, axis=-1)
        shift_c, scale_c, gate_c = jnp.split(mc, 3, axis=-1)
        hx = rmsnorm(x, g_pre[layer]) * (1.0 + scale_x) + shift_x
        hc = rmsnorm(ctx, g_pre[layer]) * (1.0 + scale_c) + shift_c
        if kind == 0:
            yx, yc = mla_mixer(hx, hc, a_w_in[j], a_g_q[j], a_w_q[j], a_g_kv[j], a_w_kv[j], a_w_out[j], with_ctx_out)
        elif kind == 1:
            yx, yc = swa_mixer(hx, hc, b_w_in[j], b_sink[j], b_w_out[j], with_ctx_out)
        else:
            yx, yc = hyena_mixer(hx, hc, c_w_in[j], c_conv_w[j], c_conv_b[j], c_f_w1[j], c_f_b1[j], c_f_freq[j],
                                 c_f_w2[j], c_f_b2[j], c_f_w3[j], c_filt_bias[j], c_w_out[j], with_ctx_out)
        x = x + gate_x * rmsnorm(yx, g_post[layer])
        if with_ctx_out:
            ctx = ctx + gate_c * rmsnorm(yc, g_post[layer])
    return x
```

```python
import functools
import math

import numpy as np
import jax
import jax.numpy as jnp
from jax import lax
from jax.experimental import pallas as pl
from jax.experimental.pallas import tpu as pltpu

F32 = jnp.float32
BF16 = jnp.bfloat16

D = 1024
DEPTH = 4
GRID_W = 64
CTX = 256
NORM_EPS = 1e-6
ROPE_BASE = 10000.0
NEG_INF = -1e30

A_HEADS = 8
A_Q_RANK = 512
A_KV_RANK = 256
A_NOPE = 128
A_ROPE = 64
A_VDIM = 128
A_HEAD_PAD = 256

B_HEADS = 16
B_KV_HEADS = 4
B_GROUP = 4
B_HDIM = 64
B_WINDOW = 128
B_BLOCK = 128

C_WIDTH = 1024
C_BANDS = 16
C_EMB = 1 + 2 * C_BANDS
C_FFN = 64
C_MIN_DECAY = math.log(1e-2) / 1.5
C_MAX_DECAY = math.log(1e-2) / 0.3

LANE = 128
TM = 256
HALO = 8
VMEM_LIMIT = 56 * 1024 * 1024


def _cparams(n_axes):
    return pltpu.CompilerParams(dimension_semantics=("arbitrary",) * n_axes,
                                vmem_limit_bytes=VMEM_LIMIT)


def _rms(x):
    return x * lax.rsqrt(jnp.mean(x * x, axis=-1, keepdims=True) + NORM_EPS)


def _silu(g):
    return g / (1.0 + jnp.exp(-g))


def _dot(a, b):
    return jnp.dot(a, b, preferred_element_type=F32)


def _dot_nt(a, b):
    return lax.dot_general(a, b, (((1,), (1,)), ((), ())), preferred_element_type=F32)


def _rope128(x, a, b, c):
    return x * a + pltpu.roll(x, 32, 1) * b + pltpu.roll(x, LANE - 32, 1) * c


def _modnorm(x, mod_ref, gpre_ref):
    m = mod_ref[0, 0]
    return _rms(x) * gpre_ref[...] * (1.0 + m[1:2]) + m[0:1]


def _mod_kernel(c_ref, w_ref, b_ref, o_ref):
    a = _silu(c_ref[...])
    o_ref[0] = _dot(a.astype(BF16), w_ref[0].astype(BF16)) + b_ref[0]


def _modulation(cond, w_mod, b_mod):
    rows = cond.shape[0]
    return pl.pallas_call(
        _mod_kernel,
        grid=(DEPTH, 3),
        in_specs=[pl.BlockSpec((rows, D), lambda l, j: (0, 0)),
                  pl.BlockSpec((1, D, D), lambda l, j: (l, 0, j)),
                  pl.BlockSpec((1, 1, D), lambda l, j: (l, 0, j))],
        out_specs=pl.BlockSpec((1, rows, D), lambda l, j: (l, 0, j)),
        out_shape=jax.ShapeDtypeStruct((DEPTH, rows, 3 * D), F32),
        compiler_params=_cparams(2),
        name="adaln_modulation",
    )(cond, w_mod, b_mod.reshape(DEPTH, 1, 3 * D))


def _tok_spec(width):
    return pl.BlockSpec((1, TM, width), lambda t, b: (b, t, 0))


def _mod_spec():
    return pl.BlockSpec((1, 1, 3, D), lambda t, b: (jnp.minimum(t, 1), b, 0, 0))


def _const_spec(shape):
    nd = len(shape)
    return pl.BlockSpec(shape, lambda t, b: (0,) * nd)


def _table_spec():
    return pl.BlockSpec((TM, LANE), lambda t, b: (t, 0))


def _mla_proj_kernel(x_ref, mod_ref, gpre_ref, win_ref, gq_ref, wq_ref, gkv_ref, wkv_ref,
                     ra_ref, rb_ref, rc_ref, q_ref, k_ref, v_ref, gate_ref):
    h = _modnorm(x_ref[0], mod_ref, gpre_ref)
    p = _dot(h.astype(BF16), win_ref[...])
    c0, c1, c2 = A_Q_RANK, A_Q_RANK + A_KV_RANK, A_Q_RANK + A_KV_RANK + LANE
    gate_ref[0] = p[:, c2:].astype(BF16)
    q = _dot((_rms(p[:, :c0]) * gq_ref[...]).astype(BF16), wq_ref[...])
    kv = _dot((_rms(p[:, c0:c1]) * gkv_ref[...]).astype(BF16), wkv_ref[...])
    ra, rb, rc = ra_ref[...], rb_ref[...], rc_ref[...]
    kr = _rope128(p[:, c1:c2], ra, rb, rc).astype(BF16)
    for hd in range(A_HEADS):
        o = hd * A_HEAD_PAD
        q_ref[0, :, o:o + LANE] = q[:, o:o + LANE].astype(BF16)
        q_ref[0, :, o + LANE:o + 2 * LANE] = _rope128(q[:, o + LANE:o + 2 * LANE], ra, rb, rc).astype(BF16)
        k_ref[0, :, o:o + LANE] = kv[:, hd * A_NOPE:(hd + 1) * A_NOPE].astype(BF16)
        k_ref[0, :, o + LANE:o + 2 * LANE] = kr
    v_ref[0] = kv[:, A_HEADS * A_NOPE:].astype(BF16)


def _mla_proj(xs, modl, gpre, win, gq, wq, gkv, wkv, tabs):
    bsz, t, _ = xs.shape
    hw = A_HEADS * A_HEAD_PAD
    return pl.pallas_call(
        _mla_proj_kernel,
        grid=(t // TM, bsz),
        in_specs=[_tok_spec(D), _mod_spec(), _const_spec((1, D)), _const_spec(win.shape),
                  _const_spec((1, A_Q_RANK)), _const_spec(wq.shape),
                  _const_spec((1, A_KV_RANK)), _const_spec(wkv.shape),
                  _table_spec(), _table_spec(), _table_spec()],
        out_specs=[_tok_spec(hw), _tok_spec(hw), _tok_spec(A_HEADS * A_VDIM), _tok_spec(A_HEADS * A_VDIM)],
        out_shape=[jax.ShapeDtypeStruct((bsz, t, hw), BF16),
                   jax.ShapeDtypeStruct((bsz, t, hw), BF16),
                   jax.ShapeDtypeStruct((bsz, t, A_HEADS * A_VDIM), BF16),
                   jax.ShapeDtypeStruct((bsz, t, A_HEADS * A_VDIM), BF16)],
        compiler_params=_cparams(2),
        name="mla_proj",
    )(xs, modl, gpre, win, gq, wq, gkv, wkv, *tabs)


def _mla_attn_kernel(q_ref, k_ref, v_ref, o_ref):
    q = q_ref[0]

    def attend(nk):
        s = _dot_nt(q, k_ref[0, :nk, :])
        m = jnp.max(s, axis=-1, keepdims=True)
        p = jnp.exp(s - m)
        l = jnp.sum(p, axis=-1, keepdims=True)
        o_ref[0] = (_dot(p.astype(BF16), v_ref[0, :nk, :]) / l).astype(BF16)

    t_all = k_ref.shape[1]
    is_ctx = pl.program_id(2) == 0
    pl.when(is_ctx)(lambda: attend(CTX))
    pl.when(jnp.logical_not(is_ctx))(lambda: attend(t_all))


def _mla_attn(q, k, v):
    bsz, t, _ = q.shape
    return pl.pallas_call(
        _mla_attn_kernel,
        grid=(bsz, A_HEADS, t // TM),
        in_specs=[pl.BlockSpec((1, TM, A_HEAD_PAD), lambda b, h, i: (b, i, h)),
                  pl.BlockSpec((1, t, A_HEAD_PAD), lambda b, h, i: (b, 0, h)),
                  pl.BlockSpec((1, t, A_VDIM), lambda b, h, i: (b, 0, h))],
        out_specs=pl.BlockSpec((1, TM, A_VDIM), lambda b, h, i: (b, i, h)),
        out_shape=jax.ShapeDtypeStruct((bsz, t, A_HEADS * A_VDIM), BF16),
        compiler_params=_cparams(3),
        name="mla_attention",
    )(q, k, v)


def _out_kernel(o_ref, gate_ref, wout_ref, x_ref, mod_ref, gpost_ref, xo_ref):
    a = o_ref[0].astype(F32) * _silu(gate_ref[0].astype(F32))
    y = _dot(a.astype(BF16), wout_ref[...])
    xo_ref[0] = x_ref[0] + mod_ref[0, 0][2:3] * (_rms(y) * gpost_ref[...])


def _out_proj(o, gate, wout, xs, modl, gpost):
    bsz, t, _ = xs.shape
    w = o.shape[-1]
    return pl.pallas_call(
        _out_kernel,
        grid=(t // TM, bsz),
        in_specs=[_tok_spec(w), _tok_spec(w), _const_spec(wout.shape), _tok_spec(D), _mod_spec(),
                  _const_spec((1, D))],
        out_specs=_tok_spec(D),
        out_shape=jax.ShapeDtypeStruct(xs.shape, F32),
        input_output_aliases={3: 0},
        compiler_params=_cparams(2),
        name="out_proj_residual",
    )(o, gate, wout, xs, modl, gpost)


def _swa_proj_kernel(x_ref, mod_ref, gpre_ref, win_ref, ra_ref, rb_ref, rc_ref,
                     q_ref, k_ref, v_ref, gate_ref):
    h = _modnorm(x_ref[0], mod_ref, gpre_ref)
    p = _dot(h.astype(BF16), win_ref[...])
    ra, rb, rc = ra_ref[...], rb_ref[...], rc_ref[...]
    qw, kw = B_HEADS * B_HDIM, B_KV_HEADS * B_HDIM
    for s in range(qw // LANE):
        q_ref[0, :, s * LANE:(s + 1) * LANE] = _rope128(p[:, s * LANE:(s + 1) * LANE], ra, rb, rc).astype(BF16)
    for s in range(kw // LANE):
        o = qw + s * LANE
        k_ref[0, :, s * LANE:(s + 1) * LANE] = _rope128(p[:, o:o + LANE], ra, rb, rc).astype(BF16)
    v_ref[0] = p[:, qw + kw:qw + 2 * kw].astype(BF16)
    gate_ref[0] = p[:, qw + 2 * kw:].astype(BF16)


def _swa_proj(xs, modl, gpre, win, tabs):
    bsz, t, _ = xs.shape
    qw, kw = B_HEADS * B_HDIM, B_KV_HEADS * B_HDIM
    return pl.pallas_call(
        _swa_proj_kernel,
        grid=(t // TM, bsz),
        in_specs=[_tok_spec(D), _mod_spec(), _const_spec((1, D)), _const_spec(win.shape),
                  _table_spec(), _table_spec(), _table_spec()],
        out_specs=[_tok_spec(qw), _tok_spec(kw), _tok_spec(kw), _tok_spec(qw)],
        out_shape=[jax.ShapeDtypeStruct((bsz, t, qw), BF16),
                   jax.ShapeDtypeStruct((bsz, t, kw), BF16),
                   jax.ShapeDtypeStruct((bsz, t, kw), BF16),
                   jax.ShapeDtypeStruct((bsz, t, qw), BF16)],
        compiler_params=_cparams(2),
        name="swa_proj",
    )(xs, modl, gpre, win, *tabs)


def _swa_attn_kernel(q_ref, k_ref, v_ref, sink_ref, o_ref):
    i = pl.program_id(1)
    n_lat = k_ref.shape[1] - CTX
    band = 3 * B_BLOCK
    rows = B_GROUP * B_BLOCK
    lo = lax.broadcasted_iota(jnp.int32, (1, LANE), 1) < B_HDIM
    zero = jnp.zeros((), BF16)

    def run(with_band):
        if with_band:
            li = i - CTX // B_BLOCK
            start = jnp.clip((li - 1) * B_BLOCK, 0, n_lat - band)
            qpos = li * B_BLOCK + (lax.broadcasted_iota(jnp.int32, (rows, band), 0) & (B_BLOCK - 1))
            kpos = start + lax.broadcasted_iota(jnp.int32, (rows, band), 1)
            mask = jnp.abs(qpos - kpos) <= B_WINDOW
            kstart = pl.multiple_of(CTX + start, B_BLOCK)
        for j in range(B_KV_HEADS // 2):
            cols = slice(j * LANE, (j + 1) * LANE)
            qs = jnp.concatenate(
                [q_ref[0, :, (j * B_GROUP + g) * LANE:(j * B_GROUP + g + 1) * LANE] for g in range(B_GROUP)], axis=0)
            kc = k_ref[0, 0:CTX, cols]
            vc = v_ref[0, 0:CTX, cols]
            if with_band:
                kb = k_ref[0, pl.ds(kstart, band), cols]
                vb = v_ref[0, pl.ds(kstart, band), cols]
            acc = jnp.zeros((rows, LANE), F32)
            for half in range(2):
                hm = lo if half == 0 else jnp.logical_not(lo)
                sk = sink_ref[2 * j + half]
                s_c = _dot_nt(qs, jnp.where(hm, kc, zero))
                m = jnp.maximum(jnp.max(s_c, axis=-1, keepdims=True), sk)
                if with_band:
                    s_b = jnp.where(mask, _dot_nt(qs, jnp.where(hm, kb, zero)), NEG_INF)
                    m = jnp.maximum(m, jnp.max(s_b, axis=-1, keepdims=True))
                p_c = jnp.exp(s_c - m)
                l = jnp.sum(p_c, axis=-1, keepdims=True) + jnp.exp(sk - m)
                pv = _dot(p_c.astype(BF16), jnp.where(hm, vc, zero))
                if with_band:
                    p_b = jnp.exp(s_b - m)
                    l = l + jnp.sum(p_b, axis=-1, keepdims=True)
                    pv = pv + _dot(p_b.astype(BF16), jnp.where(hm, vb, zero))
                acc = acc + pv / l
            for g in range(B_GROUP):
                s = j * B_GROUP + g
                o_ref[0, :, s * LANE:(s + 1) * LANE] = acc[g * B_BLOCK:(g + 1) * B_BLOCK].astype(BF16)

    is_ctx = i < CTX // B_BLOCK
    pl.when(is_ctx)(lambda: run(False))
    pl.when(jnp.logical_not(is_ctx))(lambda: run(True))


def _swa_attn(q, k, v, sink_cols):
    bsz, t, qw = q.shape
    kw = k.shape[-1]
    return pl.pallas_call(
        _swa_attn_kernel,
        grid=(bsz, t // B_BLOCK),
        in_specs=[pl.BlockSpec((1, B_BLOCK, qw), lambda b, i: (b, i, 0)),
                  pl.BlockSpec((1, t, kw), lambda b, i: (b, 0, 0)),
                  pl.BlockSpec((1, t, kw), lambda b, i: (b, 0, 0)),
                  pl.BlockSpec(sink_cols.shape, lambda b, i: (0, 0, 0))],
        out_specs=pl.BlockSpec((1, B_BLOCK, qw), lambda b, i: (b, i, 0)),
        out_shape=jax.ShapeDtypeStruct((bsz, t, qw), BF16),
        compiler_params=_cparams(2),
        name="swa_attention",
    )(q, k, v, sink_cols)


def _hyena_proj_kernel(xp_ref, x_ref, xn_ref, mod_ref, gpre_ref, win_ref, cw_ref, cb_ref, u_ref, gate_ref):
    t = pl.program_id(0)
    nt = pl.num_programs(0)
    xe = jnp.concatenate([xp_ref[0], x_ref[0], xn_ref[0]], axis=0)
    h = _modnorm(xe, mod_ref, gpre_ref)
    p = _dot(h.astype(BF16), win_ref[...])
    cwid = 3 * C_WIDTH
    u = p[:, :cwid]
    gate_ref[0] = p[HALO:HALO + TM, cwid:].astype(BF16)
    ext = TM + 2 * HALO
    um = pltpu.roll(u, 1, 0)[HALO:HALO + TM]
    up = pltpu.roll(u, ext - 1, 0)[HALO:HALO + TM]
    r = lax.broadcasted_iota(jnp.int32, (TM, 1), 0)
    um = jnp.where(jnp.logical_and(r == 0, t <= 1), 0.0, um)
    up = jnp.where(jnp.logical_and(r == TM - 1, jnp.logical_or(t == 0, t == nt - 1)), 0.0, up)
    cw = cw_ref[...]
    y = cb_ref[...] + um * cw[0:1] + u[HALO:HALO + TM] * cw[1:2] + up * cw[2:3]
    u_ref[0] = y.astype(BF16)


def _hyena_proj(xs, modl, gpre, win, conv_w, conv_b):
    bsz, t, _ = xs.shape
    nt = t // TM
    per = TM // HALO
    last = t // HALO - 1
    return pl.pallas_call(
        _hyena_proj_kernel,
        grid=(nt, bsz),
        in_specs=[pl.BlockSpec((1, HALO, D), lambda i, b: (b, jnp.maximum(i * per - 1, 0), 0)),
                  _tok_spec(D),
                  pl.BlockSpec((1, HALO, D), lambda i, b: (b, jnp.minimum((i + 1) * per, last), 0)),
                  _mod_spec(), _const_spec((1, D)), _const_spec(win.shape),
                  _const_spec(conv_w.shape), _const_spec(conv_b.shape)],
        out_specs=[_tok_spec(3 * C_WIDTH), _tok_spec(C_WIDTH)],
        out_shape=[jax.ShapeDtypeStruct((bsz, t, 3 * C_WIDTH), BF16),
                   jax.ShapeDtypeStruct((bsz, t, C_WIDTH), BF16)],
        compiler_params=_cparams(2),
        name="hyena_proj",
    )(xs, xs, xs, modl, gpre, win, conv_w, conv_b)


def _filter_kernel(z_ref, w1_ref, b1_ref, fr_ref, w2_ref, b2_ref, w3_ref, dl_ref, o_ref):
    hp = lax.Precision.HIGHEST
    z = z_ref[...]
    fr = fr_ref[...]
    h = jnp.sin(fr * (jnp.dot(z, w1_ref[...], precision=hp, preferred_element_type=F32) + b1_ref[...]))
    h = jnp.sin(fr * (jnp.dot(h, w2_ref[...], precision=hp, preferred_element_type=F32) + b2_ref[...]))
    h = jnp.dot(h, w3_ref[...], precision=hp, preferred_element_type=F32)
    o_ref[...] = (h * jnp.exp(-z[:, 0:1] * dl_ref[...])).astype(o_ref.dtype)


def _filters(n, w1, b1, fr, w2, b2, w3):
    t = np.linspace(0.0, 1.0, n, dtype=np.float32)[:, None]
    w = ((2.0 * math.pi / n) * np.arange(n, dtype=np.float32))[:, None].astype(np.float32)
    bands = np.linspace(1e-4, C_BANDS - 1, C_BANDS, dtype=np.float32)[None, :]
    z = np.zeros((n, LANE), np.float32)
    z[:, 0:1] = t
    z[:, 1:1 + C_BANDS] = np.cos(bands * w)
    z[:, 1 + C_BANDS:C_EMB] = -np.sin(bands * w)
    deltas = np.abs(np.linspace(C_MIN_DECAY, C_MAX_DECAY, C_WIDTH, dtype=np.float32))
    dl = np.tile(deltas, 4)[None, :]

    def pad(a, r, c):
        return jnp.zeros((r, c), F32).at[:a.shape[0], :a.shape[1]].set(a)

    tn = min(n, TM)
    nout = 4 * C_WIDTH
    cs = lambda shape: pl.BlockSpec(shape, lambda i: (0, 0))
    return pl.pallas_call(
        _filter_kernel,
        grid=(n // tn,),
        in_specs=[pl.BlockSpec((tn, LANE), lambda i: (i, 0)), cs((LANE, LANE)), cs((1, LANE)), cs((1, LANE)),
                  cs((LANE, LANE)), cs((1, LANE)), cs((LANE, nout)), cs((1, nout))],
        out_specs=pl.BlockSpec((tn, nout), lambda i: (i, 0)),
        out_shape=jax.ShapeDtypeStruct((n, nout), BF16),
        compiler_params=_cparams(1),
        name="hyena_filter_mlp",
    )(jnp.asarray(z), pad(w1, LANE, LANE), pad(b1[None], 1, LANE), pad(fr[None], 1, LANE),
      pad(w2, LANE, LANE), pad(b2[None], 1, LANE), pad(w3, LANE, nout), jnp.asarray(dl))


def _dft_matrix(n):
    f = np.arange(n, dtype=np.int64)[:, None]
    s = np.arange(n, dtype=np.int64)[None, :]
    ang = (2.0 * np.pi / (2 * n)) * ((f * s) % (2 * n)).astype(np.float64)
    cos = np.cos(ang)
    sin = np.sin(ang)
    sin[0, :] = np.where(np.arange(n) % 2 == 0, 1.0, -1.0)
    return cos.astype(np.float32), sin.astype(np.float32)


def _matmul_kernel(a_ref, b_ref, o_ref):
    o_ref[...] = _dot(a_ref[...], b_ref[...])


def _matmul(a, b, bm, bn):
    m, k = a.shape
    n = b.shape[1]
    return pl.pallas_call(
        _matmul_kernel,
        grid=(m // bm, n // bn),
        in_specs=[pl.BlockSpec((bm, k), lambda i, j: (i, 0)), pl.BlockSpec((k, bn), lambda i, j: (0, j))],
        out_specs=pl.BlockSpec((bm, bn), lambda i, j: (i, j)),
        out_shape=jax.ShapeDtypeStruct((m, n), F32),
        compiler_params=_cparams(2),
        name="filter_dft",
    )(a, b)


def _hyena_conv_kernel(v_ref, x1_ref, x2_ref, f_ref, ft_ref, p1_ref, p2_ref, p3_ref, fb_ref, o_ref,
                       zb_ref, y_ref, *, row0, n, fbs):
    o = pl.program_id(2)
    fb = pl.program_id(3)
    nfb = pl.num_programs(3)
    rows = slice(row0, row0 + n)

    @pl.when(jnp.logical_and(o == 0, fb == 0))
    def _():
        zb_ref[...] = v_ref[0, rows, :]

    @pl.when(fb == 0)
    def _():
        y_ref[...] = jnp.zeros_like(y_ref)

    zf = _dot(f_ref[0], zb_ref[...])
    re, im = zf[:fbs], zf[fbs:]
    p2 = p2_ref[0]
    a = re * p1_ref[0] + im * p2
    bv = im * p3_ref[0] - re * p2
    w = jnp.concatenate([a, bv], axis=0).astype(BF16)
    y_ref[...] += _dot(ft_ref[0], w)

    @pl.when(fb == nfb - 1)
    def _():
        y = y_ref[...] + zb_ref[...].astype(F32) * fb_ref[0]

        @pl.when(o == 0)
        def _():
            zb_ref[...] = (x1_ref[0, rows, :].astype(F32) * y).astype(BF16)

        @pl.when(o == 1)
        def _():
            if row0 > 0:
                o_ref[0, 0:row0, :] = jnp.zeros((row0, o_ref.shape[2]), o_ref.dtype)
            o_ref[0, rows, :] = (x2_ref[0, rows, :].astype(F32) * y).astype(BF16)


def _hyena_conv(u3, fmat, fmat_t, p1, p2, p3, fbias, *, row0, n, block_rows, prev=None):
    bsz, t, _ = u3.shape
    tc = 512
    nct = C_WIDTH // tc
    nfb = fmat.shape[0]
    fbs = fmat.shape[1] // 2
    u_spec = lambda which: pl.BlockSpec((1, block_rows, tc), lambda b, c, o, f: (b, 0, which * nct + c))
    p_spec = pl.BlockSpec((1, fbs, tc), lambda b, c, o, f: (o, f, c))
    in_specs = [u_spec(0), u_spec(1), u_spec(2),
                pl.BlockSpec((1, 2 * fbs, n), lambda b, c, o, f: (f, 0, 0)),
                pl.BlockSpec((1, n, 2 * fbs), lambda b, c, o, f: (f, 0, 0)),
                p_spec, p_spec, p_spec,
                pl.BlockSpec((1, 1, tc), lambda b, c, o, f: (o, 0, c))]
    args = [u3, u3, u3, fmat, fmat_t, p1, p2, p3, fbias]
    aliases = {}
    kern = functools.partial(_hyena_conv_kernel, row0=row0, n=n, fbs=fbs)
    if prev is not None:
        in_specs.append(pl.BlockSpec(memory_space=pl.ANY))
        args.append(prev)
        aliases = {len(args) - 1: 0}
        inner = kern
        kern = lambda *refs: inner(*refs[:9], *refs[10:])
    return pl.pallas_call(
        kern,
        grid=(bsz, nct, 2, nfb),
        in_specs=in_specs,
        out_specs=pl.BlockSpec((1, block_rows, tc), lambda b, c, o, f: (b, 0, c)),
        out_shape=jax.ShapeDtypeStruct((bsz, t, C_WIDTH), BF16),
        scratch_shapes=[pltpu.VMEM((n, tc), BF16), pltpu.VMEM((n, tc), F32)],
        input_output_aliases=aliases,
        compiler_params=_cparams(4),
        name="hyena_long_conv_n%d" % n,
    )(*args)


def _hyena_spectra(n, filt):
    cos, sin = _dft_matrix(n)
    fbs = min(n, 256)
    nfb = n // fbs
    full = jnp.asarray(np.concatenate([cos, sin], axis=0), BF16)
    hspec = _matmul(full, filt, min(2 * n, 512), 1024)
    blocks = np.concatenate([cos.reshape(nfb, fbs, n), sin.reshape(nfb, fbs, n)], axis=1)
    fmat = jnp.asarray(blocks, BF16)
    fmat_t = jnp.asarray(np.transpose(blocks, (0, 2, 1)), BF16)
    hf = hspec[:, :2 * C_WIDTH].reshape(2 * n, 2, C_WIDTH)
    hb = hspec[:, 2 * C_WIDTH:].reshape(2 * n, 2, C_WIDTH)
    scale = np.full((n, 1, 1), 2.0 / (2 * n), np.float32)
    scale[0] = 1.0 / (2 * n)
    first = (np.arange(n) == 0)[:, None, None]
    kre = (hf[:n] + hb[:n]) * scale
    kim = jnp.where(first, 0.0, (hb[n:] - hf[n:]) * scale)
    knyq = (hf[n:n + 1] + hb[n:n + 1]) * scale[0]
    p3 = jnp.where(first, knyq, kre)
    tr = lambda a: jnp.transpose(a, (1, 0, 2))
    return fmat, fmat_t, tr(kre), tr(kim), tr(p3)


def _rope_tables(seq, layout):
    rows = seq // GRID_W
    row = np.repeat(np.arange(rows, dtype=np.float32), GRID_W)
    col = np.tile(np.arange(GRID_W, dtype=np.float32), rows)
    per_axis = 32
    inv = (ROPE_BASE ** (-np.arange(0, per_axis, 2, dtype=np.float32) / per_axis)).astype(np.float32)
    ang = np.concatenate([row[:, None] * inv, col[:, None] * inv], axis=-1)
    cos = np.concatenate([np.ones((CTX, 32), np.float32), np.cos(ang)], axis=0)
    sin = np.concatenate([np.zeros((CTX, 32), np.float32), np.sin(ang)], axis=0)
    one, zero = np.ones_like(cos), np.zeros_like(cos)
    if layout == "mla":
        a = [cos, cos, one, one]
        b = [zero, sin, zero, zero]
        c = [-sin, zero, zero, zero]
    else:
        a = [cos, cos, cos, cos]
        b = [zero, sin, zero, sin]
        c = [-sin, zero, -sin, zero]
    return tuple(jnp.asarray(np.concatenate(p, axis=1), F32) for p in (a, b, c))


def _swa_head_perm():
    cols = []
    for j in range(B_KV_HEADS // 2):
        for g in range(B_GROUP):
            for hk in (2 * j, 2 * j + 1):
                h = hk * B_GROUP + g
                cols.extend(range(h * B_HDIM, (h + 1) * B_HDIM))
    return np.asarray(cols, np.int32)


def _mla_weights(w_in, w_q, w_kv):
    c1 = A_Q_RANK + A_KV_RANK
    zpad = jnp.zeros((D, LANE - A_ROPE), F32)
    win = jnp.concatenate([w_in[:, :c1 + A_ROPE], zpad, w_in[:, c1 + A_ROPE:]], axis=1)
    qscale = (A_NOPE + A_ROPE) ** -0.5
    wq = w_q.reshape(A_Q_RANK, A_HEADS, A_NOPE + A_ROPE) * qscale
    wq = jnp.concatenate([wq, jnp.zeros((A_Q_RANK, A_HEADS, A_HEAD_PAD - A_NOPE - A_ROPE), F32)], axis=-1)
    wq = wq.reshape(A_Q_RANK, A_HEADS * A_HEAD_PAD)
    wkv = w_kv.reshape(A_KV_RANK, A_HEADS, A_NOPE + A_VDIM)
    wkv = jnp.concatenate([wkv[:, :, :A_NOPE].reshape(A_KV_RANK, -1), wkv[:, :, A_NOPE:].reshape(A_KV_RANK, -1)], axis=1)
    return win.astype(BF16), wq.astype(BF16), wkv.astype(BF16)


def kernel(x, c, ctx, c_ctx, w_mod, b_mod, g_pre, g_post, a_w_in, a_g_q, a_w_q, a_g_kv, a_w_kv, a_w_out, b_w_in, b_sink, b_w_out, c_w_in, c_conv_w, c_conv_b, c_f_w1, c_f_b1, c_f_freq, c_f_w2, c_f_b2, c_f_w3, c_filt_bias, c_w_out):
    bsz, seq, _ = x.shape
    assert ctx.shape[1] == CTX and seq % TM == 0 and seq % GRID_W == 0
    xs = jnp.concatenate([ctx, x], axis=1)

    pad_rows = (-(bsz + 1)) % 8
    cond = jnp.concatenate([c, c_ctx[None], jnp.zeros((pad_rows, D), F32)], axis=0)
    mod = _modulation(cond, w_mod, b_mod)

    tabs_mla = _rope_tables(seq, "mla")
    tabs_swa = _rope_tables(seq, "swa")

    for layer in range(DEPTH):
        kind, j = layer % 3, layer // 3
        mx = mod[layer, :bsz].reshape(bsz, 3, D)
        mc = jnp.broadcast_to(mod[layer, bsz].reshape(1, 3, D), (bsz, 3, D))
        modl = jnp.stack([mc, mx], axis=0)
        gpre = g_pre[layer][None]
        gpost = g_post[layer][None]
        if kind == 0:
            win, wq, wkv = _mla_weights(a_w_in[j], a_w_q[j], a_w_kv[j])
            q, k, v, gate = _mla_proj(xs, modl, gpre, win, a_g_q[j][None], wq, a_g_kv[j][None], wkv, tabs_mla)
            o = _mla_attn(q, k, v)
            wout = a_w_out[j].astype(BF16)
        elif kind == 1:
            perm = _swa_head_perm()
            qw, kw = B_HEADS * B_HDIM, B_KV_HEADS * B_HDIM
            w = b_w_in[j]
            win = jnp.concatenate([w[:, :qw][:, perm] * (B_HDIM ** -0.5), w[:, qw:qw + 2 * kw],
                                   w[:, qw + 2 * kw:][:, perm]], axis=1).astype(BF16)
            q, k, v, gate = _swa_proj(xs, modl, gpre, win, tabs_swa)
            sink = b_sink[j].astype(F32).reshape(B_KV_HEADS, B_GROUP, 1, 1)
            sink_cols = jnp.broadcast_to(sink, (B_KV_HEADS, B_GROUP, B_BLOCK, 1)).reshape(B_KV_HEADS, B_GROUP * B_BLOCK, 1)
            o = _swa_attn(q, k, v, sink_cols)
            wout = b_w_out[j][perm, :].astype(BF16)
        else:
            u3, gate = _hyena_proj(xs, modl, gpre, c_w_in[j].astype(BF16), c_conv_w[j], c_conv_b[j][None])
            fargs = (c_f_w1[j], c_f_b1[j], c_f_freq[j], c_f_w2[j], c_f_b2[j], c_f_w3[j])
            fbias = c_filt_bias[j].reshape(2, 1, C_WIDTH)
            spec_x = _hyena_spectra(seq, _filters(seq, *fargs))
            o = _hyena_conv(u3, *spec_x, fbias, row0=CTX, n=seq, block_rows=CTX + seq)
            spec_c = _hyena_spectra(CTX, _filters(CTX, *fargs))
            o = _hyena_conv(u3, *spec_c, fbias, row0=0, n=CTX, block_rows=CTX, prev=o)
            wout = c_w_out[j].astype(BF16)
        xs = _out_proj(o, gate, wout, xs, modl, gpost)
    return xs[:, CTX:]
```

```python
import functools
import math

import numpy as np
import jax
import jax.numpy as jnp
from jax import lax
from jax.experimental import pallas as pl
from jax.experimental.pallas import tpu as pltpu

F32 = jnp.float32
BF16 = jnp.bfloat16

D = 1024
DEPTH = 4
GRID_W = 64
CTX = 256
NORM_EPS = 1e-6
ROPE_BASE = 10000.0
NEG_INF = -1e30

A_HEADS = 8
A_Q_RANK = 512
A_KV_RANK = 256
A_NOPE = 128
A_ROPE = 64
A_VDIM = 128
A_HEAD_PAD = 256

B_HEADS = 16
B_KV_HEADS = 4
B_GROUP = 4
B_HDIM = 64
B_WINDOW = 128
B_BLOCK = 128

C_WIDTH = 1024
C_BANDS = 16
C_EMB = 1 + 2 * C_BANDS
C_FFN = 64
C_MIN_DECAY = math.log(1e-2) / 1.5
C_MAX_DECAY = math.log(1e-2) / 0.3

LANE = 128
TM = 256
HALO = 8
VMEM_LIMIT = 56 * 1024 * 1024


def _cparams(n_axes):
    return pltpu.CompilerParams(dimension_semantics=("arbitrary",) * n_axes,
                                vmem_limit_bytes=VMEM_LIMIT)


def _rms(x):
    return x * lax.rsqrt(jnp.mean(x * x, axis=-1, keepdims=True) + NORM_EPS)


def _silu(g):
    return g / (1.0 + jnp.exp(-g))


def _dot(a, b):
    return jnp.dot(a, b, preferred_element_type=F32)


def _dot_nt(a, b):
    return lax.dot_general(a, b, (((1,), (1,)), ((), ())), preferred_element_type=F32)


def _rope128(x, a, b, c):
    return x * a + pltpu.roll(x, 32, 1) * b + pltpu.roll(x, LANE - 32, 1) * c


def _modnorm(x, mod_ref, gpre_ref):
    m = mod_ref[0, 0]
    return _rms(x) * gpre_ref[...] * (1.0 + m[1:2]) + m[0:1]


def _mod_kernel(c_ref, w_ref, b_ref, o_ref):
    a = _silu(c_ref[...])
    o_ref[0] = _dot(a.astype(BF16), w_ref[0].astype(BF16)) + b_ref[0]


def _modulation(cond, w_mod, b_mod):
    rows = cond.shape[0]
    return pl.pallas_call(
        _mod_kernel,
        grid=(DEPTH, 3),
        in_specs=[pl.BlockSpec((rows, D), lambda l, j: (0, 0)),
                  pl.BlockSpec((1, D, D), lambda l, j: (l, 0, j)),
                  pl.BlockSpec((1, 1, D), lambda l, j: (l, 0, j))],
        out_specs=pl.BlockSpec((1, rows, D), lambda l, j: (l, 0, j)),
        out_shape=jax.ShapeDtypeStruct((DEPTH, rows, 3 * D), F32),
        compiler_params=_cparams(2),
        name="adaln_modulation",
    )(cond, w_mod, b_mod.reshape(DEPTH, 1, 3 * D))


def _tok_spec(width):
    return pl.BlockSpec((1, TM, width), lambda t, b: (b, t, 0))


def _mod_spec():
    return pl.BlockSpec((1, 1, 3, D), lambda t, b: (jnp.minimum(t, 1), b, 0, 0))


def _const_spec(shape):
    nd = len(shape)
    return pl.BlockSpec(shape, lambda t, b: (0,) * nd)


def _head_spec(width):
    return pl.BlockSpec((1, A_HEADS, TM, width), lambda t, b: (b, 0, t, 0))


def _table_spec():
    return pl.BlockSpec((TM, LANE), lambda t, b: (t, 0))


def _mla_proj_kernel(x_ref, mod_ref, gpre_ref, win_ref, gq_ref, wq_ref, gkv_ref, wkv_ref,
                     ra_ref, rb_ref, rc_ref, q_ref, k_ref, v_ref, gate_ref):
    h = _modnorm(x_ref[0], mod_ref, gpre_ref)
    p = _dot(h.astype(BF16), win_ref[...])
    c0, c1, c2 = A_Q_RANK, A_Q_RANK + A_KV_RANK, A_Q_RANK + A_KV_RANK + LANE
    gate_ref[0] = p[:, c2:].astype(BF16)
    q = _dot((_rms(p[:, :c0]) * gq_ref[...]).astype(BF16), wq_ref[...])
    kv = _dot((_rms(p[:, c0:c1]) * gkv_ref[...]).astype(BF16), wkv_ref[...])
    ra, rb, rc = ra_ref[...], rb_ref[...], rc_ref[...]
    kr = _rope128(p[:, c1:c2], ra, rb, rc).astype(BF16)
    for hd in range(A_HEADS):
        o = hd * A_HEAD_PAD
        q_ref[0, hd, :, 0:LANE] = q[:, o:o + LANE].astype(BF16)
        q_ref[0, hd, :, LANE:2 * LANE] = _rope128(q[:, o + LANE:o + 2 * LANE], ra, rb, rc).astype(BF16)
        k_ref[0, hd, :, 0:LANE] = kv[:, hd * A_NOPE:(hd + 1) * A_NOPE].astype(BF16)
        k_ref[0, hd, :, LANE:2 * LANE] = kr
        v_ref[0, hd] = kv[:, (A_HEADS + hd) * A_NOPE:(A_HEADS + hd) * A_NOPE + A_VDIM].astype(BF16)


def _mla_proj(xs, modl, gpre, win, gq, wq, gkv, wkv, tabs):
    bsz, t, _ = xs.shape
    return pl.pallas_call(
        _mla_proj_kernel,
        grid=(t // TM, bsz),
        in_specs=[_tok_spec(D), _mod_spec(), _const_spec((1, D)), _const_spec(win.shape),
                  _const_spec((1, A_Q_RANK)), _const_spec(wq.shape),
                  _const_spec((1, A_KV_RANK)), _const_spec(wkv.shape),
                  _table_spec(), _table_spec(), _table_spec()],
        out_specs=[_head_spec(A_HEAD_PAD), _head_spec(A_HEAD_PAD), _head_spec(A_VDIM), _tok_spec(A_HEADS * A_VDIM)],
        out_shape=[jax.ShapeDtypeStruct((bsz, A_HEADS, t, A_HEAD_PAD), BF16),
                   jax.ShapeDtypeStruct((bsz, A_HEADS, t, A_HEAD_PAD), BF16),
                   jax.ShapeDtypeStruct((bsz, A_HEADS, t, A_VDIM), BF16),
                   jax.ShapeDtypeStruct((bsz, t, A_HEADS * A_VDIM), BF16)],
        compiler_params=_cparams(2),
        name="mla_proj",
    )(xs, modl, gpre, win, gq, wq, gkv, wkv, *tabs)


def _mla_attn_kernel(q_ref, k_ref, v_ref, o_ref):
    def attend(nk):
        for hd in range(A_HEADS):
            s = _dot_nt(q_ref[0, hd], k_ref[0, hd, :nk, :])
            m = jnp.max(s, axis=-1, keepdims=True)
            p = jnp.exp2(s - m)
            l = jnp.sum(p, axis=-1, keepdims=True)
            o = _dot(p.astype(BF16), v_ref[0, hd, :nk, :]) / l
            o_ref[0, :, hd * A_VDIM:(hd + 1) * A_VDIM] = o.astype(BF16)

    t_all = k_ref.shape[2]
    is_ctx = pl.program_id(1) == 0
    pl.when(is_ctx)(lambda: attend(CTX))
    pl.when(jnp.logical_not(is_ctx))(lambda: attend(t_all))


def _mla_attn(q, k, v):
    bsz, nh, t, _ = q.shape
    return pl.pallas_call(
        _mla_attn_kernel,
        grid=(bsz, t // TM),
        in_specs=[pl.BlockSpec((1, nh, TM, A_HEAD_PAD), lambda b, i: (b, 0, i, 0)),
                  pl.BlockSpec((1, nh, t, A_HEAD_PAD), lambda b, i: (b, 0, 0, 0)),
                  pl.BlockSpec((1, nh, t, A_VDIM), lambda b, i: (b, 0, 0, 0))],
        out_specs=pl.BlockSpec((1, TM, nh * A_VDIM), lambda b, i: (b, i, 0)),
        out_shape=jax.ShapeDtypeStruct((bsz, t, nh * A_VDIM), BF16),
        compiler_params=_cparams(2),
        name="mla_attention",
    )(q, k, v)


def _out_kernel(o_ref, gate_ref, wout_ref, x_ref, mod_ref, gpost_ref, xo_ref):
    a = o_ref[0].astype(F32) * _silu(gate_ref[0].astype(F32))
    y = _dot(a.astype(BF16), wout_ref[...])
    xo_ref[0] = x_ref[0] + mod_ref[0, 0][2:3] * (_rms(y) * gpost_ref[...])


def _out_proj(o, gate, wout, xs, modl, gpost):
    bsz, t, _ = xs.shape
    w = o.shape[-1]
    return pl.pallas_call(
        _out_kernel,
        grid=(t // TM, bsz),
        in_specs=[_tok_spec(w), _tok_spec(w), _const_spec(wout.shape), _tok_spec(D), _mod_spec(),
                  _const_spec((1, D))],
        out_specs=_tok_spec(D),
        out_shape=jax.ShapeDtypeStruct(xs.shape, F32),
        input_output_aliases={3: 0},
        compiler_params=_cparams(2),
        name="out_proj_residual",
    )(o, gate, wout, xs, modl, gpost)


def _swa_proj_kernel(x_ref, mod_ref, gpre_ref, win_ref, ra_ref, rb_ref, rc_ref,
                     q_ref, k_ref, v_ref, gate_ref):
    h = _modnorm(x_ref[0], mod_ref, gpre_ref)
    p = _dot(h.astype(BF16), win_ref[...])
    ra, rb, rc = ra_ref[...], rb_ref[...], rc_ref[...]
    qw, kw = B_HEADS * B_HDIM, B_KV_HEADS * B_HDIM
    lane = lax.broadcasted_iota(jnp.int32, (1, LANE), 1)
    lo = lane < B_HDIM
    ones_col = jnp.broadcast_to(jnp.where(lane == 0, 1.0, 0.0), (TM, LANE)).astype(BF16)
    for j in range(B_KV_HEADS // 2):
        for g in range(B_GROUP):
            s = j * B_GROUP + g
            r = _rope128(p[:, s * LANE:(s + 1) * LANE], ra, rb, rc)
            for half in range(2):
                d = ((2 * j + half) * B_GROUP + g) * LANE
                keep = lo if half == 0 else jnp.logical_not(lo)
                q_ref[0, :, d:d + LANE] = jnp.where(keep, r, 0.0).astype(BF16)
        o = qw + j * LANE
        k_ref[0, :, j * LANE:(j + 1) * LANE] = _rope128(p[:, o:o + LANE], ra, rb, rc).astype(BF16)
        v_ref[0, :, 2 * j * LANE:(2 * j + 1) * LANE] = p[:, o + kw:o + kw + LANE].astype(BF16)
        v_ref[0, :, (2 * j + 1) * LANE:(2 * j + 2) * LANE] = ones_col
    gate_ref[0] = p[:, qw + 2 * kw:].astype(BF16)


def _swa_proj(xs, modl, gpre, win, tabs):
    bsz, t, _ = xs.shape
    qw, kw = B_HEADS * B_HDIM, B_KV_HEADS * B_HDIM
    return pl.pallas_call(
        _swa_proj_kernel,
        grid=(t // TM, bsz),
        in_specs=[_tok_spec(D), _mod_spec(), _const_spec((1, D)), _const_spec(win.shape),
                  _table_spec(), _table_spec(), _table_spec()],
        out_specs=[_tok_spec(2 * qw), _tok_spec(kw), _tok_spec(2 * kw), _tok_spec(qw)],
        out_shape=[jax.ShapeDtypeStruct((bsz, t, 2 * qw), BF16),
                   jax.ShapeDtypeStruct((bsz, t, kw), BF16),
                   jax.ShapeDtypeStruct((bsz, t, 2 * kw), BF16),
                   jax.ShapeDtypeStruct((bsz, t, qw), BF16)],
        compiler_params=_cparams(2),
        name="swa_proj",
    )(xs, modl, gpre, win, *tabs)


def _swa_attn_kernel(q_ref, k_ref, v_ref, sink_ref, o_ref):
    i = pl.program_id(1)
    n_lat = k_ref.shape[1] - CTX
    band = 3 * B_BLOCK
    rows = B_GROUP * B_BLOCK
    lo = lax.broadcasted_iota(jnp.int32, (1, LANE), 1) < B_HDIM

    def run(with_band):
        if with_band:
            li = i - CTX // B_BLOCK
            start = jnp.clip((li - 1) * B_BLOCK, 0, n_lat - band)
            off = li * B_BLOCK - start
            d0 = (lax.broadcasted_iota(jnp.int32, (B_BLOCK, band), 0)
                  - lax.broadcasted_iota(jnp.int32, (B_BLOCK, band), 1))
            mask = jnp.abs(d0 + off) <= B_WINDOW
            kstart = pl.multiple_of(CTX + start, B_BLOCK)
        for j in range(B_KV_HEADS // 2):
            kcols = slice(j * LANE, (j + 1) * LANE)
            vcols = slice(2 * j * LANE, (2 * j + 2) * LANE)
            kk = k_ref[0, 0:CTX, kcols]
            vv = v_ref[0, 0:CTX, vcols]
            if with_band:
                kk = jnp.concatenate([kk, k_ref[0, pl.ds(kstart, band), kcols]], axis=0)
                vv = jnp.concatenate([vv, v_ref[0, pl.ds(kstart, band), vcols]], axis=0)
            outs = []
            for half in range(2):
                hk = 2 * j + half
                qs = jnp.concatenate(
                    [q_ref[0, :, (hk * B_GROUP + g) * LANE:(hk * B_GROUP + g + 1) * LANE] for g in range(B_GROUP)],
                    axis=0)
                sk = sink_ref[hk]
                s = _dot_nt(qs, kk)
                s_c = s[:, :CTX]
                m = jnp.maximum(jnp.max(s_c, axis=-1, keepdims=True), sk)
                if with_band:
                    s_b = jnp.concatenate(
                        [jnp.where(mask, s[g * B_BLOCK:(g + 1) * B_BLOCK, CTX:], NEG_INF) for g in range(B_GROUP)],
                        axis=0)
                    m = jnp.maximum(m, jnp.max(s_b, axis=-1, keepdims=True))
                    p = jnp.concatenate([jnp.exp2(s_c - m), jnp.exp2(s_b - m)], axis=1)
                else:
                    p = jnp.exp2(s_c - m)
                pv = _dot(p.astype(BF16), vv)
                l = pv[:, LANE:LANE + 1] + jnp.exp2(sk - m)
                outs.append(pv[:, :LANE] / l)
            comb = jnp.where(lo, outs[0], outs[1])
            for g in range(B_GROUP):
                s_out = j * B_GROUP + g
                o_ref[0, :, s_out * LANE:(s_out + 1) * LANE] = comb[g * B_BLOCK:(g + 1) * B_BLOCK].astype(BF16)

    is_ctx = i < CTX // B_BLOCK
    pl.when(is_ctx)(lambda: run(False))
    pl.when(jnp.logical_not(is_ctx))(lambda: run(True))


def _swa_attn(q, k, v, sink_cols):
    bsz, t, qw2 = q.shape
    kw = k.shape[-1]
    return pl.pallas_call(
        _swa_attn_kernel,
        grid=(bsz, t // B_BLOCK),
        in_specs=[pl.BlockSpec((1, B_BLOCK, qw2), lambda b, i: (b, i, 0)),
                  pl.BlockSpec((1, t, kw), lambda b, i: (b, 0, 0)),
                  pl.BlockSpec((1, t, 2 * kw), lambda b, i: (b, 0, 0)),
                  pl.BlockSpec(sink_cols.shape, lambda b, i: (0, 0, 0))],
        out_specs=pl.BlockSpec((1, B_BLOCK, qw2 // 2), lambda b, i: (b, i, 0)),
        out_shape=jax.ShapeDtypeStruct((bsz, t, qw2 // 2), BF16),
        compiler_params=_cparams(2),
        name="swa_attention",
    )(q, k, v, sink_cols)


def _hyena_proj_kernel(xp_ref, x_ref, xn_ref, mod_ref, gpre_ref, win_ref, cw_ref, cb_ref, u_ref, gate_ref):
    t = pl.program_id(0)
    nt = pl.num_programs(0)
    xe = jnp.concatenate([xp_ref[0], x_ref[0], xn_ref[0]], axis=0)
    h = _modnorm(xe, mod_ref, gpre_ref)
    p = _dot(h.astype(BF16), win_ref[...])
    cwid = 3 * C_WIDTH
    u = p[:, :cwid]
    gate_ref[0] = p[HALO:HALO + TM, cwid:].astype(BF16)
    ext = TM + 2 * HALO
    um = pltpu.roll(u, 1, 0)[HALO:HALO + TM]
    up = pltpu.roll(u, ext - 1, 0)[HALO:HALO + TM]
    r = lax.broadcasted_iota(jnp.int32, (TM, 1), 0)
    um = jnp.where(jnp.logical_and(r == 0, t <= 1), 0.0, um)
    up = jnp.where(jnp.logical_and(r == TM - 1, jnp.logical_or(t == 0, t == nt - 1)), 0.0, up)
    cw = cw_ref[...]
    y = cb_ref[...] + um * cw[0:1] + u[HALO:HALO + TM] * cw[1:2] + up * cw[2:3]
    u_ref[0] = y.astype(BF16)


def _hyena_proj(xs, modl, gpre, win, conv_w, conv_b):
    bsz, t, _ = xs.shape
    nt = t // TM
    per = TM // HALO
    last = t // HALO - 1
    return pl.pallas_call(
        _hyena_proj_kernel,
        grid=(nt, bsz),
        in_specs=[pl.BlockSpec((1, HALO, D), lambda i, b: (b, jnp.maximum(i * per - 1, 0), 0)),
                  _tok_spec(D),
                  pl.BlockSpec((1, HALO, D), lambda i, b: (b, jnp.minimum((i + 1) * per, last), 0)),
                  _mod_spec(), _const_spec((1, D)), _const_spec(win.shape),
                  _const_spec(conv_w.shape), _const_spec(conv_b.shape)],
        out_specs=[_tok_spec(3 * C_WIDTH), _tok_spec(C_WIDTH)],
        out_shape=[jax.ShapeDtypeStruct((bsz, t, 3 * C_WIDTH), BF16),
                   jax.ShapeDtypeStruct((bsz, t, C_WIDTH), BF16)],
        compiler_params=_cparams(2),
        name="hyena_proj",
    )(xs, xs, xs, modl, gpre, win, conv_w, conv_b)


def _filter_kernel(z_ref, w1_ref, b1_ref, fr_ref, w2_ref, b2_ref, w3_ref, dl_ref, o_ref):
    hp = lax.Precision.HIGHEST
    z = z_ref[...]
    fr = fr_ref[...]
    h = jnp.sin(fr * (jnp.dot(z, w1_ref[...], precision=hp, preferred_element_type=F32) + b1_ref[...]))
    h = jnp.sin(fr * (jnp.dot(h, w2_ref[...], precision=hp, preferred_element_type=F32) + b2_ref[...]))
    h = jnp.dot(h, w3_ref[...], precision=hp, preferred_element_type=F32)
    o_ref[...] = (h * jnp.exp(-z[:, 0:1] * dl_ref[...])).astype(o_ref.dtype)


def _filters(n, w1, b1, fr, w2, b2, w3):
    t = np.linspace(0.0, 1.0, n, dtype=np.float32)[:, None]
    w = ((2.0 * math.pi / n) * np.arange(n, dtype=np.float32))[:, None].astype(np.float32)
    bands = np.linspace(1e-4, C_BANDS - 1, C_BANDS, dtype=np.float32)[None, :]
    z = np.zeros((n, LANE), np.float32)
    z[:, 0:1] = t
    z[:, 1:1 + C_BANDS] = np.cos(bands * w)
    z[:, 1 + C_BANDS:C_EMB] = -np.sin(bands * w)
    deltas = np.abs(np.linspace(C_MIN_DECAY, C_MAX_DECAY, C_WIDTH, dtype=np.float32))
    dl = np.tile(deltas, 4)[None, :]

    def pad(a, r, c):
        return jnp.zeros((r, c), F32).at[:a.shape[0], :a.shape[1]].set(a)

    tn = min(n, TM)
    nout = 4 * C_WIDTH
    cs = lambda shape: pl.BlockSpec(shape, lambda i: (0, 0))
    return pl.pallas_call(
        _filter_kernel,
        grid=(n // tn,),
        in_specs=[pl.BlockSpec((tn, LANE), lambda i: (i, 0)), cs((LANE, LANE)), cs((1, LANE)), cs((1, LANE)),
                  cs((LANE, LANE)), cs((1, LANE)), cs((LANE, nout)), cs((1, nout))],
        out_specs=pl.BlockSpec((tn, nout), lambda i: (i, 0)),
        out_shape=jax.ShapeDtypeStruct((n, nout), BF16),
        compiler_params=_cparams(1),
        name="hyena_filter_mlp",
    )(jnp.asarray(z), pad(w1, LANE, LANE), pad(b1[None], 1, LANE), pad(fr[None], 1, LANE),
      pad(w2, LANE, LANE), pad(b2[None], 1, LANE), pad(w3, LANE, nout), jnp.asarray(dl))


def _dft_matrix(n):
    f = np.arange(n, dtype=np.int64)[:, None]
    s = np.arange(n, dtype=np.int64)[None, :]
    ang = (2.0 * np.pi / (2 * n)) * ((f * s) % (2 * n)).astype(np.float64)
    cos = np.cos(ang)
    sin = np.sin(ang)
    sin[0, :] = np.where(np.arange(n) % 2 == 0, 1.0, -1.0)
    return cos.astype(np.float32), sin.astype(np.float32)


def _matmul_kernel(a_ref, b_ref, o_ref):
    o_ref[...] = _dot(a_ref[...], b_ref[...])


def _matmul(a, b, bm, bn):
    m, k = a.shape
    n = b.shape[1]
    return pl.pallas_call(
        _matmul_kernel,
        grid=(m // bm, n // bn),
        in_specs=[pl.BlockSpec((bm, k), lambda i, j: (i, 0)), pl.BlockSpec((k, bn), lambda i, j: (0, j))],
        out_specs=pl.BlockSpec((bm, bn), lambda i, j: (i, j)),
        out_shape=jax.ShapeDtypeStruct((m, n), F32),
        compiler_params=_cparams(2),
        name="filter_dft",
    )(a, b)


def _hyena_conv_kernel(v_ref, x1_ref, x2_ref, f_ref, ft_ref, p1_ref, p2_ref, p3_ref, fb_ref, o_ref,
                       zb_ref, y_ref, *, row0, n, fbs):
    o = pl.program_id(2)
    fb = pl.program_id(3)
    nfb = pl.num_programs(3)
    rows = slice(row0, row0 + n)

    @pl.when(jnp.logical_and(o == 0, fb == 0))
    def _():
        zb_ref[...] = v_ref[0, rows, :]

    @pl.when(fb == 0)
    def _():
        y_ref[...] = jnp.zeros_like(y_ref)

    zf = _dot(f_ref[0], zb_ref[...])
    re, im = zf[:fbs], zf[fbs:]
    p2 = p2_ref[0]
    a = re * p1_ref[0] + im * p2
    bv = im * p3_ref[0] - re * p2
    w = jnp.concatenate([a, bv], axis=0).astype(BF16)
    y_ref[...] += _dot(ft_ref[0], w)

    @pl.when(fb == nfb - 1)
    def _():
        y = y_ref[...] + zb_ref[...].astype(F32) * fb_ref[0]

        @pl.when(o == 0)
        def _():
            zb_ref[...] = (x1_ref[0, rows, :].astype(F32) * y).astype(BF16)

        @pl.when(o == 1)
        def _():
            if row0 > 0:
                o_ref[0, 0:row0, :] = jnp.zeros((row0, o_ref.shape[2]), o_ref.dtype)
            o_ref[0, rows, :] = (x2_ref[0, rows, :].astype(F32) * y).astype(BF16)


def _hyena_conv(u3, fmat, fmat_t, p1, p2, p3, fbias, *, row0, n, block_rows, prev=None):
    bsz, t, _ = u3.shape
    tc = 512
    nct = C_WIDTH // tc
    nfb = fmat.shape[0]
    fbs = fmat.shape[1] // 2
    u_spec = lambda which: pl.BlockSpec((1, block_rows, tc), lambda b, c, o, f: (b, 0, which * nct + c))
    p_spec = pl.BlockSpec((1, fbs, tc), lambda b, c, o, f: (o, f, c))
    in_specs = [u_spec(0), u_spec(1), u_spec(2),
                pl.BlockSpec((1, 2 * fbs, n), lambda b, c, o, f: (f, 0, 0)),
                pl.BlockSpec((1, n, 2 * fbs), lambda b, c, o, f: (f, 0, 0)),
                p_spec, p_spec, p_spec,
                pl.BlockSpec((1, 1, tc), lambda b, c, o, f: (o, 0, c))]
    args = [u3, u3, u3, fmat, fmat_t, p1, p2, p3, fbias]
    aliases = {}
    kern = functools.partial(_hyena_conv_kernel, row0=row0, n=n, fbs=fbs)
    if prev is not None:
        in_specs.append(pl.BlockSpec(memory_space=pl.ANY))
        args.append(prev)
        aliases = {len(args) - 1: 0}
        inner = kern
        kern = lambda *refs: inner(*refs[:9], *refs[10:])
    return pl.pallas_call(
        kern,
        grid=(bsz, nct, 2, nfb),
        in_specs=in_specs,
        out_specs=pl.BlockSpec((1, block_rows, tc), lambda b, c, o, f: (b, 0, c)),
        out_shape=jax.ShapeDtypeStruct((bsz, t, C_WIDTH), BF16),
        scratch_shapes=[pltpu.VMEM((n, tc), BF16), pltpu.VMEM((n, tc), F32)],
        input_output_aliases=aliases,
        compiler_params=_cparams(4),
        name="hyena_long_conv_n%d" % n,
    )(*args)


def _hyena_spectra(n, filt):
    cos, sin = _dft_matrix(n)
    fbs = min(n, 256)
    nfb = n // fbs
    full = jnp.asarray(np.concatenate([cos, sin], axis=0), BF16)
    hspec = _matmul(full, filt, min(2 * n, 512), 1024)
    blocks = np.concatenate([cos.reshape(nfb, fbs, n), sin.reshape(nfb, fbs, n)], axis=1)
    fmat = jnp.asarray(blocks, BF16)
    fmat_t = jnp.asarray(np.transpose(blocks, (0, 2, 1)), BF16)
    hf = hspec[:, :2 * C_WIDTH].reshape(2 * n, 2, C_WIDTH)
    hb = hspec[:, 2 * C_WIDTH:].reshape(2 * n, 2, C_WIDTH)
    scale = np.full((n, 1, 1), 2.0 / (2 * n), np.float32)
    scale[0] = 1.0 / (2 * n)
    first = (np.arange(n) == 0)[:, None, None]
    kre = (hf[:n] + hb[:n]) * scale
    kim = jnp.where(first, 0.0, (hb[n:] - hf[n:]) * scale)
    knyq = (hf[n:n + 1] + hb[n:n + 1]) * scale[0]
    p3 = jnp.where(first, knyq, kre)
    tr = lambda a: jnp.transpose(a, (1, 0, 2))
    return fmat, fmat_t, tr(kre), tr(kim), tr(p3)


def _rope_tables(seq, layout):
    rows = seq // GRID_W
    row = np.repeat(np.arange(rows, dtype=np.float32), GRID_W)
    col = np.tile(np.arange(GRID_W, dtype=np.float32), rows)
    per_axis = 32
    inv = (ROPE_BASE ** (-np.arange(0, per_axis, 2, dtype=np.float32) / per_axis)).astype(np.float32)
    ang = np.concatenate([row[:, None] * inv, col[:, None] * inv], axis=-1)
    cos = np.concatenate([np.ones((CTX, 32), np.float32), np.cos(ang)], axis=0)
    sin = np.concatenate([np.zeros((CTX, 32), np.float32), np.sin(ang)], axis=0)
    one, zero = np.ones_like(cos), np.zeros_like(cos)
    if layout == "mla":
        a = [cos, cos, one, one]
        b = [zero, sin, zero, zero]
        c = [-sin, zero, zero, zero]
    else:
        a = [cos, cos, cos, cos]
        b = [zero, sin, zero, sin]
        c = [-sin, zero, -sin, zero]
    return tuple(jnp.asarray(np.concatenate(p, axis=1), F32) for p in (a, b, c))


def _swa_head_perm():
    cols = []
    for j in range(B_KV_HEADS // 2):
        for g in range(B_GROUP):
            for hk in (2 * j, 2 * j + 1):
                h = hk * B_GROUP + g
                cols.extend(range(h * B_HDIM, (h + 1) * B_HDIM))
    return np.asarray(cols, np.int32)


def _mla_weights(w_in, w_q, w_kv):
    c1 = A_Q_RANK + A_KV_RANK
    zpad = jnp.zeros((D, LANE - A_ROPE), F32)
    win = jnp.concatenate([w_in[:, :c1 + A_ROPE], zpad, w_in[:, c1 + A_ROPE:]], axis=1)
    qscale = (A_NOPE + A_ROPE) ** -0.5 * math.log2(math.e)
    wq = w_q.reshape(A_Q_RANK, A_HEADS, A_NOPE + A_ROPE) * qscale
    wq = jnp.concatenate([wq, jnp.zeros((A_Q_RANK, A_HEADS, A_HEAD_PAD - A_NOPE - A_ROPE), F32)], axis=-1)
    wq = wq.reshape(A_Q_RANK, A_HEADS * A_HEAD_PAD)
    wkv = w_kv.reshape(A_KV_RANK, A_HEADS, A_NOPE + A_VDIM)
    wkv = jnp.concatenate([wkv[:, :, :A_NOPE].reshape(A_KV_RANK, -1), wkv[:, :, A_NOPE:].reshape(A_KV_RANK, -1)], axis=1)
    return win.astype(BF16), wq.astype(BF16), wkv.astype(BF16)


def kernel(x, c, ctx, c_ctx, w_mod, b_mod, g_pre, g_post, a_w_in, a_g_q, a_w_q, a_g_kv, a_w_kv, a_w_out, b_w_in, b_sink, b_w_out, c_w_in, c_conv_w, c_conv_b, c_f_w1, c_f_b1, c_f_freq, c_f_w2, c_f_b2, c_f_w3, c_filt_bias, c_w_out):
    bsz, seq, _ = x.shape
    assert ctx.shape[1] == CTX and seq % TM == 0 and seq % GRID_W == 0
    xs = jnp.concatenate([ctx, x], axis=1)

    pad_rows = (-(bsz + 1)) % 8
    cond = jnp.concatenate([c, c_ctx[None], jnp.zeros((pad_rows, D), F32)], axis=0)
    mod = _modulation(cond, w_mod, b_mod)

    tabs_mla = _rope_tables(seq, "mla")
    tabs_swa = _rope_tables(seq, "swa")

    for layer in range(DEPTH):
        kind, j = layer % 3, layer // 3
        mx = mod[layer, :bsz].reshape(bsz, 3, D)
        mc = jnp.broadcast_to(mod[layer, bsz].reshape(1, 3, D), (bsz, 3, D))
        modl = jnp.stack([mc, mx], axis=0)
        gpre = g_pre[layer][None]
        gpost = g_post[layer][None]
        if kind == 0:
            win, wq, wkv = _mla_weights(a_w_in[j], a_w_q[j], a_w_kv[j])
            q, k, v, gate = _mla_proj(xs, modl, gpre, win, a_g_q[j][None], wq, a_g_kv[j][None], wkv, tabs_mla)
            o = _mla_attn(q, k, v)
            wout = a_w_out[j].astype(BF16)
        elif kind == 1:
            perm = _swa_head_perm()
            qw, kw = B_HEADS * B_HDIM, B_KV_HEADS * B_HDIM
            w = b_w_in[j]
            win = jnp.concatenate([w[:, :qw][:, perm] * (B_HDIM ** -0.5 * math.log2(math.e)), w[:, qw:qw + 2 * kw],
                                   w[:, qw + 2 * kw:][:, perm]], axis=1).astype(BF16)
            q, k, v, gate = _swa_proj(xs, modl, gpre, win, tabs_swa)
            sink = (b_sink[j].astype(F32) * math.log2(math.e)).reshape(B_KV_HEADS, B_GROUP, 1, 1)
            sink_cols = jnp.broadcast_to(sink, (B_KV_HEADS, B_GROUP, B_BLOCK, 1)).reshape(B_KV_HEADS, B_GROUP * B_BLOCK, 1)
            o = _swa_attn(q, k, v, sink_cols)
            wout = b_w_out[j][perm, :].astype(BF16)
        else:
            u3, gate = _hyena_proj(xs, modl, gpre, c_w_in[j].astype(BF16), c_conv_w[j], c_conv_b[j][None])
            fargs = (c_f_w1[j], c_f_b1[j], c_f_freq[j], c_f_w2[j], c_f_b2[j], c_f_w3[j])
            fbias = c_filt_bias[j].reshape(2, 1, C_WIDTH)
            spec_x = _hyena_spectra(seq, _filters(seq, *fargs))
            o = _hyena_conv(u3, *spec_x, fbias, row0=CTX, n=seq, block_rows=CTX + seq)
            spec_c = _hyena_spectra(CTX, _filters(CTX, *fargs))
            o = _hyena_conv(u3, *spec_c, fbias, row0=0, n=CTX, block_rows=CTX, prev=o)
            wout = c_w_out[j].astype(BF16)
        xs = _out_proj(o, gate, wout, xs, modl, gpost)
    return xs[:, CTX:]
```

```python
import functools
import math

import numpy as np
import jax
import jax.numpy as jnp
from jax import lax
from jax.experimental import pallas as pl
from jax.experimental.pallas import tpu as pltpu

F32 = jnp.float32
BF16 = jnp.bfloat16

D = 1024
DEPTH = 4
GRID_W = 64
CTX = 256
NORM_EPS = 1e-6
ROPE_BASE = 10000.0
NEG_INF = -1e30

A_HEADS = 8
A_Q_RANK = 512
A_KV_RANK = 256
A_NOPE = 128
A_ROPE = 64
A_VDIM = 128
A_HEAD_PAD = 256

B_HEADS = 16
B_KV_HEADS = 4
B_GROUP = 4
B_HDIM = 64
B_WINDOW = 128
B_BLOCK = 128

C_WIDTH = 1024
C_BANDS = 16
C_EMB = 1 + 2 * C_BANDS
C_FFN = 64
C_MIN_DECAY = math.log(1e-2) / 1.5
C_MAX_DECAY = math.log(1e-2) / 0.3

LANE = 128
MXU_W = 256
TM = 256
HALO = 8
VMEM_LIMIT = 56 * 1024 * 1024


def _cparams(n_axes):
    return pltpu.CompilerParams(dimension_semantics=("arbitrary",) * n_axes,
                                vmem_limit_bytes=VMEM_LIMIT)


def _rms(x):
    return x * lax.rsqrt(jnp.mean(x * x, axis=-1, keepdims=True) + NORM_EPS)


def _silu(g):
    return g / (1.0 + jnp.exp(-g))


def _dot(a, b):
    return jnp.dot(a, b, preferred_element_type=F32)


def _dot_nt(a, b):
    return lax.dot_general(a, b, (((1,), (1,)), ((), ())), preferred_element_type=F32)


def _rope128(x, a, b, c):
    return x * a + pltpu.roll(x, 32, 1) * b + pltpu.roll(x, LANE - 32, 1) * c


def _modnorm(x, mod_ref, gpre_ref):
    m = mod_ref[0, 0]
    return _rms(x) * gpre_ref[...] * (1.0 + m[1:2]) + m[0:1]


def _mod_kernel(c_ref, w_ref, b_ref, o_ref):
    a = _silu(c_ref[...])
    o_ref[0] = _dot(a.astype(BF16), w_ref[0].astype(BF16)) + b_ref[0]


def _modulation(cond, w_mod, b_mod):
    rows = cond.shape[0]
    return pl.pallas_call(
        _mod_kernel,
        grid=(DEPTH, 3),
        in_specs=[pl.BlockSpec((rows, D), lambda l, j: (0, 0)),
                  pl.BlockSpec((1, D, D), lambda l, j: (l, 0, j)),
                  pl.BlockSpec((1, 1, D), lambda l, j: (l, 0, j))],
        out_specs=pl.BlockSpec((1, rows, D), lambda l, j: (l, 0, j)),
        out_shape=jax.ShapeDtypeStruct((DEPTH, rows, 3 * D), F32),
        compiler_params=_cparams(2),
        name="adaln_modulation",
    )(cond, w_mod, b_mod.reshape(DEPTH, 1, 3 * D))


def _tok_spec(width):
    return pl.BlockSpec((1, TM, width), lambda t, b: (b, t, 0))


def _mod_spec():
    return pl.BlockSpec((1, 1, 3, D), lambda t, b: (jnp.minimum(t, 1), b, 0, 0))


def _const_spec(shape):
    nd = len(shape)
    return pl.BlockSpec(shape, lambda t, b: (0,) * nd)


def _head_spec(width):
    return pl.BlockSpec((1, A_HEADS, TM, width), lambda t, b: (b, 0, t, 0))


def _table_spec():
    return pl.BlockSpec((TM, LANE), lambda t, b: (t, 0))


def _mla_proj_kernel(x_ref, mod_ref, gpre_ref, win_ref, gq_ref, wq_ref, gkv_ref, wkv_ref,
                     ra_ref, rb_ref, rc_ref, q_ref, k_ref, v_ref, gate_ref):
    h = _modnorm(x_ref[0], mod_ref, gpre_ref)
    p = _dot(h.astype(BF16), win_ref[...])
    c0, c1, c2 = A_Q_RANK, A_Q_RANK + A_KV_RANK, A_Q_RANK + A_KV_RANK + LANE
    gate_ref[0] = p[:, c2:].astype(BF16)
    q = _dot((_rms(p[:, :c0]) * gq_ref[...]).astype(BF16), wq_ref[...])
    kv = _dot((_rms(p[:, c0:c1]) * gkv_ref[...]).astype(BF16), wkv_ref[...])
    ra, rb, rc = ra_ref[...], rb_ref[...], rc_ref[...]
    kr = _rope128(p[:, c1:c2], ra, rb, rc).astype(BF16)
    for hd in range(A_HEADS):
        o = hd * A_HEAD_PAD
        q_ref[0, hd, :, 0:LANE] = q[:, o:o + LANE].astype(BF16)
        q_ref[0, hd, :, LANE:2 * LANE] = _rope128(q[:, o + LANE:o + 2 * LANE], ra, rb, rc).astype(BF16)
        k_ref[0, hd, :, 0:LANE] = kv[:, hd * A_NOPE:(hd + 1) * A_NOPE].astype(BF16)
        k_ref[0, hd, :, LANE:2 * LANE] = kr
        v_ref[0, hd] = kv[:, (A_HEADS + hd) * A_NOPE:(A_HEADS + hd) * A_NOPE + A_VDIM].astype(BF16)


def _mla_proj(xs, modl, gpre, win, gq, wq, gkv, wkv, tabs):
    bsz, t, _ = xs.shape
    return pl.pallas_call(
        _mla_proj_kernel,
        grid=(t // TM, bsz),
        in_specs=[_tok_spec(D), _mod_spec(), _const_spec((1, D)), _const_spec(win.shape),
                  _const_spec((1, A_Q_RANK)), _const_spec(wq.shape),
                  _const_spec((1, A_KV_RANK)), _const_spec(wkv.shape),
                  _table_spec(), _table_spec(), _table_spec()],
        out_specs=[_head_spec(A_HEAD_PAD), _head_spec(A_HEAD_PAD), _head_spec(A_VDIM), _tok_spec(A_HEADS * A_VDIM)],
        out_shape=[jax.ShapeDtypeStruct((bsz, A_HEADS, t, A_HEAD_PAD), BF16),
                   jax.ShapeDtypeStruct((bsz, A_HEADS, t, A_HEAD_PAD), BF16),
                   jax.ShapeDtypeStruct((bsz, A_HEADS, t, A_VDIM), BF16),
                   jax.ShapeDtypeStruct((bsz, t, A_HEADS * A_VDIM), BF16)],
        compiler_params=_cparams(2),
        name="mla_proj",
    )(xs, modl, gpre, win, gq, wq, gkv, wkv, *tabs)


def _mla_attn_kernel(q_ref, k_ref, v_ref, o_ref):
    def attend(nk):
        for hd in range(A_HEADS):
            s = _dot_nt(q_ref[0, hd], k_ref[0, hd, :nk, :])
            m = jnp.max(s, axis=-1, keepdims=True)
            p = jnp.exp2(s - m)
            l = jnp.sum(p, axis=-1, keepdims=True)
            o = _dot(p.astype(BF16), v_ref[0, hd, :nk, :]) / l
            o_ref[0, :, hd * A_VDIM:(hd + 1) * A_VDIM] = o.astype(BF16)

    t_all = k_ref.shape[2]
    is_ctx = pl.program_id(1) == 0
    pl.when(is_ctx)(lambda: attend(CTX))
    pl.when(jnp.logical_not(is_ctx))(lambda: attend(t_all))


def _mla_attn(q, k, v):
    bsz, nh, t, _ = q.shape
    return pl.pallas_call(
        _mla_attn_kernel,
        grid=(bsz, t // TM),
        in_specs=[pl.BlockSpec((1, nh, TM, A_HEAD_PAD), lambda b, i: (b, 0, i, 0)),
                  pl.BlockSpec((1, nh, t, A_HEAD_PAD), lambda b, i: (b, 0, 0, 0)),
                  pl.BlockSpec((1, nh, t, A_VDIM), lambda b, i: (b, 0, 0, 0))],
        out_specs=pl.BlockSpec((1, TM, nh * A_VDIM), lambda b, i: (b, i, 0)),
        out_shape=jax.ShapeDtypeStruct((bsz, t, nh * A_VDIM), BF16),
        compiler_params=_cparams(2),
        name="mla_attention",
    )(q, k, v)


def _out_kernel(o_ref, gate_ref, wout_ref, x_ref, mod_ref, gpost_ref, xo_ref):
    a = o_ref[0].astype(F32) * _silu(gate_ref[0].astype(F32))
    y = _dot(a.astype(BF16), wout_ref[...])
    xo_ref[0] = x_ref[0] + mod_ref[0, 0][2:3] * (_rms(y) * gpost_ref[...])


def _out_proj(o, gate, wout, xs, modl, gpost):
    bsz, t, _ = xs.shape
    w = o.shape[-1]
    return pl.pallas_call(
        _out_kernel,
        grid=(t // TM, bsz),
        in_specs=[_tok_spec(w), _tok_spec(w), _const_spec(wout.shape), _tok_spec(D), _mod_spec(),
                  _const_spec((1, D))],
        out_specs=_tok_spec(D),
        out_shape=jax.ShapeDtypeStruct(xs.shape, F32),
        input_output_aliases={3: 0},
        compiler_params=_cparams(2),
        name="out_proj_residual",
    )(o, gate, wout, xs, modl, gpost)


def _swa_proj_kernel(x_ref, mod_ref, gpre_ref, win_ref, ra_ref, rb_ref, rc_ref,
                     q_ref, k_ref, v_ref, gate_ref):
    h = _modnorm(x_ref[0], mod_ref, gpre_ref)
    p = _dot(h.astype(BF16), win_ref[...])
    ra, rb, rc = ra_ref[...], rb_ref[...], rc_ref[...]
    qw, kw = B_HEADS * B_HDIM, B_KV_HEADS * B_HDIM
    lane = lax.broadcasted_iota(jnp.int32, (1, LANE), 1)
    lo = lane < B_HDIM
    ones_col = jnp.broadcast_to(jnp.where(lane == 0, 1.0, 0.0), (TM, LANE)).astype(BF16)
    for j in range(B_KV_HEADS // 2):
        for g in range(B_GROUP):
            s = j * B_GROUP + g
            r = _rope128(p[:, s * LANE:(s + 1) * LANE], ra, rb, rc)
            for half in range(2):
                d = ((2 * j + half) * B_GROUP + g) * LANE
                keep = lo if half == 0 else jnp.logical_not(lo)
                q_ref[0, :, d:d + LANE] = jnp.where(keep, r, 0.0).astype(BF16)
        o = qw + j * LANE
        k_ref[0, :, j * LANE:(j + 1) * LANE] = _rope128(p[:, o:o + LANE], ra, rb, rc).astype(BF16)
        v_ref[0, :, 2 * j * LANE:(2 * j + 1) * LANE] = p[:, o + kw:o + kw + LANE].astype(BF16)
        v_ref[0, :, (2 * j + 1) * LANE:(2 * j + 2) * LANE] = ones_col
    gate_ref[0] = p[:, qw + 2 * kw:].astype(BF16)


def _swa_proj(xs, modl, gpre, win, tabs):
    bsz, t, _ = xs.shape
    qw, kw = B_HEADS * B_HDIM, B_KV_HEADS * B_HDIM
    return pl.pallas_call(
        _swa_proj_kernel,
        grid=(t // TM, bsz),
        in_specs=[_tok_spec(D), _mod_spec(), _const_spec((1, D)), _const_spec(win.shape),
                  _table_spec(), _table_spec(), _table_spec()],
        out_specs=[_tok_spec(2 * qw), _tok_spec(kw), _tok_spec(2 * kw), _tok_spec(qw)],
        out_shape=[jax.ShapeDtypeStruct((bsz, t, 2 * qw), BF16),
                   jax.ShapeDtypeStruct((bsz, t, kw), BF16),
                   jax.ShapeDtypeStruct((bsz, t, 2 * kw), BF16),
                   jax.ShapeDtypeStruct((bsz, t, qw), BF16)],
        compiler_params=_cparams(2),
        name="swa_proj",
    )(xs, modl, gpre, win, *tabs)


def _swa_attn_kernel(q_ref, k_ref, v_ref, sink_ref, o_ref):
    i = pl.program_id(1)
    n_lat = k_ref.shape[1] - CTX
    band = 3 * B_BLOCK
    rows = B_GROUP * B_BLOCK
    lo = lax.broadcasted_iota(jnp.int32, (1, LANE), 1) < B_HDIM

    def run(with_band):
        if with_band:
            li = i - CTX // B_BLOCK
            start = jnp.clip((li - 1) * B_BLOCK, 0, n_lat - band)
            off = li * B_BLOCK - start
            d0 = (lax.broadcasted_iota(jnp.int32, (B_BLOCK, band), 0)
                  - lax.broadcasted_iota(jnp.int32, (B_BLOCK, band), 1))
            mask = jnp.abs(d0 + off) <= B_WINDOW
            kstart = pl.multiple_of(CTX + start, B_BLOCK)
        for j in range(B_KV_HEADS // 2):
            kcols = slice(j * LANE, (j + 1) * LANE)
            vcols = slice(2 * j * LANE, (2 * j + 2) * LANE)
            kk = k_ref[0, 0:CTX, kcols]
            vv = v_ref[0, 0:CTX, vcols]
            if with_band:
                kk = jnp.concatenate([kk, k_ref[0, pl.ds(kstart, band), kcols]], axis=0)
                vv = jnp.concatenate([vv, v_ref[0, pl.ds(kstart, band), vcols]], axis=0)
            outs = []
            for half in range(2):
                hk = 2 * j + half
                qs = jnp.concatenate(
                    [q_ref[0, :, (hk * B_GROUP + g) * LANE:(hk * B_GROUP + g + 1) * LANE] for g in range(B_GROUP)],
                    axis=0)
                sk = sink_ref[hk]
                s = _dot_nt(qs, kk)
                s_c = s[:, :CTX]
                m = jnp.maximum(jnp.max(s_c, axis=-1, keepdims=True), sk)
                if with_band:
                    s_b = jnp.concatenate(
                        [jnp.where(mask, s[g * B_BLOCK:(g + 1) * B_BLOCK, CTX:], NEG_INF) for g in range(B_GROUP)],
                        axis=0)
                    m = jnp.maximum(m, jnp.max(s_b, axis=-1, keepdims=True))
                    p = jnp.concatenate([jnp.exp2(s_c - m), jnp.exp2(s_b - m)], axis=1)
                else:
                    p = jnp.exp2(s_c - m)
                pv = _dot(p.astype(BF16), vv)
                l = pv[:, LANE:LANE + 1] + jnp.exp2(sk - m)
                outs.append(pv[:, :LANE] / l)
            comb = jnp.where(lo, outs[0], outs[1])
            for g in range(B_GROUP):
                s_out = j * B_GROUP + g
                o_ref[0, :, s_out * LANE:(s_out + 1) * LANE] = comb[g * B_BLOCK:(g + 1) * B_BLOCK].astype(BF16)

    is_ctx = i < CTX // B_BLOCK
    pl.when(is_ctx)(lambda: run(False))
    pl.when(jnp.logical_not(is_ctx))(lambda: run(True))


def _swa_attn(q, k, v, sink_cols):
    bsz, t, qw2 = q.shape
    kw = k.shape[-1]
    return pl.pallas_call(
        _swa_attn_kernel,
        grid=(bsz, t // B_BLOCK),
        in_specs=[pl.BlockSpec((1, B_BLOCK, qw2), lambda b, i: (b, i, 0)),
                  pl.BlockSpec((1, t, kw), lambda b, i: (b, 0, 0)),
                  pl.BlockSpec((1, t, 2 * kw), lambda b, i: (b, 0, 0)),
                  pl.BlockSpec(sink_cols.shape, lambda b, i: (0, 0, 0))],
        out_specs=pl.BlockSpec((1, B_BLOCK, qw2 // 2), lambda b, i: (b, i, 0)),
        out_shape=jax.ShapeDtypeStruct((bsz, t, qw2 // 2), BF16),
        compiler_params=_cparams(2),
        name="swa_attention",
    )(q, k, v, sink_cols)


HP = TM // 2


def _hyena_proj_kernel(xp_ref, x_ref, xn_ref, mod_ref, gpre_ref, win_ref, cw_ref, cb_ref, perm_ref,
                       ue_ref, uo_ref, ge_ref, go_ref):
    t = pl.program_id(0)
    nt = pl.num_programs(0)
    hh = _dot(perm_ref[...], _modnorm(x_ref[0], mod_ref, gpre_ref).astype(BF16))
    h = jnp.concatenate([_modnorm(xp_ref[0], mod_ref, gpre_ref), hh, _modnorm(xn_ref[0], mod_ref, gpre_ref)], axis=0)
    p = _dot(h.astype(BF16), win_ref[...])
    cwid = 3 * C_WIDTH
    u = p[:, :cwid]
    o0, e0 = HALO, HALO + HP
    go_ref[0] = p[o0:o0 + HP, cwid:].astype(BF16)
    ge_ref[0] = p[e0:e0 + HP, cwid:].astype(BF16)
    po, pe = u[o0:o0 + HP], u[e0:e0 + HP]
    po_prev = pltpu.roll(u[0:e0], 1, 0)[o0:e0]
    pe_next = pltpu.roll(u[e0:], HP + HALO - 1, 0)[0:HP]
    r = lax.broadcasted_iota(jnp.int32, (HP, 1), 0)
    po_prev = jnp.where(jnp.logical_and(r == 0, t <= 1), 0.0, po_prev)
    pe_next = jnp.where(jnp.logical_and(r == HP - 1, jnp.logical_or(t == 0, t == nt - 1)), 0.0, pe_next)
    cw = cw_ref[...]
    cb = cb_ref[...]
    ue_ref[0] = (cb + po_prev * cw[0:1] + pe * cw[1:2] + po * cw[2:3]).astype(BF16)
    uo_ref[0] = (cb + pe * cw[0:1] + po * cw[1:2] + pe_next * cw[2:3]).astype(BF16)


def _half_spec(width):
    return pl.BlockSpec((1, HP, width), lambda t, b: (b, t, 0))


def _hyena_proj(xs, modl, gpre, win, conv_w, conv_b):
    bsz, t, _ = xs.shape
    nt = t // TM
    per = TM // HALO
    last = t // HALO - 1
    half = lambda w: jax.ShapeDtypeStruct((bsz, t // 2, w), BF16)
    return pl.pallas_call(
        _hyena_proj_kernel,
        grid=(nt, bsz),
        in_specs=[pl.BlockSpec((1, HALO, D), lambda i, b: (b, jnp.maximum(i * per - 1, 0), 0)),
                  _tok_spec(D),
                  pl.BlockSpec((1, HALO, D), lambda i, b: (b, jnp.minimum((i + 1) * per, last), 0)),
                  _mod_spec(), _const_spec((1, D)), _const_spec(win.shape),
                  _const_spec(conv_w.shape), _const_spec(conv_b.shape), _const_spec((TM, TM))],
        out_specs=[_half_spec(3 * C_WIDTH), _half_spec(3 * C_WIDTH), _half_spec(C_WIDTH), _half_spec(C_WIDTH)],
        out_shape=[half(3 * C_WIDTH), half(3 * C_WIDTH), half(C_WIDTH), half(C_WIDTH)],
        compiler_params=_cparams(2),
        name="hyena_proj",
    )(xs, xs, xs, modl, gpre, win, conv_w, conv_b, _parity_perm(odd_first=True))


def _parity_perm(odd_first):
    r = np.arange(HP)
    src = np.concatenate([2 * r + 1, 2 * r]) if odd_first else np.concatenate([2 * r, 2 * r + 1])
    m = np.zeros((TM, TM), np.float32)
    m[np.arange(TM), src] = 1.0
    return jnp.asarray(m, BF16)


def _hyena_out_kernel(oe_ref, oo_ref, ge_ref, go_ref, wout_ref, x_ref, mod_ref, gpost_ref, perm_ref, xo_ref):
    a = jnp.concatenate([oe_ref[0].astype(F32) * _silu(ge_ref[0].astype(F32)),
                         oo_ref[0].astype(F32) * _silu(go_ref[0].astype(F32))], axis=0)
    a = _dot(perm_ref[...], a.astype(BF16)).astype(BF16)
    y = _dot(a, wout_ref[...])
    xo_ref[0] = x_ref[0] + mod_ref[0, 0][2:3] * (_rms(y) * gpost_ref[...])


def _hyena_out_proj(oe, oo, ge, go, wout, xs, modl, gpost):
    bsz, t, _ = xs.shape
    w = oe.shape[-1]
    return pl.pallas_call(
        _hyena_out_kernel,
        grid=(t // TM, bsz),
        in_specs=[_half_spec(w), _half_spec(w), _half_spec(w), _half_spec(w), _const_spec(wout.shape),
                  _tok_spec(D), _mod_spec(), _const_spec((1, D)), _const_spec((TM, TM))],
        out_specs=_tok_spec(D),
        out_shape=jax.ShapeDtypeStruct(xs.shape, F32),
        input_output_aliases={5: 0},
        compiler_params=_cparams(2),
        name="hyena_out_proj_residual",
    )(oe, oo, ge, go, wout, xs, modl, gpost, _parity_perm(odd_first=False).T)


def _filter_kernel(z_ref, w1_ref, b1_ref, fr_ref, w2_ref, b2_ref, w3_ref, dl_ref, o_ref):
    hp = lax.Precision.HIGHEST
    z = z_ref[...]
    fr = fr_ref[...]
    h = jnp.sin(fr * (jnp.dot(z, w1_ref[...], precision=hp, preferred_element_type=F32) + b1_ref[...]))
    h = jnp.sin(fr * (jnp.dot(h, w2_ref[...], precision=hp, preferred_element_type=F32) + b2_ref[...]))
    h = jnp.dot(h, w3_ref[...], precision=hp, preferred_element_type=F32)
    o_ref[...] = (h * jnp.exp(-z[:, 0:1] * dl_ref[...])).astype(o_ref.dtype)


def _filters(n, w1, b1, fr, w2, b2, w3):
    t = np.linspace(0.0, 1.0, n, dtype=np.float32)[:, None]
    w = ((2.0 * math.pi / n) * np.arange(n, dtype=np.float32))[:, None].astype(np.float32)
    bands = np.linspace(1e-4, C_BANDS - 1, C_BANDS, dtype=np.float32)[None, :]
    z = np.zeros((n, LANE), np.float32)
    z[:, 0:1] = t
    z[:, 1:1 + C_BANDS] = np.cos(bands * w)
    z[:, 1 + C_BANDS:C_EMB] = -np.sin(bands * w)
    deltas = np.abs(np.linspace(C_MIN_DECAY, C_MAX_DECAY, C_WIDTH, dtype=np.float32))
    dl = np.tile(deltas, 4)[None, :]

    def pad(a, r, c):
        return jnp.zeros((r, c), F32).at[:a.shape[0], :a.shape[1]].set(a)

    tn = min(n, TM)
    nout = 4 * C_WIDTH
    cs = lambda shape: pl.BlockSpec(shape, lambda i: (0, 0))
    return pl.pallas_call(
        _filter_kernel,
        grid=(n // tn,),
        in_specs=[pl.BlockSpec((tn, LANE), lambda i: (i, 0)), cs((LANE, LANE)), cs((1, LANE)), cs((1, LANE)),
                  cs((LANE, LANE)), cs((1, LANE)), cs((LANE, nout)), cs((1, nout))],
        out_specs=pl.BlockSpec((tn, nout), lambda i: (i, 0)),
        out_shape=jax.ShapeDtypeStruct((n, nout), BF16),
        compiler_params=_cparams(1),
        name="hyena_filter_mlp",
    )(jnp.asarray(z), pad(w1, LANE, LANE), pad(b1[None], 1, LANE), pad(fr[None], 1, LANE),
      pad(w2, LANE, LANE), pad(b2[None], 1, LANE), pad(w3, LANE, nout), jnp.asarray(dl))


def _dft_matrix(n):
    f = np.arange(n, dtype=np.int64)[:, None]
    s = np.arange(n, dtype=np.int64)[None, :]
    ang = (2.0 * np.pi / (2 * n)) * ((f * s) % (2 * n)).astype(np.float64)
    cos = np.cos(ang)
    sin = np.sin(ang)
    sin[0, :] = np.where(np.arange(n) % 2 == 0, 1.0, -1.0)
    return cos.astype(np.float32), sin.astype(np.float32)


def _matmul_kernel(a_ref, b_ref, o_ref):
    o_ref[...] = _dot(a_ref[...], b_ref[...])


def _matmul(a, b, bm, bn):
    m, k = a.shape
    n = b.shape[1]
    return pl.pallas_call(
        _matmul_kernel,
        grid=(m // bm, n // bn),
        in_specs=[pl.BlockSpec((bm, k), lambda i, j: (i, 0)), pl.BlockSpec((k, bn), lambda i, j: (0, j))],
        out_specs=pl.BlockSpec((bm, bn), lambda i, j: (i, j)),
        out_shape=jax.ShapeDtypeStruct((m, n), F32),
        compiler_params=_cparams(2),
        name="filter_dft",
    )(a, b)


def _hyena_conv_kernel(ve_ref, x1e_ref, x2e_ref, vo_ref, x1o_ref, x2o_ref, me_ref, mo_ref, met_ref, mot_ref,
                       kr1_ref, ki1_ref, kr2_ref, ki2_ref, kh_ref, fb_ref, oe_ref, oo_ref,
                       ze_ref, zo_ref, ye_ref, yo_ref, *, row0, h, fbs):
    o = pl.program_id(2)
    fb = pl.program_id(3)
    nfb = pl.num_programs(3)
    rows = slice(row0, row0 + h)

    @pl.when(jnp.logical_and(o == 0, fb == 0))
    def _():
        ze_ref[...] = ve_ref[0, rows, :]
        zo_ref[...] = vo_ref[0, rows, :]

    @pl.when(fb == 0)
    def _():
        ye_ref[...] = jnp.zeros_like(ye_ref)
        yo_ref[...] = jnp.zeros_like(yo_ref)

    first = jnp.logical_and(lax.broadcasted_iota(jnp.int32, (8, 1), 0) == 0, fb == 0)
    for cs in (slice(0, MXU_W), slice(MXU_W, 2 * MXU_W)):
        ge = _dot(me_ref[0], ze_ref[:, cs])
        go = _dot(mo_ref[0], zo_ref[:, cs])
        ce, se, co, so = ge[:fbs], ge[fbs:], go[:fbs], go[fbs:]
        xr1, xs1, xr2, xs2 = ce + co, se + so, ce - co, so - se
        kr1, ki1, kr2, ki2 = kr1_ref[0, :, cs], ki1_ref[0, :, cs], kr2_ref[0, :, cs], ki2_ref[0, :, cs]
        a1, b1 = xr1 * kr1 + xs1 * ki1, xs1 * kr1 - xr1 * ki1
        a2, b2 = xr2 * kr2 + xs2 * ki2, xs2 * kr2 - xr2 * ki2
        bm, bp = b1 - b2, b1 + b2
        krh, kih = kh_ref[0, 0:1, cs], kh_ref[0, 1:2, cs]
        se8, so8 = se[0:8], so[0:8]
        bm = jnp.concatenate([jnp.where(first, se8 * krh + so8 * kih, bm[0:8]), bm[8:]], axis=0)
        bp = jnp.concatenate([jnp.where(first, so8 * krh - se8 * kih, bp[0:8]), bp[8:]], axis=0)
        we = jnp.concatenate([a1 + a2, bm], axis=0).astype(BF16)
        wo = jnp.concatenate([a1 - a2, bp], axis=0).astype(BF16)
        ye_ref[:, cs] += _dot(met_ref[0], we)
        yo_ref[:, cs] += _dot(mot_ref[0], wo)

    @pl.when(fb == nfb - 1)
    def _():
        fe = ye_ref[...] + ze_ref[...].astype(F32) * fb_ref[0]
        fo = yo_ref[...] + zo_ref[...].astype(F32) * fb_ref[0]

        @pl.when(o == 0)
        def _():
            ze_ref[...] = (x1e_ref[0, rows, :].astype(F32) * fe).astype(BF16)
            zo_ref[...] = (x1o_ref[0, rows, :].astype(F32) * fo).astype(BF16)

        @pl.when(o == 1)
        def _():
            if row0 > 0:
                oe_ref[0, 0:row0, :] = jnp.zeros((row0, oe_ref.shape[2]), oe_ref.dtype)
                oo_ref[0, 0:row0, :] = jnp.zeros((row0, oo_ref.shape[2]), oo_ref.dtype)
            oe_ref[0, rows, :] = (x2e_ref[0, rows, :].astype(F32) * fe).astype(BF16)
            oo_ref[0, rows, :] = (x2o_ref[0, rows, :].astype(F32) * fo).astype(BF16)


def _hyena_conv_aliased_kernel(*refs, **kw):
    _hyena_conv_kernel(*refs[:16], *refs[18:], **kw)


def _hyena_conv(ue, uo, mats, planes, fbias, *, row0, h, block_rows, prev=None):
    bsz, t2, _ = ue.shape
    tc = 2 * MXU_W
    nct = C_WIDTH // tc
    me, mo, met, mot = mats
    nfb = me.shape[0]
    fbs = me.shape[1] // 2
    u_spec = lambda which: pl.BlockSpec((1, block_rows, tc), lambda b, c, o, f: (b, 0, which * nct + c))
    m_spec = pl.BlockSpec((1, 2 * fbs, h), lambda b, c, o, f: (f, 0, 0))
    mt_spec = pl.BlockSpec((1, h, 2 * fbs), lambda b, c, o, f: (f, 0, 0))
    p_spec = pl.BlockSpec((1, fbs, tc), lambda b, c, o, f: (o, f, c))
    in_specs = [u_spec(0), u_spec(1), u_spec(2), u_spec(0), u_spec(1), u_spec(2),
                m_spec, m_spec, mt_spec, mt_spec, p_spec, p_spec, p_spec, p_spec,
                pl.BlockSpec((1, 2, tc), lambda b, c, o, f: (o, 0, c)),
                pl.BlockSpec((1, 1, tc), lambda b, c, o, f: (o, 0, c))]
    args = [ue, ue, ue, uo, uo, uo, me, mo, met, mot, *planes, fbias]
    aliases = {}
    kern = functools.partial(_hyena_conv_kernel, row0=row0, h=h, fbs=fbs)
    if prev is not None:
        in_specs += [pl.BlockSpec(memory_space=pl.ANY)] * 2
        args += list(prev)
        aliases = {16: 0, 17: 1}
        kern = functools.partial(_hyena_conv_aliased_kernel, row0=row0, h=h, fbs=fbs)
    o_spec = pl.BlockSpec((1, block_rows, tc), lambda b, c, o, f: (b, 0, c))
    return pl.pallas_call(
        kern,
        grid=(bsz, nct, 2, nfb),
        in_specs=in_specs,
        out_specs=[o_spec, o_spec],
        out_shape=[jax.ShapeDtypeStruct((bsz, t2, C_WIDTH), BF16)] * 2,
        scratch_shapes=[pltpu.VMEM((h, tc), BF16), pltpu.VMEM((h, tc), BF16),
                        pltpu.VMEM((h, tc), F32), pltpu.VMEM((h, tc), F32)],
        input_output_aliases=aliases,
        compiler_params=_cparams(4),
        name="hyena_long_conv_h%d" % h,
    )(*args)


def _radix2_matrices(n):
    h = n // 2
    fbs = min(h, 256)
    nfb = h // fbs
    f = np.arange(h, dtype=np.int64)[:, None]
    r = np.arange(h, dtype=np.int64)[None, :]
    alt = np.where(np.arange(h) % 2 == 0, 1.0, -1.0)
    out = []
    for pos in (2 * r, 2 * r + 1):
        ang = (2.0 * np.pi / (2 * n)) * ((f * pos) % (2 * n)).astype(np.float64)
        cos, sin = np.cos(ang), np.sin(ang)
        sin[0, :] = alt
        blocks = np.concatenate([cos.reshape(nfb, fbs, h), sin.reshape(nfb, fbs, h)], axis=1)
        out.append(blocks.astype(np.float32))
    me, mo = out
    tr = lambda a: np.ascontiguousarray(np.transpose(a, (0, 2, 1)))
    return tuple(jnp.asarray(a, BF16) for a in (me, mo, tr(me), tr(mo)))


def _hyena_spectra(n, filt):
    h = n // 2
    cos, sin = _dft_matrix(n)
    full = jnp.asarray(np.concatenate([cos, sin], axis=0), BF16)
    hspec = _matmul(full, filt, min(2 * n, 512), 1024)
    hf = hspec[:, :2 * C_WIDTH].reshape(2 * n, 2, C_WIDTH)
    hb = hspec[:, 2 * C_WIDTH:].reshape(2 * n, 2, C_WIDTH)
    scale = np.full((n, 1, 1), 2.0 / (2 * n), np.float32)
    scale[0] = 1.0 / (2 * n)
    first = (np.arange(n) == 0)[:, None, None]
    kre = (hf[:n] + hb[:n]) * scale
    kim = jnp.where(first, 0.0, (hb[n:] - hf[n:]) * scale)
    knyq = (hf[n:n + 1] + hb[n:n + 1]) * scale[0]
    kr2 = jnp.concatenate([knyq, kre[:h:-1]], axis=0)
    ki2 = jnp.concatenate([jnp.zeros_like(knyq), kim[:h:-1]], axis=0)
    tr = lambda a: jnp.transpose(a, (1, 0, 2))
    kh = jnp.stack([kre[h], kim[h]], axis=1)
    return (tr(kre[:h]), tr(kim[:h]), tr(kr2), tr(ki2), kh)


def _rope_tables(seq, layout):
    rows = seq // GRID_W
    row = np.repeat(np.arange(rows, dtype=np.float32), GRID_W)
    col = np.tile(np.arange(GRID_W, dtype=np.float32), rows)
    per_axis = 32
    inv = (ROPE_BASE ** (-np.arange(0, per_axis, 2, dtype=np.float32) / per_axis)).astype(np.float32)
    ang = np.concatenate([row[:, None] * inv, col[:, None] * inv], axis=-1)
    cos = np.concatenate([np.ones((CTX, 32), np.float32), np.cos(ang)], axis=0)
    sin = np.concatenate([np.zeros((CTX, 32), np.float32), np.sin(ang)], axis=0)
    one, zero = np.ones_like(cos), np.zeros_like(cos)
    if layout == "mla":
        a = [cos, cos, one, one]
        b = [zero, sin, zero, zero]
        c = [-sin, zero, zero, zero]
    else:
        a = [cos, cos, cos, cos]
        b = [zero, sin, zero, sin]
        c = [-sin, zero, -sin, zero]
    return tuple(jnp.asarray(np.concatenate(p, axis=1), F32) for p in (a, b, c))


def _swa_head_perm():
    cols = []
    for j in range(B_KV_HEADS // 2):
        for g in range(B_GROUP):
            for hk in (2 * j, 2 * j + 1):
                h = hk * B_GROUP + g
                cols.extend(range(h * B_HDIM, (h + 1) * B_HDIM))
    return np.asarray(cols, np.int32)


def _mla_weights(w_in, w_q, w_kv):
    c1 = A_Q_RANK + A_KV_RANK
    zpad = jnp.zeros((D, LANE - A_ROPE), F32)
    win = jnp.concatenate([w_in[:, :c1 + A_ROPE], zpad, w_in[:, c1 + A_ROPE:]], axis=1)
    qscale = (A_NOPE + A_ROPE) ** -0.5 * math.log2(math.e)
    wq = w_q.reshape(A_Q_RANK, A_HEADS, A_NOPE + A_ROPE) * qscale
    wq = jnp.concatenate([wq, jnp.zeros((A_Q_RANK, A_HEADS, A_HEAD_PAD - A_NOPE - A_ROPE), F32)], axis=-1)
    wq = wq.reshape(A_Q_RANK, A_HEADS * A_HEAD_PAD)
    wkv = w_kv.reshape(A_KV_RANK, A_HEADS, A_NOPE + A_VDIM)
    wkv = jnp.concatenate([wkv[:, :, :A_NOPE].reshape(A_KV_RANK, -1), wkv[:, :, A_NOPE:].reshape(A_KV_RANK, -1)], axis=1)
    return win.astype(BF16), wq.astype(BF16), wkv.astype(BF16)


def kernel(x, c, ctx, c_ctx, w_mod, b_mod, g_pre, g_post, a_w_in, a_g_q, a_w_q, a_g_kv, a_w_kv, a_w_out, b_w_in, b_sink, b_w_out, c_w_in, c_conv_w, c_conv_b, c_f_w1, c_f_b1, c_f_freq, c_f_w2, c_f_b2, c_f_w3, c_filt_bias, c_w_out):
    bsz, seq, _ = x.shape
    assert ctx.shape[1] == CTX and seq % TM == 0 and seq % GRID_W == 0
    xs = jnp.concatenate([ctx, x], axis=1)

    pad_rows = (-(bsz + 1)) % 8
    cond = jnp.concatenate([c, c_ctx[None], jnp.zeros((pad_rows, D), F32)], axis=0)
    mod = _modulation(cond, w_mod, b_mod)

    tabs_mla = _rope_tables(seq, "mla")
    tabs_swa = _rope_tables(seq, "swa")

    for layer in range(DEPTH):
        kind, j = layer % 3, layer // 3
        mx = mod[layer, :bsz].reshape(bsz, 3, D)
        mc = jnp.broadcast_to(mod[layer, bsz].reshape(1, 3, D), (bsz, 3, D))
        modl = jnp.stack([mc, mx], axis=0)
        gpre = g_pre[layer][None]
        gpost = g_post[layer][None]
        if kind == 0:
            win, wq, wkv = _mla_weights(a_w_in[j], a_w_q[j], a_w_kv[j])
            q, k, v, gate = _mla_proj(xs, modl, gpre, win, a_g_q[j][None], wq, a_g_kv[j][None], wkv, tabs_mla)
            o = _mla_attn(q, k, v)
            wout = a_w_out[j].astype(BF16)
        elif kind == 1:
            perm = _swa_head_perm()
            qw, kw = B_HEADS * B_HDIM, B_KV_HEADS * B_HDIM
            w = b_w_in[j]
            win = jnp.concatenate([w[:, :qw][:, perm] * (B_HDIM ** -0.5 * math.log2(math.e)), w[:, qw:qw + 2 * kw],
                                   w[:, qw + 2 * kw:][:, perm]], axis=1).astype(BF16)
            q, k, v, gate = _swa_proj(xs, modl, gpre, win, tabs_swa)
            sink = (b_sink[j].astype(F32) * math.log2(math.e)).reshape(B_KV_HEADS, B_GROUP, 1, 1)
            sink_cols = jnp.broadcast_to(sink, (B_KV_HEADS, B_GROUP, B_BLOCK, 1)).reshape(B_KV_HEADS, B_GROUP * B_BLOCK, 1)
            o = _swa_attn(q, k, v, sink_cols)
            wout = b_w_out[j][perm, :].astype(BF16)
        else:
            ue, uo, ge, go = _hyena_proj(xs, modl, gpre, c_w_in[j].astype(BF16), c_conv_w[j], c_conv_b[j][None])
            fargs = (c_f_w1[j], c_f_b1[j], c_f_freq[j], c_f_w2[j], c_f_b2[j], c_f_w3[j])
            fbias = c_filt_bias[j].reshape(2, 1, C_WIDTH)
            oeo = _hyena_conv(ue, uo, _radix2_matrices(seq), _hyena_spectra(seq, _filters(seq, *fargs)), fbias,
                              row0=CTX // 2, h=seq // 2, block_rows=(CTX + seq) // 2)
            oe, oo = _hyena_conv(ue, uo, _radix2_matrices(CTX), _hyena_spectra(CTX, _filters(CTX, *fargs)), fbias,
                                 row0=0, h=CTX // 2, block_rows=CTX // 2, prev=oeo)
            xs = _hyena_out_proj(oe, oo, ge, go, c_w_out[j].astype(BF16), xs, modl, gpost)
            continue
        xs = _out_proj(o, gate, wout, xs, modl, gpost)
    return xs[:, CTX:]
```

```python
import functools
import math

import numpy as np
import jax
import jax.numpy as jnp
from jax import lax
from jax.experimental import pallas as pl
from jax.experimental.pallas import tpu as pltpu

F32 = jnp.float32
BF16 = jnp.bfloat16

D = 1024
DEPTH = 4
GRID_W = 64
CTX = 256
NORM_EPS = 1e-6
ROPE_BASE = 10000.0
NEG_INF = -1e30

A_HEADS = 8
A_Q_RANK = 512
A_KV_RANK = 256
A_NOPE = 128
A_ROPE = 64
A_VDIM = 128
A_HEAD_PAD = 256

B_HEADS = 16
B_KV_HEADS = 4
B_GROUP = 4
B_HDIM = 64
B_WINDOW = 128
B_BLOCK = 128

C_WIDTH = 1024
C_BANDS = 16
C_EMB = 1 + 2 * C_BANDS
C_FFN = 64
C_MIN_DECAY = math.log(1e-2) / 1.5
C_MAX_DECAY = math.log(1e-2) / 0.3

LANE = 128
MXU_W = 256
TM = 256
HALO = 8
VMEM_LIMIT = 56 * 1024 * 1024


def _cparams(n_axes):
    return pltpu.CompilerParams(dimension_semantics=("arbitrary",) * n_axes,
                                vmem_limit_bytes=VMEM_LIMIT)


def _rms(x):
    return x * lax.rsqrt(jnp.mean(x * x, axis=-1, keepdims=True) + NORM_EPS)


def _silu(g):
    return g / (1.0 + jnp.exp(-g))


def _dot(a, b):
    return jnp.dot(a, b, preferred_element_type=F32)


def _dot_nt(a, b):
    return lax.dot_general(a, b, (((1,), (1,)), ((), ())), preferred_element_type=F32)


def _rope128(x, a, b, c):
    return x * a + pltpu.roll(x, 32, 1) * b + pltpu.roll(x, LANE - 32, 1) * c


def _modnorm(x, mod_ref, gpre_ref):
    m = mod_ref[0, 0]
    return _rms(x) * gpre_ref[...] * (1.0 + m[1:2]) + m[0:1]


def _mod_kernel(c_ref, w_ref, b_ref, o_ref):
    a = _silu(c_ref[...])
    o_ref[0] = _dot(a.astype(BF16), w_ref[0].astype(BF16)) + b_ref[0]


def _modulation(cond, w_mod, b_mod):
    rows = cond.shape[0]
    return pl.pallas_call(
        _mod_kernel,
        grid=(DEPTH, 3),
        in_specs=[pl.BlockSpec((rows, D), lambda l, j: (0, 0)),
                  pl.BlockSpec((1, D, D), lambda l, j: (l, 0, j)),
                  pl.BlockSpec((1, 1, D), lambda l, j: (l, 0, j))],
        out_specs=pl.BlockSpec((1, rows, D), lambda l, j: (l, 0, j)),
        out_shape=jax.ShapeDtypeStruct((DEPTH, rows, 3 * D), F32),
        compiler_params=_cparams(2),
        name="adaln_modulation",
    )(cond, w_mod, b_mod.reshape(DEPTH, 1, 3 * D))


def _tok_spec(width, rows=TM, skip=0):
    return pl.BlockSpec((1, rows, width), lambda b, t: (b, t + skip, 0))


def _mod_spec(ctx_tiles=1, skip=0):
    return pl.BlockSpec((1, 1, 3, D), lambda b, t: (jnp.minimum((t + skip) // ctx_tiles, 1), b, 0, 0))


def _const_spec(shape):
    nd = len(shape)
    return pl.BlockSpec(shape, lambda b, t: (0,) * nd)


def _head_spec(width):
    return pl.BlockSpec((1, A_HEADS, TM, width), lambda b, t: (b, 0, t, 0))


def _table_spec():
    return pl.BlockSpec((TM, LANE), lambda b, t: (t, 0))


def _residual_specs(dual, rows=TM, skip=0):
    if not dual:
        return [_tok_spec(D, rows, skip)]
    per = CTX // rows
    return [pl.BlockSpec((1, rows, D), lambda b, t: (b, jnp.minimum(t, per - 1), 0)),
            pl.BlockSpec((1, rows, D), lambda b, t: (b, jnp.maximum(t - per, 0), 0))]


def _residual_tile(refs, dual, rows=TM):
    if not dual:
        return refs[0][0]
    return jnp.where(pl.program_id(1) < CTX // rows, refs[0][0], refs[1][0])


def _finish(o, gate_ref, wout_ref, x, mod_ref, gpost_ref):
    a = o.astype(F32) * _silu(gate_ref[0].astype(F32))
    y = _dot(a.astype(BF16), wout_ref[...])
    return x + mod_ref[0, 0][2:3] * (_rms(y) * gpost_ref[...])


def _mla_proj_kernel(*refs, dual):
    nx = 2 if dual else 1
    (mod_ref, gpre_ref, win_ref, gq_ref, wq_ref, gkv_ref, wkv_ref,
     ra_ref, rb_ref, rc_ref, q_ref, k_ref, v_ref, gate_ref) = refs[nx:]
    h = _modnorm(_residual_tile(refs[:nx], dual), mod_ref, gpre_ref)
    p = _dot(h.astype(BF16), win_ref[...])
    c0, c1, c2 = A_Q_RANK, A_Q_RANK + A_KV_RANK, A_Q_RANK + A_KV_RANK + LANE
    gate_ref[0] = p[:, c2:].astype(BF16)
    q = _dot((_rms(p[:, :c0]) * gq_ref[...]).astype(BF16), wq_ref[...])
    kv = _dot((_rms(p[:, c0:c1]) * gkv_ref[...]).astype(BF16), wkv_ref[...])
    ra, rb, rc = ra_ref[...], rb_ref[...], rc_ref[...]
    kr = _rope128(p[:, c1:c2], ra, rb, rc).astype(BF16)
    for hd in range(A_HEADS):
        o = hd * A_HEAD_PAD
        q_ref[0, hd, :, 0:LANE] = q[:, o:o + LANE].astype(BF16)
        q_ref[0, hd, :, LANE:2 * LANE] = _rope128(q[:, o + LANE:o + 2 * LANE], ra, rb, rc).astype(BF16)
        k_ref[0, hd, :, 0:LANE] = kv[:, hd * A_NOPE:(hd + 1) * A_NOPE].astype(BF16)
        k_ref[0, hd, :, LANE:2 * LANE] = kr
        v_ref[0, hd] = kv[:, (A_HEADS + hd) * A_NOPE:(A_HEADS + hd) * A_NOPE + A_VDIM].astype(BF16)


def _mla_proj(xsrc, modl, gpre, win, gq, wq, gkv, wkv, tabs):
    dual = len(xsrc) == 2
    bsz = xsrc[0].shape[0]
    t = tabs[0].shape[0]
    return pl.pallas_call(
        functools.partial(_mla_proj_kernel, dual=dual),
        grid=(bsz, t // TM),
        in_specs=[*_residual_specs(dual), _mod_spec(), _const_spec((1, D)), _const_spec(win.shape),
                  _const_spec((1, A_Q_RANK)), _const_spec(wq.shape),
                  _const_spec((1, A_KV_RANK)), _const_spec(wkv.shape),
                  _table_spec(), _table_spec(), _table_spec()],
        out_specs=[_head_spec(A_HEAD_PAD), _head_spec(A_HEAD_PAD), _head_spec(A_VDIM), _tok_spec(A_HEADS * A_VDIM)],
        out_shape=[jax.ShapeDtypeStruct((bsz, A_HEADS, t, A_HEAD_PAD), BF16),
                   jax.ShapeDtypeStruct((bsz, A_HEADS, t, A_HEAD_PAD), BF16),
                   jax.ShapeDtypeStruct((bsz, A_HEADS, t, A_VDIM), BF16),
                   jax.ShapeDtypeStruct((bsz, t, A_HEADS * A_VDIM), BF16)],
        compiler_params=_cparams(2),
        name="mla_proj",
    )(*xsrc, modl, gpre, win, gq, wq, gkv, wkv, *tabs)


def _mla_attn_kernel(*refs, dual, latents_only):
    nx = 2 if dual else 1
    q_ref, k_ref, v_ref, gate_ref, wout_ref = refs[:5]
    mod_ref, gpost_ref, xo_ref, o_ref = refs[5 + nx:]

    def attend(nk):
        for hd in range(A_HEADS):
            s = _dot_nt(q_ref[0, hd], k_ref[0, hd, :nk, :])
            m = jnp.max(s, axis=-1, keepdims=True)
            p = jnp.exp2(s - m)
            l = jnp.sum(p, axis=-1, keepdims=True)
            o = _dot(p.astype(BF16), v_ref[0, hd, :nk, :]) / l
            o_ref[:, hd * A_VDIM:(hd + 1) * A_VDIM] = o.astype(BF16)

    t_all = k_ref.shape[2]
    if latents_only:
        attend(t_all)
    else:
        is_ctx = pl.program_id(1) == 0
        pl.when(is_ctx)(lambda: attend(CTX))
        pl.when(jnp.logical_not(is_ctx))(lambda: attend(t_all))
    x = _residual_tile(refs[5:5 + nx], dual)
    xo_ref[0] = _finish(o_ref[...], gate_ref, wout_ref, x, mod_ref, gpost_ref)


def _mla_attn(q, k, v, gate, wout, xsrc, modl, gpost, latents_only):
    dual = len(xsrc) == 2
    bsz, nh, t, _ = q.shape
    skip = CTX // TM if latents_only else 0
    nt = t // TM - skip
    return pl.pallas_call(
        functools.partial(_mla_attn_kernel, dual=dual, latents_only=latents_only),
        grid=(bsz, nt),
        in_specs=[pl.BlockSpec((1, nh, TM, A_HEAD_PAD), lambda b, i: (b, 0, i + skip, 0)),
                  pl.BlockSpec((1, nh, t, A_HEAD_PAD), lambda b, i: (b, 0, 0, 0)),
                  pl.BlockSpec((1, nh, t, A_VDIM), lambda b, i: (b, 0, 0, 0)),
                  _tok_spec(nh * A_VDIM, skip=skip), _const_spec(wout.shape),
                  *_residual_specs(dual, skip=skip), _mod_spec(skip=skip), _const_spec((1, D))],
        out_specs=_tok_spec(D),
        out_shape=jax.ShapeDtypeStruct((bsz, nt * TM, D), F32),
        scratch_shapes=[pltpu.VMEM((TM, nh * A_VDIM), BF16)],
        input_output_aliases={} if (dual or latents_only) else {5: 0},
        compiler_params=_cparams(2),
        name="mla_attention_out",
    )(q, k, v, gate, wout, *xsrc, modl, gpost)


def _swa_proj_kernel(x_ref, mod_ref, gpre_ref, win_ref, ra_ref, rb_ref, rc_ref,
                     q_ref, k_ref, v_ref, gate_ref):
    h = _modnorm(x_ref[0], mod_ref, gpre_ref)
    p = _dot(h.astype(BF16), win_ref[...])
    ra, rb, rc = ra_ref[...], rb_ref[...], rc_ref[...]
    qw, kw = B_HEADS * B_HDIM, B_KV_HEADS * B_HDIM
    lane = lax.broadcasted_iota(jnp.int32, (1, LANE), 1)
    lo = lane < B_HDIM
    ones_col = jnp.broadcast_to(jnp.where(lane == 0, 1.0, 0.0), (TM, LANE)).astype(BF16)
    for j in range(B_KV_HEADS // 2):
        for g in range(B_GROUP):
            s = j * B_GROUP + g
            r = _rope128(p[:, s * LANE:(s + 1) * LANE], ra, rb, rc)
            for half in range(2):
                d = ((2 * j + half) * B_GROUP + g) * LANE
                keep = lo if half == 0 else jnp.logical_not(lo)
                q_ref[0, :, d:d + LANE] = jnp.where(keep, r, 0.0).astype(BF16)
        o = qw + j * LANE
        k_ref[0, :, j * LANE:(j + 1) * LANE] = _rope128(p[:, o:o + LANE], ra, rb, rc).astype(BF16)
        v_ref[0, :, 2 * j * LANE:(2 * j + 1) * LANE] = p[:, o + kw:o + kw + LANE].astype(BF16)
        v_ref[0, :, (2 * j + 1) * LANE:(2 * j + 2) * LANE] = ones_col
    gate_ref[0] = p[:, qw + 2 * kw:].astype(BF16)


def _swa_proj(xs, modl, gpre, win, tabs):
    bsz, t, _ = xs.shape
    qw, kw = B_HEADS * B_HDIM, B_KV_HEADS * B_HDIM
    return pl.pallas_call(
        _swa_proj_kernel,
        grid=(bsz, t // TM),
        in_specs=[_tok_spec(D), _mod_spec(), _const_spec((1, D)), _const_spec(win.shape),
                  _table_spec(), _table_spec(), _table_spec()],
        out_specs=[_tok_spec(2 * qw), _tok_spec(kw), _tok_spec(2 * kw), _tok_spec(qw)],
        out_shape=[jax.ShapeDtypeStruct((bsz, t, 2 * qw), BF16),
                   jax.ShapeDtypeStruct((bsz, t, kw), BF16),
                   jax.ShapeDtypeStruct((bsz, t, 2 * kw), BF16),
                   jax.ShapeDtypeStruct((bsz, t, qw), BF16)],
        compiler_params=_cparams(2),
        name="swa_proj",
    )(xs, modl, gpre, win, *tabs)


def _swa_attn_kernel(q_ref, k_ref, v_ref, sink_ref, gate_ref, wout_ref, x_ref, mod_ref, gpost_ref, xo_ref, o_ref):
    i = pl.program_id(1)
    n_lat = k_ref.shape[1] - CTX
    band = 3 * B_BLOCK
    rows = B_GROUP * B_BLOCK
    lo = lax.broadcasted_iota(jnp.int32, (1, LANE), 1) < B_HDIM

    def run(with_band):
        if with_band:
            li = i - CTX // B_BLOCK
            start = jnp.clip((li - 1) * B_BLOCK, 0, n_lat - band)
            off = li * B_BLOCK - start
            d0 = (lax.broadcasted_iota(jnp.int32, (B_BLOCK, band), 0)
                  - lax.broadcasted_iota(jnp.int32, (B_BLOCK, band), 1))
            mask = jnp.abs(d0 + off) <= B_WINDOW
            kstart = pl.multiple_of(CTX + start, B_BLOCK)
        for j in range(B_KV_HEADS // 2):
            kcols = slice(j * LANE, (j + 1) * LANE)
            vcols = slice(2 * j * LANE, (2 * j + 2) * LANE)
            kk = k_ref[0, 0:CTX, kcols]
            vv = v_ref[0, 0:CTX, vcols]
            if with_band:
                kk = jnp.concatenate([kk, k_ref[0, pl.ds(kstart, band), kcols]], axis=0)
                vv = jnp.concatenate([vv, v_ref[0, pl.ds(kstart, band), vcols]], axis=0)
            outs = []
            for half in range(2):
                hk = 2 * j + half
                qs = jnp.concatenate(
                    [q_ref[0, :, (hk * B_GROUP + g) * LANE:(hk * B_GROUP + g + 1) * LANE] for g in range(B_GROUP)],
                    axis=0)
                sk = sink_ref[hk]
                s = _dot_nt(qs, kk)
                s_c = s[:, :CTX]
                m = jnp.maximum(jnp.max(s_c, axis=-1, keepdims=True), sk)
                if with_band:
                    s_b = jnp.concatenate(
                        [jnp.where(mask, s[g * B_BLOCK:(g + 1) * B_BLOCK, CTX:], NEG_INF) for g in range(B_GROUP)],
                        axis=0)
                    m = jnp.maximum(m, jnp.max(s_b, axis=-1, keepdims=True))
                    p = jnp.concatenate([jnp.exp2(s_c - m), jnp.exp2(s_b - m)], axis=1)
                else:
                    p = jnp.exp2(s_c - m)
                pv = _dot(p.astype(BF16), vv)
                l = pv[:, LANE:LANE + 1] + jnp.exp2(sk - m)
                outs.append(pv[:, :LANE] / l)
            comb = jnp.where(lo, outs[0], outs[1])
            for g in range(B_GROUP):
                s_out = j * B_GROUP + g
                o_ref[:, s_out * LANE:(s_out + 1) * LANE] = comb[g * B_BLOCK:(g + 1) * B_BLOCK].astype(BF16)

    is_ctx = i < CTX // B_BLOCK
    pl.when(is_ctx)(lambda: run(False))
    pl.when(jnp.logical_not(is_ctx))(lambda: run(True))
    xo_ref[0] = _finish(o_ref[...], gate_ref, wout_ref, x_ref[0], mod_ref, gpost_ref)


def _swa_attn(q, k, v, sink_cols, gate, wout, xs, modl, gpost):
    bsz, t, qw2 = q.shape
    kw = k.shape[-1]
    qw = qw2 // 2
    return pl.pallas_call(
        _swa_attn_kernel,
        grid=(bsz, t // B_BLOCK),
        in_specs=[pl.BlockSpec((1, B_BLOCK, qw2), lambda b, i: (b, i, 0)),
                  pl.BlockSpec((1, t, kw), lambda b, i: (b, 0, 0)),
                  pl.BlockSpec((1, t, 2 * kw), lambda b, i: (b, 0, 0)),
                  pl.BlockSpec(sink_cols.shape, lambda b, i: (0, 0, 0)),
                  _tok_spec(qw, B_BLOCK), _const_spec(wout.shape), _tok_spec(D, B_BLOCK),
                  _mod_spec(CTX // B_BLOCK), _const_spec((1, D))],
        out_specs=_tok_spec(D, B_BLOCK),
        out_shape=jax.ShapeDtypeStruct(xs.shape, F32),
        scratch_shapes=[pltpu.VMEM((B_BLOCK, qw), BF16)],
        input_output_aliases={6: 0},
        compiler_params=_cparams(2),
        name="swa_attention_out",
    )(q, k, v, sink_cols, gate, wout, xs, modl, gpost)


HP = TM // 2


def _hyena_proj_kernel(xp_ref, x_ref, xn_ref, mod_ref, gpre_ref, win_ref, cw_ref, cb_ref, perm_ref,
                       ue_ref, uo_ref, ge_ref, go_ref):
    t = pl.program_id(1)
    nt = pl.num_programs(1)
    hh = _dot(perm_ref[...], _modnorm(x_ref[0], mod_ref, gpre_ref).astype(BF16))
    h = jnp.concatenate([_modnorm(xp_ref[0], mod_ref, gpre_ref), hh, _modnorm(xn_ref[0], mod_ref, gpre_ref)], axis=0)
    p = _dot(h.astype(BF16), win_ref[...])
    cwid = 3 * C_WIDTH
    u = p[:, :cwid]
    o0, e0 = HALO, HALO + HP
    go_ref[0] = p[o0:o0 + HP, cwid:].astype(BF16)
    ge_ref[0] = p[e0:e0 + HP, cwid:].astype(BF16)
    po, pe = u[o0:o0 + HP], u[e0:e0 + HP]
    po_prev = pltpu.roll(u[0:e0], 1, 0)[o0:e0]
    pe_next = pltpu.roll(u[e0:], HP + HALO - 1, 0)[0:HP]
    r = lax.broadcasted_iota(jnp.int32, (HP, 1), 0)
    po_prev = jnp.where(jnp.logical_and(r == 0, t <= 1), 0.0, po_prev)
    pe_next = jnp.where(jnp.logical_and(r == HP - 1, jnp.logical_or(t == 0, t == nt - 1)), 0.0, pe_next)
    cw = cw_ref[...]
    cb = cb_ref[...]
    ue_ref[0] = (cb + po_prev * cw[0:1] + pe * cw[1:2] + po * cw[2:3]).astype(BF16)
    uo_ref[0] = (cb + pe * cw[0:1] + po * cw[1:2] + pe_next * cw[2:3]).astype(BF16)


def _half_spec(width):
    return pl.BlockSpec((1, HP, width), lambda b, t: (b, t, 0))


def _hyena_proj(xs, modl, gpre, win, conv_w, conv_b):
    bsz, t, _ = xs.shape
    nt = t // TM
    per = TM // HALO
    last = t // HALO - 1
    half = lambda w: jax.ShapeDtypeStruct((bsz, t // 2, w), BF16)
    return pl.pallas_call(
        _hyena_proj_kernel,
        grid=(bsz, nt),
        in_specs=[pl.BlockSpec((1, HALO, D), lambda b, i: (b, jnp.maximum(i * per - 1, 0), 0)),
                  _tok_spec(D),
                  pl.BlockSpec((1, HALO, D), lambda b, i: (b, jnp.minimum((i + 1) * per, last), 0)),
                  _mod_spec(), _const_spec((1, D)), _const_spec(win.shape),
                  _const_spec(conv_w.shape), _const_spec(conv_b.shape), _const_spec((TM, TM))],
        out_specs=[_half_spec(3 * C_WIDTH), _half_spec(3 * C_WIDTH), _half_spec(C_WIDTH), _half_spec(C_WIDTH)],
        out_shape=[half(3 * C_WIDTH), half(3 * C_WIDTH), half(C_WIDTH), half(C_WIDTH)],
        compiler_params=_cparams(2),
        name="hyena_proj",
    )(xs, xs, xs, modl, gpre, win, conv_w, conv_b, _parity_perm(odd_first=True))


def _parity_perm(odd_first):
    r = np.arange(HP)
    src = np.concatenate([2 * r + 1, 2 * r]) if odd_first else np.concatenate([2 * r, 2 * r + 1])
    m = np.zeros((TM, TM), np.float32)
    m[np.arange(TM), src] = 1.0
    return jnp.asarray(m, BF16)


def _hyena_out_kernel(oe_ref, oo_ref, ge_ref, go_ref, wout_ref, x_ref, mod_ref, gpost_ref, perm_ref, xo_ref):
    a = jnp.concatenate([oe_ref[0].astype(F32) * _silu(ge_ref[0].astype(F32)),
                         oo_ref[0].astype(F32) * _silu(go_ref[0].astype(F32))], axis=0)
    a = _dot(perm_ref[...], a.astype(BF16)).astype(BF16)
    y = _dot(a, wout_ref[...])
    xo_ref[0] = x_ref[0] + mod_ref[0, 0][2:3] * (_rms(y) * gpost_ref[...])


def _hyena_out_proj(oe, oo, ge, go, wout, xs, modl, gpost):
    bsz, t, _ = xs.shape
    w = oe.shape[-1]
    return pl.pallas_call(
        _hyena_out_kernel,
        grid=(bsz, t // TM),
        in_specs=[_half_spec(w), _half_spec(w), _half_spec(w), _half_spec(w), _const_spec(wout.shape),
                  _tok_spec(D), _mod_spec(), _const_spec((1, D)), _const_spec((TM, TM))],
        out_specs=_tok_spec(D),
        out_shape=jax.ShapeDtypeStruct(xs.shape, F32),
        input_output_aliases={5: 0},
        compiler_params=_cparams(2),
        name="hyena_out_proj_residual",
    )(oe, oo, ge, go, wout, xs, modl, gpost, _parity_perm(odd_first=False).T)


def _filter_kernel(z_ref, w1_ref, b1_ref, fr_ref, w2_ref, b2_ref, w3_ref, dl_ref, o_ref):
    hp = lax.Precision.HIGHEST
    z = z_ref[...]
    fr = fr_ref[...]
    h = jnp.sin(fr * (jnp.dot(z, w1_ref[...], precision=hp, preferred_element_type=F32) + b1_ref[...]))
    h = jnp.sin(fr * (jnp.dot(h, w2_ref[...], precision=hp, preferred_element_type=F32) + b2_ref[...]))
    h = jnp.dot(h, w3_ref[...], precision=hp, preferred_element_type=F32)
    o_ref[...] = (h * jnp.exp(-z[:, 0:1] * dl_ref[...])).astype(o_ref.dtype)


def _filters(n, w1, b1, fr, w2, b2, w3):
    t = np.linspace(0.0, 1.0, n, dtype=np.float32)[:, None]
    w = ((2.0 * math.pi / n) * np.arange(n, dtype=np.float32))[:, None].astype(np.float32)
    bands = np.linspace(1e-4, C_BANDS - 1, C_BANDS, dtype=np.float32)[None, :]
    z = np.zeros((n, LANE), np.float32)
    z[:, 0:1] = t
    z[:, 1:1 + C_BANDS] = np.cos(bands * w)
    z[:, 1 + C_BANDS:C_EMB] = -np.sin(bands * w)
    deltas = np.abs(np.linspace(C_MIN_DECAY, C_MAX_DECAY, C_WIDTH, dtype=np.float32))
    dl = np.tile(deltas, 4)[None, :]

    def pad(a, r, c):
        return jnp.zeros((r, c), F32).at[:a.shape[0], :a.shape[1]].set(a)

    tn = min(n, TM)
    nout = 4 * C_WIDTH
    cs = lambda shape: pl.BlockSpec(shape, lambda i: (0, 0))
    return pl.pallas_call(
        _filter_kernel,
        grid=(n // tn,),
        in_specs=[pl.BlockSpec((tn, LANE), lambda i: (i, 0)), cs((LANE, LANE)), cs((1, LANE)), cs((1, LANE)),
                  cs((LANE, LANE)), cs((1, LANE)), cs((LANE, nout)), cs((1, nout))],
        out_specs=pl.BlockSpec((tn, nout), lambda i: (i, 0)),
        out_shape=jax.ShapeDtypeStruct((n, nout), BF16),
        compiler_params=_cparams(1),
        name="hyena_filter_mlp",
    )(jnp.asarray(z), pad(w1, LANE, LANE), pad(b1[None], 1, LANE), pad(fr[None], 1, LANE),
      pad(w2, LANE, LANE), pad(b2[None], 1, LANE), pad(w3, LANE, nout), jnp.asarray(dl))


def _dft_matrix(n):
    f = np.arange(n, dtype=np.int64)[:, None]
    s = np.arange(n, dtype=np.int64)[None, :]
    ang = (2.0 * np.pi / (2 * n)) * ((f * s) % (2 * n)).astype(np.float64)
    cos = np.cos(ang)
    sin = np.sin(ang)
    sin[0, :] = np.where(np.arange(n) % 2 == 0, 1.0, -1.0)
    return cos.astype(np.float32), sin.astype(np.float32)


def _matmul_kernel(a_ref, b_ref, o_ref):
    o_ref[...] = _dot(a_ref[...], b_ref[...])


def _matmul(a, b, bm, bn):
    m, k = a.shape
    n = b.shape[1]
    assert m % bm == 0 and n % bn == 0
    return pl.pallas_call(
        _matmul_kernel,
        grid=(m // bm, n // bn),
        in_specs=[pl.BlockSpec((bm, k), lambda i, j: (i, 0)), pl.BlockSpec((k, bn), lambda i, j: (0, j))],
        out_specs=pl.BlockSpec((bm, bn), lambda i, j: (i, j)),
        out_shape=jax.ShapeDtypeStruct((m, n), F32),
        compiler_params=_cparams(2),
        name="filter_dft",
    )(a, b)


def _hyena_conv_kernel(ve_ref, x1e_ref, x2e_ref, vo_ref, x1o_ref, x2o_ref, me_ref, mo_ref, met_ref, mot_ref,
                       kr1_ref, ki1_ref, kr2_ref, ki2_ref, kh_ref, fb_ref, oe_ref, oo_ref,
                       ze_ref, zo_ref, ye_ref, yo_ref, *, row0, h, fbs):
    o = pl.program_id(2)
    fb = pl.program_id(3)
    nfb = pl.num_programs(3)
    rows = slice(row0, row0 + h)

    @pl.when(jnp.logical_and(o == 0, fb == 0))
    def _():
        ze_ref[...] = ve_ref[0, rows, :]
        zo_ref[...] = vo_ref[0, rows, :]

    @pl.when(fb == 0)
    def _():
        ye_ref[...] = jnp.zeros_like(ye_ref)
        yo_ref[...] = jnp.zeros_like(yo_ref)

    first = jnp.logical_and(lax.broadcasted_iota(jnp.int32, (8, 1), 0) == 0, fb == 0)
    for cs in (slice(0, MXU_W), slice(MXU_W, 2 * MXU_W)):
        ge = _dot(me_ref[0], ze_ref[:, cs])
        go = _dot(mo_ref[0], zo_ref[:, cs])
        ce, se, co, so = ge[:fbs], ge[fbs:], go[:fbs], go[fbs:]
        xr1, xs1, xr2, xs2 = ce + co, se + so, ce - co, so - se
        kr1, ki1, kr2, ki2 = kr1_ref[0, :, cs], ki1_ref[0, :, cs], kr2_ref[0, :, cs], ki2_ref[0, :, cs]
        a1, b1 = xr1 * kr1 + xs1 * ki1, xs1 * kr1 - xr1 * ki1
        a2, b2 = xr2 * kr2 + xs2 * ki2, xs2 * kr2 - xr2 * ki2
        bm, bp = b1 - b2, b1 + b2
        krh, kih = kh_ref[0, 0:1, cs], kh_ref[0, 1:2, cs]
        se8, so8 = se[0:8], so[0:8]
        bm = jnp.concatenate([jnp.where(first, se8 * krh + so8 * kih, bm[0:8]), bm[8:]], axis=0)
        bp = jnp.concatenate([jnp.where(first, so8 * krh - se8 * kih, bp[0:8]), bp[8:]], axis=0)
        we = jnp.concatenate([a1 + a2, bm], axis=0).astype(BF16)
        wo = jnp.concatenate([a1 - a2, bp], axis=0).astype(BF16)
        ye_ref[:, cs] += _dot(met_ref[0], we)
        yo_ref[:, cs] += _dot(mot_ref[0], wo)

    @pl.when(fb == nfb - 1)
    def _():
        fe = ye_ref[...] + ze_ref[...].astype(F32) * fb_ref[0]
        fo = yo_ref[...] + zo_ref[...].astype(F32) * fb_ref[0]

        @pl.when(o == 0)
        def _():
            ze_ref[...] = (x1e_ref[0, rows, :].astype(F32) * fe).astype(BF16)
            zo_ref[...] = (x1o_ref[0, rows, :].astype(F32) * fo).astype(BF16)

        @pl.when(o == 1)
        def _():
            if row0 > 0:
                oe_ref[0, 0:row0, :] = jnp.zeros((row0, oe_ref.shape[2]), oe_ref.dtype)
                oo_ref[0, 0:row0, :] = jnp.zeros((row0, oo_ref.shape[2]), oo_ref.dtype)
            oe_ref[0, rows, :] = (x2e_ref[0, rows, :].astype(F32) * fe).astype(BF16)
            oo_ref[0, rows, :] = (x2o_ref[0, rows, :].astype(F32) * fo).astype(BF16)


def _hyena_conv_aliased_kernel(*refs, **kw):
    _hyena_conv_kernel(*refs[:16], *refs[18:], **kw)


def _hyena_conv(ue, uo, mats, planes, fbias, *, row0, h, block_rows, prev=None):
    bsz, t2, _ = ue.shape
    tc = 2 * MXU_W
    nct = C_WIDTH // tc
    me, mo, met, mot = mats
    nfb = me.shape[0]
    fbs = me.shape[1] // 2
    u_spec = lambda which: pl.BlockSpec((1, block_rows, tc), lambda b, c, o, f: (b, 0, which * nct + c))
    m_spec = pl.BlockSpec((1, 2 * fbs, h), lambda b, c, o, f: (f, 0, 0))
    mt_spec = pl.BlockSpec((1, h, 2 * fbs), lambda b, c, o, f: (f, 0, 0))
    p_spec = pl.BlockSpec((1, fbs, tc), lambda b, c, o, f: (o, f, c))
    in_specs = [u_spec(0), u_spec(1), u_spec(2), u_spec(0), u_spec(1), u_spec(2),
                m_spec, m_spec, mt_spec, mt_spec, p_spec, p_spec, p_spec, p_spec,
                pl.BlockSpec((1, 2, tc), lambda b, c, o, f: (o, 0, c)),
                pl.BlockSpec((1, 1, tc), lambda b, c, o, f: (o, 0, c))]
    args = [ue, ue, ue, uo, uo, uo, me, mo, met, mot, *planes, fbias]
    aliases = {}
    kern = functools.partial(_hyena_conv_kernel, row0=row0, h=h, fbs=fbs)
    if prev is not None:
        in_specs += [pl.BlockSpec(memory_space=pl.ANY)] * 2
        args += list(prev)
        aliases = {16: 0, 17: 1}
        kern = functools.partial(_hyena_conv_aliased_kernel, row0=row0, h=h, fbs=fbs)
    o_spec = pl.BlockSpec((1, block_rows, tc), lambda b, c, o, f: (b, 0, c))
    return pl.pallas_call(
        kern,
        grid=(bsz, nct, 2, nfb),
        in_specs=in_specs,
        out_specs=[o_spec, o_spec],
        out_shape=[jax.ShapeDtypeStruct((bsz, t2, C_WIDTH), BF16)] * 2,
        scratch_shapes=[pltpu.VMEM((h, tc), BF16), pltpu.VMEM((h, tc), BF16),
                        pltpu.VMEM((h, tc), F32), pltpu.VMEM((h, tc), F32)],
        input_output_aliases=aliases,
        compiler_params=_cparams(4),
        name="hyena_long_conv_h%d" % h,
    )(*args)


def _radix2_matrices(n):
    h = n // 2
    fbs = min(h, 256)
    nfb = h // fbs
    f = np.arange(h, dtype=np.int64)[:, None]
    r = np.arange(h, dtype=np.int64)[None, :]
    alt = np.where(np.arange(h) % 2 == 0, 1.0, -1.0)
    out = []
    for pos in (2 * r, 2 * r + 1):
        ang = (2.0 * np.pi / (2 * n)) * ((f * pos) % (2 * n)).astype(np.float64)
        cos, sin = np.cos(ang), np.sin(ang)
        sin[0, :] = alt
        blocks = np.concatenate([cos.reshape(nfb, fbs, h), sin.reshape(nfb, fbs, h)], axis=1)
        out.append(blocks.astype(np.float32))
    me, mo = out
    tr = lambda a: np.ascontiguousarray(np.transpose(a, (0, 2, 1)))
    return tuple(jnp.asarray(a, BF16) for a in (me, mo, tr(me), tr(mo)))


def _hyena_spectra(n, filt):
    h = n // 2
    cos, sin = _dft_matrix(n)
    mirror = np.concatenate([[0], np.arange(n - 1, h, -1)])
    rows = np.concatenate([cos[:h], np.concatenate([sin[:1], cos[mirror[1:]]]), sin[:h], sin[mirror],
                           cos[h:h + 1], sin[h:h + 1], np.zeros((LANE - 2, n), np.float32)], axis=0)
    nr = 4 * h + LANE
    hspec = _matmul(jnp.asarray(rows, BF16), filt, 3 * LANE if nr % (3 * LANE) == 0 else nr, 1024)
    hf = hspec[:, :2 * C_WIDTH].reshape(nr, 2, C_WIDTH)
    hb = hspec[:, 2 * C_WIDTH:].reshape(nr, 2, C_WIDTH)
    w = np.full((h, 1, 1), 2.0 / (2 * n), np.float32)
    w[0] = 1.0 / (2 * n)
    nz = (np.arange(h) != 0)[:, None, None]
    tr = lambda a: jnp.transpose(a, (1, 0, 2))
    kr1 = (hf[:h] + hb[:h]) * w
    kr2 = (hf[h:2 * h] + hb[h:2 * h]) * w
    ki1 = jnp.where(nz, (hb[2 * h:3 * h] - hf[2 * h:3 * h]) * w, 0.0)
    ki2 = jnp.where(nz, (hb[3 * h:4 * h] - hf[3 * h:4 * h]) * w, 0.0)
    kh = jnp.stack([hf[4 * h] + hb[4 * h], hb[4 * h + 1] - hf[4 * h + 1]], axis=1) * (2.0 / (2 * n))
    return (tr(kr1), tr(ki1), tr(kr2), tr(ki2), kh)


def _rope_tables(seq, layout):
    rows = seq // GRID_W
    row = np.repeat(np.arange(rows, dtype=np.float32), GRID_W)
    col = np.tile(np.arange(GRID_W, dtype=np.float32), rows)
    per_axis = 32
    inv = (ROPE_BASE ** (-np.arange(0, per_axis, 2, dtype=np.float32) / per_axis)).astype(np.float32)
    ang = np.concatenate([row[:, None] * inv, col[:, None] * inv], axis=-1)
    cos = np.concatenate([np.ones((CTX, 32), np.float32), np.cos(ang)], axis=0)
    sin = np.concatenate([np.zeros((CTX, 32), np.float32), np.sin(ang)], axis=0)
    one, zero = np.ones_like(cos), np.zeros_like(cos)
    if layout == "mla":
        a = [cos, cos, one, one]
        b = [zero, sin, zero, zero]
        c = [-sin, zero, zero, zero]
    else:
        a = [cos, cos, cos, cos]
        b = [zero, sin, zero, sin]
        c = [-sin, zero, -sin, zero]
    return tuple(jnp.asarray(np.concatenate(p, axis=1), F32) for p in (a, b, c))


def _swa_head_perm():
    cols = []
    for j in range(B_KV_HEADS // 2):
        for g in range(B_GROUP):
            for hk in (2 * j, 2 * j + 1):
                h = hk * B_GROUP + g
                cols.extend(range(h * B_HDIM, (h + 1) * B_HDIM))
    return np.asarray(cols, np.int32)


def _mla_weights(w_in, w_q, w_kv):
    c1 = A_Q_RANK + A_KV_RANK
    zpad = jnp.zeros((D, LANE - A_ROPE), F32)
    win = jnp.concatenate([w_in[:, :c1 + A_ROPE], zpad, w_in[:, c1 + A_ROPE:]], axis=1)
    qscale = (A_NOPE + A_ROPE) ** -0.5 * math.log2(math.e)
    wq = w_q.reshape(A_Q_RANK, A_HEADS, A_NOPE + A_ROPE) * qscale
    wq = jnp.concatenate([wq, jnp.zeros((A_Q_RANK, A_HEADS, A_HEAD_PAD - A_NOPE - A_ROPE), F32)], axis=-1)
    wq = wq.reshape(A_Q_RANK, A_HEADS * A_HEAD_PAD)
    wkv = w_kv.reshape(A_KV_RANK, A_HEADS, A_NOPE + A_VDIM)
    wkv = jnp.concatenate([wkv[:, :, :A_NOPE].reshape(A_KV_RANK, -1), wkv[:, :, A_NOPE:].reshape(A_KV_RANK, -1)], axis=1)
    return win.astype(BF16), wq.astype(BF16), wkv.astype(BF16)


def kernel(x, c, ctx, c_ctx, w_mod, b_mod, g_pre, g_post, a_w_in, a_g_q, a_w_q, a_g_kv, a_w_kv, a_w_out, b_w_in, b_sink, b_w_out, c_w_in, c_conv_w, c_conv_b, c_f_w1, c_f_b1, c_f_freq, c_f_w2, c_f_b2, c_f_w3, c_filt_bias, c_w_out):
    bsz, seq, _ = x.shape
    assert ctx.shape[1] == CTX and seq % TM == 0 and seq % GRID_W == 0
    xsrc = (ctx, x)

    pad_rows = (-(bsz + 1)) % 8
    cond = jnp.concatenate([c, c_ctx[None], jnp.zeros((pad_rows, D), F32)], axis=0)
    mod = _modulation(cond, w_mod, b_mod)

    tabs_mla = _rope_tables(seq, "mla")
    tabs_swa = _rope_tables(seq, "swa")

    for layer in range(DEPTH):
        kind, j = layer % 3, layer // 3
        mx = mod[layer, :bsz].reshape(bsz, 3, D)
        mc = jnp.broadcast_to(mod[layer, bsz].reshape(1, 3, D), (bsz, 3, D))
        modl = jnp.stack([mc, mx], axis=0)
        gpre = g_pre[layer][None]
        gpost = g_post[layer][None]
        if kind == 0:
            win, wq, wkv = _mla_weights(a_w_in[j], a_w_q[j], a_w_kv[j])
            q, k, v, gate = _mla_proj(xsrc, modl, gpre, win, a_g_q[j][None], wq, a_g_kv[j][None], wkv, tabs_mla)
            xs = _mla_attn(q, k, v, gate, a_w_out[j].astype(BF16), xsrc, modl, gpost,
                           latents_only=layer == DEPTH - 1)
        elif kind == 1:
            xs, = xsrc
            perm = _swa_head_perm()
            qw, kw = B_HEADS * B_HDIM, B_KV_HEADS * B_HDIM
            w = b_w_in[j]
            win = jnp.concatenate([w[:, :qw][:, perm] * (B_HDIM ** -0.5 * math.log2(math.e)), w[:, qw:qw + 2 * kw],
                                   w[:, qw + 2 * kw:][:, perm]], axis=1).astype(BF16)
            q, k, v, gate = _swa_proj(xs, modl, gpre, win, tabs_swa)
            sink = (b_sink[j].astype(F32) * math.log2(math.e)).reshape(B_KV_HEADS, B_GROUP, 1, 1)
            sink_cols = jnp.broadcast_to(sink, (B_KV_HEADS, B_GROUP, B_BLOCK, 1)).reshape(B_KV_HEADS, B_GROUP * B_BLOCK, 1)
            xs = _swa_attn(q, k, v, sink_cols, gate, b_w_out[j][perm, :].astype(BF16), xs, modl, gpost)
        else:
            xs, = xsrc
            ue, uo, ge, go = _hyena_proj(xs, modl, gpre, c_w_in[j].astype(BF16), c_conv_w[j], c_conv_b[j][None])
            fargs = (c_f_w1[j], c_f_b1[j], c_f_freq[j], c_f_w2[j], c_f_b2[j], c_f_w3[j])
            fbias = c_filt_bias[j].reshape(2, 1, C_WIDTH)
            oeo = _hyena_conv(ue, uo, _radix2_matrices(seq), _hyena_spectra(seq, _filters(seq, *fargs)), fbias,
                              row0=CTX // 2, h=seq // 2, block_rows=(CTX + seq) // 2)
            oe, oo = _hyena_conv(ue, uo, _radix2_matrices(CTX), _hyena_spectra(CTX, _filters(CTX, *fargs)), fbias,
                                 row0=0, h=CTX // 2, block_rows=CTX // 2, prev=oeo)
            xs = _hyena_out_proj(oe, oo, ge, go, c_w_out[j].astype(BF16), xs, modl, gpost)
        xsrc = (xs,)
    return xs
```

```python
import functools
import math

import numpy as np
import jax
import jax.numpy as jnp
from jax import lax
from jax.experimental import pallas as pl
from jax.experimental.pallas import tpu as pltpu

F32 = jnp.float32
BF16 = jnp.bfloat16

D = 1024
DEPTH = 4
GRID_W = 64
CTX = 256
NORM_EPS = 1e-6
ROPE_BASE = 10000.0
NEG_INF = -1e30

A_HEADS = 8
A_Q_RANK = 512
A_KV_RANK = 256
A_NOPE = 128
A_ROPE = 64
A_VDIM = 128
A_HEAD_PAD = 256

B_HEADS = 16
B_KV_HEADS = 4
B_GROUP = 4
B_HDIM = 64
B_WINDOW = 128
B_BLOCK = 128

C_WIDTH = 1024
C_BANDS = 16
C_EMB = 1 + 2 * C_BANDS
C_FFN = 64
C_MIN_DECAY = math.log(1e-2) / 1.5
C_MAX_DECAY = math.log(1e-2) / 0.3

LANE = 128
MXU_W = 256
TM = 256
TP = 3 * TM
HALO = 8
VMEM_LIMIT = 56 * 1024 * 1024


def _cparams(n_axes):
    return pltpu.CompilerParams(dimension_semantics=("arbitrary",) * n_axes,
                                vmem_limit_bytes=VMEM_LIMIT)


def _rms(x):
    return x * lax.rsqrt(jnp.mean(x * x, axis=-1, keepdims=True) + NORM_EPS)


def _silu(g):
    return g / (1.0 + jnp.exp(-g))


def _dot(a, b):
    return jnp.dot(a, b, preferred_element_type=F32)


def _dot_nt(a, b):
    return lax.dot_general(a, b, (((1,), (1,)), ((), ())), preferred_element_type=F32)


def _rope128(x, a, b, c):
    return x * a + pltpu.roll(x, 32, 1) * b + pltpu.roll(x, LANE - 32, 1) * c


def _modnorm(x, mod_ref, gpre_ref):
    m = mod_ref[0, 0]
    return _rms(x) * gpre_ref[...] * (1.0 + m[1:2]) + m[0:1]


def _mod_kernel(c_ref, w_ref, b_ref, o_ref):
    a = _silu(c_ref[...])
    o_ref[0] = _dot(a.astype(BF16), w_ref[0].astype(BF16)) + b_ref[0]


def _modulation(cond, w_mod, b_mod):
    rows = cond.shape[0]
    return pl.pallas_call(
        _mod_kernel,
        grid=(DEPTH, 3),
        in_specs=[pl.BlockSpec((rows, D), lambda l, j: (0, 0)),
                  pl.BlockSpec((1, D, D), lambda l, j: (l, 0, j)),
                  pl.BlockSpec((1, 1, D), lambda l, j: (l, 0, j))],
        out_specs=pl.BlockSpec((1, rows, D), lambda l, j: (l, 0, j)),
        out_shape=jax.ShapeDtypeStruct((DEPTH, rows, 3 * D), F32),
        compiler_params=_cparams(2),
        name="adaln_modulation",
    )(cond, w_mod, b_mod.reshape(DEPTH, 1, 3 * D))


def _tok_spec(width, rows=TM, skip=0):
    return pl.BlockSpec((1, rows, width), lambda b, t: (b, t + skip, 0))


def _mod_spec(ctx_tiles=1, skip=0):
    return pl.BlockSpec((1, 1, 3, D), lambda b, t: (jnp.minimum((t + skip) // ctx_tiles, 1), b, 0, 0))


def _const_spec(shape):
    nd = len(shape)
    return pl.BlockSpec(shape, lambda b, t: (0,) * nd)


def _residual_specs(dual, rows=TM, skip=0):
    if not dual:
        return [_tok_spec(D, rows, skip)]
    per = CTX // rows
    return [pl.BlockSpec((1, rows, D), lambda b, t: (b, jnp.minimum(t, per - 1), 0)),
            pl.BlockSpec((1, rows, D), lambda b, t: (b, jnp.maximum(t - per, 0), 0))]


def _wide_specs(dual):
    n = TP // TM
    if not dual:
        return [pl.BlockSpec((1, TM, D), functools.partial(lambda j, b, t: (b, n * t + j, 0), j)) for j in range(n)]
    last = lambda j, b, t: (b, jnp.maximum(n * t + j - CTX // TM, 0), 0)
    return ([pl.BlockSpec((1, CTX, D), lambda b, t: (b, 0, 0))]
            + [pl.BlockSpec((1, TM, D), functools.partial(last, j)) for j in range(n)])


def _wide_modnorm(refs, dual, mod_ref, gpre_ref):
    t = pl.program_id(1)
    blocks = [r[0] for r in refs[1:]] if dual else [r[0] for r in refs]
    if dual:
        blocks[0] = jnp.where(t == 0, refs[0][0], blocks[0])
    x = jnp.concatenate(blocks, axis=0)
    is_ctx = jnp.logical_and(lax.broadcasted_iota(jnp.int32, (TP, 1), 0) < CTX, t == 0)
    mc, mx = mod_ref[0, 0], mod_ref[1, 0]
    scale = jnp.where(is_ctx, mc[1:2], mx[1:2])
    shift = jnp.where(is_ctx, mc[0:1], mx[0:1])
    return _rms(x) * gpre_ref[...] * (1.0 + scale) + shift


def _both_mod_spec():
    return pl.BlockSpec((2, 1, 3, D), lambda b, t: (0, b, 0, 0))


def _resident(shape):
    nd = len(shape)
    return pl.BlockSpec(shape, lambda b, t: (0,) * nd, pipeline_mode=pl.Buffered(1))


def _residual_tile(refs, dual, rows=TM):
    if not dual:
        return refs[0][0]
    return jnp.where(pl.program_id(1) < CTX // rows, refs[0][0], refs[1][0])


def _finish(o, gate_ref, wout_ref, x, mod_ref, gpost_ref):
    a = o.astype(F32) * _silu(gate_ref[0].astype(F32))
    y = _dot(a.astype(BF16), wout_ref[...])
    return x + mod_ref[0, 0][2:3] * (_rms(y) * gpost_ref[...])


def _mla_proj_kernel(*refs, dual):
    nx = TP // TM + (1 if dual else 0)
    (mod_ref, gpre_ref, win_ref, gq_ref, wq_ref, gkv_ref, wkv_ref,
     ra_ref, rb_ref, rc_ref, q_ref, k_ref, v_ref, gate_ref) = refs[nx:]
    h = _wide_modnorm(refs[:nx], dual, mod_ref, gpre_ref)
    p = _dot(h.astype(BF16), win_ref[...])
    c0, c1, c2 = A_Q_RANK, A_Q_RANK + A_KV_RANK, A_Q_RANK + A_KV_RANK + LANE
    gate_ref[0] = p[:, c2:].astype(BF16)
    qn = (_rms(p[:, :c0]) * gq_ref[...]).astype(BF16)
    kvn = (_rms(p[:, c0:c1]) * gkv_ref[...]).astype(BF16)
    ra, rb, rc = ra_ref[...], rb_ref[...], rc_ref[...]
    kr = _rope128(p[:, c1:c2], ra, rb, rc).astype(BF16)
    for hd in range(A_HEADS):
        q = _dot(qn, wq_ref[:, hd * A_HEAD_PAD:(hd + 1) * A_HEAD_PAD])
        q_ref[0, hd, :, 0:LANE] = q[:, 0:LANE].astype(BF16)
        q_ref[0, hd, :, LANE:2 * LANE] = _rope128(q[:, LANE:2 * LANE], ra, rb, rc).astype(BF16)
        kv = _dot(kvn, wkv_ref[:, hd * A_HEAD_PAD:(hd + 1) * A_HEAD_PAD])
        k_ref[0, hd, :, 0:LANE] = kv[:, 0:A_NOPE].astype(BF16)
        k_ref[0, hd, :, LANE:2 * LANE] = kr
        v_ref[0, hd] = kv[:, A_NOPE:].astype(BF16)


def _mla_proj(xsrc, modl, gpre, win, gq, wq, gkv, wkv, tabs):
    dual = len(xsrc) == 2
    bsz = xsrc[0].shape[0]
    t = tabs[0].shape[0]
    xargs = [xsrc[0]] + [xsrc[1]] * (TP // TM) if dual else [xsrc[0]] * (TP // TM)
    head = lambda w: pl.BlockSpec((1, A_HEADS, TP, w), lambda b, i: (b, 0, i, 0))
    tab = pl.BlockSpec((TP, LANE), lambda b, i: (i, 0))
    return pl.pallas_call(
        functools.partial(_mla_proj_kernel, dual=dual),
        grid=(bsz, t // TP),
        in_specs=[*_wide_specs(dual), _both_mod_spec(), _const_spec((1, D)), _resident(win.shape),
                  _const_spec((1, A_Q_RANK)), _resident(wq.shape),
                  _const_spec((1, A_KV_RANK)), _resident(wkv.shape), tab, tab, tab],
        out_specs=[head(A_HEAD_PAD), head(A_HEAD_PAD), head(A_VDIM), _tok_spec(A_HEADS * A_VDIM, TP)],
        out_shape=[jax.ShapeDtypeStruct((bsz, A_HEADS, t, A_HEAD_PAD), BF16),
                   jax.ShapeDtypeStruct((bsz, A_HEADS, t, A_HEAD_PAD), BF16),
                   jax.ShapeDtypeStruct((bsz, A_HEADS, t, A_VDIM), BF16),
                   jax.ShapeDtypeStruct((bsz, t, A_HEADS * A_VDIM), BF16)],
        compiler_params=_cparams(2),
        name="mla_proj",
    )(*xargs, modl, gpre, win, gq, wq, gkv, wkv, *tabs)


def _mla_attn_kernel(*refs, dual, latents_only):
    nx = 2 if dual else 1
    q_ref, k_ref, v_ref, gate_ref, wout_ref = refs[:5]
    mod_ref, gpost_ref, xo_ref, o_ref = refs[5 + nx:]

    def attend(nk):
        for hd in range(A_HEADS):
            s = _dot_nt(q_ref[0, hd], k_ref[0, hd, :nk, :])
            m = jnp.max(s, axis=-1, keepdims=True)
            p = jnp.exp2(s - m)
            l = jnp.sum(p, axis=-1, keepdims=True)
            o = _dot(p.astype(BF16), v_ref[0, hd, :nk, :]) / l
            o_ref[:, hd * A_VDIM:(hd + 1) * A_VDIM] = o.astype(BF16)

    t_all = k_ref.shape[2]
    if latents_only:
        attend(t_all)
    else:
        is_ctx = pl.program_id(1) == 0
        pl.when(is_ctx)(lambda: attend(CTX))
        pl.when(jnp.logical_not(is_ctx))(lambda: attend(t_all))
    x = _residual_tile(refs[5:5 + nx], dual)
    xo_ref[0] = _finish(o_ref[...], gate_ref, wout_ref, x, mod_ref, gpost_ref)


def _mla_attn(q, k, v, gate, wout, xsrc, modl, gpost, latents_only):
    dual = len(xsrc) == 2
    bsz, nh, t, _ = q.shape
    skip = CTX // TM if latents_only else 0
    nt = t // TM - skip
    return pl.pallas_call(
        functools.partial(_mla_attn_kernel, dual=dual, latents_only=latents_only),
        grid=(bsz, nt),
        in_specs=[pl.BlockSpec((1, nh, TM, A_HEAD_PAD), lambda b, i: (b, 0, i + skip, 0)),
                  pl.BlockSpec((1, nh, t, A_HEAD_PAD), lambda b, i: (b, 0, 0, 0)),
                  pl.BlockSpec((1, nh, t, A_VDIM), lambda b, i: (b, 0, 0, 0)),
                  _tok_spec(nh * A_VDIM, skip=skip), _const_spec(wout.shape),
                  *_residual_specs(dual, skip=skip), _mod_spec(skip=skip), _const_spec((1, D))],
        out_specs=_tok_spec(D),
        out_shape=jax.ShapeDtypeStruct((bsz, nt * TM, D), F32),
        scratch_shapes=[pltpu.VMEM((TM, nh * A_VDIM), BF16)],
        input_output_aliases={} if (dual or latents_only) else {5: 0},
        compiler_params=_cparams(2),
        name="mla_attention_out",
    )(q, k, v, gate, wout, *xsrc, modl, gpost)


def _swa_proj_kernel(*refs):
    xrefs = refs[:TP // TM]
    mod_ref, gpre_ref, win_ref, ra_ref, rb_ref, rc_ref, q_ref, k_ref, v_ref, gate_ref = refs[TP // TM:]
    h = _wide_modnorm(xrefs, False, mod_ref, gpre_ref)
    p = _dot(h.astype(BF16), win_ref[...])
    ra, rb, rc = ra_ref[...], rb_ref[...], rc_ref[...]
    qw, kw = B_HEADS * B_HDIM, B_KV_HEADS * B_HDIM
    lane = lax.broadcasted_iota(jnp.int32, (1, LANE), 1)
    lo = lane < B_HDIM
    ones_col = jnp.broadcast_to(jnp.where(lane == 0, 1.0, 0.0), (TP, LANE)).astype(BF16)
    for j in range(B_KV_HEADS // 2):
        for g in range(B_GROUP):
            s = j * B_GROUP + g
            r = _rope128(p[:, s * LANE:(s + 1) * LANE], ra, rb, rc)
            for half in range(2):
                d = ((2 * j + half) * B_GROUP + g) * LANE
                keep = lo if half == 0 else jnp.logical_not(lo)
                q_ref[0, :, d:d + LANE] = jnp.where(keep, r, 0.0).astype(BF16)
        o = qw + j * LANE
        k_ref[0, :, j * LANE:(j + 1) * LANE] = _rope128(p[:, o:o + LANE], ra, rb, rc).astype(BF16)
        v_ref[0, :, 2 * j * LANE:(2 * j + 1) * LANE] = p[:, o + kw:o + kw + LANE].astype(BF16)
        v_ref[0, :, (2 * j + 1) * LANE:(2 * j + 2) * LANE] = ones_col
    gate_ref[0] = p[:, qw + 2 * kw:].astype(BF16)


def _swa_proj(xs, modl, gpre, win, tabs):
    bsz, t, _ = xs.shape
    qw, kw = B_HEADS * B_HDIM, B_KV_HEADS * B_HDIM
    tab = pl.BlockSpec((TP, LANE), lambda b, i: (i, 0))
    return pl.pallas_call(
        _swa_proj_kernel,
        grid=(bsz, t // TP),
        in_specs=[*_wide_specs(False), _both_mod_spec(), _const_spec((1, D)), _resident(win.shape), tab, tab, tab],
        out_specs=[_tok_spec(2 * qw, TP), _tok_spec(kw, TP), _tok_spec(2 * kw, TP), _tok_spec(qw, TP)],
        out_shape=[jax.ShapeDtypeStruct((bsz, t, 2 * qw), BF16),
                   jax.ShapeDtypeStruct((bsz, t, kw), BF16),
                   jax.ShapeDtypeStruct((bsz, t, 2 * kw), BF16),
                   jax.ShapeDtypeStruct((bsz, t, qw), BF16)],
        compiler_params=_cparams(2),
        name="swa_proj",
    )(*[xs] * (TP // TM), modl, gpre, win, *tabs)


def _swa_attn_kernel(q_ref, k_ref, v_ref, sink_ref, gate_ref, wout_ref, x_ref, mod_ref, gpost_ref, xo_ref, o_ref):
    i = pl.program_id(1)
    n_lat = k_ref.shape[1] - CTX
    band = 3 * B_BLOCK
    lo = lax.broadcasted_iota(jnp.int32, (1, LANE), 1) < B_HDIM

    def block(sub, with_band):
        r0 = sub * B_BLOCK
        if with_band:
            li = (i - CTX // TM) * (TM // B_BLOCK) + sub
            start = jnp.clip((li - 1) * B_BLOCK, 0, n_lat - band)
            off = li * B_BLOCK - start
            d0 = (lax.broadcasted_iota(jnp.int32, (B_BLOCK, band), 0)
                  - lax.broadcasted_iota(jnp.int32, (B_BLOCK, band), 1))
            mask = jnp.abs(d0 + off) <= B_WINDOW
            kstart = pl.multiple_of(CTX + start, B_BLOCK)
        for j in range(B_KV_HEADS // 2):
            kcols = slice(j * LANE, (j + 1) * LANE)
            vcols = slice(2 * j * LANE, (2 * j + 2) * LANE)
            kk = k_ref[0, 0:CTX, kcols]
            vv = v_ref[0, 0:CTX, vcols]
            if with_band:
                kk = jnp.concatenate([kk, k_ref[0, pl.ds(kstart, band), kcols]], axis=0)
                vv = jnp.concatenate([vv, v_ref[0, pl.ds(kstart, band), vcols]], axis=0)
            outs = []
            for half in range(2):
                hk = 2 * j + half
                qs = jnp.concatenate(
                    [q_ref[0, r0:r0 + B_BLOCK, (hk * B_GROUP + g) * LANE:(hk * B_GROUP + g + 1) * LANE]
                     for g in range(B_GROUP)], axis=0)
                sk = sink_ref[hk]
                s = _dot_nt(qs, kk)
                s_c = s[:, :CTX]
                m = jnp.maximum(jnp.max(s_c, axis=-1, keepdims=True), sk)
                if with_band:
                    s_b = jnp.concatenate(
                        [jnp.where(mask, s[g * B_BLOCK:(g + 1) * B_BLOCK, CTX:], NEG_INF) for g in range(B_GROUP)],
                        axis=0)
                    m = jnp.maximum(m, jnp.max(s_b, axis=-1, keepdims=True))
                    p = jnp.concatenate([jnp.exp2(s_c - m), jnp.exp2(s_b - m)], axis=1)
                else:
                    p = jnp.exp2(s_c - m)
                pv = _dot(p.astype(BF16), vv)
                l = pv[:, LANE:LANE + 1] + jnp.exp2(sk - m)
                outs.append(pv[:, :LANE] / l)
            comb = jnp.where(lo, outs[0], outs[1])
            for g in range(B_GROUP):
                s_out = j * B_GROUP + g
                o_ref[r0:r0 + B_BLOCK, s_out * LANE:(s_out + 1) * LANE] = (
                    comb[g * B_BLOCK:(g + 1) * B_BLOCK].astype(BF16))

    def run(with_band):
        for sub in range(TM // B_BLOCK):
            block(sub, with_band)

    is_ctx = i < CTX // TM
    pl.when(is_ctx)(lambda: run(False))
    pl.when(jnp.logical_not(is_ctx))(lambda: run(True))
    xo_ref[0] = _finish(o_ref[...], gate_ref, wout_ref, x_ref[0], mod_ref, gpost_ref)


def _swa_attn(q, k, v, sink_cols, gate, wout, xs, modl, gpost):
    bsz, t, qw2 = q.shape
    kw = k.shape[-1]
    qw = qw2 // 2
    return pl.pallas_call(
        _swa_attn_kernel,
        grid=(bsz, t // TM),
        in_specs=[pl.BlockSpec((1, TM, qw2), lambda b, i: (b, i, 0)),
                  pl.BlockSpec((1, t, kw), lambda b, i: (b, 0, 0)),
                  pl.BlockSpec((1, t, 2 * kw), lambda b, i: (b, 0, 0)),
                  pl.BlockSpec(sink_cols.shape, lambda b, i: (0, 0, 0)),
                  _tok_spec(qw), _const_spec(wout.shape), _tok_spec(D), _mod_spec(), _const_spec((1, D))],
        out_specs=_tok_spec(D),
        out_shape=jax.ShapeDtypeStruct(xs.shape, F32),
        scratch_shapes=[pltpu.VMEM((TM, qw), BF16)],
        input_output_aliases={6: 0},
        compiler_params=_cparams(2),
        name="swa_attention_out",
    )(q, k, v, sink_cols, gate, wout, xs, modl, gpost)


HP = TM // 2


def _hyena_proj_kernel(xp_ref, x_ref, xn_ref, mod_ref, gpre_ref, win_ref, cw_ref, cb_ref, perm_ref,
                       ue_ref, uo_ref, ge_ref, go_ref):
    t = pl.program_id(1)
    nt = pl.num_programs(1)
    hh = _dot(perm_ref[...], _modnorm(x_ref[0], mod_ref, gpre_ref).astype(BF16))
    h = jnp.concatenate([_modnorm(xp_ref[0], mod_ref, gpre_ref), hh, _modnorm(xn_ref[0], mod_ref, gpre_ref)], axis=0)
    p = _dot(h.astype(BF16), win_ref[...])
    cwid = 3 * C_WIDTH
    u = p[:, :cwid]
    o0, e0 = HALO, HALO + HP
    go_ref[0] = p[o0:o0 + HP, cwid:].astype(BF16)
    ge_ref[0] = p[e0:e0 + HP, cwid:].astype(BF16)
    po, pe = u[o0:o0 + HP], u[e0:e0 + HP]
    po_prev = pltpu.roll(u[0:e0], 1, 0)[o0:e0]
    pe_next = pltpu.roll(u[e0:], HP + HALO - 1, 0)[0:HP]
    r = lax.broadcasted_iota(jnp.int32, (HP, 1), 0)
    po_prev = jnp.where(jnp.logical_and(r == 0, t <= 1), 0.0, po_prev)
    pe_next = jnp.where(jnp.logical_and(r == HP - 1, jnp.logical_or(t == 0, t == nt - 1)), 0.0, pe_next)
    cw = cw_ref[...]
    cb = cb_ref[...]
    ue_ref[0] = (cb + po_prev * cw[0:1] + pe * cw[1:2] + po * cw[2:3]).astype(BF16)
    uo_ref[0] = (cb + pe * cw[0:1] + po * cw[1:2] + pe_next * cw[2:3]).astype(BF16)


def _half_spec(width):
    return pl.BlockSpec((1, HP, width), lambda b, t: (b, t, 0))


def _hyena_proj(xs, modl, gpre, win, conv_w, conv_b):
    bsz, t, _ = xs.shape
    nt = t // TM
    per = TM // HALO
    last = t // HALO - 1
    half = lambda w: jax.ShapeDtypeStruct((bsz, t // 2, w), BF16)
    return pl.pallas_call(
        _hyena_proj_kernel,
        grid=(bsz, nt),
        in_specs=[pl.BlockSpec((1, HALO, D), lambda b, i: (b, jnp.maximum(i * per - 1, 0), 0)),
                  _tok_spec(D),
                  pl.BlockSpec((1, HALO, D), lambda b, i: (b, jnp.minimum((i + 1) * per, last), 0)),
                  _mod_spec(), _const_spec((1, D)), _const_spec(win.shape),
                  _const_spec(conv_w.shape), _const_spec(conv_b.shape), _const_spec((TM, TM))],
        out_specs=[_half_spec(3 * C_WIDTH), _half_spec(3 * C_WIDTH), _half_spec(C_WIDTH), _half_spec(C_WIDTH)],
        out_shape=[half(3 * C_WIDTH), half(3 * C_WIDTH), half(C_WIDTH), half(C_WIDTH)],
        compiler_params=_cparams(2),
        name="hyena_proj",
    )(xs, xs, xs, modl, gpre, win, conv_w, conv_b, _parity_perm(odd_first=True))


def _parity_perm(odd_first):
    r = np.arange(HP)
    src = np.concatenate([2 * r + 1, 2 * r]) if odd_first else np.concatenate([2 * r, 2 * r + 1])
    m = np.zeros((TM, TM), np.float32)
    m[np.arange(TM), src] = 1.0
    return jnp.asarray(m, BF16)


def _hyena_out_kernel(oe_ref, oo_ref, ge_ref, go_ref, wout_ref, x_ref, mod_ref, gpost_ref, perm_ref, xo_ref):
    a = jnp.concatenate([oe_ref[0].astype(F32) * _silu(ge_ref[0].astype(F32)),
                         oo_ref[0].astype(F32) * _silu(go_ref[0].astype(F32))], axis=0)
    a = _dot(perm_ref[...], a.astype(BF16)).astype(BF16)
    y = _dot(a, wout_ref[...])
    xo_ref[0] = x_ref[0] + mod_ref[0, 0][2:3] * (_rms(y) * gpost_ref[...])


def _hyena_out_proj(oe, oo, ge, go, wout, xs, modl, gpost):
    bsz, t, _ = xs.shape
    w = oe.shape[-1]
    return pl.pallas_call(
        _hyena_out_kernel,
        grid=(bsz, t // TM),
        in_specs=[_half_spec(w), _half_spec(w), _half_spec(w), _half_spec(w), _const_spec(wout.shape),
                  _tok_spec(D), _mod_spec(), _const_spec((1, D)), _const_spec((TM, TM))],
        out_specs=_tok_spec(D),
        out_shape=jax.ShapeDtypeStruct(xs.shape, F32),
        input_output_aliases={5: 0},
        compiler_params=_cparams(2),
        name="hyena_out_proj_residual",
    )(oe, oo, ge, go, wout, xs, modl, gpost, _parity_perm(odd_first=False).T)


def _filter_kernel(z_ref, w1_ref, b1_ref, fr_ref, w2_ref, b2_ref, w3_ref, dl_ref, o_ref):
    hp = lax.Precision.HIGHEST
    z = z_ref[...]
    fr = fr_ref[...]
    h = jnp.sin(fr * (jnp.dot(z, w1_ref[...], precision=hp, preferred_element_type=F32) + b1_ref[...]))
    h = jnp.sin(fr * (jnp.dot(h, w2_ref[...], precision=hp, preferred_element_type=F32) + b2_ref[...]))
    h = jnp.dot(h, w3_ref[...], precision=hp, preferred_element_type=F32)
    o_ref[...] = (h * jnp.exp(-z[:, 0:1] * dl_ref[...])).astype(o_ref.dtype)


def _filters(n, w1, b1, fr, w2, b2, w3):
    t = np.linspace(0.0, 1.0, n, dtype=np.float32)[:, None]
    w = ((2.0 * math.pi / n) * np.arange(n, dtype=np.float32))[:, None].astype(np.float32)
    bands = np.linspace(1e-4, C_BANDS - 1, C_BANDS, dtype=np.float32)[None, :]
    z = np.zeros((n, LANE), np.float32)
    z[:, 0:1] = t
    z[:, 1:1 + C_BANDS] = np.cos(bands * w)
    z[:, 1 + C_BANDS:C_EMB] = -np.sin(bands * w)
    deltas = np.abs(np.linspace(C_MIN_DECAY, C_MAX_DECAY, C_WIDTH, dtype=np.float32))
    dl = np.tile(deltas, 4)[None, :]

    def pad(a, r, c):
        return jnp.zeros((r, c), F32).at[:a.shape[0], :a.shape[1]].set(a)

    tn = min(n, TM)
    nout = 4 * C_WIDTH
    cs = lambda shape: pl.BlockSpec(shape, lambda i: (0, 0))
    return pl.pallas_call(
        _filter_kernel,
        grid=(n // tn,),
        in_specs=[pl.BlockSpec((tn, LANE), lambda i: (i, 0)), cs((LANE, LANE)), cs((1, LANE)), cs((1, LANE)),
                  cs((LANE, LANE)), cs((1, LANE)), cs((LANE, nout)), cs((1, nout))],
        out_specs=pl.BlockSpec((tn, nout), lambda i: (i, 0)),
        out_shape=jax.ShapeDtypeStruct((n, nout), BF16),
        compiler_params=_cparams(1),
        name="hyena_filter_mlp",
    )(jnp.asarray(z), pad(w1, LANE, LANE), pad(b1[None], 1, LANE), pad(fr[None], 1, LANE),
      pad(w2, LANE, LANE), pad(b2[None], 1, LANE), pad(w3, LANE, nout), jnp.asarray(dl))


def _dft_matrix(n):
    f = np.arange(n, dtype=np.int64)[:, None]
    s = np.arange(n, dtype=np.int64)[None, :]
    ang = (2.0 * np.pi / (2 * n)) * ((f * s) % (2 * n)).astype(np.float64)
    cos = np.cos(ang)
    sin = np.sin(ang)
    sin[0, :] = np.where(np.arange(n) % 2 == 0, 1.0, -1.0)
    return cos.astype(np.float32), sin.astype(np.float32)


def _matmul_kernel(a_ref, b_ref, o_ref):
    o_ref[...] = _dot(a_ref[...], b_ref[...])


def _matmul(a, b, bm, bn):
    m, k = a.shape
    n = b.shape[1]
    assert m % bm == 0 and n % bn == 0
    return pl.pallas_call(
        _matmul_kernel,
        grid=(m // bm, n // bn),
        in_specs=[pl.BlockSpec((bm, k), lambda i, j: (i, 0)), pl.BlockSpec((k, bn), lambda i, j: (0, j))],
        out_specs=pl.BlockSpec((bm, bn), lambda i, j: (i, j)),
        out_shape=jax.ShapeDtypeStruct((m, n), F32),
        compiler_params=_cparams(2),
        name="filter_dft",
    )(a, b)


def _hyena_conv_kernel(ve_ref, x1e_ref, x2e_ref, vo_ref, x1o_ref, x2o_ref, me_ref, mo_ref, met_ref, mot_ref,
                       kr1_ref, ki1_ref, kr2_ref, ki2_ref, kh_ref, fb_ref, oe_ref, oo_ref,
                       ze_ref, zo_ref, ye_ref, yo_ref, *, row0, h, fbs):
    o = pl.program_id(2)
    fb = pl.program_id(3)
    nfb = pl.num_programs(3)
    rows = slice(row0, row0 + h)

    @pl.when(jnp.logical_and(o == 0, fb == 0))
    def _():
        ze_ref[...] = ve_ref[0, rows, :]
        zo_ref[...] = vo_ref[0, rows, :]

    @pl.when(fb == 0)
    def _():
        ye_ref[...] = jnp.zeros_like(ye_ref)
        yo_ref[...] = jnp.zeros_like(yo_ref)

    first = jnp.logical_and(lax.broadcasted_iota(jnp.int32, (8, 1), 0) == 0, fb == 0)
    for cs in (slice(0, MXU_W), slice(MXU_W, 2 * MXU_W)):
        ge = _dot(me_ref[0], ze_ref[:, cs])
        go = _dot(mo_ref[0], zo_ref[:, cs])
        ce, se, co, so = ge[:fbs], ge[fbs:], go[:fbs], go[fbs:]
        xr1, xs1, xr2, xs2 = ce + co, se + so, ce - co, so - se
        kr1, ki1, kr2, ki2 = kr1_ref[0, :, cs], ki1_ref[0, :, cs], kr2_ref[0, :, cs], ki2_ref[0, :, cs]
        a1, b1 = xr1 * kr1 + xs1 * ki1, xs1 * kr1 - xr1 * ki1
        a2, b2 = xr2 * kr2 + xs2 * ki2, xs2 * kr2 - xr2 * ki2
        bm, bp = b1 - b2, b1 + b2
        krh, kih = kh_ref[0, 0:1, cs], kh_ref[0, 1:2, cs]
        se8, so8 = se[0:8], so[0:8]
        bm = jnp.concatenate([jnp.where(first, se8 * krh + so8 * kih, bm[0:8]), bm[8:]], axis=0)
        bp = jnp.concatenate([jnp.where(first, so8 * krh - se8 * kih, bp[0:8]), bp[8:]], axis=0)
        we = jnp.concatenate([a1 + a2, bm], axis=0).astype(BF16)
        wo = jnp.concatenate([a1 - a2, bp], axis=0).astype(BF16)
        ye_ref[:, cs] += _dot(met_ref[0], we)
        yo_ref[:, cs] += _dot(mot_ref[0], wo)

    @pl.when(fb == nfb - 1)
    def _():
        fe = ye_ref[...] + ze_ref[...].astype(F32) * fb_ref[0]
        fo = yo_ref[...] + zo_ref[...].astype(F32) * fb_ref[0]

        @pl.when(o == 0)
        def _():
            ze_ref[...] = (x1e_ref[0, rows, :].astype(F32) * fe).astype(BF16)
            zo_ref[...] = (x1o_ref[0, rows, :].astype(F32) * fo).astype(BF16)

        @pl.when(o == 1)
        def _():
            if row0 > 0:
                oe_ref[0, 0:row0, :] = jnp.zeros((row0, oe_ref.shape[2]), oe_ref.dtype)
                oo_ref[0, 0:row0, :] = jnp.zeros((row0, oo_ref.shape[2]), oo_ref.dtype)
            oe_ref[0, rows, :] = (x2e_ref[0, rows, :].astype(F32) * fe).astype(BF16)
            oo_ref[0, rows, :] = (x2o_ref[0, rows, :].astype(F32) * fo).astype(BF16)


def _hyena_conv_aliased_kernel(*refs, **kw):
    _hyena_conv_kernel(*refs[:16], *refs[18:], **kw)


def _hyena_conv(ue, uo, mats, planes, fbias, *, row0, h, block_rows, prev=None):
    bsz, t2, _ = ue.shape
    tc = 2 * MXU_W
    nct = C_WIDTH // tc
    me, mo, met, mot = mats
    nfb = me.shape[0]
    fbs = me.shape[1] // 2
    u_spec = lambda which: pl.BlockSpec((1, block_rows, tc), lambda b, c, o, f: (b, 0, which * nct + c))
    m_spec = pl.BlockSpec((1, 2 * fbs, h), lambda b, c, o, f: (f, 0, 0))
    mt_spec = pl.BlockSpec((1, h, 2 * fbs), lambda b, c, o, f: (f, 0, 0))
    p_spec = pl.BlockSpec((1, fbs, tc), lambda b, c, o, f: (o, f, c))
    in_specs = [u_spec(0), u_spec(1), u_spec(2), u_spec(0), u_spec(1), u_spec(2),
                m_spec, m_spec, mt_spec, mt_spec, p_spec, p_spec, p_spec, p_spec,
                pl.BlockSpec((1, 2, tc), lambda b, c, o, f: (o, 0, c)),
                pl.BlockSpec((1, 1, tc), lambda b, c, o, f: (o, 0, c))]
    args = [ue, ue, ue, uo, uo, uo, me, mo, met, mot, *planes, fbias]
    aliases = {}
    kern = functools.partial(_hyena_conv_kernel, row0=row0, h=h, fbs=fbs)
    if prev is not None:
        in_specs += [pl.BlockSpec(memory_space=pl.ANY)] * 2
        args += list(prev)
        aliases = {16: 0, 17: 1}
        kern = functools.partial(_hyena_conv_aliased_kernel, row0=row0, h=h, fbs=fbs)
    o_spec = pl.BlockSpec((1, block_rows, tc), lambda b, c, o, f: (b, 0, c))
    return pl.pallas_call(
        kern,
        grid=(bsz, nct, 2, nfb),
        in_specs=in_specs,
        out_specs=[o_spec, o_spec],
        out_shape=[jax.ShapeDtypeStruct((bsz, t2, C_WIDTH), BF16)] * 2,
        scratch_shapes=[pltpu.VMEM((h, tc), BF16), pltpu.VMEM((h, tc), BF16),
                        pltpu.VMEM((h, tc), F32), pltpu.VMEM((h, tc), F32)],
        input_output_aliases=aliases,
        compiler_params=_cparams(4),
        name="hyena_long_conv_h%d" % h,
    )(*args)


def _radix2_matrices(n):
    h = n // 2
    fbs = min(h, 256)
    nfb = h // fbs
    f = np.arange(h, dtype=np.int64)[:, None]
    r = np.arange(h, dtype=np.int64)[None, :]
    alt = np.where(np.arange(h) % 2 == 0, 1.0, -1.0)
    out = []
    for pos in (2 * r, 2 * r + 1):
        ang = (2.0 * np.pi / (2 * n)) * ((f * pos) % (2 * n)).astype(np.float64)
        cos, sin = np.cos(ang), np.sin(ang)
        sin[0, :] = alt
        blocks = np.concatenate([cos.reshape(nfb, fbs, h), sin.reshape(nfb, fbs, h)], axis=1)
        out.append(blocks.astype(np.float32))
    me, mo = out
    tr = lambda a: np.ascontiguousarray(np.transpose(a, (0, 2, 1)))
    return tuple(jnp.asarray(a, BF16) for a in (me, mo, tr(me), tr(mo)))


def _hyena_spectra(n, filt):
    h = n // 2
    cos, sin = _dft_matrix(n)
    mirror = np.concatenate([[0], np.arange(n - 1, h, -1)])
    rows = np.concatenate([cos[:h], np.concatenate([sin[:1], cos[mirror[1:]]]), sin[:h], sin[mirror],
                           cos[h:h + 1], sin[h:h + 1], np.zeros((LANE - 2, n), np.float32)], axis=0)
    nr = 4 * h + LANE
    hspec = _matmul(jnp.asarray(rows, BF16), filt, 3 * LANE if nr % (3 * LANE) == 0 else nr, 1024)
    hf = hspec[:, :2 * C_WIDTH].reshape(nr, 2, C_WIDTH)
    hb = hspec[:, 2 * C_WIDTH:].reshape(nr, 2, C_WIDTH)
    w = np.full((h, 1, 1), 2.0 / (2 * n), np.float32)
    w[0] = 1.0 / (2 * n)
    nz = (np.arange(h) != 0)[:, None, None]
    tr = lambda a: jnp.transpose(a, (1, 0, 2))
    kr1 = (hf[:h] + hb[:h]) * w
    kr2 = (hf[h:2 * h] + hb[h:2 * h]) * w
    ki1 = jnp.where(nz, (hb[2 * h:3 * h] - hf[2 * h:3 * h]) * w, 0.0)
    ki2 = jnp.where(nz, (hb[3 * h:4 * h] - hf[3 * h:4 * h]) * w, 0.0)
    kh = jnp.stack([hf[4 * h] + hb[4 * h], hb[4 * h + 1] - hf[4 * h + 1]], axis=1) * (2.0 / (2 * n))
    return (tr(kr1), tr(ki1), tr(kr2), tr(ki2), kh)


def _rope_tables(seq, layout):
    rows = seq // GRID_W
    row = np.repeat(np.arange(rows, dtype=np.float32), GRID_W)
    col = np.tile(np.arange(GRID_W, dtype=np.float32), rows)
    per_axis = 32
    inv = (ROPE_BASE ** (-np.arange(0, per_axis, 2, dtype=np.float32) / per_axis)).astype(np.float32)
    ang = np.concatenate([row[:, None] * inv, col[:, None] * inv], axis=-1)
    cos = np.concatenate([np.ones((CTX, 32), np.float32), np.cos(ang)], axis=0)
    sin = np.concatenate([np.zeros((CTX, 32), np.float32), np.sin(ang)], axis=0)
    one, zero = np.ones_like(cos), np.zeros_like(cos)
    if layout == "mla":
        a = [cos, cos, one, one]
        b = [zero, sin, zero, zero]
        c = [-sin, zero, zero, zero]
    else:
        a = [cos, cos, cos, cos]
        b = [zero, sin, zero, sin]
        c = [-sin, zero, -sin, zero]
    return tuple(jnp.asarray(np.concatenate(p, axis=1), F32) for p in (a, b, c))


def _swa_head_perm():
    cols = []
    for j in range(B_KV_HEADS // 2):
        for g in range(B_GROUP):
            for hk in (2 * j, 2 * j + 1):
                h = hk * B_GROUP + g
                cols.extend(range(h * B_HDIM, (h + 1) * B_HDIM))
    return np.asarray(cols, np.int32)


def _mla_weights(w_in, w_q, w_kv):
    c1 = A_Q_RANK + A_KV_RANK
    zpad = jnp.zeros((D, LANE - A_ROPE), F32)
    win = jnp.concatenate([w_in[:, :c1 + A_ROPE], zpad, w_in[:, c1 + A_ROPE:]], axis=1)
    qscale = (A_NOPE + A_ROPE) ** -0.5 * math.log2(math.e)
    wq = w_q.reshape(A_Q_RANK, A_HEADS, A_NOPE + A_ROPE) * qscale
    wq = jnp.concatenate([wq, jnp.zeros((A_Q_RANK, A_HEADS, A_HEAD_PAD - A_NOPE - A_ROPE), F32)], axis=-1)
    wq = wq.reshape(A_Q_RANK, A_HEADS * A_HEAD_PAD)
    return win.astype(BF16), wq.astype(BF16), w_kv.astype(BF16)


def kernel(x, c, ctx, c_ctx, w_mod, b_mod, g_pre, g_post, a_w_in, a_g_q, a_w_q, a_g_kv, a_w_kv, a_w_out, b_w_in, b_sink, b_w_out, c_w_in, c_conv_w, c_conv_b, c_f_w1, c_f_b1, c_f_freq, c_f_w2, c_f_b2, c_f_w3, c_filt_bias, c_w_out):
    bsz, seq, _ = x.shape
    assert ctx.shape[1] == CTX and (CTX + seq) % TP == 0 and seq % GRID_W == 0
    xsrc = (ctx, x)

    pad_rows = (-(bsz + 1)) % 8
    cond = jnp.concatenate([c, c_ctx[None], jnp.zeros((pad_rows, D), F32)], axis=0)
    mod = _modulation(cond, w_mod, b_mod)

    tabs_mla = _rope_tables(seq, "mla")
    tabs_swa = _rope_tables(seq, "swa")

    for layer in range(DEPTH):
        kind, j = layer % 3, layer // 3
        mx = mod[layer, :bsz].reshape(bsz, 3, D)
        mc = jnp.broadcast_to(mod[layer, bsz].reshape(1, 3, D), (bsz, 3, D))
        modl = jnp.stack([mc, mx], axis=0)
        gpre = g_pre[layer][None]
        gpost = g_post[layer][None]
        if kind == 0:
            win, wq, wkv = _mla_weights(a_w_in[j], a_w_q[j], a_w_kv[j])
            q, k, v, gate = _mla_proj(xsrc, modl, gpre, win, a_g_q[j][None], wq, a_g_kv[j][None], wkv, tabs_mla)
            xs = _mla_attn(q, k, v, gate, a_w_out[j].astype(BF16), xsrc, modl, gpost,
                           latents_only=layer == DEPTH - 1)
        elif kind == 1:
            xs, = xsrc
            perm = _swa_head_perm()
            qw, kw = B_HEADS * B_HDIM, B_KV_HEADS * B_HDIM
            w = b_w_in[j]
            win = jnp.concatenate([w[:, :qw][:, perm] * (B_HDIM ** -0.5 * math.log2(math.e)), w[:, qw:qw + 2 * kw],
                                   w[:, qw + 2 * kw:][:, perm]], axis=1).astype(BF16)
            q, k, v, gate = _swa_proj(xs, modl, gpre, win, tabs_swa)
            sink = (b_sink[j].astype(F32) * math.log2(math.e)).reshape(B_KV_HEADS, B_GROUP, 1, 1)
            sink_cols = jnp.broadcast_to(sink, (B_KV_HEADS, B_GROUP, B_BLOCK, 1)).reshape(B_KV_HEADS, B_GROUP * B_BLOCK, 1)
            xs = _swa_attn(q, k, v, sink_cols, gate, b_w_out[j][perm, :].astype(BF16), xs, modl, gpost)
        else:
            xs, = xsrc
            ue, uo, ge, go = _hyena_proj(xs, modl, gpre, c_w_in[j].astype(BF16), c_conv_w[j], c_conv_b[j][None])
            fargs = (c_f_w1[j], c_f_b1[j], c_f_freq[j], c_f_w2[j], c_f_b2[j], c_f_w3[j])
            fbias = c_filt_bias[j].reshape(2, 1, C_WIDTH)
            oeo = _hyena_conv(ue, uo, _radix2_matrices(seq), _hyena_spectra(seq, _filters(seq, *fargs)), fbias,
                              row0=CTX // 2, h=seq // 2, block_rows=(CTX + seq) // 2)
            oe, oo = _hyena_conv(ue, uo, _radix2_matrices(CTX), _hyena_spectra(CTX, _filters(CTX, *fargs)), fbias,
                                 row0=0, h=CTX // 2, block_rows=CTX // 2, prev=oeo)
            xs = _hyena_out_proj(oe, oo, ge, go, c_w_out[j].astype(BF16), xs, modl, gpost)
        xsrc = (xs,)
    return xs
```

```python
import functools
import math

import numpy as np
import jax
import jax.numpy as jnp
from jax import lax
from jax.experimental import pallas as pl
from jax.experimental.pallas import tpu as pltpu

F32 = jnp.float32
BF16 = jnp.bfloat16

D = 1024
DEPTH = 4
GRID_W = 64
CTX = 256
NORM_EPS = 1e-6
ROPE_BASE = 10000.0
NEG_INF = -1e30

A_HEADS = 8
A_Q_RANK = 512
A_KV_RANK = 256
A_NOPE = 128
A_ROPE = 64
A_VDIM = 128
A_HEAD_PAD = 256

B_HEADS = 16
B_KV_HEADS = 4
B_GROUP = 4
B_HDIM = 64
B_WINDOW = 128
B_BLOCK = 128

C_WIDTH = 1024
C_BANDS = 16
C_EMB = 1 + 2 * C_BANDS
C_FFN = 64
C_MIN_DECAY = math.log(1e-2) / 1.5
C_MAX_DECAY = math.log(1e-2) / 0.3

LANE = 128
MXU_W = 256
TM = 256
TP = 3 * TM
HALO = 8
VMEM_LIMIT = 56 * 1024 * 1024


def _cparams(n_axes):
    return pltpu.CompilerParams(dimension_semantics=("arbitrary",) * n_axes,
                                vmem_limit_bytes=VMEM_LIMIT)


def _rms(x):
    return x * lax.rsqrt(jnp.mean(x * x, axis=-1, keepdims=True) + NORM_EPS)


def _silu(g):
    return g / (1.0 + jnp.exp(-g))


def _dot(a, b):
    return jnp.dot(a, b, preferred_element_type=F32)


def _dot_nt(a, b):
    return lax.dot_general(a, b, (((1,), (1,)), ((), ())), preferred_element_type=F32)


def _rope128(x, a, b, c):
    return x * a + pltpu.roll(x, 32, 1) * b + pltpu.roll(x, LANE - 32, 1) * c


def _modnorm(x, mod_ref, gpre_ref):
    m = mod_ref[0, 0]
    return _rms(x) * gpre_ref[...] * (1.0 + m[1:2]) + m[0:1]


def _mod_kernel(c_ref, w_ref, b_ref, o_ref):
    a = _silu(c_ref[...])
    o_ref[0] = _dot(a.astype(BF16), w_ref[0].astype(BF16)) + b_ref[0]


def _modulation(cond, w_mod, b_mod):
    rows = cond.shape[0]
    return pl.pallas_call(
        _mod_kernel,
        grid=(DEPTH, 3),
        in_specs=[pl.BlockSpec((rows, D), lambda l, j: (0, 0)),
                  pl.BlockSpec((1, D, D), lambda l, j: (l, 0, j)),
                  pl.BlockSpec((1, 1, D), lambda l, j: (l, 0, j))],
        out_specs=pl.BlockSpec((1, rows, D), lambda l, j: (l, 0, j)),
        out_shape=jax.ShapeDtypeStruct((DEPTH, rows, 3 * D), F32),
        compiler_params=_cparams(2),
        name="adaln_modulation",
    )(cond, w_mod, b_mod.reshape(DEPTH, 1, 3 * D))


def _tok_spec(width, rows=TM, skip=0):
    return pl.BlockSpec((1, rows, width), lambda b, t: (b, t + skip, 0))


def _mod_spec(ctx_tiles=1, skip=0):
    return pl.BlockSpec((1, 1, 3, D), lambda b, t: (jnp.minimum((t + skip) // ctx_tiles, 1), b, 0, 0))


def _const_spec(shape):
    nd = len(shape)
    return pl.BlockSpec(shape, lambda b, t: (0,) * nd)


def _residual_specs(dual, rows=TM, skip=0):
    if not dual:
        return [_tok_spec(D, rows, skip)]
    per = CTX // rows
    return [pl.BlockSpec((1, rows, D), lambda b, t: (b, jnp.minimum(t, per - 1), 0)),
            pl.BlockSpec((1, rows, D), lambda b, t: (b, jnp.maximum(t - per, 0), 0))]


def _wide_specs(dual):
    n = TP // TM
    if not dual:
        return [pl.BlockSpec((1, TM, D), functools.partial(lambda j, b, t: (b, n * t + j, 0), j)) for j in range(n)]
    last = lambda j, b, t: (b, jnp.maximum(n * t + j - CTX // TM, 0), 0)
    return ([pl.BlockSpec((1, CTX, D), lambda b, t: (b, 0, 0))]
            + [pl.BlockSpec((1, TM, D), functools.partial(last, j)) for j in range(n)])


def _wide_modnorm(refs, dual, mod_ref, gpre_ref):
    t = pl.program_id(1)
    blocks = [r[0] for r in refs[1:]] if dual else [r[0] for r in refs]
    if dual:
        blocks[0] = jnp.where(t == 0, refs[0][0], blocks[0])
    x = jnp.concatenate(blocks, axis=0)
    is_ctx = jnp.logical_and(lax.broadcasted_iota(jnp.int32, (TP, 1), 0) < CTX, t == 0)
    mc, mx = mod_ref[0, 0], mod_ref[1, 0]
    scale = jnp.where(is_ctx, mc[1:2], mx[1:2])
    shift = jnp.where(is_ctx, mc[0:1], mx[0:1])
    return _rms(x) * gpre_ref[...] * (1.0 + scale) + shift


def _both_mod_spec():
    return pl.BlockSpec((2, 1, 3, D), lambda b, t: (0, b, 0, 0))


def _resident(shape):
    nd = len(shape)
    return pl.BlockSpec(shape, lambda b, t: (0,) * nd, pipeline_mode=pl.Buffered(1))


def _residual_tile(refs, dual, rows=TM):
    if not dual:
        return refs[0][0]
    return jnp.where(pl.program_id(1) < CTX // rows, refs[0][0], refs[1][0])


def _finish(o, gate_ref, wout_ref, x, mod_ref, gpost_ref):
    a = o.astype(F32) * _silu(gate_ref[0].astype(F32))
    y = _dot(a.astype(BF16), wout_ref[...])
    return x + mod_ref[0, 0][2:3] * (_rms(y) * gpost_ref[...])


def _mla_proj_kernel(*refs, dual):
    nx = TP // TM + (1 if dual else 0)
    (mod_ref, gpre_ref, win_ref, gq_ref, wq_ref, gkv_ref, wkv_ref,
     ra_ref, rb_ref, rc_ref, q_ref, k_ref, v_ref, gate_ref) = refs[nx:]
    h = _wide_modnorm(refs[:nx], dual, mod_ref, gpre_ref)
    p = _dot(h.astype(BF16), win_ref[...])
    c0, c1, c2 = A_Q_RANK, A_Q_RANK + A_KV_RANK, A_Q_RANK + A_KV_RANK + LANE
    gate_ref[0] = p[:, c2:].astype(BF16)
    qn = (_rms(p[:, :c0]) * gq_ref[...]).astype(BF16)
    kvn = (_rms(p[:, c0:c1]) * gkv_ref[...]).astype(BF16)
    ra, rb, rc = ra_ref[...], rb_ref[...], rc_ref[...]
    kr = _rope128(p[:, c1:c2], ra, rb, rc).astype(BF16)
    for hd in range(A_HEADS):
        q = _dot(qn, wq_ref[:, hd * A_HEAD_PAD:(hd + 1) * A_HEAD_PAD])
        q_ref[0, hd, :, 0:LANE] = q[:, 0:LANE].astype(BF16)
        q_ref[0, hd, :, LANE:2 * LANE] = _rope128(q[:, LANE:2 * LANE], ra, rb, rc).astype(BF16)
        kv = _dot(kvn, wkv_ref[:, hd * A_HEAD_PAD:(hd + 1) * A_HEAD_PAD])
        k_ref[0, hd, :, 0:LANE] = kv[:, 0:A_NOPE].astype(BF16)
        k_ref[0, hd, :, LANE:2 * LANE] = kr
        v_ref[0, hd] = kv[:, A_NOPE:].astype(BF16)


def _mla_proj(xsrc, modl, gpre, win, gq, wq, gkv, wkv, tabs):
    dual = len(xsrc) == 2
    bsz = xsrc[0].shape[0]
    t = tabs[0].shape[0]
    xargs = [xsrc[0]] + [xsrc[1]] * (TP // TM) if dual else [xsrc[0]] * (TP // TM)
    head = lambda w: pl.BlockSpec((1, A_HEADS, TP, w), lambda b, i: (b, 0, i, 0))
    tab = pl.BlockSpec((TP, LANE), lambda b, i: (i, 0))
    return pl.pallas_call(
        functools.partial(_mla_proj_kernel, dual=dual),
        grid=(bsz, t // TP),
        in_specs=[*_wide_specs(dual), _both_mod_spec(), _const_spec((1, D)), _resident(win.shape),
                  _const_spec((1, A_Q_RANK)), _resident(wq.shape),
                  _const_spec((1, A_KV_RANK)), _resident(wkv.shape), tab, tab, tab],
        out_specs=[head(A_HEAD_PAD), head(A_HEAD_PAD), head(A_VDIM), _tok_spec(A_HEADS * A_VDIM, TP)],
        out_shape=[jax.ShapeDtypeStruct((bsz, A_HEADS, t, A_HEAD_PAD), BF16),
                   jax.ShapeDtypeStruct((bsz, A_HEADS, t, A_HEAD_PAD), BF16),
                   jax.ShapeDtypeStruct((bsz, A_HEADS, t, A_VDIM), BF16),
                   jax.ShapeDtypeStruct((bsz, t, A_HEADS * A_VDIM), BF16)],
        compiler_params=_cparams(2),
        name="mla_proj",
    )(*xargs, modl, gpre, win, gq, wq, gkv, wkv, *tabs)


def _mla_attn_kernel(*refs, dual, latents_only):
    nx = 2 if dual else 1
    q_ref, k_ref, v_ref, gate_ref, wout_ref = refs[:5]
    mod_ref, gpost_ref, xo_ref, o_ref = refs[5 + nx:]

    def attend(nk):
        s_next = _dot_nt(q_ref[0, 0], k_ref[0, 0, :nk, :])
        for hd in range(A_HEADS):
            s = s_next
            if hd + 1 < A_HEADS:
                s_next = _dot_nt(q_ref[0, hd + 1], k_ref[0, hd + 1, :nk, :])
            m = jnp.max(s, axis=-1, keepdims=True)
            p = jnp.exp2(s - m)
            l = jnp.sum(p, axis=-1, keepdims=True)
            o = _dot(p.astype(BF16), v_ref[0, hd, :nk, :]) / l
            o_ref[:, hd * A_VDIM:(hd + 1) * A_VDIM] = o.astype(BF16)

    t_all = k_ref.shape[2]
    if latents_only:
        attend(t_all)
    else:
        is_ctx = pl.program_id(1) == 0
        pl.when(is_ctx)(lambda: attend(CTX))
        pl.when(jnp.logical_not(is_ctx))(lambda: attend(t_all))
    x = _residual_tile(refs[5:5 + nx], dual)
    xo_ref[0] = _finish(o_ref[...], gate_ref, wout_ref, x, mod_ref, gpost_ref)


def _mla_attn(q, k, v, gate, wout, xsrc, modl, gpost, latents_only):
    dual = len(xsrc) == 2
    bsz, nh, t, _ = q.shape
    skip = CTX // TM if latents_only else 0
    nt = t // TM - skip
    return pl.pallas_call(
        functools.partial(_mla_attn_kernel, dual=dual, latents_only=latents_only),
        grid=(bsz, nt),
        in_specs=[pl.BlockSpec((1, nh, TM, A_HEAD_PAD), lambda b, i: (b, 0, i + skip, 0)),
                  pl.BlockSpec((1, nh, t, A_HEAD_PAD), lambda b, i: (b, 0, 0, 0)),
                  pl.BlockSpec((1, nh, t, A_VDIM), lambda b, i: (b, 0, 0, 0)),
                  _tok_spec(nh * A_VDIM, skip=skip), _const_spec(wout.shape),
                  *_residual_specs(dual, skip=skip), _mod_spec(skip=skip), _const_spec((1, D))],
        out_specs=_tok_spec(D),
        out_shape=jax.ShapeDtypeStruct((bsz, nt * TM, D), F32),
        scratch_shapes=[pltpu.VMEM((TM, nh * A_VDIM), BF16)],
        input_output_aliases={} if (dual or latents_only) else {5: 0},
        compiler_params=_cparams(2),
        name="mla_attention_out",
    )(q, k, v, gate, wout, *xsrc, modl, gpost)


def _swa_proj_kernel(*refs):
    xrefs = refs[:TP // TM]
    mod_ref, gpre_ref, win_ref, ra_ref, rb_ref, rc_ref, q_ref, k_ref, v_ref, gate_ref = refs[TP // TM:]
    h = _wide_modnorm(xrefs, False, mod_ref, gpre_ref)
    p = _dot(h.astype(BF16), win_ref[...])
    ra, rb, rc = ra_ref[...], rb_ref[...], rc_ref[...]
    qw, kw = B_HEADS * B_HDIM, B_KV_HEADS * B_HDIM
    lane = lax.broadcasted_iota(jnp.int32, (1, LANE), 1)
    lo = lane < B_HDIM
    ones_col = jnp.broadcast_to(jnp.where(lane == 0, 1.0, 0.0), (TP, LANE)).astype(BF16)
    for j in range(B_KV_HEADS // 2):
        for g in range(B_GROUP):
            s = j * B_GROUP + g
            r = _rope128(p[:, s * LANE:(s + 1) * LANE], ra, rb, rc)
            for half in range(2):
                d = ((2 * j + half) * B_GROUP + g) * LANE
                keep = lo if half == 0 else jnp.logical_not(lo)
                q_ref[0, :, d:d + LANE] = jnp.where(keep, r, 0.0).astype(BF16)
        o = qw + j * LANE
        k_ref[0, :, j * LANE:(j + 1) * LANE] = _rope128(p[:, o:o + LANE], ra, rb, rc).astype(BF16)
        v_ref[0, :, 2 * j * LANE:(2 * j + 1) * LANE] = p[:, o + kw:o + kw + LANE].astype(BF16)
        v_ref[0, :, (2 * j + 1) * LANE:(2 * j + 2) * LANE] = ones_col
    gate_ref[0] = p[:, qw + 2 * kw:].astype(BF16)


def _swa_proj(xs, modl, gpre, win, tabs):
    bsz, t, _ = xs.shape
    qw, kw = B_HEADS * B_HDIM, B_KV_HEADS * B_HDIM
    tab = pl.BlockSpec((TP, LANE), lambda b, i: (i, 0))
    return pl.pallas_call(
        _swa_proj_kernel,
        grid=(bsz, t // TP),
        in_specs=[*_wide_specs(False), _both_mod_spec(), _const_spec((1, D)), _resident(win.shape), tab, tab, tab],
        out_specs=[_tok_spec(2 * qw, TP), _tok_spec(kw, TP), _tok_spec(2 * kw, TP), _tok_spec(qw, TP)],
        out_shape=[jax.ShapeDtypeStruct((bsz, t, 2 * qw), BF16),
                   jax.ShapeDtypeStruct((bsz, t, kw), BF16),
                   jax.ShapeDtypeStruct((bsz, t, 2 * kw), BF16),
                   jax.ShapeDtypeStruct((bsz, t, qw), BF16)],
        compiler_params=_cparams(2),
        name="swa_proj",
    )(*[xs] * (TP // TM), modl, gpre, win, *tabs)


def _swa_attn_kernel(q_ref, k_ref, v_ref, sink_ref, gate_ref, wout_ref, x_ref, mod_ref, gpost_ref, xo_ref, o_ref):
    i = pl.program_id(1)
    n_lat = k_ref.shape[1] - CTX
    band = 3 * B_BLOCK
    lo = lax.broadcasted_iota(jnp.int32, (1, LANE), 1) < B_HDIM

    def run(with_band):
        chains = [(sub, j, half) for sub in range(TM // B_BLOCK) for j in range(B_KV_HEADS // 2) for half in range(2)]
        window, keys = {}, {}

        def block_window(sub):
            if sub not in window:
                li = (i - CTX // TM) * (TM // B_BLOCK) + sub
                start = jnp.clip((li - 1) * B_BLOCK, 0, n_lat - band)
                off = li * B_BLOCK - start
                d0 = (lax.broadcasted_iota(jnp.int32, (B_BLOCK, band), 0)
                      - lax.broadcasted_iota(jnp.int32, (B_BLOCK, band), 1))
                mask = jnp.abs(d0 + off) <= B_WINDOW
                window[sub] = (mask, pl.multiple_of(CTX + start, B_BLOCK))
            return window[sub]

        def block_keys(sub, j):
            if (sub, j) not in keys:
                kcols = slice(j * LANE, (j + 1) * LANE)
                vcols = slice(2 * j * LANE, (2 * j + 2) * LANE)
                kk = k_ref[0, 0:CTX, kcols]
                vv = v_ref[0, 0:CTX, vcols]
                if with_band:
                    kstart = block_window(sub)[1]
                    kk = jnp.concatenate([kk, k_ref[0, pl.ds(kstart, band), kcols]], axis=0)
                    vv = jnp.concatenate([vv, v_ref[0, pl.ds(kstart, band), vcols]], axis=0)
                keys[(sub, j)] = (kk, vv)
            return keys[(sub, j)]

        def scores(chain):
            sub, j, half = chain
            hk = 2 * j + half
            r0 = sub * B_BLOCK
            qs = jnp.concatenate(
                [q_ref[0, r0:r0 + B_BLOCK, (hk * B_GROUP + g) * LANE:(hk * B_GROUP + g + 1) * LANE]
                 for g in range(B_GROUP)], axis=0)
            return _dot_nt(qs, block_keys(sub, j)[0])

        outs = []
        s_next = scores(chains[0])
        for n, (sub, j, half) in enumerate(chains):
            s = s_next
            if n + 1 < len(chains):
                s_next = scores(chains[n + 1])
            sk = sink_ref[2 * j + half]
            s_c = s[:, :CTX]
            m = jnp.maximum(jnp.max(s_c, axis=-1, keepdims=True), sk)
            if with_band:
                mask = block_window(sub)[0]
                s_b = jnp.concatenate(
                    [jnp.where(mask, s[g * B_BLOCK:(g + 1) * B_BLOCK, CTX:], NEG_INF) for g in range(B_GROUP)],
                    axis=0)
                m = jnp.maximum(m, jnp.max(s_b, axis=-1, keepdims=True))
                p = jnp.concatenate([jnp.exp2(s_c - m), jnp.exp2(s_b - m)], axis=1)
            else:
                p = jnp.exp2(s_c - m)
            pv = _dot(p.astype(BF16), block_keys(sub, j)[1])
            l = pv[:, LANE:LANE + 1] + jnp.exp2(sk - m)
            outs.append(pv[:, :LANE] / l)
            if half == 1:
                comb = jnp.where(lo, outs[-2], outs[-1])
                r0 = sub * B_BLOCK
                for g in range(B_GROUP):
                    s_out = j * B_GROUP + g
                    o_ref[r0:r0 + B_BLOCK, s_out * LANE:(s_out + 1) * LANE] = (
                        comb[g * B_BLOCK:(g + 1) * B_BLOCK].astype(BF16))

    is_ctx = i < CTX // TM
    pl.when(is_ctx)(lambda: run(False))
    pl.when(jnp.logical_not(is_ctx))(lambda: run(True))
    xo_ref[0] = _finish(o_ref[...], gate_ref, wout_ref, x_ref[0], mod_ref, gpost_ref)


def _swa_attn(q, k, v, sink_cols, gate, wout, xs, modl, gpost):
    bsz, t, qw2 = q.shape
    kw = k.shape[-1]
    qw = qw2 // 2
    return pl.pallas_call(
        _swa_attn_kernel,
        grid=(bsz, t // TM),
        in_specs=[pl.BlockSpec((1, TM, qw2), lambda b, i: (b, i, 0)),
                  pl.BlockSpec((1, t, kw), lambda b, i: (b, 0, 0)),
                  pl.BlockSpec((1, t, 2 * kw), lambda b, i: (b, 0, 0)),
                  pl.BlockSpec(sink_cols.shape, lambda b, i: (0, 0, 0)),
                  _tok_spec(qw), _const_spec(wout.shape), _tok_spec(D), _mod_spec(), _const_spec((1, D))],
        out_specs=_tok_spec(D),
        out_shape=jax.ShapeDtypeStruct(xs.shape, F32),
        scratch_shapes=[pltpu.VMEM((TM, qw), BF16)],
        input_output_aliases={6: 0},
        compiler_params=_cparams(2),
        name="swa_attention_out",
    )(q, k, v, sink_cols, gate, wout, xs, modl, gpost)


HP = TM // 2


def _hyena_proj_kernel(xp_ref, x_ref, xn_ref, mod_ref, gpre_ref, win_ref, cw_ref, cb_ref, perm_ref,
                       ue_ref, uo_ref, ge_ref, go_ref):
    t = pl.program_id(1)
    nt = pl.num_programs(1)
    hh = _dot(perm_ref[...], _modnorm(x_ref[0], mod_ref, gpre_ref).astype(BF16))
    h = jnp.concatenate([_modnorm(xp_ref[0], mod_ref, gpre_ref), hh, _modnorm(xn_ref[0], mod_ref, gpre_ref)], axis=0)
    p = _dot(h.astype(BF16), win_ref[...])
    cwid = 3 * C_WIDTH
    u = p[:, :cwid]
    o0, e0 = HALO, HALO + HP
    go_ref[0] = p[o0:o0 + HP, cwid:].astype(BF16)
    ge_ref[0] = p[e0:e0 + HP, cwid:].astype(BF16)
    po, pe = u[o0:o0 + HP], u[e0:e0 + HP]
    po_prev = pltpu.roll(u[0:e0], 1, 0)[o0:e0]
    pe_next = pltpu.roll(u[e0:], HP + HALO - 1, 0)[0:HP]
    r = lax.broadcasted_iota(jnp.int32, (HP, 1), 0)
    po_prev = jnp.where(jnp.logical_and(r == 0, t <= 1), 0.0, po_prev)
    pe_next = jnp.where(jnp.logical_and(r == HP - 1, jnp.logical_or(t == 0, t == nt - 1)), 0.0, pe_next)
    cw = cw_ref[...]
    cb = cb_ref[...]
    ue_ref[0] = (cb + po_prev * cw[0:1] + pe * cw[1:2] + po * cw[2:3]).astype(BF16)
    uo_ref[0] = (cb + pe * cw[0:1] + po * cw[1:2] + pe_next * cw[2:3]).astype(BF16)


def _half_spec(width):
    return pl.BlockSpec((1, HP, width), lambda b, t: (b, t, 0))


def _hyena_proj(xs, modl, gpre, win, conv_w, conv_b):
    bsz, t, _ = xs.shape
    nt = t // TM
    per = TM // HALO
    last = t // HALO - 1
    half = lambda w: jax.ShapeDtypeStruct((bsz, t // 2, w), BF16)
    return pl.pallas_call(
        _hyena_proj_kernel,
        grid=(bsz, nt),
        in_specs=[pl.BlockSpec((1, HALO, D), lambda b, i: (b, jnp.maximum(i * per - 1, 0), 0)),
                  _tok_spec(D),
                  pl.BlockSpec((1, HALO, D), lambda b, i: (b, jnp.minimum((i + 1) * per, last), 0)),
                  _mod_spec(), _const_spec((1, D)), _const_spec(win.shape),
                  _const_spec(conv_w.shape), _const_spec(conv_b.shape), _const_spec((TM, TM))],
        out_specs=[_half_spec(3 * C_WIDTH), _half_spec(3 * C_WIDTH), _half_spec(C_WIDTH), _half_spec(C_WIDTH)],
        out_shape=[half(3 * C_WIDTH), half(3 * C_WIDTH), half(C_WIDTH), half(C_WIDTH)],
        compiler_params=_cparams(2),
        name="hyena_proj",
    )(xs, xs, xs, modl, gpre, win, conv_w, conv_b, _parity_perm(odd_first=True))


def _parity_perm(odd_first):
    r = np.arange(HP)
    src = np.concatenate([2 * r + 1, 2 * r]) if odd_first else np.concatenate([2 * r, 2 * r + 1])
    m = np.zeros((TM, TM), np.float32)
    m[np.arange(TM), src] = 1.0
    return jnp.asarray(m, BF16)


def _hyena_out_kernel(oe_ref, oo_ref, ge_ref, go_ref, wout_ref, x_ref, mod_ref, gpost_ref, perm_ref, xo_ref):
    a = jnp.concatenate([oe_ref[0].astype(F32) * _silu(ge_ref[0].astype(F32)),
                         oo_ref[0].astype(F32) * _silu(go_ref[0].astype(F32))], axis=0)
    a = _dot(perm_ref[...], a.astype(BF16)).astype(BF16)
    y = _dot(a, wout_ref[...])
    xo_ref[0] = x_ref[0] + mod_ref[0, 0][2:3] * (_rms(y) * gpost_ref[...])


def _hyena_out_proj(oe, oo, ge, go, wout, xs, modl, gpost):
    bsz, t, _ = xs.shape
    w = oe.shape[-1]
    return pl.pallas_call(
        _hyena_out_kernel,
        grid=(bsz, t // TM),
        in_specs=[_half_spec(w), _half_spec(w), _half_spec(w), _half_spec(w), _const_spec(wout.shape),
                  _tok_spec(D), _mod_spec(), _const_spec((1, D)), _const_spec((TM, TM))],
        out_specs=_tok_spec(D),
        out_shape=jax.ShapeDtypeStruct(xs.shape, F32),
        input_output_aliases={5: 0},
        compiler_params=_cparams(2),
        name="hyena_out_proj_residual",
    )(oe, oo, ge, go, wout, xs, modl, gpost, _parity_perm(odd_first=False).T)


def _filter_kernel(z_ref, w1_ref, b1_ref, fr_ref, w2_ref, b2_ref, w3_ref, dl_ref, o_ref):
    hp = lax.Precision.HIGHEST
    z = z_ref[...]
    fr = fr_ref[...]
    h = jnp.sin(fr * (jnp.dot(z, w1_ref[...], precision=hp, preferred_element_type=F32) + b1_ref[...]))
    h = jnp.sin(fr * (jnp.dot(h, w2_ref[...], precision=hp, preferred_element_type=F32) + b2_ref[...]))
    h = jnp.dot(h, w3_ref[...], precision=hp, preferred_element_type=F32)
    o_ref[...] = (h * jnp.exp(-z[:, 0:1] * dl_ref[...])).astype(o_ref.dtype)


def _filters(n, w1, b1, fr, w2, b2, w3):
    t = np.linspace(0.0, 1.0, n, dtype=np.float32)[:, None]
    w = ((2.0 * math.pi / n) * np.arange(n, dtype=np.float32))[:, None].astype(np.float32)
    bands = np.linspace(1e-4, C_BANDS - 1, C_BANDS, dtype=np.float32)[None, :]
    z = np.zeros((n, LANE), np.float32)
    z[:, 0:1] = t
    z[:, 1:1 + C_BANDS] = np.cos(bands * w)
    z[:, 1 + C_BANDS:C_EMB] = -np.sin(bands * w)
    deltas = np.abs(np.linspace(C_MIN_DECAY, C_MAX_DECAY, C_WIDTH, dtype=np.float32))
    dl = np.tile(deltas, 4)[None, :]

    def pad(a, r, c):
        return jnp.zeros((r, c), F32).at[:a.shape[0], :a.shape[1]].set(a)

    tn = min(n, TM)
    nout = 4 * C_WIDTH
    cs = lambda shape: pl.BlockSpec(shape, lambda i: (0, 0))
    return pl.pallas_call(
        _filter_kernel,
        grid=(n // tn,),
        in_specs=[pl.BlockSpec((tn, LANE), lambda i: (i, 0)), cs((LANE, LANE)), cs((1, LANE)), cs((1, LANE)),
                  cs((LANE, LANE)), cs((1, LANE)), cs((LANE, nout)), cs((1, nout))],
        out_specs=pl.BlockSpec((tn, nout), lambda i: (i, 0)),
        out_shape=jax.ShapeDtypeStruct((n, nout), BF16),
        compiler_params=_cparams(1),
        name="hyena_filter_mlp",
    )(jnp.asarray(z), pad(w1, LANE, LANE), pad(b1[None], 1, LANE), pad(fr[None], 1, LANE),
      pad(w2, LANE, LANE), pad(b2[None], 1, LANE), pad(w3, LANE, nout), jnp.asarray(dl))


def _dft_matrix(n):
    f = np.arange(n, dtype=np.int64)[:, None]
    s = np.arange(n, dtype=np.int64)[None, :]
    ang = (2.0 * np.pi / (2 * n)) * ((f * s) % (2 * n)).astype(np.float64)
    cos = np.cos(ang)
    sin = np.sin(ang)
    sin[0, :] = np.where(np.arange(n) % 2 == 0, 1.0, -1.0)
    return cos.astype(np.float32), sin.astype(np.float32)


def _matmul_kernel(a_ref, b_ref, o_ref):
    o_ref[...] = _dot(a_ref[...], b_ref[...])


def _matmul(a, b, bm, bn):
    m, k = a.shape
    n = b.shape[1]
    assert m % bm == 0 and n % bn == 0
    return pl.pallas_call(
        _matmul_kernel,
        grid=(m // bm, n // bn),
        in_specs=[pl.BlockSpec((bm, k), lambda i, j: (i, 0)), pl.BlockSpec((k, bn), lambda i, j: (0, j))],
        out_specs=pl.BlockSpec((bm, bn), lambda i, j: (i, j)),
        out_shape=jax.ShapeDtypeStruct((m, n), F32),
        compiler_params=_cparams(2),
        name="filter_dft",
    )(a, b)


def _hyena_conv_kernel(ve_ref, x1e_ref, x2e_ref, vo_ref, x1o_ref, x2o_ref, me_ref, mo_ref, met_ref, mot_ref,
                       kr1_ref, ki1_ref, kr2_ref, ki2_ref, kh_ref, fb_ref, oe_ref, oo_ref,
                       ze_ref, zo_ref, ye_ref, yo_ref, *, row0, h, fbs):
    o = pl.program_id(2)
    fb = pl.program_id(3)
    nfb = pl.num_programs(3)
    rows = slice(row0, row0 + h)

    @pl.when(jnp.logical_and(o == 0, fb == 0))
    def _():
        ze_ref[...] = ve_ref[0, rows, :]
        zo_ref[...] = vo_ref[0, rows, :]

    @pl.when(fb == 0)
    def _():
        ye_ref[...] = jnp.zeros_like(ye_ref)
        yo_ref[...] = jnp.zeros_like(yo_ref)

    first = jnp.logical_and(lax.broadcasted_iota(jnp.int32, (8, 1), 0) == 0, fb == 0)
    halves = (slice(0, MXU_W), slice(MXU_W, 2 * MXU_W))
    fwd = [(_dot(me_ref[0], ze_ref[:, cs]), _dot(mo_ref[0], zo_ref[:, cs])) for cs in halves]
    for cs, (ge, go) in zip(halves, fwd):
        ce, se, co, so = ge[:fbs], ge[fbs:], go[:fbs], go[fbs:]
        xr1, xs1, xr2, xs2 = ce + co, se + so, ce - co, so - se
        kr1, ki1, kr2, ki2 = kr1_ref[0, :, cs], ki1_ref[0, :, cs], kr2_ref[0, :, cs], ki2_ref[0, :, cs]
        a1, b1 = xr1 * kr1 + xs1 * ki1, xs1 * kr1 - xr1 * ki1
        a2, b2 = xr2 * kr2 + xs2 * ki2, xs2 * kr2 - xr2 * ki2
        bm, bp = b1 - b2, b1 + b2
        krh, kih = kh_ref[0, 0:1, cs], kh_ref[0, 1:2, cs]
        se8, so8 = se[0:8], so[0:8]
        bm = jnp.concatenate([jnp.where(first, se8 * krh + so8 * kih, bm[0:8]), bm[8:]], axis=0)
        bp = jnp.concatenate([jnp.where(first, so8 * krh - se8 * kih, bp[0:8]), bp[8:]], axis=0)
        we = jnp.concatenate([a1 + a2, bm], axis=0).astype(BF16)
        wo = jnp.concatenate([a1 - a2, bp], axis=0).astype(BF16)
        ye_ref[:, cs] += _dot(met_ref[0], we)
        yo_ref[:, cs] += _dot(mot_ref[0], wo)

    @pl.when(fb == nfb - 1)
    def _():
        fe = ye_ref[...] + ze_ref[...].astype(F32) * fb_ref[0]
        fo = yo_ref[...] + zo_ref[...].astype(F32) * fb_ref[0]

        @pl.when(o == 0)
        def _():
            ze_ref[...] = (x1e_ref[0, rows, :].astype(F32) * fe).astype(BF16)
            zo_ref[...] = (x1o_ref[0, rows, :].astype(F32) * fo).astype(BF16)

        @pl.when(o == 1)
        def _():
            if row0 > 0:
                oe_ref[0, 0:row0, :] = jnp.zeros((row0, oe_ref.shape[2]), oe_ref.dtype)
                oo_ref[0, 0:row0, :] = jnp.zeros((row0, oo_ref.shape[2]), oo_ref.dtype)
            oe_ref[0, rows, :] = (x2e_ref[0, rows, :].astype(F32) * fe).astype(BF16)
            oo_ref[0, rows, :] = (x2o_ref[0, rows, :].astype(F32) * fo).astype(BF16)


def _hyena_conv_aliased_kernel(*refs, **kw):
    _hyena_conv_kernel(*refs[:16], *refs[18:], **kw)


def _hyena_conv(ue, uo, mats, planes, fbias, *, row0, h, block_rows, prev=None):
    bsz, t2, _ = ue.shape
    tc = 2 * MXU_W
    nct = C_WIDTH // tc
    me, mo, met, mot = mats
    nfb = me.shape[0]
    fbs = me.shape[1] // 2
    u_spec = lambda which: pl.BlockSpec((1, block_rows, tc), lambda b, c, o, f: (b, 0, which * nct + c))
    m_spec = pl.BlockSpec((1, 2 * fbs, h), lambda b, c, o, f: (f, 0, 0))
    mt_spec = pl.BlockSpec((1, h, 2 * fbs), lambda b, c, o, f: (f, 0, 0))
    p_spec = pl.BlockSpec((1, fbs, tc), lambda b, c, o, f: (o, f, c))
    in_specs = [u_spec(0), u_spec(1), u_spec(2), u_spec(0), u_spec(1), u_spec(2),
                m_spec, m_spec, mt_spec, mt_spec, p_spec, p_spec, p_spec, p_spec,
                pl.BlockSpec((1, 2, tc), lambda b, c, o, f: (o, 0, c)),
                pl.BlockSpec((1, 1, tc), lambda b, c, o, f: (o, 0, c))]
    args = [ue, ue, ue, uo, uo, uo, me, mo, met, mot, *planes, fbias]
    aliases = {}
    kern = functools.partial(_hyena_conv_kernel, row0=row0, h=h, fbs=fbs)
    if prev is not None:
        in_specs += [pl.BlockSpec(memory_space=pl.ANY)] * 2
        args += list(prev)
        aliases = {16: 0, 17: 1}
        kern = functools.partial(_hyena_conv_aliased_kernel, row0=row0, h=h, fbs=fbs)
    o_spec = pl.BlockSpec((1, block_rows, tc), lambda b, c, o, f: (b, 0, c))
    return pl.pallas_call(
        kern,
        grid=(bsz, nct, 2, nfb),
        in_specs=in_specs,
        out_specs=[o_spec, o_spec],
        out_shape=[jax.ShapeDtypeStruct((bsz, t2, C_WIDTH), BF16)] * 2,
        scratch_shapes=[pltpu.VMEM((h, tc), BF16), pltpu.VMEM((h, tc), BF16),
                        pltpu.VMEM((h, tc), F32), pltpu.VMEM((h, tc), F32)],
        input_output_aliases=aliases,
        compiler_params=_cparams(4),
        name="hyena_long_conv_h%d" % h,
    )(*args)


def _radix2_matrices(n):
    h = n // 2
    fbs = min(h, 256)
    nfb = h // fbs
    f = np.arange(h, dtype=np.int64)[:, None]
    r = np.arange(h, dtype=np.int64)[None, :]
    alt = np.where(np.arange(h) % 2 == 0, 1.0, -1.0)
    out = []
    for pos in (2 * r, 2 * r + 1):
        ang = (2.0 * np.pi / (2 * n)) * ((f * pos) % (2 * n)).astype(np.float64)
        cos, sin = np.cos(ang), np.sin(ang)
        sin[0, :] = alt
        blocks = np.concatenate([cos.reshape(nfb, fbs, h), sin.reshape(nfb, fbs, h)], axis=1)
        out.append(blocks.astype(np.float32))
    me, mo = out
    tr = lambda a: np.ascontiguousarray(np.transpose(a, (0, 2, 1)))
    return tuple(jnp.asarray(a, BF16) for a in (me, mo, tr(me), tr(mo)))


def _hyena_spectra(n, filt):
    h = n // 2
    cos, sin = _dft_matrix(n)
    mirror = np.concatenate([[0], np.arange(n - 1, h, -1)])
    rows = np.concatenate([cos[:h], np.concatenate([sin[:1], cos[mirror[1:]]]), sin[:h], sin[mirror],
                           cos[h:h + 1], sin[h:h + 1], np.zeros((LANE - 2, n), np.float32)], axis=0)
    nr = 4 * h + LANE
    hspec = _matmul(jnp.asarray(rows, BF16), filt, 3 * LANE if nr % (3 * LANE) == 0 else nr, 1024)
    hf = hspec[:, :2 * C_WIDTH].reshape(nr, 2, C_WIDTH)
    hb = hspec[:, 2 * C_WIDTH:].reshape(nr, 2, C_WIDTH)
    w = np.full((h, 1, 1), 2.0 / (2 * n), np.float32)
    w[0] = 1.0 / (2 * n)
    nz = (np.arange(h) != 0)[:, None, None]
    tr = lambda a: jnp.transpose(a, (1, 0, 2))
    kr1 = (hf[:h] + hb[:h]) * w
    kr2 = (hf[h:2 * h] + hb[h:2 * h]) * w
    ki1 = jnp.where(nz, (hb[2 * h:3 * h] - hf[2 * h:3 * h]) * w, 0.0)
    ki2 = jnp.where(nz, (hb[3 * h:4 * h] - hf[3 * h:4 * h]) * w, 0.0)
    kh = jnp.stack([hf[4 * h] + hb[4 * h], hb[4 * h + 1] - hf[4 * h + 1]], axis=1) * (2.0 / (2 * n))
    return (tr(kr1), tr(ki1), tr(kr2), tr(ki2), kh)


def _rope_tables(seq, layout):
    rows = seq // GRID_W
    row = np.repeat(np.arange(rows, dtype=np.float32), GRID_W)
    col = np.tile(np.arange(GRID_W, dtype=np.float32), rows)
    per_axis = 32
    inv = (ROPE_BASE ** (-np.arange(0, per_axis, 2, dtype=np.float32) / per_axis)).astype(np.float32)
    ang = np.concatenate([row[:, None] * inv, col[:, None] * inv], axis=-1)
    cos = np.concatenate([np.ones((CTX, 32), np.float32), np.cos(ang)], axis=0)
    sin = np.concatenate([np.zeros((CTX, 32), np.float32), np.sin(ang)], axis=0)
    one, zero = np.ones_like(cos), np.zeros_like(cos)
    if layout == "mla":
        a = [cos, cos, one, one]
        b = [zero, sin, zero, zero]
        c = [-sin, zero, zero, zero]
    else:
        a = [cos, cos, cos, cos]
        b = [zero, sin, zero, sin]
        c = [-sin, zero, -sin, zero]
    return tuple(jnp.asarray(np.concatenate(p, axis=1), F32) for p in (a, b, c))


def _swa_head_perm():
    cols = []
    for j in range(B_KV_HEADS // 2):
        for g in range(B_GROUP):
            for hk in (2 * j, 2 * j + 1):
                h = hk * B_GROUP + g
                cols.extend(range(h * B_HDIM, (h + 1) * B_HDIM))
    return np.asarray(cols, np.int32)


def _mla_weights(w_in, w_q, w_kv):
    c1 = A_Q_RANK + A_KV_RANK
    zpad = jnp.zeros((D, LANE - A_ROPE), F32)
    win = jnp.concatenate([w_in[:, :c1 + A_ROPE], zpad, w_in[:, c1 + A_ROPE:]], axis=1)
    qscale = (A_NOPE + A_ROPE) ** -0.5 * math.log2(math.e)
    wq = w_q.reshape(A_Q_RANK, A_HEADS, A_NOPE + A_ROPE) * qscale
    wq = jnp.concatenate([wq, jnp.zeros((A_Q_RANK, A_HEADS, A_HEAD_PAD - A_NOPE - A_ROPE), F32)], axis=-1)
    wq = wq.reshape(A_Q_RANK, A_HEADS * A_HEAD_PAD)
    return win.astype(BF16), wq.astype(BF16), w_kv.astype(BF16)


def kernel(x, c, ctx, c_ctx, w_mod, b_mod, g_pre, g_post, a_w_in, a_g_q, a_w_q, a_g_kv, a_w_kv, a_w_out, b_w_in, b_sink, b_w_out, c_w_in, c_conv_w, c_conv_b, c_f_w1, c_f_b1, c_f_freq, c_f_w2, c_f_b2, c_f_w3, c_filt_bias, c_w_out):
    bsz, seq, _ = x.shape
    assert ctx.shape[1] == CTX and (CTX + seq) % TP == 0 and seq % GRID_W == 0
    xsrc = (ctx, x)

    pad_rows = (-(bsz + 1)) % 8
    cond = jnp.concatenate([c, c_ctx[None], jnp.zeros((pad_rows, D), F32)], axis=0)
    mod = _modulation(cond, w_mod, b_mod)

    tabs_mla = _rope_tables(seq, "mla")
    tabs_swa = _rope_tables(seq, "swa")

    for layer in range(DEPTH):
        kind, j = layer % 3, layer // 3
        mx = mod[layer, :bsz].reshape(bsz, 3, D)
        mc = jnp.broadcast_to(mod[layer, bsz].reshape(1, 3, D), (bsz, 3, D))
        modl = jnp.stack([mc, mx], axis=0)
        gpre = g_pre[layer][None]
        gpost = g_post[layer][None]
        if kind == 0:
            win, wq, wkv = _mla_weights(a_w_in[j], a_w_q[j], a_w_kv[j])
            q, k, v, gate = _mla_proj(xsrc, modl, gpre, win, a_g_q[j][None], wq, a_g_kv[j][None], wkv, tabs_mla)
            xs = _mla_attn(q, k, v, gate, a_w_out[j].astype(BF16), xsrc, modl, gpost,
                           latents_only=layer == DEPTH - 1)
        elif kind == 1:
            xs, = xsrc
            perm = _swa_head_perm()
            qw, kw = B_HEADS * B_HDIM, B_KV_HEADS * B_HDIM
            w = b_w_in[j]
            win = jnp.concatenate([w[:, :qw][:, perm] * (B_HDIM ** -0.5 * math.log2(math.e)), w[:, qw:qw + 2 * kw],
                                   w[:, qw + 2 * kw:][:, perm]], axis=1).astype(BF16)
            q, k, v, gate = _swa_proj(xs, modl, gpre, win, tabs_swa)
            sink = (b_sink[j].astype(F32) * math.log2(math.e)).reshape(B_KV_HEADS, B_GROUP, 1, 1)
            sink_cols = jnp.broadcast_to(sink, (B_KV_HEADS, B_GROUP, B_BLOCK, 1)).reshape(B_KV_HEADS, B_GROUP * B_BLOCK, 1)
            xs = _swa_attn(q, k, v, sink_cols, gate, b_w_out[j][perm, :].astype(BF16), xs, modl, gpost)
        else:
            xs, = xsrc
            ue, uo, ge, go = _hyena_proj(xs, modl, gpre, c_w_in[j].astype(BF16), c_conv_w[j], c_conv_b[j][None])
            fargs = (c_f_w1[j], c_f_b1[j], c_f_freq[j], c_f_w2[j], c_f_b2[j], c_f_w3[j])
            fbias = c_filt_bias[j].reshape(2, 1, C_WIDTH)
            oeo = _hyena_conv(ue, uo, _radix2_matrices(seq), _hyena_spectra(seq, _filters(seq, *fargs)), fbias,
                              row0=CTX // 2, h=seq // 2, block_rows=(CTX + seq) // 2)
            oe, oo = _hyena_conv(ue, uo, _radix2_matrices(CTX), _hyena_spectra(CTX, _filters(CTX, *fargs)), fbias,
                                 row0=0, h=CTX // 2, block_rows=CTX // 2, prev=oeo)
            xs = _hyena_out_proj(oe, oo, ge, go, c_w_out[j].astype(BF16), xs, modl, gpost)
        xsrc = (xs,)
    return xs
```

```python
import functools
import math

import numpy as np
import jax
import jax.numpy as jnp
from jax import lax
from jax.experimental import pallas as pl
from jax.experimental.pallas import tpu as pltpu

F32 = jnp.float32
BF16 = jnp.bfloat16

D = 1024
DEPTH = 4
GRID_W = 64
CTX = 256
NORM_EPS = 1e-6
ROPE_BASE = 10000.0
NEG_INF = -1e30

A_HEADS = 8
A_Q_RANK = 512
A_KV_RANK = 256
A_NOPE = 128
A_ROPE = 64
A_VDIM = 128
A_HEAD_PAD = 256

B_HEADS = 16
B_KV_HEADS = 4
B_GROUP = 4
B_HDIM = 64
B_WINDOW = 128
B_BLOCK = 128

C_WIDTH = 1024
C_BANDS = 16
C_EMB = 1 + 2 * C_BANDS
C_FFN = 64
C_MIN_DECAY = math.log(1e-2) / 1.5
C_MAX_DECAY = math.log(1e-2) / 0.3

LANE = 128
MXU_W = 256
TM = 256
TP = 3 * TM
HALO = 8
VMEM_LIMIT = 56 * 1024 * 1024


def _cparams(n_axes):
    return pltpu.CompilerParams(dimension_semantics=("arbitrary",) * n_axes,
                                vmem_limit_bytes=VMEM_LIMIT)


def _rms(x):
    return x * lax.rsqrt(jnp.mean(x * x, axis=-1, keepdims=True) + NORM_EPS)


def _silu(g):
    return g / (1.0 + jnp.exp(-g))


def _dot(a, b):
    return jnp.dot(a, b, preferred_element_type=F32)


def _dot_nt(a, b):
    return lax.dot_general(a, b, (((1,), (1,)), ((), ())), preferred_element_type=F32)


def _rope128(x, a, b, c):
    return x * a + pltpu.roll(x, 32, 1) * b + pltpu.roll(x, LANE - 32, 1) * c


def _modnorm(x, mod_ref, gpre_ref):
    m = mod_ref[0, 0]
    return _rms(x) * gpre_ref[...] * (1.0 + m[1:2]) + m[0:1]


def _mod_kernel(c_ref, w_ref, b_ref, o_ref):
    a = _silu(c_ref[...])
    o_ref[0] = _dot(a.astype(BF16), w_ref[0].astype(BF16)) + b_ref[0]


def _modulation(cond, w_mod, b_mod):
    rows = cond.shape[0]
    return pl.pallas_call(
        _mod_kernel,
        grid=(DEPTH, 3),
        in_specs=[pl.BlockSpec((rows, D), lambda l, j: (0, 0)),
                  pl.BlockSpec((1, D, D), lambda l, j: (l, 0, j)),
                  pl.BlockSpec((1, 1, D), lambda l, j: (l, 0, j))],
        out_specs=pl.BlockSpec((1, rows, D), lambda l, j: (l, 0, j)),
        out_shape=jax.ShapeDtypeStruct((DEPTH, rows, 3 * D), F32),
        compiler_params=_cparams(2),
        name="adaln_modulation",
    )(cond, w_mod, b_mod.reshape(DEPTH, 1, 3 * D))


def _tok_spec(width, rows=TM, skip=0):
    return pl.BlockSpec((1, rows, width), lambda b, t: (b, t + skip, 0))


def _mod_spec(ctx_tiles=1, skip=0):
    return pl.BlockSpec((1, 1, 3, D), lambda b, t: (jnp.minimum((t + skip) // ctx_tiles, 1), b, 0, 0))


def _const_spec(shape):
    nd = len(shape)
    return pl.BlockSpec(shape, lambda b, t: (0,) * nd)


def _residual_specs(dual, rows=TM, skip=0):
    if not dual:
        return [_tok_spec(D, rows, skip)]
    per = CTX // rows
    return [pl.BlockSpec((1, rows, D), lambda b, t: (b, jnp.minimum(t, per - 1), 0)),
            pl.BlockSpec((1, rows, D), lambda b, t: (b, jnp.maximum(t - per, 0), 0))]


def _wide_specs(dual):
    n = TP // TM
    if not dual:
        return [pl.BlockSpec((1, TM, D), functools.partial(lambda j, b, t: (b, n * t + j, 0), j)) for j in range(n)]
    last = lambda j, b, t: (b, jnp.maximum(n * t + j - CTX // TM, 0), 0)
    return ([pl.BlockSpec((1, CTX, D), lambda b, t: (b, 0, 0))]
            + [pl.BlockSpec((1, TM, D), functools.partial(last, j)) for j in range(n)])


def _wide_modnorm(refs, dual, mod_ref, gpre_ref):
    t = pl.program_id(1)
    blocks = [r[0] for r in refs[1:]] if dual else [r[0] for r in refs]
    if dual:
        blocks[0] = jnp.where(t == 0, refs[0][0], blocks[0])
    x = jnp.concatenate(blocks, axis=0)
    is_ctx = jnp.logical_and(lax.broadcasted_iota(jnp.int32, (TP, 1), 0) < CTX, t == 0)
    mc, mx = mod_ref[0, 0], mod_ref[1, 0]
    scale = jnp.where(is_ctx, mc[1:2], mx[1:2])
    shift = jnp.where(is_ctx, mc[0:1], mx[0:1])
    return _rms(x) * gpre_ref[...] * (1.0 + scale) + shift


def _both_mod_spec():
    return pl.BlockSpec((2, 1, 3, D), lambda b, t: (0, b, 0, 0))


def _resident(shape):
    nd = len(shape)
    return pl.BlockSpec(shape, lambda b, t: (0,) * nd, pipeline_mode=pl.Buffered(1))


def _residual_tile(refs, dual, rows=TM):
    if not dual:
        return refs[0][0]
    return jnp.where(pl.program_id(1) < CTX // rows, refs[0][0], refs[1][0])


def _finish(o, gate_ref, wout_ref, x, mod_ref, gpost_ref):
    a = o.astype(F32) * _silu(gate_ref[0].astype(F32))
    y = _dot(a.astype(BF16), wout_ref[...])
    return x + mod_ref[0, 0][2:3] * (_rms(y) * gpost_ref[...])


def _mla_proj_kernel(*refs, dual):
    nx = TP // TM + (1 if dual else 0)
    (mod_ref, gpre_ref, win_ref, gq_ref, wq_ref, gkv_ref, wkv_ref,
     ra_ref, rb_ref, rc_ref, q_ref, k_ref, v_ref, gate_ref) = refs[nx:]
    h = _wide_modnorm(refs[:nx], dual, mod_ref, gpre_ref)
    p = _dot(h.astype(BF16), win_ref[...])
    c0, c1, c2 = A_Q_RANK, A_Q_RANK + A_KV_RANK, A_Q_RANK + A_KV_RANK + LANE
    gate_ref[0] = p[:, c2:].astype(BF16)
    qn = (_rms(p[:, :c0]) * gq_ref[...]).astype(BF16)
    kvn = (_rms(p[:, c0:c1]) * gkv_ref[...]).astype(BF16)
    ra, rb, rc = ra_ref[...], rb_ref[...], rc_ref[...]
    kr = _rope128(p[:, c1:c2], ra, rb, rc).astype(BF16)
    for hd in range(A_HEADS):
        q = _dot(qn, wq_ref[:, hd * A_HEAD_PAD:(hd + 1) * A_HEAD_PAD])
        q_ref[0, hd, :, 0:LANE] = q[:, 0:LANE].astype(BF16)
        q_ref[0, hd, :, LANE:2 * LANE] = _rope128(q[:, LANE:2 * LANE], ra, rb, rc).astype(BF16)
        kv = _dot(kvn, wkv_ref[:, hd * A_HEAD_PAD:(hd + 1) * A_HEAD_PAD])
        k_ref[0, hd, :, 0:LANE] = kv[:, 0:A_NOPE].astype(BF16)
        k_ref[0, hd, :, LANE:2 * LANE] = kr
        v_ref[0, hd] = kv[:, A_NOPE:].astype(BF16)


def _mla_proj(xsrc, modl, gpre, win, gq, wq, gkv, wkv, tabs):
    dual = len(xsrc) == 2
    bsz = xsrc[0].shape[0]
    t = tabs[0].shape[0]
    xargs = [xsrc[0]] + [xsrc[1]] * (TP // TM) if dual else [xsrc[0]] * (TP // TM)
    head = lambda w: pl.BlockSpec((1, A_HEADS, TP, w), lambda b, i: (b, 0, i, 0))
    tab = pl.BlockSpec((TP, LANE), lambda b, i: (i, 0))
    return pl.pallas_call(
        functools.partial(_mla_proj_kernel, dual=dual),
        grid=(bsz, t // TP),
        in_specs=[*_wide_specs(dual), _both_mod_spec(), _const_spec((1, D)), _resident(win.shape),
                  _const_spec((1, A_Q_RANK)), _resident(wq.shape),
                  _const_spec((1, A_KV_RANK)), _resident(wkv.shape), tab, tab, tab],
        out_specs=[head(A_HEAD_PAD), head(A_HEAD_PAD), head(A_VDIM), _tok_spec(A_HEADS * A_VDIM, TP)],
        out_shape=[jax.ShapeDtypeStruct((bsz, A_HEADS, t, A_HEAD_PAD), BF16),
                   jax.ShapeDtypeStruct((bsz, A_HEADS, t, A_HEAD_PAD), BF16),
                   jax.ShapeDtypeStruct((bsz, A_HEADS, t, A_VDIM), BF16),
                   jax.ShapeDtypeStruct((bsz, t, A_HEADS * A_VDIM), BF16)],
        compiler_params=_cparams(2),
        name="mla_proj",
    )(*xargs, modl, gpre, win, gq, wq, gkv, wkv, *tabs)


def _mla_attn_kernel(*refs, dual, latents_only):
    nx = 2 if dual else 1
    q_ref, k_ref, v_ref, gate_ref, wout_ref = refs[:5]
    mod_ref, gpost_ref, xo_ref, o_ref = refs[5 + nx:]

    def attend(nk):
        s_next = _dot_nt(q_ref[0, 0], k_ref[0, 0, :nk, :])
        for hd in range(A_HEADS):
            s = s_next
            if hd + 1 < A_HEADS:
                s_next = _dot_nt(q_ref[0, hd + 1], k_ref[0, hd + 1, :nk, :])
            m = jnp.max(s, axis=-1, keepdims=True)
            p = jnp.exp2(s - m)
            l = jnp.sum(p, axis=-1, keepdims=True)
            o = _dot(p.astype(BF16), v_ref[0, hd, :nk, :]) / l
            o_ref[:, hd * A_VDIM:(hd + 1) * A_VDIM] = o.astype(BF16)

    t_all = k_ref.shape[2]
    if latents_only:
        attend(t_all)
    else:
        is_ctx = pl.program_id(1) == 0
        pl.when(is_ctx)(lambda: attend(CTX))
        pl.when(jnp.logical_not(is_ctx))(lambda: attend(t_all))
    x = _residual_tile(refs[5:5 + nx], dual)
    xo_ref[0] = _finish(o_ref[...], gate_ref, wout_ref, x, mod_ref, gpost_ref)


def _mla_attn(q, k, v, gate, wout, xsrc, modl, gpost, latents_only):
    dual = len(xsrc) == 2
    bsz, nh, t, _ = q.shape
    skip = CTX // TM if latents_only else 0
    nt = t // TM - skip
    return pl.pallas_call(
        functools.partial(_mla_attn_kernel, dual=dual, latents_only=latents_only),
        grid=(bsz, nt),
        in_specs=[pl.BlockSpec((1, nh, TM, A_HEAD_PAD), lambda b, i: (b, 0, i + skip, 0)),
                  pl.BlockSpec((1, nh, t, A_HEAD_PAD), lambda b, i: (b, 0, 0, 0)),
                  pl.BlockSpec((1, nh, t, A_VDIM), lambda b, i: (b, 0, 0, 0)),
                  _tok_spec(nh * A_VDIM, skip=skip), _const_spec(wout.shape),
                  *_residual_specs(dual, skip=skip), _mod_spec(skip=skip), _const_spec((1, D))],
        out_specs=_tok_spec(D),
        out_shape=jax.ShapeDtypeStruct((bsz, nt * TM, D), F32),
        scratch_shapes=[pltpu.VMEM((TM, nh * A_VDIM), BF16)],
        input_output_aliases={} if (dual or latents_only) else {5: 0},
        compiler_params=_cparams(2),
        name="mla_attention_out",
    )(q, k, v, gate, wout, *xsrc, modl, gpost)


def _swa_proj_kernel(*refs):
    xrefs = refs[:TP // TM]
    mod_ref, gpre_ref, win_ref, ra_ref, rb_ref, rc_ref, q_ref, k_ref, v_ref, gate_ref = refs[TP // TM:]
    h = _wide_modnorm(xrefs, False, mod_ref, gpre_ref)
    p = _dot(h.astype(BF16), win_ref[...])
    ra, rb, rc = ra_ref[...], rb_ref[...], rc_ref[...]
    qw, kw = B_HEADS * B_HDIM, B_KV_HEADS * B_HDIM
    lane = lax.broadcasted_iota(jnp.int32, (1, LANE), 1)
    lo = lane < B_HDIM
    ones_col = jnp.broadcast_to(jnp.where(lane == 0, 1.0, 0.0), (TP, LANE)).astype(BF16)
    for j in range(B_KV_HEADS // 2):
        for g in range(B_GROUP):
            s = j * B_GROUP + g
            r = _rope128(p[:, s * LANE:(s + 1) * LANE], ra, rb, rc)
            for half in range(2):
                d = ((2 * j + half) * B_GROUP + g) * LANE
                keep = lo if half == 0 else jnp.logical_not(lo)
                q_ref[0, :, d:d + LANE] = jnp.where(keep, r, 0.0).astype(BF16)
        o = qw + j * LANE
        k_ref[0, :, j * LANE:(j + 1) * LANE] = _rope128(p[:, o:o + LANE], ra, rb, rc).astype(BF16)
        v_ref[0, :, 2 * j * LANE:(2 * j + 1) * LANE] = p[:, o + kw:o + kw + LANE].astype(BF16)
        v_ref[0, :, (2 * j + 1) * LANE:(2 * j + 2) * LANE] = ones_col
    gate_ref[0] = p[:, qw + 2 * kw:].astype(BF16)


def _swa_proj(xs, modl, gpre, win, tabs):
    bsz, t, _ = xs.shape
    qw, kw = B_HEADS * B_HDIM, B_KV_HEADS * B_HDIM
    tab = pl.BlockSpec((TP, LANE), lambda b, i: (i, 0))
    return pl.pallas_call(
        _swa_proj_kernel,
        grid=(bsz, t // TP),
        in_specs=[*_wide_specs(False), _both_mod_spec(), _const_spec((1, D)), _resident(win.shape), tab, tab, tab],
        out_specs=[_tok_spec(2 * qw, TP), _tok_spec(kw, TP), _tok_spec(2 * kw, TP), _tok_spec(qw, TP)],
        out_shape=[jax.ShapeDtypeStruct((bsz, t, 2 * qw), BF16),
                   jax.ShapeDtypeStruct((bsz, t, kw), BF16),
                   jax.ShapeDtypeStruct((bsz, t, 2 * kw), BF16),
                   jax.ShapeDtypeStruct((bsz, t, qw), BF16)],
        compiler_params=_cparams(2),
        name="swa_proj",
    )(*[xs] * (TP // TM), modl, gpre, win, *tabs)


def _swa_attn_kernel(q_ref, k_ref, v_ref, sink_ref, gate_ref, wout_ref, x_ref, mod_ref, gpost_ref, xo_ref, o_ref):
    i = pl.program_id(1)
    n_lat = k_ref.shape[1] - CTX
    band = 3 * B_BLOCK
    lo = lax.broadcasted_iota(jnp.int32, (1, LANE), 1) < B_HDIM

    def run(with_band):
        chains = [(sub, j, half) for sub in range(TM // B_BLOCK) for j in range(B_KV_HEADS // 2) for half in range(2)]
        window, keys = {}, {}

        def block_window(sub):
            if sub not in window:
                li = (i - CTX // TM) * (TM // B_BLOCK) + sub
                start = jnp.clip((li - 1) * B_BLOCK, 0, n_lat - band)
                off = li * B_BLOCK - start
                d0 = (lax.broadcasted_iota(jnp.int32, (B_BLOCK, band), 0)
                      - lax.broadcasted_iota(jnp.int32, (B_BLOCK, band), 1))
                mask = jnp.abs(d0 + off) <= B_WINDOW
                window[sub] = (mask, pl.multiple_of(CTX + start, B_BLOCK))
            return window[sub]

        def block_keys(sub, j):
            if (sub, j) not in keys:
                kcols = slice(j * LANE, (j + 1) * LANE)
                vcols = slice(2 * j * LANE, (2 * j + 2) * LANE)
                kk = k_ref[0, 0:CTX, kcols]
                vv = v_ref[0, 0:CTX, vcols]
                if with_band:
                    kstart = block_window(sub)[1]
                    kk = jnp.concatenate([kk, k_ref[0, pl.ds(kstart, band), kcols]], axis=0)
                    vv = jnp.concatenate([vv, v_ref[0, pl.ds(kstart, band), vcols]], axis=0)
                keys[(sub, j)] = (kk, vv)
            return keys[(sub, j)]

        def scores(chain):
            sub, j, half = chain
            hk = 2 * j + half
            r0 = sub * B_BLOCK
            qs = jnp.concatenate(
                [q_ref[0, r0:r0 + B_BLOCK, (hk * B_GROUP + g) * LANE:(hk * B_GROUP + g + 1) * LANE]
                 for g in range(B_GROUP)], axis=0)
            return _dot_nt(qs, block_keys(sub, j)[0])

        outs = []
        s_next = scores(chains[0])
        for n, (sub, j, half) in enumerate(chains):
            s = s_next
            if n + 1 < len(chains):
                s_next = scores(chains[n + 1])
            sk = sink_ref[2 * j + half]
            s_c = s[:, :CTX]
            m = jnp.maximum(jnp.max(s_c, axis=-1, keepdims=True), sk)
            if with_band:
                mask = block_window(sub)[0]
                s_b = jnp.concatenate(
                    [jnp.where(mask, s[g * B_BLOCK:(g + 1) * B_BLOCK, CTX:], NEG_INF) for g in range(B_GROUP)],
                    axis=0)
                m = jnp.maximum(m, jnp.max(s_b, axis=-1, keepdims=True))
                p = jnp.concatenate([jnp.exp2(s_c - m), jnp.exp2(s_b - m)], axis=1)
            else:
                p = jnp.exp2(s_c - m)
            pv = _dot(p.astype(BF16), block_keys(sub, j)[1])
            l = pv[:, LANE:LANE + 1] + jnp.exp2(sk - m)
            outs.append(pv[:, :LANE] / l)
            if half == 1:
                comb = jnp.where(lo, outs[-2], outs[-1])
                r0 = sub * B_BLOCK
                for g in range(B_GROUP):
                    s_out = j * B_GROUP + g
                    o_ref[r0:r0 + B_BLOCK, s_out * LANE:(s_out + 1) * LANE] = (
                        comb[g * B_BLOCK:(g + 1) * B_BLOCK].astype(BF16))

    is_ctx = i < CTX // TM
    pl.when(is_ctx)(lambda: run(False))
    pl.when(jnp.logical_not(is_ctx))(lambda: run(True))
    xo_ref[0] = _finish(o_ref[...], gate_ref, wout_ref, x_ref[0], mod_ref, gpost_ref)


def _swa_attn(q, k, v, sink_cols, gate, wout, xs, modl, gpost):
    bsz, t, qw2 = q.shape
    kw = k.shape[-1]
    qw = qw2 // 2
    return pl.pallas_call(
        _swa_attn_kernel,
        grid=(bsz, t // TM),
        in_specs=[pl.BlockSpec((1, TM, qw2), lambda b, i: (b, i, 0)),
                  pl.BlockSpec((1, t, kw), lambda b, i: (b, 0, 0)),
                  pl.BlockSpec((1, t, 2 * kw), lambda b, i: (b, 0, 0)),
                  pl.BlockSpec(sink_cols.shape, lambda b, i: (0, 0, 0)),
                  _tok_spec(qw), _const_spec(wout.shape), _tok_spec(D), _mod_spec(), _const_spec((1, D))],
        out_specs=_tok_spec(D),
        out_shape=jax.ShapeDtypeStruct(xs.shape, F32),
        scratch_shapes=[pltpu.VMEM((TM, qw), BF16)],
        input_output_aliases={6: 0},
        compiler_params=_cparams(2),
        name="swa_attention_out",
    )(q, k, v, sink_cols, gate, wout, xs, modl, gpost)


RADIX = 4
QP = TM // RADIX
TILE_ORDER = (3, 1, 2, 0)


def _hyena_proj_kernel(xp_ref, x_ref, xn_ref, mod_ref, gpre_ref, win_ref, cw_ref, cb_ref, perm_ref, *out_refs):
    u_refs, g_refs = out_refs[:RADIX], out_refs[RADIX:]
    t = pl.program_id(1)
    nt = pl.num_programs(1)
    hh = _dot(perm_ref[...], _modnorm(x_ref[0], mod_ref, gpre_ref).astype(BF16))
    h = jnp.concatenate([_modnorm(xp_ref[0], mod_ref, gpre_ref), hh, _modnorm(xn_ref[0], mod_ref, gpre_ref)], axis=0)
    p = _dot(h.astype(BF16), win_ref[...])
    cwid = 3 * C_WIDTH
    u = p[:, :cwid]
    start = {rho: HALO + k * QP for k, rho in enumerate(TILE_ORDER)}
    cls = {rho: u[start[rho]:start[rho] + QP] for rho in range(RADIX)}
    for rho in range(RADIX):
        g_refs[rho][0] = p[start[rho]:start[rho] + QP, cwid:].astype(BF16)
    before0 = pltpu.roll(u[0:HALO + QP], 1, 0)[HALO:]
    after3 = pltpu.roll(u[start[0]:], QP + HALO - 1, 0)[0:QP]
    r = lax.broadcasted_iota(jnp.int32, (QP, 1), 0)
    before0 = jnp.where(jnp.logical_and(r == 0, t <= 1), 0.0, before0)
    after3 = jnp.where(jnp.logical_and(r == QP - 1, jnp.logical_or(t == 0, t == nt - 1)), 0.0, after3)
    cw = cw_ref[...]
    cb = cb_ref[...]
    left = {0: before0, 1: cls[0], 2: cls[1], 3: cls[2]}
    right = {0: cls[1], 1: cls[2], 2: cls[3], 3: after3}
    for rho in range(RADIX):
        u_refs[rho][0] = (cb + left[rho] * cw[0:1] + cls[rho] * cw[1:2] + right[rho] * cw[2:3]).astype(BF16)


def _class_spec(width):
    return pl.BlockSpec((1, QP, width), lambda b, t: (b, t, 0))


def _hyena_proj(xs, modl, gpre, win, conv_w, conv_b):
    bsz, t, _ = xs.shape
    nt = t // TM
    per = TM // HALO
    last = t // HALO - 1
    part = lambda w: jax.ShapeDtypeStruct((bsz, t // RADIX, w), BF16)
    return pl.pallas_call(
        _hyena_proj_kernel,
        grid=(bsz, nt),
        in_specs=[pl.BlockSpec((1, HALO, D), lambda b, i: (b, jnp.maximum(i * per - 1, 0), 0)),
                  _tok_spec(D),
                  pl.BlockSpec((1, HALO, D), lambda b, i: (b, jnp.minimum((i + 1) * per, last), 0)),
                  _mod_spec(), _const_spec((1, D)), _const_spec(win.shape),
                  _const_spec(conv_w.shape), _const_spec(conv_b.shape), _const_spec((TM, TM))],
        out_specs=[_class_spec(3 * C_WIDTH)] * RADIX + [_class_spec(C_WIDTH)] * RADIX,
        out_shape=[part(3 * C_WIDTH)] * RADIX + [part(C_WIDTH)] * RADIX,
        compiler_params=_cparams(2),
        name="hyena_proj",
    )(xs, xs, xs, modl, gpre, win, conv_w, conv_b, _class_perm(TILE_ORDER))


def _class_perm(order):
    r = np.arange(QP)
    src = np.concatenate([RADIX * r + rho for rho in order])
    m = np.zeros((TM, TM), np.float32)
    m[np.arange(TM), src] = 1.0
    return jnp.asarray(m, BF16)


def _hyena_out_kernel(*refs):
    o_refs, g_refs = refs[:RADIX], refs[RADIX:2 * RADIX]
    wout_ref, x_ref, mod_ref, gpost_ref, perm_ref, xo_ref = refs[2 * RADIX:]
    a = jnp.concatenate([o_refs[rho][0].astype(F32) * _silu(g_refs[rho][0].astype(F32)) for rho in range(RADIX)],
                        axis=0)
    a = _dot(perm_ref[...], a.astype(BF16)).astype(BF16)
    y = _dot(a, wout_ref[...])
    xo_ref[0] = x_ref[0] + mod_ref[0, 0][2:3] * (_rms(y) * gpost_ref[...])


def _hyena_out_proj(os, gs, wout, xs, modl, gpost):
    bsz, t, _ = xs.shape
    w = os[0].shape[-1]
    return pl.pallas_call(
        _hyena_out_kernel,
        grid=(bsz, t // TM),
        in_specs=[_class_spec(w)] * (2 * RADIX) + [_const_spec(wout.shape), _tok_spec(D), _mod_spec(),
                                                   _const_spec((1, D)), _const_spec((TM, TM))],
        out_specs=_tok_spec(D),
        out_shape=jax.ShapeDtypeStruct(xs.shape, F32),
        input_output_aliases={2 * RADIX + 1: 0},
        compiler_params=_cparams(2),
        name="hyena_out_proj_residual",
    )(*os, *gs, wout, xs, modl, gpost, _class_perm(range(RADIX)).T)


def _filter_kernel(z_ref, w1_ref, b1_ref, fr_ref, w2_ref, b2_ref, w3_ref, dl_ref, o_ref):
    hp = lax.Precision.HIGHEST
    z = z_ref[...]
    fr = fr_ref[...]
    h = jnp.sin(fr * (jnp.dot(z, w1_ref[...], precision=hp, preferred_element_type=F32) + b1_ref[...]))
    h = jnp.sin(fr * (jnp.dot(h, w2_ref[...], precision=hp, preferred_element_type=F32) + b2_ref[...]))
    h = jnp.dot(h, w3_ref[...], precision=hp, preferred_element_type=F32)
    o_ref[...] = (h * jnp.exp(-z[:, 0:1] * dl_ref[...])).astype(o_ref.dtype)


def _filters(n, w1, b1, fr, w2, b2, w3):
    t = np.linspace(0.0, 1.0, n, dtype=np.float32)[:, None]
    w = ((2.0 * math.pi / n) * np.arange(n, dtype=np.float32))[:, None].astype(np.float32)
    bands = np.linspace(1e-4, C_BANDS - 1, C_BANDS, dtype=np.float32)[None, :]
    z = np.zeros((n, LANE), np.float32)
    z[:, 0:1] = t
    z[:, 1:1 + C_BANDS] = np.cos(bands * w)
    z[:, 1 + C_BANDS:C_EMB] = -np.sin(bands * w)
    deltas = np.abs(np.linspace(C_MIN_DECAY, C_MAX_DECAY, C_WIDTH, dtype=np.float32))
    dl = np.tile(deltas, 4)[None, :]

    def pad(a, r, c):
        return jnp.zeros((r, c), F32).at[:a.shape[0], :a.shape[1]].set(a)

    tn = min(n, TM)
    nout = 4 * C_WIDTH
    cs = lambda shape: pl.BlockSpec(shape, lambda i: (0, 0))
    return pl.pallas_call(
        _filter_kernel,
        grid=(n // tn,),
        in_specs=[pl.BlockSpec((tn, LANE), lambda i: (i, 0)), cs((LANE, LANE)), cs((1, LANE)), cs((1, LANE)),
                  cs((LANE, LANE)), cs((1, LANE)), cs((LANE, nout)), cs((1, nout))],
        out_specs=pl.BlockSpec((tn, nout), lambda i: (i, 0)),
        out_shape=jax.ShapeDtypeStruct((n, nout), BF16),
        compiler_params=_cparams(1),
        name="hyena_filter_mlp",
    )(jnp.asarray(z), pad(w1, LANE, LANE), pad(b1[None], 1, LANE), pad(fr[None], 1, LANE),
      pad(w2, LANE, LANE), pad(b2[None], 1, LANE), pad(w3, LANE, nout), jnp.asarray(dl))


def _dft_matrix(n):
    f = np.arange(n, dtype=np.int64)[:, None]
    s = np.arange(n, dtype=np.int64)[None, :]
    ang = (2.0 * np.pi / (2 * n)) * ((f * s) % (2 * n)).astype(np.float64)
    cos = np.cos(ang)
    sin = np.sin(ang)
    sin[0, :] = np.where(np.arange(n) % 2 == 0, 1.0, -1.0)
    return cos.astype(np.float32), sin.astype(np.float32)


def _matmul_kernel(a_ref, b_ref, o_ref):
    o_ref[...] = _dot(a_ref[...], b_ref[...])


def _matmul(a, b, bm, bn):
    m, k = a.shape
    n = b.shape[1]
    assert m % bm == 0 and n % bn == 0
    return pl.pallas_call(
        _matmul_kernel,
        grid=(m // bm, n // bn),
        in_specs=[pl.BlockSpec((bm, k), lambda i, j: (i, 0)), pl.BlockSpec((k, bn), lambda i, j: (0, j))],
        out_specs=pl.BlockSpec((bm, bn), lambda i, j: (i, j)),
        out_shape=jax.ShapeDtypeStruct((m, n), F32),
        compiler_params=_cparams(2),
        name="filter_dft",
    )(a, b)


_PHASE = [(math.cos(math.pi * rho / 4), math.sin(math.pi * rho / 4),
           math.cos(3 * math.pi * rho / 4), math.sin(3 * math.pi * rho / 4)) for rho in range(RADIX)]


def _cmul(xr, xs, kr, ki):
    return xr * kr + xs * ki, xs * kr - xr * ki


def _hyena_conv_kernel(*refs, row0, q, aliased):
    nu = 3 * RADIX
    u_refs = refs[:nu]
    m_refs = refs[nu:nu + RADIX]
    mt_refs = refs[nu + RADIX:nu + 2 * RADIX]
    k_refs = refs[nu + 2 * RADIX:nu + 2 * RADIX + 8]
    ks_ref, fb_ref = refs[nu + 2 * RADIX + 8:nu + 2 * RADIX + 10]
    rest = refs[nu + 2 * RADIX + 10 + (RADIX if aliased else 0):]
    o_refs, z_refs = rest[:RADIX], rest[RADIX:]
    o = pl.program_id(2)
    rows = slice(row0, row0 + q)

    @pl.when(o == 0)
    def _():
        for rho in range(RADIX):
            z_refs[rho][...] = u_refs[3 * rho][0, rows, :]

    fwd = [_dot(m_refs[rho][...], z_refs[rho][...]) for rho in range(RADIX)]
    c = [g[:q] for g in fwd]
    sn = [g[q:] for g in fwd]
    ar, as_, br, bs = c[0] + c[2], sn[0] + sn[2], c[1] + c[3], sn[1] + sn[3]
    cr, cs, dr, ds = c[0] - c[2], sn[0] - sn[2], c[1] - c[3], sn[1] - sn[3]
    k = [r[0] for r in k_refs]
    y1r, y1s = _cmul(ar + br, as_ + bs, k[0], k[1])
    y2r, y2s = _cmul(cr - ds, cs + dr, k[2], k[3])
    y3r, y3s = _cmul(ar - br, bs - as_, k[4], k[5])
    y4r, y4s = _cmul(cr + ds, dr - cs, k[6], k[7])
    pr, ps, qr, qs = y1r + y3r, y1s - y3s, y1r - y3r, y1s + y3s
    rr, rs, tr, ts = y2r + y4r, y2s - y4s, y2r - y4r, y2s + y4s
    wr = [pr + rr, qr + ts, pr - rr, qr - ts]
    ws = [ps + rs, qs - tr, ps - rs, qs + tr]

    ksp = ks_ref[0]
    c0 = [x[0:8] for x in c]
    a0 = [x[0:8] for x in sn]
    y0 = (c0[0] + c0[1] + c0[2] + c0[3]) * ksp[0:1]
    yn = (c0[0] - c0[1] + c0[2] - c0[3]) * ksp[1:2]
    yhr, yhs = _cmul(c0[0] - c0[2], c0[1] - c0[3], ksp[2:3], ksp[3:4])
    xqr = sum(_PHASE[rho][0] * a0[rho] for rho in range(1, RADIX)) + a0[0]
    xqs = sum(_PHASE[rho][1] * a0[rho] for rho in range(1, RADIX))
    xgr = sum(_PHASE[rho][2] * a0[rho] for rho in range(1, RADIX)) + a0[0]
    xgs = sum(_PHASE[rho][3] * a0[rho] for rho in range(1, RADIX))
    yqr, yqs = _cmul(xqr, xqs, ksp[4:5], ksp[5:6])
    ygr, ygs = _cmul(xgr, xgs, ksp[6:7], ksp[7:8])
    first = lax.broadcasted_iota(jnp.int32, (8, 1), 0) == 0
    turn = [yhr, yhs, -yhr, -yhs]
    f_out = []
    for rho in range(RADIX):
        pc, psn, gc, gsn = _PHASE[rho]
        cos0 = y0 + (yn if rho % 2 == 0 else -yn) + turn[rho]
        sin0 = yqr * pc + yqs * psn + ygr * gc + ygs * gsn
        w_r = jnp.concatenate([jnp.where(first, cos0, wr[rho][0:8]), wr[rho][8:]], axis=0)
        w_s = jnp.concatenate([jnp.where(first, sin0, ws[rho][0:8]), ws[rho][8:]], axis=0)
        y = _dot(mt_refs[rho][...], jnp.concatenate([w_r, w_s], axis=0).astype(BF16))
        f_out.append(y + z_refs[rho][...].astype(F32) * fb_ref[0])

    @pl.when(o == 0)
    def _():
        for rho in range(RADIX):
            z_refs[rho][...] = (u_refs[3 * rho + 1][0, rows, :].astype(F32) * f_out[rho]).astype(BF16)

    @pl.when(o == 1)
    def _():
        for rho in range(RADIX):
            if row0 > 0:
                o_refs[rho][0, 0:row0, :] = jnp.zeros((row0, o_refs[rho].shape[2]), BF16)
            o_refs[rho][0, rows, :] = (u_refs[3 * rho + 2][0, rows, :].astype(F32) * f_out[rho]).astype(BF16)


def _hyena_conv(us, mats, planes, kspecial, fbias, *, row0, q, block_rows, prev=None):
    bsz, t4, _ = us[0].shape
    tc = MXU_W
    nct = C_WIDTH // tc
    u_spec = lambda which: pl.BlockSpec((1, block_rows, tc), lambda b, c, o: (b, 0, which * nct + c))
    m_spec = pl.BlockSpec((2 * q, q), lambda b, c, o: (0, 0))
    mt_spec = pl.BlockSpec((q, 2 * q), lambda b, c, o: (0, 0))
    p_spec = pl.BlockSpec((1, q, tc), lambda b, c, o: (o, 0, c))
    in_specs = ([u_spec(w) for _ in range(RADIX) for w in range(3)] + [m_spec] * RADIX + [mt_spec] * RADIX
                + [p_spec] * 8 + [pl.BlockSpec((1, 8, tc), lambda b, c, o: (o, 0, c)),
                                  pl.BlockSpec((1, 1, tc), lambda b, c, o: (o, 0, c))])
    args = [u for u in us for _ in range(3)] + list(mats) + list(planes) + [kspecial, fbias]
    aliases = {}
    if prev is not None:
        in_specs += [pl.BlockSpec(memory_space=pl.ANY)] * RADIX
        aliases = {len(args) + rho: rho for rho in range(RADIX)}
        args += list(prev)
    o_spec = pl.BlockSpec((1, block_rows, tc), lambda b, c, o: (b, 0, c))
    return pl.pallas_call(
        functools.partial(_hyena_conv_kernel, row0=row0, q=q, aliased=prev is not None),
        grid=(bsz, nct, 2),
        in_specs=in_specs,
        out_specs=[o_spec] * RADIX,
        out_shape=[jax.ShapeDtypeStruct((bsz, t4, C_WIDTH), BF16)] * RADIX,
        scratch_shapes=[pltpu.VMEM((q, tc), BF16)] * RADIX,
        input_output_aliases=aliases,
        compiler_params=_cparams(3),
        name="hyena_long_conv_q%d" % q,
    )(*args)


def _radix4_matrices(n):
    q = n // RADIX
    f = np.arange(q, dtype=np.int64)[:, None]
    r = np.arange(q, dtype=np.int64)[None, :]
    alt = np.where(np.arange(q) % 2 == 0, 1.0, -1.0)
    mats = []
    for rho in range(RADIX):
        ang = (2.0 * np.pi / (2 * n)) * ((f * (RADIX * r + rho)) % (2 * n)).astype(np.float64)
        cos, sin = np.cos(ang), np.sin(ang)
        sin[0, :] = alt
        mats.append(np.concatenate([cos, sin], axis=0).astype(np.float32))
    return tuple(jnp.asarray(m, BF16) for m in mats) + tuple(jnp.asarray(np.ascontiguousarray(m.T), BF16) for m in mats)


def _hyena_spectra(n, filt):
    h, q = n // 2, n // RADIX
    cos, sin = _dft_matrix(n)
    f = np.arange(q)
    groups = [f, h + f, np.maximum(n - f, 1) % n, h - f]
    special_c = [cos[0:1], sin[0:1], cos[h:h + 1], cos[q:q + 1], cos[h + q:h + q + 1]]
    special_s = [sin[h:h + 1], sin[q:q + 1], sin[h + q:h + q + 1]]
    rows = np.concatenate([cos[g] for g in groups] + [sin[g] for g in groups] + special_c + special_s
                          + [np.zeros((LANE - 8, n), np.float32)], axis=0)
    nr = 8 * q + LANE
    hspec = _matmul(jnp.asarray(rows, BF16), filt, 3 * LANE if nr % (3 * LANE) == 0 else nr, 1024)
    hf = hspec[:, :2 * C_WIDTH].reshape(nr, 2, C_WIDTH)
    hb = hspec[:, 2 * C_WIDTH:].reshape(nr, 2, C_WIDTH)
    scale = 2.0 / (2 * n)
    re = (hf + hb) * scale
    im = (hb - hf) * scale
    tr = lambda a: jnp.transpose(a, (1, 0, 2))
    planes = []
    for g in range(4):
        planes += [tr(re[g * q:(g + 1) * q]), tr(im[(4 + g) * q:(5 + g) * q])]
    s0 = 8 * q
    kspecial = jnp.stack([re[s0] * 0.5, re[s0 + 1] * 0.5, re[s0 + 2], im[s0 + 5],
                          re[s0 + 3], im[s0 + 6], re[s0 + 4], im[s0 + 7]], axis=1)
    return planes, kspecial


def _rope_tables(seq, layout):
    rows = seq // GRID_W
    row = np.repeat(np.arange(rows, dtype=np.float32), GRID_W)
    col = np.tile(np.arange(GRID_W, dtype=np.float32), rows)
    per_axis = 32
    inv = (ROPE_BASE ** (-np.arange(0, per_axis, 2, dtype=np.float32) / per_axis)).astype(np.float32)
    ang = np.concatenate([row[:, None] * inv, col[:, None] * inv], axis=-1)
    cos = np.concatenate([np.ones((CTX, 32), np.float32), np.cos(ang)], axis=0)
    sin = np.concatenate([np.zeros((CTX, 32), np.float32), np.sin(ang)], axis=0)
    one, zero = np.ones_like(cos), np.zeros_like(cos)
    if layout == "mla":
        a = [cos, cos, one, one]
        b = [zero, sin, zero, zero]
        c = [-sin, zero, zero, zero]
    else:
        a = [cos, cos, cos, cos]
        b = [zero, sin, zero, sin]
        c = [-sin, zero, -sin, zero]
    return tuple(jnp.asarray(np.concatenate(p, axis=1), F32) for p in (a, b, c))


def _swa_head_perm():
    cols = []
    for j in range(B_KV_HEADS // 2):
        for g in range(B_GROUP):
            for hk in (2 * j, 2 * j + 1):
                h = hk * B_GROUP + g
                cols.extend(range(h * B_HDIM, (h + 1) * B_HDIM))
    return np.asarray(cols, np.int32)


def _mla_weights(w_in, w_q, w_kv):
    c1 = A_Q_RANK + A_KV_RANK
    zpad = jnp.zeros((D, LANE - A_ROPE), F32)
    win = jnp.concatenate([w_in[:, :c1 + A_ROPE], zpad, w_in[:, c1 + A_ROPE:]], axis=1)
    qscale = (A_NOPE + A_ROPE) ** -0.5 * math.log2(math.e)
    wq = w_q.reshape(A_Q_RANK, A_HEADS, A_NOPE + A_ROPE) * qscale
    wq = jnp.concatenate([wq, jnp.zeros((A_Q_RANK, A_HEADS, A_HEAD_PAD - A_NOPE - A_ROPE), F32)], axis=-1)
    wq = wq.reshape(A_Q_RANK, A_HEADS * A_HEAD_PAD)
    return win.astype(BF16), wq.astype(BF16), w_kv.astype(BF16)


def kernel(x, c, ctx, c_ctx, w_mod, b_mod, g_pre, g_post, a_w_in, a_g_q, a_w_q, a_g_kv, a_w_kv, a_w_out, b_w_in, b_sink, b_w_out, c_w_in, c_conv_w, c_conv_b, c_f_w1, c_f_b1, c_f_freq, c_f_w2, c_f_b2, c_f_w3, c_filt_bias, c_w_out):
    bsz, seq, _ = x.shape
    assert ctx.shape[1] == CTX and (CTX + seq) % TP == 0 and seq % GRID_W == 0
    xsrc = (ctx, x)

    pad_rows = (-(bsz + 1)) % 8
    cond = jnp.concatenate([c, c_ctx[None], jnp.zeros((pad_rows, D), F32)], axis=0)
    mod = _modulation(cond, w_mod, b_mod)

    tabs_mla = _rope_tables(seq, "mla")
    tabs_swa = _rope_tables(seq, "swa")

    for layer in range(DEPTH):
        kind, j = layer % 3, layer // 3
        mx = mod[layer, :bsz].reshape(bsz, 3, D)
        mc = jnp.broadcast_to(mod[layer, bsz].reshape(1, 3, D), (bsz, 3, D))
        modl = jnp.stack([mc, mx], axis=0)
        gpre = g_pre[layer][None]
        gpost = g_post[layer][None]
        if kind == 0:
            win, wq, wkv = _mla_weights(a_w_in[j], a_w_q[j], a_w_kv[j])
            q, k, v, gate = _mla_proj(xsrc, modl, gpre, win, a_g_q[j][None], wq, a_g_kv[j][None], wkv, tabs_mla)
            xs = _mla_attn(q, k, v, gate, a_w_out[j].astype(BF16), xsrc, modl, gpost,
                           latents_only=layer == DEPTH - 1)
        elif kind == 1:
            xs, = xsrc
            perm = _swa_head_perm()
            qw, kw = B_HEADS * B_HDIM, B_KV_HEADS * B_HDIM
            w = b_w_in[j]
            win = jnp.concatenate([w[:, :qw][:, perm] * (B_HDIM ** -0.5 * math.log2(math.e)), w[:, qw:qw + 2 * kw],
                                   w[:, qw + 2 * kw:][:, perm]], axis=1).astype(BF16)
            q, k, v, gate = _swa_proj(xs, modl, gpre, win, tabs_swa)
            sink = (b_sink[j].astype(F32) * math.log2(math.e)).reshape(B_KV_HEADS, B_GROUP, 1, 1)
            sink_cols = jnp.broadcast_to(sink, (B_KV_HEADS, B_GROUP, B_BLOCK, 1)).reshape(B_KV_HEADS, B_GROUP * B_BLOCK, 1)
            xs = _swa_attn(q, k, v, sink_cols, gate, b_w_out[j][perm, :].astype(BF16), xs, modl, gpost)
        else:
            xs, = xsrc
            parts = _hyena_proj(xs, modl, gpre, c_w_in[j].astype(BF16), c_conv_w[j], c_conv_b[j][None])
            us, gs = parts[:RADIX], parts[RADIX:]
            fargs = (c_f_w1[j], c_f_b1[j], c_f_freq[j], c_f_w2[j], c_f_b2[j], c_f_w3[j])
            fbias = c_filt_bias[j].reshape(2, 1, C_WIDTH)
            os = _hyena_conv(us, _radix4_matrices(seq), *_hyena_spectra(seq, _filters(seq, *fargs)), fbias,
                             row0=CTX // RADIX, q=seq // RADIX, block_rows=(CTX + seq) // RADIX)
            os = _hyena_conv(us, _radix4_matrices(CTX), *_hyena_spectra(CTX, _filters(CTX, *fargs)), fbias,
                             row0=0, q=CTX // RADIX, block_rows=CTX // RADIX, prev=os)
            xs = _hyena_out_proj(os, gs, c_w_out[j].astype(BF16), xs, modl, gpost)
        xsrc = (xs,)
    return xs
```

```python
import functools
import math

import numpy as np
import jax
import jax.numpy as jnp
from jax import lax
from jax.experimental import pallas as pl
from jax.experimental.pallas import tpu as pltpu

F32 = jnp.float32
BF16 = jnp.bfloat16

D = 1024
DEPTH = 4
GRID_W = 64
CTX = 256
NORM_EPS = 1e-6
ROPE_BASE = 10000.0
NEG_INF = -1e30

A_HEADS = 8
A_Q_RANK = 512
A_KV_RANK = 256
A_NOPE = 128
A_ROPE = 64
A_VDIM = 128
A_HEAD_PAD = 256

B_HEADS = 16
B_KV_HEADS = 4
B_GROUP = 4
B_HDIM = 64
B_WINDOW = 128
B_BLOCK = 128

C_WIDTH = 1024
C_BANDS = 16
C_EMB = 1 + 2 * C_BANDS
C_FFN = 64
C_MIN_DECAY = math.log(1e-2) / 1.5
C_MAX_DECAY = math.log(1e-2) / 0.3

LANE = 128
MXU_W = 256
TM = 256
TP = 3 * TM
HALO = 8
VMEM_LIMIT = 56 * 1024 * 1024


def _cparams(n_axes):
    return pltpu.CompilerParams(dimension_semantics=("arbitrary",) * n_axes,
                                vmem_limit_bytes=VMEM_LIMIT)


def _rms(x):
    return x * lax.rsqrt(jnp.mean(x * x, axis=-1, keepdims=True) + NORM_EPS)


def _silu(g):
    return g / (1.0 + jnp.exp(-g))


def _dot(a, b):
    return jnp.dot(a, b, preferred_element_type=F32)


def _dot_nt(a, b):
    return lax.dot_general(a, b, (((1,), (1,)), ((), ())), preferred_element_type=F32)


def _rope128(x, a, b, c):
    return x * a + pltpu.roll(x, 32, 1) * b + pltpu.roll(x, LANE - 32, 1) * c


def _modnorm(x, mod_ref, gpre_ref):
    m = mod_ref[0, 0]
    return _rms(x) * gpre_ref[...] * (1.0 + m[1:2]) + m[0:1]


def _mod_kernel(c_ref, w_ref, b_ref, o_ref):
    a = _silu(c_ref[...])
    o_ref[0] = _dot(a.astype(BF16), w_ref[0].astype(BF16)) + b_ref[0]


def _modulation(cond, w_mod, b_mod):
    rows = cond.shape[0]
    return pl.pallas_call(
        _mod_kernel,
        grid=(DEPTH, 3),
        in_specs=[pl.BlockSpec((rows, D), lambda l, j: (0, 0)),
                  pl.BlockSpec((1, D, D), lambda l, j: (l, 0, j)),
                  pl.BlockSpec((1, 1, D), lambda l, j: (l, 0, j))],
        out_specs=pl.BlockSpec((1, rows, D), lambda l, j: (l, 0, j)),
        out_shape=jax.ShapeDtypeStruct((DEPTH, rows, 3 * D), F32),
        compiler_params=_cparams(2),
        name="adaln_modulation",
    )(cond, w_mod, b_mod.reshape(DEPTH, 1, 3 * D))


def _tok_spec(width, rows=TM, skip=0):
    return pl.BlockSpec((1, rows, width), lambda b, t: (b, t + skip, 0))


def _mod_spec(ctx_tiles=1, skip=0):
    return pl.BlockSpec((1, 1, 3, D), lambda b, t: (jnp.minimum((t + skip) // ctx_tiles, 1), b, 0, 0))


def _const_spec(shape):
    nd = len(shape)
    return pl.BlockSpec(shape, lambda b, t: (0,) * nd)


def _residual_specs(dual, rows=TM, skip=0):
    if not dual:
        return [_tok_spec(D, rows, skip)]
    per = CTX // rows
    return [pl.BlockSpec((1, rows, D), lambda b, t: (b, jnp.minimum(t, per - 1), 0)),
            pl.BlockSpec((1, rows, D), lambda b, t: (b, jnp.maximum(t - per, 0), 0))]


def _wide_specs(dual):
    n = TP // TM
    if not dual:
        return [pl.BlockSpec((1, TM, D), functools.partial(lambda j, b, t: (b, n * t + j, 0), j)) for j in range(n)]
    last = lambda j, b, t: (b, jnp.maximum(n * t + j - CTX // TM, 0), 0)
    return ([pl.BlockSpec((1, CTX, D), lambda b, t: (b, 0, 0))]
            + [pl.BlockSpec((1, TM, D), functools.partial(last, j)) for j in range(n)])


def _wide_modnorm(refs, dual, mod_ref, gpre_ref):
    t = pl.program_id(1)
    blocks = [r[0] for r in refs[1:]] if dual else [r[0] for r in refs]
    if dual:
        blocks[0] = jnp.where(t == 0, refs[0][0], blocks[0])
    x = jnp.concatenate(blocks, axis=0)
    is_ctx = jnp.logical_and(lax.broadcasted_iota(jnp.int32, (TP, 1), 0) < CTX, t == 0)
    mc, mx = mod_ref[0, 0], mod_ref[1, 0]
    scale = jnp.where(is_ctx, mc[1:2], mx[1:2])
    shift = jnp.where(is_ctx, mc[0:1], mx[0:1])
    return _rms(x) * gpre_ref[...] * (1.0 + scale) + shift


def _both_mod_spec():
    return pl.BlockSpec((2, 1, 3, D), lambda b, t: (0, b, 0, 0))


def _resident(shape):
    nd = len(shape)
    return pl.BlockSpec(shape, lambda b, t: (0,) * nd, pipeline_mode=pl.Buffered(1))


def _residual_tile(refs, dual, rows=TM):
    if not dual:
        return refs[0][0]
    return jnp.where(pl.program_id(1) < CTX // rows, refs[0][0], refs[1][0])


def _finish(o, gate_ref, wout_ref, x, mod_ref, gpost_ref):
    a = o.astype(F32) * _silu(gate_ref[0].astype(F32))
    y = _dot(a.astype(BF16), wout_ref[...])
    return x + mod_ref[0, 0][2:3] * (_rms(y) * gpost_ref[...])


def _mla_proj_kernel(*refs, dual):
    nx = TP // TM + (1 if dual else 0)
    (mod_ref, gpre_ref, win_ref, gq_ref, wq_ref, gkv_ref, wkv_ref,
     ra_ref, rb_ref, rc_ref, q_ref, k_ref, v_ref, gate_ref) = refs[nx:]
    h = _wide_modnorm(refs[:nx], dual, mod_ref, gpre_ref)
    p = _dot(h.astype(BF16), win_ref[...])
    c0, c1, c2 = A_Q_RANK, A_Q_RANK + A_KV_RANK, A_Q_RANK + A_KV_RANK + LANE
    gate_ref[0] = p[:, c2:].astype(BF16)
    qn = (_rms(p[:, :c0]) * gq_ref[...]).astype(BF16)
    kvn = (_rms(p[:, c0:c1]) * gkv_ref[...]).astype(BF16)
    ra, rb, rc = ra_ref[...], rb_ref[...], rc_ref[...]
    kr = _rope128(p[:, c1:c2], ra, rb, rc).astype(BF16)
    for hd in range(A_HEADS):
        q = _dot(qn, wq_ref[:, hd * A_HEAD_PAD:(hd + 1) * A_HEAD_PAD])
        q_ref[0, hd, :, 0:LANE] = q[:, 0:LANE].astype(BF16)
        q_ref[0, hd, :, LANE:2 * LANE] = _rope128(q[:, LANE:2 * LANE], ra, rb, rc).astype(BF16)
        kv = _dot(kvn, wkv_ref[:, hd * A_HEAD_PAD:(hd + 1) * A_HEAD_PAD])
        k_ref[0, hd, :, 0:LANE] = kv[:, 0:A_NOPE].astype(BF16)
        k_ref[0, hd, :, LANE:2 * LANE] = kr
        v_ref[0, hd] = kv[:, A_NOPE:].astype(BF16)


def _mla_proj(xsrc, modl, gpre, win, gq, wq, gkv, wkv, tabs):
    dual = len(xsrc) == 2
    bsz = xsrc[0].shape[0]
    t = tabs[0].shape[0]
    xargs = [xsrc[0]] + [xsrc[1]] * (TP // TM) if dual else [xsrc[0]] * (TP // TM)
    head = lambda w: pl.BlockSpec((1, A_HEADS, TP, w), lambda b, i: (b, 0, i, 0))
    tab = pl.BlockSpec((TP, LANE), lambda b, i: (i, 0))
    return pl.pallas_call(
        functools.partial(_mla_proj_kernel, dual=dual),
        grid=(bsz, t // TP),
        in_specs=[*_wide_specs(dual), _both_mod_spec(), _const_spec((1, D)), _resident(win.shape),
                  _const_spec((1, A_Q_RANK)), _resident(wq.shape),
                  _const_spec((1, A_KV_RANK)), _resident(wkv.shape), tab, tab, tab],
        out_specs=[head(A_HEAD_PAD), head(A_HEAD_PAD), head(A_VDIM), _tok_spec(A_HEADS * A_VDIM, TP)],
        out_shape=[jax.ShapeDtypeStruct((bsz, A_HEADS, t, A_HEAD_PAD), BF16),
                   jax.ShapeDtypeStruct((bsz, A_HEADS, t, A_HEAD_PAD), BF16),
                   jax.ShapeDtypeStruct((bsz, A_HEADS, t, A_VDIM), BF16),
                   jax.ShapeDtypeStruct((bsz, t, A_HEADS * A_VDIM), BF16)],
        compiler_params=_cparams(2),
        name="mla_proj",
    )(*xargs, modl, gpre, win, gq, wq, gkv, wkv, *tabs)


def _mla_attn_kernel(*refs, dual, latents_only):
    nx = 2 if dual else 1
    q_ref, k_ref, v_ref, gate_ref, wout_ref = refs[:5]
    mod_ref, gpost_ref, xo_ref, o_ref = refs[5 + nx:]

    def attend(nk):
        s_next = _dot_nt(q_ref[0, 0], k_ref[0, 0, :nk, :])
        for hd in range(A_HEADS):
            s = s_next
            if hd + 1 < A_HEADS:
                s_next = _dot_nt(q_ref[0, hd + 1], k_ref[0, hd + 1, :nk, :])
            m = jnp.max(s, axis=-1, keepdims=True)
            p = jnp.exp2(s - m)
            l = jnp.sum(p, axis=-1, keepdims=True)
            o = _dot(p.astype(BF16), v_ref[0, hd, :nk, :]) / l
            o_ref[:, hd * A_VDIM:(hd + 1) * A_VDIM] = o.astype(BF16)

    t_all = k_ref.shape[2]
    if latents_only:
        attend(t_all)
    else:
        is_ctx = pl.program_id(1) == 0
        pl.when(is_ctx)(lambda: attend(CTX))
        pl.when(jnp.logical_not(is_ctx))(lambda: attend(t_all))
    x = _residual_tile(refs[5:5 + nx], dual)
    xo_ref[0] = _finish(o_ref[...], gate_ref, wout_ref, x, mod_ref, gpost_ref)


def _mla_attn(q, k, v, gate, wout, xsrc, modl, gpost, latents_only):
    dual = len(xsrc) == 2
    bsz, nh, t, _ = q.shape
    skip = CTX // TM if latents_only else 0
    nt = t // TM - skip
    return pl.pallas_call(
        functools.partial(_mla_attn_kernel, dual=dual, latents_only=latents_only),
        grid=(bsz, nt),
        in_specs=[pl.BlockSpec((1, nh, TM, A_HEAD_PAD), lambda b, i: (b, 0, i + skip, 0)),
                  pl.BlockSpec((1, nh, t, A_HEAD_PAD), lambda b, i: (b, 0, 0, 0)),
                  pl.BlockSpec((1, nh, t, A_VDIM), lambda b, i: (b, 0, 0, 0)),
                  _tok_spec(nh * A_VDIM, skip=skip), _const_spec(wout.shape),
                  *_residual_specs(dual, skip=skip), _mod_spec(skip=skip), _const_spec((1, D))],
        out_specs=_tok_spec(D),
        out_shape=jax.ShapeDtypeStruct((bsz, nt * TM, D), F32),
        scratch_shapes=[pltpu.VMEM((TM, nh * A_VDIM), BF16)],
        input_output_aliases={} if (dual or latents_only) else {5: 0},
        compiler_params=_cparams(2),
        name="mla_attention_out",
    )(q, k, v, gate, wout, *xsrc, modl, gpost)


def _swa_proj_kernel(*refs):
    xrefs = refs[:TP // TM]
    mod_ref, gpre_ref, win_ref, ra_ref, rb_ref, rc_ref, q_ref, k_ref, v_ref, gate_ref = refs[TP // TM:]
    h = _wide_modnorm(xrefs, False, mod_ref, gpre_ref)
    p = _dot(h.astype(BF16), win_ref[...])
    ra, rb, rc = ra_ref[...], rb_ref[...], rc_ref[...]
    qw, kw = B_HEADS * B_HDIM, B_KV_HEADS * B_HDIM
    lane = lax.broadcasted_iota(jnp.int32, (1, LANE), 1)
    lo = lane < B_HDIM
    ones_col = jnp.broadcast_to(jnp.where(lane == 0, 1.0, 0.0), (TP, LANE)).astype(BF16)
    for j in range(B_KV_HEADS // 2):
        for g in range(B_GROUP):
            s = j * B_GROUP + g
            r = _rope128(p[:, s * LANE:(s + 1) * LANE], ra, rb, rc)
            for half in range(2):
                d = ((2 * j + half) * B_GROUP + g) * LANE
                keep = lo if half == 0 else jnp.logical_not(lo)
                q_ref[0, :, d:d + LANE] = jnp.where(keep, r, 0.0).astype(BF16)
        o = qw + j * LANE
        k_ref[0, :, j * LANE:(j + 1) * LANE] = _rope128(p[:, o:o + LANE], ra, rb, rc).astype(BF16)
        v_ref[0, :, 2 * j * LANE:(2 * j + 1) * LANE] = p[:, o + kw:o + kw + LANE].astype(BF16)
        v_ref[0, :, (2 * j + 1) * LANE:(2 * j + 2) * LANE] = ones_col
    gate_ref[0] = p[:, qw + 2 * kw:].astype(BF16)


def _swa_proj(xs, modl, gpre, win, tabs):
    bsz, t, _ = xs.shape
    qw, kw = B_HEADS * B_HDIM, B_KV_HEADS * B_HDIM
    tab = pl.BlockSpec((TP, LANE), lambda b, i: (i, 0))
    return pl.pallas_call(
        _swa_proj_kernel,
        grid=(bsz, t // TP),
        in_specs=[*_wide_specs(False), _both_mod_spec(), _const_spec((1, D)), _resident(win.shape), tab, tab, tab],
        out_specs=[_tok_spec(2 * qw, TP), _tok_spec(kw, TP), _tok_spec(2 * kw, TP), _tok_spec(qw, TP)],
        out_shape=[jax.ShapeDtypeStruct((bsz, t, 2 * qw), BF16),
                   jax.ShapeDtypeStruct((bsz, t, kw), BF16),
                   jax.ShapeDtypeStruct((bsz, t, 2 * kw), BF16),
                   jax.ShapeDtypeStruct((bsz, t, qw), BF16)],
        compiler_params=_cparams(2),
        name="swa_proj",
    )(*[xs] * (TP // TM), modl, gpre, win, *tabs)


def _swa_attn_kernel(q_ref, k_ref, v_ref, sink_ref, gate_ref, wout_ref, x_ref, mod_ref, gpost_ref, xo_ref, o_ref):
    i = pl.program_id(1)
    n_lat = k_ref.shape[1] - CTX
    band = 3 * B_BLOCK
    lo = lax.broadcasted_iota(jnp.int32, (1, LANE), 1) < B_HDIM

    def run(with_band):
        chains = [(sub, j, half) for sub in range(TM // B_BLOCK) for j in range(B_KV_HEADS // 2) for half in range(2)]
        window, keys = {}, {}

        def block_window(sub):
            if sub not in window:
                li = (i - CTX // TM) * (TM // B_BLOCK) + sub
                start = jnp.clip((li - 1) * B_BLOCK, 0, n_lat - band)
                off = li * B_BLOCK - start
                d0 = (lax.broadcasted_iota(jnp.int32, (B_BLOCK, band), 0)
                      - lax.broadcasted_iota(jnp.int32, (B_BLOCK, band), 1))
                mask = jnp.abs(d0 + off) <= B_WINDOW
                window[sub] = (mask, pl.multiple_of(CTX + start, B_BLOCK))
            return window[sub]

        def block_keys(sub, j):
            if (sub, j) not in keys:
                kcols = slice(j * LANE, (j + 1) * LANE)
                vcols = slice(2 * j * LANE, (2 * j + 2) * LANE)
                kk = k_ref[0, 0:CTX, kcols]
                vv = v_ref[0, 0:CTX, vcols]
                if with_band:
                    kstart = block_window(sub)[1]
                    kk = jnp.concatenate([kk, k_ref[0, pl.ds(kstart, band), kcols]], axis=0)
                    vv = jnp.concatenate([vv, v_ref[0, pl.ds(kstart, band), vcols]], axis=0)
                keys[(sub, j)] = (kk, vv)
            return keys[(sub, j)]

        def scores(chain):
            sub, j, half = chain
            hk = 2 * j + half
            r0 = sub * B_BLOCK
            qs = jnp.concatenate(
                [q_ref[0, r0:r0 + B_BLOCK, (hk * B_GROUP + g) * LANE:(hk * B_GROUP + g + 1) * LANE]
                 for g in range(B_GROUP)], axis=0)
            return _dot_nt(qs, block_keys(sub, j)[0])

        outs = []
        s_next = scores(chains[0])
        for n, (sub, j, half) in enumerate(chains):
            s = s_next
            if n + 1 < len(chains):
                s_next = scores(chains[n + 1])
            sk = sink_ref[2 * j + half]
            s_c = s[:, :CTX]
            m = jnp.maximum(jnp.max(s_c, axis=-1, keepdims=True), sk)
            if with_band:
                mask = block_window(sub)[0]
                s_b = jnp.concatenate(
                    [jnp.where(mask, s[g * B_BLOCK:(g + 1) * B_BLOCK, CTX:], NEG_INF) for g in range(B_GROUP)],
                    axis=0)
                m = jnp.maximum(m, jnp.max(s_b, axis=-1, keepdims=True))
                p = jnp.concatenate([jnp.exp2(s_c - m), jnp.exp2(s_b - m)], axis=1)
            else:
                p = jnp.exp2(s_c - m)
            pv = _dot(p.astype(BF16), block_keys(sub, j)[1])
            l = pv[:, LANE:LANE + 1] + jnp.exp2(sk - m)
            outs.append(pv[:, :LANE] / l)
            if half == 1:
                comb = jnp.where(lo, outs[-2], outs[-1])
                r0 = sub * B_BLOCK
                for g in range(B_GROUP):
                    s_out = j * B_GROUP + g
                    o_ref[r0:r0 + B_BLOCK, s_out * LANE:(s_out + 1) * LANE] = (
                        comb[g * B_BLOCK:(g + 1) * B_BLOCK].astype(BF16))

    is_ctx = i < CTX // TM
    pl.when(is_ctx)(lambda: run(False))
    pl.when(jnp.logical_not(is_ctx))(lambda: run(True))
    xo_ref[0] = _finish(o_ref[...], gate_ref, wout_ref, x_ref[0], mod_ref, gpost_ref)


def _swa_attn(q, k, v, sink_cols, gate, wout, xs, modl, gpost):
    bsz, t, qw2 = q.shape
    kw = k.shape[-1]
    qw = qw2 // 2
    return pl.pallas_call(
        _swa_attn_kernel,
        grid=(bsz, t // TM),
        in_specs=[pl.BlockSpec((1, TM, qw2), lambda b, i: (b, i, 0)),
                  pl.BlockSpec((1, t, kw), lambda b, i: (b, 0, 0)),
                  pl.BlockSpec((1, t, 2 * kw), lambda b, i: (b, 0, 0)),
                  pl.BlockSpec(sink_cols.shape, lambda b, i: (0, 0, 0)),
                  _tok_spec(qw), _const_spec(wout.shape), _tok_spec(D), _mod_spec(), _const_spec((1, D))],
        out_specs=_tok_spec(D),
        out_shape=jax.ShapeDtypeStruct(xs.shape, F32),
        scratch_shapes=[pltpu.VMEM((TM, qw), BF16)],
        input_output_aliases={6: 0},
        compiler_params=_cparams(2),
        name="swa_attention_out",
    )(q, k, v, sink_cols, gate, wout, xs, modl, gpost)


RADIX = 4
QP = TM // RADIX
TILE_ORDER = (3, 1, 2, 0)


def _hyena_proj_kernel(xp_ref, x_ref, xn_ref, mod_ref, gpre_ref, win_ref, cw_ref, cb_ref, perm_ref, *out_refs):
    u_refs, g_refs = out_refs[:RADIX], out_refs[RADIX:]
    t = pl.program_id(1)
    nt = pl.num_programs(1)
    hh = _dot(perm_ref[...], _modnorm(x_ref[0], mod_ref, gpre_ref).astype(BF16))
    h = jnp.concatenate([_modnorm(xp_ref[0], mod_ref, gpre_ref), hh, _modnorm(xn_ref[0], mod_ref, gpre_ref)], axis=0)
    p = _dot(h.astype(BF16), win_ref[...])
    cwid = 3 * C_WIDTH
    u = p[:, :cwid]
    start = {rho: HALO + k * QP for k, rho in enumerate(TILE_ORDER)}
    cls = {rho: u[start[rho]:start[rho] + QP] for rho in range(RADIX)}
    for rho in range(RADIX):
        g_refs[rho][0] = p[start[rho]:start[rho] + QP, cwid:].astype(BF16)
    before0 = pltpu.roll(u[0:HALO + QP], 1, 0)[HALO:]
    after3 = pltpu.roll(u[start[0]:], QP + HALO - 1, 0)[0:QP]
    r = lax.broadcasted_iota(jnp.int32, (QP, 1), 0)
    before0 = jnp.where(jnp.logical_and(r == 0, t <= 1), 0.0, before0)
    after3 = jnp.where(jnp.logical_and(r == QP - 1, jnp.logical_or(t == 0, t == nt - 1)), 0.0, after3)
    cw = cw_ref[...]
    cb = cb_ref[...]
    left = {0: before0, 1: cls[0], 2: cls[1], 3: cls[2]}
    right = {0: cls[1], 1: cls[2], 2: cls[3], 3: after3}
    for rho in range(RADIX):
        u_refs[rho][0] = (cb + left[rho] * cw[0:1] + cls[rho] * cw[1:2] + right[rho] * cw[2:3]).astype(BF16)


def _class_spec(width):
    return pl.BlockSpec((1, QP, width), lambda b, t: (b, t, 0))


def _hyena_proj(xs, modl, gpre, win, conv_w, conv_b):
    bsz, t, _ = xs.shape
    nt = t // TM
    per = TM // HALO
    last = t // HALO - 1
    part = lambda w: jax.ShapeDtypeStruct((bsz, t // RADIX, w), BF16)
    return pl.pallas_call(
        _hyena_proj_kernel,
        grid=(bsz, nt),
        in_specs=[pl.BlockSpec((1, HALO, D), lambda b, i: (b, jnp.maximum(i * per - 1, 0), 0)),
                  _tok_spec(D),
                  pl.BlockSpec((1, HALO, D), lambda b, i: (b, jnp.minimum((i + 1) * per, last), 0)),
                  _mod_spec(), _const_spec((1, D)), _const_spec(win.shape),
                  _const_spec(conv_w.shape), _const_spec(conv_b.shape), _const_spec((TM, TM))],
        out_specs=[_class_spec(3 * C_WIDTH)] * RADIX + [_class_spec(C_WIDTH)] * RADIX,
        out_shape=[part(3 * C_WIDTH)] * RADIX + [part(C_WIDTH)] * RADIX,
        compiler_params=_cparams(2),
        name="hyena_proj",
    )(xs, xs, xs, modl, gpre, win, conv_w, conv_b, _class_perm(TILE_ORDER))


def _class_perm(order):
    r = np.arange(QP)
    src = np.concatenate([RADIX * r + rho for rho in order])
    m = np.zeros((TM, TM), np.float32)
    m[np.arange(TM), src] = 1.0
    return jnp.asarray(m, BF16)


def _hyena_out_kernel(*refs):
    o_refs, g_refs = refs[:RADIX], refs[RADIX:2 * RADIX]
    wout_ref, x_ref, mod_ref, gpost_ref, perm_ref, xo_ref = refs[2 * RADIX:]
    a = jnp.concatenate([o_refs[rho][0].astype(F32) * _silu(g_refs[rho][0].astype(F32)) for rho in range(RADIX)],
                        axis=0)
    a = _dot(perm_ref[...], a.astype(BF16)).astype(BF16)
    y = _dot(a, wout_ref[...])
    xo_ref[0] = x_ref[0] + mod_ref[0, 0][2:3] * (_rms(y) * gpost_ref[...])


def _hyena_out_proj(os, gs, wout, xs, modl, gpost):
    bsz, t, _ = xs.shape
    w = os[0].shape[-1]
    return pl.pallas_call(
        _hyena_out_kernel,
        grid=(bsz, t // TM),
        in_specs=[_class_spec(w)] * (2 * RADIX) + [_const_spec(wout.shape), _tok_spec(D), _mod_spec(),
                                                   _const_spec((1, D)), _const_spec((TM, TM))],
        out_specs=_tok_spec(D),
        out_shape=jax.ShapeDtypeStruct(xs.shape, F32),
        input_output_aliases={2 * RADIX + 1: 0},
        compiler_params=_cparams(2),
        name="hyena_out_proj_residual",
    )(*os, *gs, wout, xs, modl, gpost, _class_perm(range(RADIX)).T)


def _filter_kernel(z_ref, w1_ref, b1_ref, fr_ref, w2_ref, b2_ref, w3_ref, dl_ref, o_ref):
    hp = lax.Precision.HIGHEST
    z = z_ref[...]
    fr = fr_ref[...]
    h = jnp.sin(fr * (jnp.dot(z, w1_ref[...], precision=hp, preferred_element_type=F32) + b1_ref[...]))
    h = jnp.sin(fr * (jnp.dot(h, w2_ref[...], precision=hp, preferred_element_type=F32) + b2_ref[...]))
    h = jnp.dot(h, w3_ref[...], precision=hp, preferred_element_type=F32)
    o_ref[...] = (h * jnp.exp(-z[:, 0:1] * dl_ref[...])).astype(o_ref.dtype)


def _filters(n, w1, b1, fr, w2, b2, w3):
    t = np.linspace(0.0, 1.0, n, dtype=np.float32)[:, None]
    w = ((2.0 * math.pi / n) * np.arange(n, dtype=np.float32))[:, None].astype(np.float32)
    bands = np.linspace(1e-4, C_BANDS - 1, C_BANDS, dtype=np.float32)[None, :]
    z = np.zeros((n, LANE), np.float32)
    z[:, 0:1] = t
    z[:, 1:1 + C_BANDS] = np.cos(bands * w)
    z[:, 1 + C_BANDS:C_EMB] = -np.sin(bands * w)
    deltas = np.abs(np.linspace(C_MIN_DECAY, C_MAX_DECAY, C_WIDTH, dtype=np.float32))
    dl = np.tile(deltas, 4)[None, :]

    def pad(a, r, c):
        return jnp.zeros((r, c), F32).at[:a.shape[0], :a.shape[1]].set(a)

    tn = min(n, TM)
    nout = 4 * C_WIDTH
    cs = lambda shape: pl.BlockSpec(shape, lambda i: (0, 0))
    return pl.pallas_call(
        _filter_kernel,
        grid=(n // tn,),
        in_specs=[pl.BlockSpec((tn, LANE), lambda i: (i, 0)), cs((LANE, LANE)), cs((1, LANE)), cs((1, LANE)),
                  cs((LANE, LANE)), cs((1, LANE)), cs((LANE, nout)), cs((1, nout))],
        out_specs=pl.BlockSpec((tn, nout), lambda i: (i, 0)),
        out_shape=jax.ShapeDtypeStruct((n, nout), BF16),
        compiler_params=_cparams(1),
        name="hyena_filter_mlp",
    )(jnp.asarray(z), pad(w1, LANE, LANE), pad(b1[None], 1, LANE), pad(fr[None], 1, LANE),
      pad(w2, LANE, LANE), pad(b2[None], 1, LANE), pad(w3, LANE, nout), jnp.asarray(dl))


def _dft_matrix(n):
    f = np.arange(n, dtype=np.int64)[:, None]
    s = np.arange(n, dtype=np.int64)[None, :]
    ang = (2.0 * np.pi / (2 * n)) * ((f * s) % (2 * n)).astype(np.float64)
    cos = np.cos(ang)
    sin = np.sin(ang)
    sin[0, :] = np.where(np.arange(n) % 2 == 0, 1.0, -1.0)
    return cos.astype(np.float32), sin.astype(np.float32)


def _matmul_kernel(a_ref, b_ref, o_ref):
    o_ref[...] = _dot(a_ref[...], b_ref[...])


def _matmul(a, b, bm, bn):
    m, k = a.shape
    n = b.shape[1]
    assert m % bm == 0 and n % bn == 0
    return pl.pallas_call(
        _matmul_kernel,
        grid=(m // bm, n // bn),
        in_specs=[pl.BlockSpec((bm, k), lambda i, j: (i, 0)), pl.BlockSpec((k, bn), lambda i, j: (0, j))],
        out_specs=pl.BlockSpec((bm, bn), lambda i, j: (i, j)),
        out_shape=jax.ShapeDtypeStruct((m, n), F32),
        compiler_params=_cparams(2),
        name="filter_dft",
    )(a, b)


_PHASE = [(math.cos(math.pi * rho / 4), math.sin(math.pi * rho / 4),
           math.cos(3 * math.pi * rho / 4), math.sin(3 * math.pi * rho / 4)) for rho in range(RADIX)]


def _cmul(xr, xs, kr, ki):
    return xr * kr + xs * ki, xs * kr - xr * ki


def _hyena_conv_kernel(*refs, row0, q, aliased):
    nu = 3 * RADIX
    u_refs = refs[:nu]
    m_refs = refs[nu:nu + RADIX]
    mt_refs = refs[nu + RADIX:nu + 2 * RADIX]
    k_refs = refs[nu + 2 * RADIX:nu + 2 * RADIX + 8]
    ks_ref, fb_ref = refs[nu + 2 * RADIX + 8:nu + 2 * RADIX + 10]
    rest = refs[nu + 2 * RADIX + 10 + (RADIX if aliased else 0):]
    o_refs, z_refs = rest[:RADIX], rest[RADIX:]
    o = pl.program_id(2)
    rows = slice(row0, row0 + q)

    @pl.when(o == 0)
    def _():
        for rho in range(RADIX):
            z_refs[rho][...] = u_refs[3 * rho][0, rows, :]

    ksp = ks_ref[0]
    first = lax.broadcasted_iota(jnp.int32, (8, 1), 0) == 0
    nblk = m_refs[0].shape[0]
    fbs = m_refs[0].shape[1] // 2

    def forward(j):
        return [_dot(m_refs[rho][j], z_refs[rho][...]) for rho in range(RADIX)]

    def spectral(j, fwd):
        c = [g[:fbs] for g in fwd]
        sn = [g[fbs:] for g in fwd]
        ar, as_, br, bs = c[0] + c[2], sn[0] + sn[2], c[1] + c[3], sn[1] + sn[3]
        cr, cs, dr, ds = c[0] - c[2], sn[0] - sn[2], c[1] - c[3], sn[1] - sn[3]
        k = [r[0, j * fbs:(j + 1) * fbs, :] for r in k_refs]
        y1r, y1s = _cmul(ar + br, as_ + bs, k[0], k[1])
        y2r, y2s = _cmul(cr - ds, cs + dr, k[2], k[3])
        y3r, y3s = _cmul(ar - br, bs - as_, k[4], k[5])
        y4r, y4s = _cmul(cr + ds, dr - cs, k[6], k[7])
        pr, ps, qr, qs = y1r + y3r, y1s - y3s, y1r - y3r, y1s + y3s
        rr, rs, tr, ts = y2r + y4r, y2s - y4s, y2r - y4r, y2s + y4s
        wr = [pr + rr, qr + ts, pr - rr, qr - ts]
        ws = [ps + rs, qs - tr, ps - rs, qs + tr]
        if j == 0:
            c0 = [x[0:8] for x in c]
            a0 = [x[0:8] for x in sn]
            y0 = (c0[0] + c0[1] + c0[2] + c0[3]) * ksp[0:1]
            yn = (c0[0] - c0[1] + c0[2] - c0[3]) * ksp[1:2]
            yhr, yhs = _cmul(c0[0] - c0[2], c0[1] - c0[3], ksp[2:3], ksp[3:4])
            xqr = sum(_PHASE[rho][0] * a0[rho] for rho in range(1, RADIX)) + a0[0]
            xqs = sum(_PHASE[rho][1] * a0[rho] for rho in range(1, RADIX))
            xgr = sum(_PHASE[rho][2] * a0[rho] for rho in range(1, RADIX)) + a0[0]
            xgs = sum(_PHASE[rho][3] * a0[rho] for rho in range(1, RADIX))
            yqr, yqs = _cmul(xqr, xqs, ksp[4:5], ksp[5:6])
            ygr, ygs = _cmul(xgr, xgs, ksp[6:7], ksp[7:8])
            turn = [yhr, yhs, -yhr, -yhs]
            for rho in range(RADIX):
                pc, psn, gc, gsn = _PHASE[rho]
                cos0 = y0 + (yn if rho % 2 == 0 else -yn) + turn[rho]
                sin0 = yqr * pc + yqs * psn + ygr * gc + ygs * gsn
                wr[rho] = jnp.concatenate([jnp.where(first, cos0, wr[rho][0:8]), wr[rho][8:]], axis=0)
                ws[rho] = jnp.concatenate([jnp.where(first, sin0, ws[rho][0:8]), ws[rho][8:]], axis=0)
        return [jnp.concatenate([wr[rho], ws[rho]], axis=0).astype(BF16) for rho in range(RADIX)]

    acc = [None] * RADIX
    fwd_next = forward(0)
    for j in range(nblk):
        fwd = fwd_next
        if j + 1 < nblk:
            fwd_next = forward(j + 1)
        w = spectral(j, fwd)
        for rho in range(RADIX):
            part = _dot(mt_refs[rho][j], w[rho])
            acc[rho] = part if acc[rho] is None else acc[rho] + part
    f_out = [acc[rho] + z_refs[rho][...].astype(F32) * fb_ref[0] for rho in range(RADIX)]

    @pl.when(o == 0)
    def _():
        for rho in range(RADIX):
            z_refs[rho][...] = (u_refs[3 * rho + 1][0, rows, :].astype(F32) * f_out[rho]).astype(BF16)

    @pl.when(o == 1)
    def _():
        for rho in range(RADIX):
            if row0 > 0:
                o_refs[rho][0, 0:row0, :] = jnp.zeros((row0, o_refs[rho].shape[2]), BF16)
            o_refs[rho][0, rows, :] = (u_refs[3 * rho + 2][0, rows, :].astype(F32) * f_out[rho]).astype(BF16)


def _hyena_conv(us, mats, planes, kspecial, fbias, *, row0, q, block_rows, tc, prev=None):
    bsz, t4, _ = us[0].shape
    nct = C_WIDTH // tc
    u_spec = lambda which: pl.BlockSpec((1, block_rows, tc), lambda b, c, o: (b, 0, which * nct + c))
    m_spec = pl.BlockSpec(mats[0].shape, lambda b, c, o: (0, 0, 0))
    mt_spec = pl.BlockSpec(mats[RADIX].shape, lambda b, c, o: (0, 0, 0))
    p_spec = pl.BlockSpec((1, q, tc), lambda b, c, o: (o, 0, c))
    in_specs = ([u_spec(w) for _ in range(RADIX) for w in range(3)] + [m_spec] * RADIX + [mt_spec] * RADIX
                + [p_spec] * 8 + [pl.BlockSpec((1, 8, tc), lambda b, c, o: (o, 0, c)),
                                  pl.BlockSpec((1, 1, tc), lambda b, c, o: (o, 0, c))])
    args = [u for u in us for _ in range(3)] + list(mats) + list(planes) + [kspecial, fbias]
    aliases = {}
    if prev is not None:
        in_specs += [pl.BlockSpec(memory_space=pl.ANY)] * RADIX
        aliases = {len(args) + rho: rho for rho in range(RADIX)}
        args += list(prev)
    o_spec = pl.BlockSpec((1, block_rows, tc), lambda b, c, o: (b, 0, c))
    return pl.pallas_call(
        functools.partial(_hyena_conv_kernel, row0=row0, q=q, aliased=prev is not None),
        grid=(bsz, nct, 2),
        in_specs=in_specs,
        out_specs=[o_spec] * RADIX,
        out_shape=[jax.ShapeDtypeStruct((bsz, t4, C_WIDTH), BF16)] * RADIX,
        scratch_shapes=[pltpu.VMEM((q, tc), BF16)] * RADIX,
        input_output_aliases=aliases,
        compiler_params=_cparams(3),
        name="hyena_long_conv_q%d" % q,
    )(*args)


def _radix4_matrices(n):
    q = n // RADIX
    fbs = min(q, LANE)
    nblk = q // fbs
    f = np.arange(q, dtype=np.int64)[:, None]
    r = np.arange(q, dtype=np.int64)[None, :]
    alt = np.where(np.arange(q) % 2 == 0, 1.0, -1.0)
    mats = []
    for rho in range(RADIX):
        ang = (2.0 * np.pi / (2 * n)) * ((f * (RADIX * r + rho)) % (2 * n)).astype(np.float64)
        cos, sin = np.cos(ang), np.sin(ang)
        sin[0, :] = alt
        mats.append(np.concatenate([cos.reshape(nblk, fbs, q), sin.reshape(nblk, fbs, q)], axis=1).astype(np.float32))
    tr = lambda m: np.ascontiguousarray(np.transpose(m, (0, 2, 1)))
    return tuple(jnp.asarray(m, BF16) for m in mats) + tuple(jnp.asarray(tr(m), BF16) for m in mats)


def _hyena_spectra(n, filt):
    h, q = n // 2, n // RADIX
    cos, sin = _dft_matrix(n)
    f = np.arange(q)
    groups = [f, h + f, np.maximum(n - f, 1) % n, h - f]
    special_c = [cos[0:1], sin[0:1], cos[h:h + 1], cos[q:q + 1], cos[h + q:h + q + 1]]
    special_s = [sin[h:h + 1], sin[q:q + 1], sin[h + q:h + q + 1]]
    rows = np.concatenate([cos[g] for g in groups] + [sin[g] for g in groups] + special_c + special_s
                          + [np.zeros((LANE - 8, n), np.float32)], axis=0)
    nr = 8 * q + LANE
    hspec = _matmul(jnp.asarray(rows, BF16), filt, 3 * LANE if nr % (3 * LANE) == 0 else nr, 1024)
    hf = hspec[:, :2 * C_WIDTH].reshape(nr, 2, C_WIDTH)
    hb = hspec[:, 2 * C_WIDTH:].reshape(nr, 2, C_WIDTH)
    scale = 2.0 / (2 * n)
    re = (hf + hb) * scale
    im = (hb - hf) * scale
    tr = lambda a: jnp.transpose(a, (1, 0, 2))
    planes = []
    for g in range(4):
        planes += [tr(re[g * q:(g + 1) * q]), tr(im[(4 + g) * q:(5 + g) * q])]
    s0 = 8 * q
    kspecial = jnp.stack([re[s0] * 0.5, re[s0 + 1] * 0.5, re[s0 + 2], im[s0 + 5],
                          re[s0 + 3], im[s0 + 6], re[s0 + 4], im[s0 + 7]], axis=1)
    return planes, kspecial


def _rope_tables(seq, layout):
    rows = seq // GRID_W
    row = np.repeat(np.arange(rows, dtype=np.float32), GRID_W)
    col = np.tile(np.arange(GRID_W, dtype=np.float32), rows)
    per_axis = 32
    inv = (ROPE_BASE ** (-np.arange(0, per_axis, 2, dtype=np.float32) / per_axis)).astype(np.float32)
    ang = np.concatenate([row[:, None] * inv, col[:, None] * inv], axis=-1)
    cos = np.concatenate([np.ones((CTX, 32), np.float32), np.cos(ang)], axis=0)
    sin = np.concatenate([np.zeros((CTX, 32), np.float32), np.sin(ang)], axis=0)
    one, zero = np.ones_like(cos), np.zeros_like(cos)
    if layout == "mla":
        a = [cos, cos, one, one]
        b = [zero, sin, zero, zero]
        c = [-sin, zero, zero, zero]
    else:
        a = [cos, cos, cos, cos]
        b = [zero, sin, zero, sin]
        c = [-sin, zero, -sin, zero]
    return tuple(jnp.asarray(np.concatenate(p, axis=1), F32) for p in (a, b, c))


def _swa_head_perm():
    cols = []
    for j in range(B_KV_HEADS // 2):
        for g in range(B_GROUP):
            for hk in (2 * j, 2 * j + 1):
                h = hk * B_GROUP + g
                cols.extend(range(h * B_HDIM, (h + 1) * B_HDIM))
    return np.asarray(cols, np.int32)


def _mla_weights(w_in, w_q, w_kv):
    c1 = A_Q_RANK + A_KV_RANK
    zpad = jnp.zeros((D, LANE - A_ROPE), F32)
    win = jnp.concatenate([w_in[:, :c1 + A_ROPE], zpad, w_in[:, c1 + A_ROPE:]], axis=1)
    qscale = (A_NOPE + A_ROPE) ** -0.5 * math.log2(math.e)
    wq = w_q.reshape(A_Q_RANK, A_HEADS, A_NOPE + A_ROPE) * qscale
    wq = jnp.concatenate([wq, jnp.zeros((A_Q_RANK, A_HEADS, A_HEAD_PAD - A_NOPE - A_ROPE), F32)], axis=-1)
    wq = wq.reshape(A_Q_RANK, A_HEADS * A_HEAD_PAD)
    return win.astype(BF16), wq.astype(BF16), w_kv.astype(BF16)


def kernel(x, c, ctx, c_ctx, w_mod, b_mod, g_pre, g_post, a_w_in, a_g_q, a_w_q, a_g_kv, a_w_kv, a_w_out, b_w_in, b_sink, b_w_out, c_w_in, c_conv_w, c_conv_b, c_f_w1, c_f_b1, c_f_freq, c_f_w2, c_f_b2, c_f_w3, c_filt_bias, c_w_out):
    bsz, seq, _ = x.shape
    assert ctx.shape[1] == CTX and (CTX + seq) % TP == 0 and seq % GRID_W == 0
    xsrc = (ctx, x)

    pad_rows = (-(bsz + 1)) % 8
    cond = jnp.concatenate([c, c_ctx[None], jnp.zeros((pad_rows, D), F32)], axis=0)
    mod = _modulation(cond, w_mod, b_mod)

    tabs_mla = _rope_tables(seq, "mla")
    tabs_swa = _rope_tables(seq, "swa")

    for layer in range(DEPTH):
        kind, j = layer % 3, layer // 3
        mx = mod[layer, :bsz].reshape(bsz, 3, D)
        mc = jnp.broadcast_to(mod[layer, bsz].reshape(1, 3, D), (bsz, 3, D))
        modl = jnp.stack([mc, mx], axis=0)
        gpre = g_pre[layer][None]
        gpost = g_post[layer][None]
        if kind == 0:
            win, wq, wkv = _mla_weights(a_w_in[j], a_w_q[j], a_w_kv[j])
            q, k, v, gate = _mla_proj(xsrc, modl, gpre, win, a_g_q[j][None], wq, a_g_kv[j][None], wkv, tabs_mla)
            xs = _mla_attn(q, k, v, gate, a_w_out[j].astype(BF16), xsrc, modl, gpost,
                           latents_only=layer == DEPTH - 1)
        elif kind == 1:
            xs, = xsrc
            perm = _swa_head_perm()
            qw, kw = B_HEADS * B_HDIM, B_KV_HEADS * B_HDIM
            w = b_w_in[j]
            win = jnp.concatenate([w[:, :qw][:, perm] * (B_HDIM ** -0.5 * math.log2(math.e)), w[:, qw:qw + 2 * kw],
                                   w[:, qw + 2 * kw:][:, perm]], axis=1).astype(BF16)
            q, k, v, gate = _swa_proj(xs, modl, gpre, win, tabs_swa)
            sink = (b_sink[j].astype(F32) * math.log2(math.e)).reshape(B_KV_HEADS, B_GROUP, 1, 1)
            sink_cols = jnp.broadcast_to(sink, (B_KV_HEADS, B_GROUP, B_BLOCK, 1)).reshape(B_KV_HEADS, B_GROUP * B_BLOCK, 1)
            xs = _swa_attn(q, k, v, sink_cols, gate, b_w_out[j][perm, :].astype(BF16), xs, modl, gpost)
        else:
            xs, = xsrc
            parts = _hyena_proj(xs, modl, gpre, c_w_in[j].astype(BF16), c_conv_w[j], c_conv_b[j][None])
            us, gs = parts[:RADIX], parts[RADIX:]
            fargs = (c_f_w1[j], c_f_b1[j], c_f_freq[j], c_f_w2[j], c_f_b2[j], c_f_w3[j])
            fbias = c_filt_bias[j].reshape(2, 1, C_WIDTH)
            os = _hyena_conv(us, _radix4_matrices(seq), *_hyena_spectra(seq, _filters(seq, *fargs)), fbias,
                             row0=CTX // RADIX, q=seq // RADIX, block_rows=(CTX + seq) // RADIX, tc=MXU_W)
            os = _hyena_conv(us, _radix4_matrices(CTX), *_hyena_spectra(CTX, _filters(CTX, *fargs)), fbias,
                             row0=0, q=CTX // RADIX, block_rows=CTX // RADIX, tc=C_WIDTH, prev=os)
            xs = _hyena_out_proj(os, gs, c_w_out[j].astype(BF16), xs, modl, gpost)
        xsrc = (xs,)
    return xs
```

```python
import functools
import math

import numpy as np
import jax
import jax.numpy as jnp
from jax import lax
from jax.experimental import pallas as pl
from jax.experimental.pallas import tpu as pltpu

F32 = jnp.float32
BF16 = jnp.bfloat16

D = 1024
DEPTH = 4
GRID_W = 64
CTX = 256
NORM_EPS = 1e-6
ROPE_BASE = 10000.0
NEG_INF = -1e30

A_HEADS = 8
A_Q_RANK = 512
A_KV_RANK = 256
A_NOPE = 128
A_ROPE = 64
A_VDIM = 128
A_HEAD_PAD = 256

B_HEADS = 16
B_KV_HEADS = 4
B_GROUP = 4
B_HDIM = 64
B_WINDOW = 128
B_BLOCK = 128

C_WIDTH = 1024
C_BANDS = 16
C_EMB = 1 + 2 * C_BANDS
C_FFN = 64
C_MIN_DECAY = math.log(1e-2) / 1.5
C_MAX_DECAY = math.log(1e-2) / 0.3

LANE = 128
MXU_W = 256
TM = 256
TP = 3 * TM
HALO = 8
VMEM_LIMIT = 56 * 1024 * 1024


def _cparams(n_axes):
    return pltpu.CompilerParams(dimension_semantics=("arbitrary",) * n_axes,
                                vmem_limit_bytes=VMEM_LIMIT)


def _rms(x):
    return x * lax.rsqrt(jnp.mean(x * x, axis=-1, keepdims=True) + NORM_EPS)


def _silu(g):
    return g / (1.0 + jnp.exp(-g))


def _dot(a, b):
    return jnp.dot(a, b, preferred_element_type=F32)


def _dot_nt(a, b):
    return lax.dot_general(a, b, (((1,), (1,)), ((), ())), preferred_element_type=F32)


def _rope128(x, a, b, c):
    return x * a + pltpu.roll(x, 32, 1) * b + pltpu.roll(x, LANE - 32, 1) * c


def _modnorm(x, mod_ref, gpre_ref):
    m = mod_ref[0, 0]
    return _rms(x) * gpre_ref[...] * (1.0 + m[1:2]) + m[0:1]


def _mod_kernel(c_ref, w_ref, b_ref, o_ref):
    a = _silu(c_ref[...])
    o_ref[0] = _dot(a.astype(BF16), w_ref[0].astype(BF16)) + b_ref[0]


def _modulation(cond, w_mod, b_mod):
    rows = cond.shape[0]
    return pl.pallas_call(
        _mod_kernel,
        grid=(DEPTH, 3),
        in_specs=[pl.BlockSpec((rows, D), lambda l, j: (0, 0)),
                  pl.BlockSpec((1, D, D), lambda l, j: (l, 0, j)),
                  pl.BlockSpec((1, 1, D), lambda l, j: (l, 0, j))],
        out_specs=pl.BlockSpec((1, rows, D), lambda l, j: (l, 0, j)),
        out_shape=jax.ShapeDtypeStruct((DEPTH, rows, 3 * D), F32),
        compiler_params=_cparams(2),
        name="adaln_modulation",
    )(cond, w_mod, b_mod.reshape(DEPTH, 1, 3 * D))


def _tok_spec(width, rows=TM, skip=0):
    return pl.BlockSpec((1, rows, width), lambda b, t: (b, t + skip, 0))


def _mod_spec(ctx_tiles=1, skip=0):
    return pl.BlockSpec((1, 1, 3, D), lambda b, t: (jnp.minimum((t + skip) // ctx_tiles, 1), b, 0, 0))


def _const_spec(shape):
    nd = len(shape)
    return pl.BlockSpec(shape, lambda b, t: (0,) * nd)


def _residual_specs(dual, rows=TM, skip=0):
    if not dual:
        return [_tok_spec(D, rows, skip)]
    per = CTX // rows
    return [pl.BlockSpec((1, rows, D), lambda b, t: (b, jnp.minimum(t, per - 1), 0)),
            pl.BlockSpec((1, rows, D), lambda b, t: (b, jnp.maximum(t - per, 0), 0))]


def _wide_specs(dual):
    n = TP // TM
    if not dual:
        return [pl.BlockSpec((1, TM, D), functools.partial(lambda j, b, t: (b, n * t + j, 0), j)) for j in range(n)]
    last = lambda j, b, t: (b, jnp.maximum(n * t + j - CTX // TM, 0), 0)
    return ([pl.BlockSpec((1, CTX, D), lambda b, t: (b, 0, 0))]
            + [pl.BlockSpec((1, TM, D), functools.partial(last, j)) for j in range(n)])


def _wide_modnorm(refs, dual, mod_ref, gpre_ref):
    t = pl.program_id(1)
    blocks = [r[0] for r in refs[1:]] if dual else [r[0] for r in refs]
    if dual:
        blocks[0] = jnp.where(t == 0, refs[0][0], blocks[0])
    x = jnp.concatenate(blocks, axis=0)
    is_ctx = jnp.logical_and(lax.broadcasted_iota(jnp.int32, (TP, 1), 0) < CTX, t == 0)
    mc, mx = mod_ref[0, 0], mod_ref[1, 0]
    scale = jnp.where(is_ctx, mc[1:2], mx[1:2])
    shift = jnp.where(is_ctx, mc[0:1], mx[0:1])
    return _rms(x) * gpre_ref[...] * (1.0 + scale) + shift


def _both_mod_spec():
    return pl.BlockSpec((2, 1, 3, D), lambda b, t: (0, b, 0, 0))


def _resident(shape):
    nd = len(shape)
    return pl.BlockSpec(shape, lambda b, t: (0,) * nd, pipeline_mode=pl.Buffered(1))


def _residual_tile(refs, dual, rows=TM):
    if not dual:
        return refs[0][0]
    return jnp.where(pl.program_id(1) < CTX // rows, refs[0][0], refs[1][0])


def _finish(o, gate_ref, wout_ref, x, mod_ref, gpost_ref):
    a = o.astype(F32) * _silu(gate_ref[0].astype(F32))
    y = _dot(a.astype(BF16), wout_ref[...])
    return x + mod_ref[0, 0][2:3] * (_rms(y) * gpost_ref[...])


def _mla_proj_kernel(*refs, dual):
    nx = TP // TM + (1 if dual else 0)
    (mod_ref, gpre_ref, win_ref, gq_ref, wq_ref, gkv_ref, wkv_ref,
     ra_ref, rb_ref, rc_ref, q_ref, k_ref, v_ref, gate_ref) = refs[nx:]
    h = _wide_modnorm(refs[:nx], dual, mod_ref, gpre_ref)
    p = _dot(h.astype(BF16), win_ref[...])
    c0, c1, c2 = A_Q_RANK, A_Q_RANK + A_KV_RANK, A_Q_RANK + A_KV_RANK + LANE
    gate_ref[0] = p[:, c2:].astype(BF16)
    qn = (_rms(p[:, :c0]) * gq_ref[...]).astype(BF16)
    kvn = (_rms(p[:, c0:c1]) * gkv_ref[...]).astype(BF16)
    ra, rb, rc = ra_ref[...], rb_ref[...], rc_ref[...]
    kr = _rope128(p[:, c1:c2], ra, rb, rc).astype(BF16)
    for hd in range(A_HEADS):
        q = _dot(qn, wq_ref[:, hd * A_HEAD_PAD:(hd + 1) * A_HEAD_PAD])
        q_ref[0, hd, :, 0:LANE] = q[:, 0:LANE].astype(BF16)
        q_ref[0, hd, :, LANE:2 * LANE] = _rope128(q[:, LANE:2 * LANE], ra, rb, rc).astype(BF16)
        kv = _dot(kvn, wkv_ref[:, hd * A_HEAD_PAD:(hd + 1) * A_HEAD_PAD])
        k_ref[0, hd, :, 0:LANE] = kv[:, 0:A_NOPE].astype(BF16)
        k_ref[0, hd, :, LANE:2 * LANE] = kr
        v_ref[0, hd] = kv[:, A_NOPE:].astype(BF16)


def _mla_proj(xsrc, modl, gpre, win, gq, wq, gkv, wkv, tabs):
    dual = len(xsrc) == 2
    bsz = xsrc[0].shape[0]
    t = tabs[0].shape[0]
    xargs = [xsrc[0]] + [xsrc[1]] * (TP // TM) if dual else [xsrc[0]] * (TP // TM)
    head = lambda w: pl.BlockSpec((1, A_HEADS, TP, w), lambda b, i: (b, 0, i, 0))
    tab = pl.BlockSpec((TP, LANE), lambda b, i: (i, 0))
    return pl.pallas_call(
        functools.partial(_mla_proj_kernel, dual=dual),
        grid=(bsz, t // TP),
        in_specs=[*_wide_specs(dual), _both_mod_spec(), _const_spec((1, D)), _resident(win.shape),
                  _const_spec((1, A_Q_RANK)), _resident(wq.shape),
                  _const_spec((1, A_KV_RANK)), _resident(wkv.shape), tab, tab, tab],
        out_specs=[head(A_HEAD_PAD), head(A_HEAD_PAD), head(A_VDIM), _tok_spec(A_HEADS * A_VDIM, TP)],
        out_shape=[jax.ShapeDtypeStruct((bsz, A_HEADS, t, A_HEAD_PAD), BF16),
                   jax.ShapeDtypeStruct((bsz, A_HEADS, t, A_HEAD_PAD), BF16),
                   jax.ShapeDtypeStruct((bsz, A_HEADS, t, A_VDIM), BF16),
                   jax.ShapeDtypeStruct((bsz, t, A_HEADS * A_VDIM), BF16)],
        compiler_params=_cparams(2),
        name="mla_proj",
    )(*xargs, modl, gpre, win, gq, wq, gkv, wkv, *tabs)


def _mla_attn_kernel(*refs, dual, latents_only):
    nx = 2 if dual else 1
    q_ref, k_ref, v_ref, gate_ref, wout_ref = refs[:5]
    mod_ref, gpost_ref, xo_ref, o_ref = refs[5 + nx:]

    def attend(nk):
        s_next = _dot_nt(q_ref[0, 0], k_ref[0, 0, :nk, :])
        for hd in range(A_HEADS):
            s = s_next
            if hd + 1 < A_HEADS:
                s_next = _dot_nt(q_ref[0, hd + 1], k_ref[0, hd + 1, :nk, :])
            m = jnp.max(s, axis=-1, keepdims=True)
            p = jnp.exp2(s - m)
            l = jnp.sum(p, axis=-1, keepdims=True)
            o = _dot(p.astype(BF16), v_ref[0, hd, :nk, :]) / l
            o_ref[:, hd * A_VDIM:(hd + 1) * A_VDIM] = o.astype(BF16)

    t_all = k_ref.shape[2]
    if latents_only:
        attend(t_all)
    else:
        is_ctx = pl.program_id(1) == 0
        pl.when(is_ctx)(lambda: attend(CTX))
        pl.when(jnp.logical_not(is_ctx))(lambda: attend(t_all))
    x = _residual_tile(refs[5:5 + nx], dual)
    xo_ref[0] = _finish(o_ref[...], gate_ref, wout_ref, x, mod_ref, gpost_ref)


def _mla_attn(q, k, v, gate, wout, xsrc, modl, gpost, latents_only):
    dual = len(xsrc) == 2
    bsz, nh, t, _ = q.shape
    skip = CTX // TM if latents_only else 0
    nt = t // TM - skip
    return pl.pallas_call(
        functools.partial(_mla_attn_kernel, dual=dual, latents_only=latents_only),
        grid=(bsz, nt),
        in_specs=[pl.BlockSpec((1, nh, TM, A_HEAD_PAD), lambda b, i: (b, 0, i + skip, 0)),
                  pl.BlockSpec((1, nh, t, A_HEAD_PAD), lambda b, i: (b, 0, 0, 0)),
                  pl.BlockSpec((1, nh, t, A_VDIM), lambda b, i: (b, 0, 0, 0)),
                  _tok_spec(nh * A_VDIM, skip=skip), _const_spec(wout.shape),
                  *_residual_specs(dual, skip=skip), _mod_spec(skip=skip), _const_spec((1, D))],
        out_specs=_tok_spec(D),
        out_shape=jax.ShapeDtypeStruct((bsz, nt * TM, D), F32),
        scratch_shapes=[pltpu.VMEM((TM, nh * A_VDIM), BF16)],
        input_output_aliases={} if (dual or latents_only) else {5: 0},
        compiler_params=_cparams(2),
        name="mla_attention_out",
    )(q, k, v, gate, wout, *xsrc, modl, gpost)


def _swa_proj_kernel(*refs):
    xrefs = refs[:TP // TM]
    mod_ref, gpre_ref, win_ref, ra_ref, rb_ref, rc_ref, q_ref, k_ref, v_ref, gate_ref = refs[TP // TM:]
    h = _wide_modnorm(xrefs, False, mod_ref, gpre_ref)
    p = _dot(h.astype(BF16), win_ref[...])
    ra, rb, rc = ra_ref[...], rb_ref[...], rc_ref[...]
    qw, kw = B_HEADS * B_HDIM, B_KV_HEADS * B_HDIM
    lane = lax.broadcasted_iota(jnp.int32, (1, LANE), 1)
    lo = lane < B_HDIM
    ones_col = jnp.broadcast_to(jnp.where(lane == 0, 1.0, 0.0), (TP, LANE)).astype(BF16)
    for j in range(B_KV_HEADS // 2):
        for g in range(B_GROUP):
            s = j * B_GROUP + g
            r = _rope128(p[:, s * LANE:(s + 1) * LANE], ra, rb, rc)
            for half in range(2):
                d = ((2 * j + half) * B_GROUP + g) * LANE
                keep = lo if half == 0 else jnp.logical_not(lo)
                q_ref[0, :, d:d + LANE] = jnp.where(keep, r, 0.0).astype(BF16)
        o = qw + j * LANE
        k_ref[0, :, j * LANE:(j + 1) * LANE] = _rope128(p[:, o:o + LANE], ra, rb, rc).astype(BF16)
        v_ref[0, :, 2 * j * LANE:(2 * j + 1) * LANE] = p[:, o + kw:o + kw + LANE].astype(BF16)
        v_ref[0, :, (2 * j + 1) * LANE:(2 * j + 2) * LANE] = ones_col
    gate_ref[0] = p[:, qw + 2 * kw:].astype(BF16)


def _swa_proj(xs, modl, gpre, win, tabs):
    bsz, t, _ = xs.shape
    qw, kw = B_HEADS * B_HDIM, B_KV_HEADS * B_HDIM
    tab = pl.BlockSpec((TP, LANE), lambda b, i: (i, 0))
    return pl.pallas_call(
        _swa_proj_kernel,
        grid=(bsz, t // TP),
        in_specs=[*_wide_specs(False), _both_mod_spec(), _const_spec((1, D)), _resident(win.shape), tab, tab, tab],
        out_specs=[_tok_spec(2 * qw, TP), _tok_spec(kw, TP), _tok_spec(2 * kw, TP), _tok_spec(qw, TP)],
        out_shape=[jax.ShapeDtypeStruct((bsz, t, 2 * qw), BF16),
                   jax.ShapeDtypeStruct((bsz, t, kw), BF16),
                   jax.ShapeDtypeStruct((bsz, t, 2 * kw), BF16),
                   jax.ShapeDtypeStruct((bsz, t, qw), BF16)],
        compiler_params=_cparams(2),
        name="swa_proj",
    )(*[xs] * (TP // TM), modl, gpre, win, *tabs)


def _swa_attn_kernel(q_ref, k_ref, v_ref, sink_ref, gate_ref, wout_ref, x_ref, mod_ref, gpost_ref, xo_ref, o_ref):
    i = pl.program_id(1)
    n_lat = k_ref.shape[1] - CTX
    band = 3 * B_BLOCK
    lo = lax.broadcasted_iota(jnp.int32, (1, LANE), 1) < B_HDIM

    def run(with_band):
        chains = [(sub, j, half) for sub in range(TM // B_BLOCK) for j in range(B_KV_HEADS // 2) for half in range(2)]
        window, keys = {}, {}

        def block_window(sub):
            if sub not in window:
                li = (i - CTX // TM) * (TM // B_BLOCK) + sub
                start = jnp.clip((li - 1) * B_BLOCK, 0, n_lat - band)
                off = li * B_BLOCK - start
                d0 = (lax.broadcasted_iota(jnp.int32, (B_BLOCK, band), 0)
                      - lax.broadcasted_iota(jnp.int32, (B_BLOCK, band), 1))
                mask = jnp.abs(d0 + off) <= B_WINDOW
                window[sub] = (mask, pl.multiple_of(CTX + start, B_BLOCK))
            return window[sub]

        def block_keys(sub, j):
            if (sub, j) not in keys:
                kcols = slice(j * LANE, (j + 1) * LANE)
                vcols = slice(2 * j * LANE, (2 * j + 2) * LANE)
                kk = k_ref[0, 0:CTX, kcols]
                vv = v_ref[0, 0:CTX, vcols]
                if with_band:
                    kstart = block_window(sub)[1]
                    kk = jnp.concatenate([kk, k_ref[0, pl.ds(kstart, band), kcols]], axis=0)
                    vv = jnp.concatenate([vv, v_ref[0, pl.ds(kstart, band), vcols]], axis=0)
                keys[(sub, j)] = (kk, vv)
            return keys[(sub, j)]

        def scores(chain):
            sub, j, half = chain
            hk = 2 * j + half
            r0 = sub * B_BLOCK
            qs = jnp.concatenate(
                [q_ref[0, r0:r0 + B_BLOCK, (hk * B_GROUP + g) * LANE:(hk * B_GROUP + g + 1) * LANE]
                 for g in range(B_GROUP)], axis=0)
            return _dot_nt(qs, block_keys(sub, j)[0])

        outs = []
        s_next = scores(chains[0])
        for n, (sub, j, half) in enumerate(chains):
            s = s_next
            if n + 1 < len(chains):
                s_next = scores(chains[n + 1])
            sk = sink_ref[2 * j + half]
            if with_band:
                mask = block_window(sub)[0]
                s_b = jnp.concatenate(
                    [jnp.where(mask, s[g * B_BLOCK:(g + 1) * B_BLOCK, CTX:], NEG_INF) for g in range(B_GROUP)],
                    axis=0)
                s = jnp.concatenate([s[:, :CTX], s_b], axis=1)
            m = jnp.maximum(jnp.max(s, axis=-1, keepdims=True), sk)
            p = jnp.exp2(s - m)
            pv = _dot(p.astype(BF16), block_keys(sub, j)[1])
            l = pv[:, LANE:LANE + 1] + jnp.exp2(sk - m)
            outs.append(pv[:, :LANE] / l)
            if half == 1:
                comb = jnp.where(lo, outs[-2], outs[-1])
                r0 = sub * B_BLOCK
                for g in range(B_GROUP):
                    s_out = j * B_GROUP + g
                    o_ref[r0:r0 + B_BLOCK, s_out * LANE:(s_out + 1) * LANE] = (
                        comb[g * B_BLOCK:(g + 1) * B_BLOCK].astype(BF16))

    is_ctx = i < CTX // TM
    pl.when(is_ctx)(lambda: run(False))
    pl.when(jnp.logical_not(is_ctx))(lambda: run(True))
    xo_ref[0] = _finish(o_ref[...], gate_ref, wout_ref, x_ref[0], mod_ref, gpost_ref)


def _swa_attn(q, k, v, sink_cols, gate, wout, xs, modl, gpost):
    bsz, t, qw2 = q.shape
    kw = k.shape[-1]
    qw = qw2 // 2
    return pl.pallas_call(
        _swa_attn_kernel,
        grid=(bsz, t // TM),
        in_specs=[pl.BlockSpec((1, TM, qw2), lambda b, i: (b, i, 0)),
                  pl.BlockSpec((1, t, kw), lambda b, i: (b, 0, 0)),
                  pl.BlockSpec((1, t, 2 * kw), lambda b, i: (b, 0, 0)),
                  pl.BlockSpec(sink_cols.shape, lambda b, i: (0, 0, 0)),
                  _tok_spec(qw), _const_spec(wout.shape), _tok_spec(D), _mod_spec(), _const_spec((1, D))],
        out_specs=_tok_spec(D),
        out_shape=jax.ShapeDtypeStruct(xs.shape, F32),
        scratch_shapes=[pltpu.VMEM((TM, qw), BF16)],
        input_output_aliases={6: 0},
        compiler_params=_cparams(2),
        name="swa_attention_out",
    )(q, k, v, sink_cols, gate, wout, xs, modl, gpost)


RADIX = 4
QP = TM // RADIX
TILE_ORDER = (3, 1, 2, 0)


def _hyena_proj_kernel(xp_ref, x_ref, xn_ref, mod_ref, gpre_ref, win_ref, cw_ref, cb_ref, perm_ref, *out_refs):
    u_refs, g_refs = out_refs[:RADIX], out_refs[RADIX:]
    t = pl.program_id(1)
    nt = pl.num_programs(1)
    hh = _dot(perm_ref[...], _modnorm(x_ref[0], mod_ref, gpre_ref).astype(BF16))
    h = jnp.concatenate([_modnorm(xp_ref[0], mod_ref, gpre_ref), hh, _modnorm(xn_ref[0], mod_ref, gpre_ref)], axis=0)
    p = _dot(h.astype(BF16), win_ref[...])
    cwid = 3 * C_WIDTH
    u = p[:, :cwid]
    start = {rho: HALO + k * QP for k, rho in enumerate(TILE_ORDER)}
    cls = {rho: u[start[rho]:start[rho] + QP] for rho in range(RADIX)}
    for rho in range(RADIX):
        g_refs[rho][0] = p[start[rho]:start[rho] + QP, cwid:].astype(BF16)
    before0 = pltpu.roll(u[0:HALO + QP], 1, 0)[HALO:]
    after3 = pltpu.roll(u[start[0]:], QP + HALO - 1, 0)[0:QP]
    r = lax.broadcasted_iota(jnp.int32, (QP, 1), 0)
    before0 = jnp.where(jnp.logical_and(r == 0, t <= 1), 0.0, before0)
    after3 = jnp.where(jnp.logical_and(r == QP - 1, jnp.logical_or(t == 0, t == nt - 1)), 0.0, after3)
    cw = cw_ref[...]
    cb = cb_ref[...]
    left = {0: before0, 1: cls[0], 2: cls[1], 3: cls[2]}
    right = {0: cls[1], 1: cls[2], 2: cls[3], 3: after3}
    for rho in range(RADIX):
        u_refs[rho][0] = (cb + left[rho] * cw[0:1] + cls[rho] * cw[1:2] + right[rho] * cw[2:3]).astype(BF16)


def _class_spec(width):
    return pl.BlockSpec((1, QP, width), lambda b, t: (b, t, 0))


def _hyena_proj(xs, modl, gpre, win, conv_w, conv_b):
    bsz, t, _ = xs.shape
    nt = t // TM
    per = TM // HALO
    last = t // HALO - 1
    part = lambda w: jax.ShapeDtypeStruct((bsz, t // RADIX, w), BF16)
    return pl.pallas_call(
        _hyena_proj_kernel,
        grid=(bsz, nt),
        in_specs=[pl.BlockSpec((1, HALO, D), lambda b, i: (b, jnp.maximum(i * per - 1, 0), 0)),
                  _tok_spec(D),
                  pl.BlockSpec((1, HALO, D), lambda b, i: (b, jnp.minimum((i + 1) * per, last), 0)),
                  _mod_spec(), _const_spec((1, D)), _const_spec(win.shape),
                  _const_spec(conv_w.shape), _const_spec(conv_b.shape), _const_spec((TM, TM))],
        out_specs=[_class_spec(3 * C_WIDTH)] * RADIX + [_class_spec(C_WIDTH)] * RADIX,
        out_shape=[part(3 * C_WIDTH)] * RADIX + [part(C_WIDTH)] * RADIX,
        compiler_params=_cparams(2),
        name="hyena_proj",
    )(xs, xs, xs, modl, gpre, win, conv_w, conv_b, _class_perm(TILE_ORDER))


def _class_perm(order):
    r = np.arange(QP)
    src = np.concatenate([RADIX * r + rho for rho in order])
    m = np.zeros((TM, TM), np.float32)
    m[np.arange(TM), src] = 1.0
    return jnp.asarray(m, BF16)


def _hyena_out_kernel(*refs):
    o_refs, g_refs = refs[:RADIX], refs[RADIX:2 * RADIX]
    wout_ref, x_ref, mod_ref, gpost_ref, perm_ref, xo_ref = refs[2 * RADIX:]
    a = jnp.concatenate([o_refs[rho][0].astype(F32) * _silu(g_refs[rho][0].astype(F32)) for rho in range(RADIX)],
                        axis=0)
    a = _dot(perm_ref[...], a.astype(BF16)).astype(BF16)
    y = _dot(a, wout_ref[...])
    xo_ref[0] = x_ref[0] + mod_ref[0, 0][2:3] * (_rms(y) * gpost_ref[...])


def _hyena_out_proj(os, gs, wout, xs, modl, gpost):
    bsz, t, _ = xs.shape
    w = os[0].shape[-1]
    return pl.pallas_call(
        _hyena_out_kernel,
        grid=(bsz, t // TM),
        in_specs=[_class_spec(w)] * (2 * RADIX) + [_const_spec(wout.shape), _tok_spec(D), _mod_spec(),
                                                   _const_spec((1, D)), _const_spec((TM, TM))],
        out_specs=_tok_spec(D),
        out_shape=jax.ShapeDtypeStruct(xs.shape, F32),
        input_output_aliases={2 * RADIX + 1: 0},
        compiler_params=_cparams(2),
        name="hyena_out_proj_residual",
    )(*os, *gs, wout, xs, modl, gpost, _class_perm(range(RADIX)).T)


def _filter_kernel(z_ref, w1_ref, b1_ref, fr_ref, w2_ref, b2_ref, w3_ref, dl_ref, o_ref):
    hp = lax.Precision.HIGHEST
    z = z_ref[...]
    fr = fr_ref[...]
    h = jnp.sin(fr * (jnp.dot(z, w1_ref[...], precision=hp, preferred_element_type=F32) + b1_ref[...]))
    h = jnp.sin(fr * (jnp.dot(h, w2_ref[...], precision=hp, preferred_element_type=F32) + b2_ref[...]))
    h = jnp.dot(h, w3_ref[...], precision=hp, preferred_element_type=F32)
    o_ref[...] = (h * jnp.exp(-z[:, 0:1] * dl_ref[...])).astype(o_ref.dtype)


def _filters(n, w1, b1, fr, w2, b2, w3):
    t = np.linspace(0.0, 1.0, n, dtype=np.float32)[:, None]
    w = ((2.0 * math.pi / n) * np.arange(n, dtype=np.float32))[:, None].astype(np.float32)
    bands = np.linspace(1e-4, C_BANDS - 1, C_BANDS, dtype=np.float32)[None, :]
    z = np.zeros((n, LANE), np.float32)
    z[:, 0:1] = t
    z[:, 1:1 + C_BANDS] = np.cos(bands * w)
    z[:, 1 + C_BANDS:C_EMB] = -np.sin(bands * w)
    deltas = np.abs(np.linspace(C_MIN_DECAY, C_MAX_DECAY, C_WIDTH, dtype=np.float32))
    dl = np.tile(deltas, 4)[None, :]

    def pad(a, r, c):
        return jnp.zeros((r, c), F32).at[:a.shape[0], :a.shape[1]].set(a)

    tn = min(n, TM)
    nout = 4 * C_WIDTH
    cs = lambda shape: pl.BlockSpec(shape, lambda i: (0, 0))
    return pl.pallas_call(
        _filter_kernel,
        grid=(n // tn,),
        in_specs=[pl.BlockSpec((tn, LANE), lambda i: (i, 0)), cs((LANE, LANE)), cs((1, LANE)), cs((1, LANE)),
                  cs((LANE, LANE)), cs((1, LANE)), cs((LANE, nout)), cs((1, nout))],
        out_specs=pl.BlockSpec((tn, nout), lambda i: (i, 0)),
        out_shape=jax.ShapeDtypeStruct((n, nout), BF16),
        compiler_params=_cparams(1),
        name="hyena_filter_mlp",
    )(jnp.asarray(z), pad(w1, LANE, LANE), pad(b1[None], 1, LANE), pad(fr[None], 1, LANE),
      pad(w2, LANE, LANE), pad(b2[None], 1, LANE), pad(w3, LANE, nout), jnp.asarray(dl))


def _dft_matrix(n):
    f = np.arange(n, dtype=np.int64)[:, None]
    s = np.arange(n, dtype=np.int64)[None, :]
    ang = (2.0 * np.pi / (2 * n)) * ((f * s) % (2 * n)).astype(np.float64)
    cos = np.cos(ang)
    sin = np.sin(ang)
    sin[0, :] = np.where(np.arange(n) % 2 == 0, 1.0, -1.0)
    return cos.astype(np.float32), sin.astype(np.float32)


def _matmul_kernel(a_ref, b_ref, o_ref):
    o_ref[...] = _dot(a_ref[...], b_ref[...])


def _matmul(a, b, bm, bn):
    m, k = a.shape
    n = b.shape[1]
    assert m % bm == 0 and n % bn == 0
    return pl.pallas_call(
        _matmul_kernel,
        grid=(m // bm, n // bn),
        in_specs=[pl.BlockSpec((bm, k), lambda i, j: (i, 0)), pl.BlockSpec((k, bn), lambda i, j: (0, j))],
        out_specs=pl.BlockSpec((bm, bn), lambda i, j: (i, j)),
        out_shape=jax.ShapeDtypeStruct((m, n), F32),
        compiler_params=_cparams(2),
        name="filter_dft",
    )(a, b)


_PHASE = [(math.cos(math.pi * rho / 4), math.sin(math.pi * rho / 4),
           math.cos(3 * math.pi * rho / 4), math.sin(3 * math.pi * rho / 4)) for rho in range(RADIX)]


def _cmul(xr, xs, kr, ki):
    return xr * kr + xs * ki, xs * kr - xr * ki


def _hyena_conv_kernel(*refs, row0, q, aliased):
    nu = 3 * RADIX
    u_refs = refs[:nu]
    m_refs = refs[nu:nu + RADIX]
    mt_refs = refs[nu + RADIX:nu + 2 * RADIX]
    k_refs = refs[nu + 2 * RADIX:nu + 2 * RADIX + 8]
    ks_ref, fb_ref = refs[nu + 2 * RADIX + 8:nu + 2 * RADIX + 10]
    rest = refs[nu + 2 * RADIX + 10 + (RADIX if aliased else 0):]
    o_refs, z_refs = rest[:RADIX], rest[RADIX:]
    o = pl.program_id(2)
    rows = slice(row0, row0 + q)

    @pl.when(o == 0)
    def _():
        for rho in range(RADIX):
            z_refs[rho][...] = u_refs[3 * rho][0, rows, :]

    ksp = ks_ref[0]
    first = lax.broadcasted_iota(jnp.int32, (8, 1), 0) == 0
    nblk = m_refs[0].shape[0]
    fbs = m_refs[0].shape[1] // 2

    def forward(j):
        return [_dot(m_refs[rho][j], z_refs[rho][...]) for rho in range(RADIX)]

    def spectral(j, fwd):
        c = [g[:fbs] for g in fwd]
        sn = [g[fbs:] for g in fwd]
        ar, as_, br, bs = c[0] + c[2], sn[0] + sn[2], c[1] + c[3], sn[1] + sn[3]
        cr, cs, dr, ds = c[0] - c[2], sn[0] - sn[2], c[1] - c[3], sn[1] - sn[3]
        k = [r[0, j * fbs:(j + 1) * fbs, :] for r in k_refs]
        y1r, y1s = _cmul(ar + br, as_ + bs, k[0], k[1])
        y2r, y2s = _cmul(cr - ds, cs + dr, k[2], k[3])
        y3r, y3s = _cmul(ar - br, bs - as_, k[4], k[5])
        y4r, y4s = _cmul(cr + ds, dr - cs, k[6], k[7])
        pr, ps, qr, qs = y1r + y3r, y1s - y3s, y1r - y3r, y1s + y3s
        rr, rs, tr, ts = y2r + y4r, y2s - y4s, y2r - y4r, y2s + y4s
        wr = [pr + rr, qr + ts, pr - rr, qr - ts]
        ws = [ps + rs, qs - tr, ps - rs, qs + tr]
        if j == 0:
            c0 = [x[0:8] for x in c]
            a0 = [x[0:8] for x in sn]
            y0 = (c0[0] + c0[1] + c0[2] + c0[3]) * ksp[0:1]
            yn = (c0[0] - c0[1] + c0[2] - c0[3]) * ksp[1:2]
            yhr, yhs = _cmul(c0[0] - c0[2], c0[1] - c0[3], ksp[2:3], ksp[3:4])
            xqr = sum(_PHASE[rho][0] * a0[rho] for rho in range(1, RADIX)) + a0[0]
            xqs = sum(_PHASE[rho][1] * a0[rho] for rho in range(1, RADIX))
            xgr = sum(_PHASE[rho][2] * a0[rho] for rho in range(1, RADIX)) + a0[0]
            xgs = sum(_PHASE[rho][3] * a0[rho] for rho in range(1, RADIX))
            yqr, yqs = _cmul(xqr, xqs, ksp[4:5], ksp[5:6])
            ygr, ygs = _cmul(xgr, xgs, ksp[6:7], ksp[7:8])
            turn = [yhr, yhs, -yhr, -yhs]
            for rho in range(RADIX):
                pc, psn, gc, gsn = _PHASE[rho]
                cos0 = y0 + (yn if rho % 2 == 0 else -yn) + turn[rho]
                sin0 = yqr * pc + yqs * psn + ygr * gc + ygs * gsn
                wr[rho] = jnp.concatenate([jnp.where(first, cos0, wr[rho][0:8]), wr[rho][8:]], axis=0)
                ws[rho] = jnp.concatenate([jnp.where(first, sin0, ws[rho][0:8]), ws[rho][8:]], axis=0)
        return [jnp.concatenate([wr[rho], ws[rho]], axis=0).astype(BF16) for rho in range(RADIX)]

    acc = [None] * RADIX
    fwd_next = forward(0)
    for j in range(nblk):
        fwd = fwd_next
        if j + 1 < nblk:
            fwd_next = forward(j + 1)
        w = spectral(j, fwd)
        for rho in range(RADIX):
            part = _dot(mt_refs[rho][j], w[rho])
            acc[rho] = part if acc[rho] is None else acc[rho] + part
    f_out = [acc[rho] + z_refs[rho][...].astype(F32) * fb_ref[0] for rho in range(RADIX)]

    @pl.when(o == 0)
    def _():
        for rho in range(RADIX):
            z_refs[rho][...] = (u_refs[3 * rho + 1][0, rows, :].astype(F32) * f_out[rho]).astype(BF16)

    @pl.when(o == 1)
    def _():
        for rho in range(RADIX):
            if row0 > 0:
                o_refs[rho][0, 0:row0, :] = jnp.zeros((row0, o_refs[rho].shape[2]), BF16)
            o_refs[rho][0, rows, :] = (u_refs[3 * rho + 2][0, rows, :].astype(F32) * f_out[rho]).astype(BF16)


def _hyena_conv(us, mats, planes, kspecial, fbias, *, row0, q, block_rows, tc, prev=None):
    bsz, t4, _ = us[0].shape
    nct = C_WIDTH // tc
    u_spec = lambda which: pl.BlockSpec((1, block_rows, tc), lambda b, c, o: (b, 0, which * nct + c))
    m_spec = pl.BlockSpec(mats[0].shape, lambda b, c, o: (0, 0, 0))
    mt_spec = pl.BlockSpec(mats[RADIX].shape, lambda b, c, o: (0, 0, 0))
    p_spec = pl.BlockSpec((1, q, tc), lambda b, c, o: (o, 0, c))
    in_specs = ([u_spec(w) for _ in range(RADIX) for w in range(3)] + [m_spec] * RADIX + [mt_spec] * RADIX
                + [p_spec] * 8 + [pl.BlockSpec((1, 8, tc), lambda b, c, o: (o, 0, c)),
                                  pl.BlockSpec((1, 1, tc), lambda b, c, o: (o, 0, c))])
    args = [u for u in us for _ in range(3)] + list(mats) + list(planes) + [kspecial, fbias]
    aliases = {}
    if prev is not None:
        in_specs += [pl.BlockSpec(memory_space=pl.ANY)] * RADIX
        aliases = {len(args) + rho: rho for rho in range(RADIX)}
        args += list(prev)
    o_spec = pl.BlockSpec((1, block_rows, tc), lambda b, c, o: (b, 0, c))
    return pl.pallas_call(
        functools.partial(_hyena_conv_kernel, row0=row0, q=q, aliased=prev is not None),
        grid=(bsz, nct, 2),
        in_specs=in_specs,
        out_specs=[o_spec] * RADIX,
        out_shape=[jax.ShapeDtypeStruct((bsz, t4, C_WIDTH), BF16)] * RADIX,
        scratch_shapes=[pltpu.VMEM((q, tc), BF16)] * RADIX,
        input_output_aliases=aliases,
        compiler_params=_cparams(3),
        name="hyena_long_conv_q%d" % q,
    )(*args)


def _radix4_matrices(n):
    q = n // RADIX
    fbs = min(q, LANE)
    nblk = q // fbs
    f = np.arange(q, dtype=np.int64)[:, None]
    r = np.arange(q, dtype=np.int64)[None, :]
    alt = np.where(np.arange(q) % 2 == 0, 1.0, -1.0)
    mats = []
    for rho in range(RADIX):
        ang = (2.0 * np.pi / (2 * n)) * ((f * (RADIX * r + rho)) % (2 * n)).astype(np.float64)
        cos, sin = np.cos(ang), np.sin(ang)
        sin[0, :] = alt
        mats.append(np.concatenate([cos.reshape(nblk, fbs, q), sin.reshape(nblk, fbs, q)], axis=1).astype(np.float32))
    tr = lambda m: np.ascontiguousarray(np.transpose(m, (0, 2, 1)))
    return tuple(jnp.asarray(m, BF16) for m in mats) + tuple(jnp.asarray(tr(m), BF16) for m in mats)


def _hyena_spectra(n, filt):
    h, q = n // 2, n // RADIX
    cos, sin = _dft_matrix(n)
    f = np.arange(q)
    groups = [f, h + f, np.maximum(n - f, 1) % n, h - f]
    special_c = [cos[0:1], sin[0:1], cos[h:h + 1], cos[q:q + 1], cos[h + q:h + q + 1]]
    special_s = [sin[h:h + 1], sin[q:q + 1], sin[h + q:h + q + 1]]
    rows = np.concatenate([cos[g] for g in groups] + [sin[g] for g in groups] + special_c + special_s
                          + [np.zeros((LANE - 8, n), np.float32)], axis=0)
    nr = 8 * q + LANE
    hspec = _matmul(jnp.asarray(rows, BF16), filt, 3 * LANE if nr % (3 * LANE) == 0 else nr, 1024)
    hf = hspec[:, :2 * C_WIDTH].reshape(nr, 2, C_WIDTH)
    hb = hspec[:, 2 * C_WIDTH:].reshape(nr, 2, C_WIDTH)
    scale = 2.0 / (2 * n)
    re = (hf + hb) * scale
    im = (hb - hf) * scale
    tr = lambda a: jnp.transpose(a, (1, 0, 2))
    planes = []
    for g in range(4):
        planes += [tr(re[g * q:(g + 1) * q]), tr(im[(4 + g) * q:(5 + g) * q])]
    s0 = 8 * q
    kspecial = jnp.stack([re[s0] * 0.5, re[s0 + 1] * 0.5, re[s0 + 2], im[s0 + 5],
                          re[s0 + 3], im[s0 + 6], re[s0 + 4], im[s0 + 7]], axis=1)
    return planes, kspecial


def _rope_tables(seq, layout):
    rows = seq // GRID_W
    row = np.repeat(np.arange(rows, dtype=np.float32), GRID_W)
    col = np.tile(np.arange(GRID_W, dtype=np.float32), rows)
    per_axis = 32
    inv = (ROPE_BASE ** (-np.arange(0, per_axis, 2, dtype=np.float32) / per_axis)).astype(np.float32)
    ang = np.concatenate([row[:, None] * inv, col[:, None] * inv], axis=-1)
    cos = np.concatenate([np.ones((CTX, 32), np.float32), np.cos(ang)], axis=0)
    sin = np.concatenate([np.zeros((CTX, 32), np.float32), np.sin(ang)], axis=0)
    one, zero = np.ones_like(cos), np.zeros_like(cos)
    if layout == "mla":
        a = [cos, cos, one, one]
        b = [zero, sin, zero, zero]
        c = [-sin, zero, zero, zero]
    else:
        a = [cos, cos, cos, cos]
        b = [zero, sin, zero, sin]
        c = [-sin, zero, -sin, zero]
    return tuple(jnp.asarray(np.concatenate(p, axis=1), F32) for p in (a, b, c))


def _swa_head_perm():
    cols = []
    for j in range(B_KV_HEADS // 2):
        for g in range(B_GROUP):
            for hk in (2 * j, 2 * j + 1):
                h = hk * B_GROUP + g
                cols.extend(range(h * B_HDIM, (h + 1) * B_HDIM))
    return np.asarray(cols, np.int32)


def _mla_weights(w_in, w_q, w_kv):
    c1 = A_Q_RANK + A_KV_RANK
    zpad = jnp.zeros((D, LANE - A_ROPE), F32)
    win = jnp.concatenate([w_in[:, :c1 + A_ROPE], zpad, w_in[:, c1 + A_ROPE:]], axis=1)
    qscale = (A_NOPE + A_ROPE) ** -0.5 * math.log2(math.e)
    wq = w_q.reshape(A_Q_RANK, A_HEADS, A_NOPE + A_ROPE) * qscale
    wq = jnp.concatenate([wq, jnp.zeros((A_Q_RANK, A_HEADS, A_HEAD_PAD - A_NOPE - A_ROPE), F32)], axis=-1)
    wq = wq.reshape(A_Q_RANK, A_HEADS * A_HEAD_PAD)
    return win.astype(BF16), wq.astype(BF16), w_kv.astype(BF16)


def kernel(x, c, ctx, c_ctx, w_mod, b_mod, g_pre, g_post, a_w_in, a_g_q, a_w_q, a_g_kv, a_w_kv, a_w_out, b_w_in, b_sink, b_w_out, c_w_in, c_conv_w, c_conv_b, c_f_w1, c_f_b1, c_f_freq, c_f_w2, c_f_b2, c_f_w3, c_filt_bias, c_w_out):
    bsz, seq, _ = x.shape
    assert ctx.shape[1] == CTX and (CTX + seq) % TP == 0 and seq % GRID_W == 0
    xsrc = (ctx, x)

    pad_rows = (-(bsz + 1)) % 8
    cond = jnp.concatenate([c, c_ctx[None], jnp.zeros((pad_rows, D), F32)], axis=0)
    mod = _modulation(cond, w_mod, b_mod)

    tabs_mla = _rope_tables(seq, "mla")
    tabs_swa = _rope_tables(seq, "swa")

    for layer in range(DEPTH):
        kind, j = layer % 3, layer // 3
        mx = mod[layer, :bsz].reshape(bsz, 3, D)
        mc = jnp.broadcast_to(mod[layer, bsz].reshape(1, 3, D), (bsz, 3, D))
        modl = jnp.stack([mc, mx], axis=0)
        gpre = g_pre[layer][None]
        gpost = g_post[layer][None]
        if kind == 0:
            win, wq, wkv = _mla_weights(a_w_in[j], a_w_q[j], a_w_kv[j])
            q, k, v, gate = _mla_proj(xsrc, modl, gpre, win, a_g_q[j][None], wq, a_g_kv[j][None], wkv, tabs_mla)
            xs = _mla_attn(q, k, v, gate, a_w_out[j].astype(BF16), xsrc, modl, gpost,
                           latents_only=layer == DEPTH - 1)
        elif kind == 1:
            xs, = xsrc
            perm = _swa_head_perm()
            qw, kw = B_HEADS * B_HDIM, B_KV_HEADS * B_HDIM
            w = b_w_in[j]
            win = jnp.concatenate([w[:, :qw][:, perm] * (B_HDIM ** -0.5 * math.log2(math.e)), w[:, qw:qw + 2 * kw],
                                   w[:, qw + 2 * kw:][:, perm]], axis=1).astype(BF16)
            q, k, v, gate = _swa_proj(xs, modl, gpre, win, tabs_swa)
            sink = (b_sink[j].astype(F32) * math.log2(math.e)).reshape(B_KV_HEADS, B_GROUP, 1, 1)
            sink_cols = jnp.broadcast_to(sink, (B_KV_HEADS, B_GROUP, B_BLOCK, 1)).reshape(B_KV_HEADS, B_GROUP * B_BLOCK, 1)
            xs = _swa_attn(q, k, v, sink_cols, gate, b_w_out[j][perm, :].astype(BF16), xs, modl, gpost)
        else:
            xs, = xsrc
            parts = _hyena_proj(xs, modl, gpre, c_w_in[j].astype(BF16), c_conv_w[j], c_conv_b[j][None])
            us, gs = parts[:RADIX], parts[RADIX:]
            fargs = (c_f_w1[j], c_f_b1[j], c_f_freq[j], c_f_w2[j], c_f_b2[j], c_f_w3[j])
            fbias = c_filt_bias[j].reshape(2, 1, C_WIDTH)
            os = _hyena_conv(us, _radix4_matrices(seq), *_hyena_spectra(seq, _filters(seq, *fargs)), fbias,
                             row0=CTX // RADIX, q=seq // RADIX, block_rows=(CTX + seq) // RADIX, tc=MXU_W)
            os = _hyena_conv(us, _radix4_matrices(CTX), *_hyena_spectra(CTX, _filters(CTX, *fargs)), fbias,
                             row0=0, q=CTX // RADIX, block_rows=CTX // RADIX, tc=C_WIDTH, prev=os)
            xs = _hyena_out_proj(os, gs, c_w_out[j].astype(BF16), xs, modl, gpost)
        xsrc = (xs,)
    return xs
```

```python
import functools
import math

import numpy as np
import jax
import jax.numpy as jnp
from jax import lax
from jax.experimental import pallas as pl
from jax.experimental.pallas import tpu as pltpu

F32 = jnp.float32
BF16 = jnp.bfloat16

D = 1024
DEPTH = 4
GRID_W = 64
CTX = 256
NORM_EPS = 1e-6
ROPE_BASE = 10000.0
NEG_INF = -1e30

A_HEADS = 8
A_Q_RANK = 512
A_KV_RANK = 256
A_NOPE = 128
A_ROPE = 64
A_VDIM = 128
A_HEAD_PAD = 256

B_HEADS = 16
B_KV_HEADS = 4
B_GROUP = 4
B_HDIM = 64
B_WINDOW = 128
B_BLOCK = 128

C_WIDTH = 1024
C_BANDS = 16
C_EMB = 1 + 2 * C_BANDS
C_FFN = 64
C_MIN_DECAY = math.log(1e-2) / 1.5
C_MAX_DECAY = math.log(1e-2) / 0.3

LANE = 128
MXU_W = 256
TM = 256
TP = 3 * TM
HALO = 8
VMEM_LIMIT = 56 * 1024 * 1024


def _cparams(n_axes):
    return pltpu.CompilerParams(dimension_semantics=("arbitrary",) * n_axes,
                                vmem_limit_bytes=VMEM_LIMIT)


def _rms(x):
    return x * lax.rsqrt(jnp.mean(x * x, axis=-1, keepdims=True) + NORM_EPS)


def _silu(g):
    return g / (1.0 + jnp.exp(-g))


def _dot(a, b):
    return jnp.dot(a, b, preferred_element_type=F32)


def _dot_nt(a, b):
    return lax.dot_general(a, b, (((1,), (1,)), ((), ())), preferred_element_type=F32)


def _rope128(x, a, b, c):
    return x * a + pltpu.roll(x, 32, 1) * b + pltpu.roll(x, LANE - 32, 1) * c


def _modnorm(x, mod_ref, gpre_ref):
    m = mod_ref[0, 0]
    return _rms(x) * gpre_ref[...] * (1.0 + m[1:2]) + m[0:1]


def _mod_kernel(c_ref, w_ref, b_ref, o_ref):
    a = _silu(c_ref[...])
    o_ref[0] = _dot(a.astype(BF16), w_ref[0].astype(BF16)) + b_ref[0]


def _modulation(cond, w_mod, b_mod):
    rows = cond.shape[0]
    return pl.pallas_call(
        _mod_kernel,
        grid=(DEPTH, 3),
        in_specs=[pl.BlockSpec((rows, D), lambda l, j: (0, 0)),
                  pl.BlockSpec((1, D, D), lambda l, j: (l, 0, j)),
                  pl.BlockSpec((1, 1, D), lambda l, j: (l, 0, j))],
        out_specs=pl.BlockSpec((1, rows, D), lambda l, j: (l, 0, j)),
        out_shape=jax.ShapeDtypeStruct((DEPTH, rows, 3 * D), F32),
        compiler_params=_cparams(2),
        name="adaln_modulation",
    )(cond, w_mod, b_mod.reshape(DEPTH, 1, 3 * D))


def _tok_spec(width, rows=TM, skip=0):
    return pl.BlockSpec((1, rows, width), lambda b, t: (b, t + skip, 0))


def _mod_spec(ctx_tiles=1, skip=0):
    return pl.BlockSpec((1, 1, 3, D), lambda b, t: (jnp.minimum((t + skip) // ctx_tiles, 1), b, 0, 0))


def _const_spec(shape):
    nd = len(shape)
    return pl.BlockSpec(shape, lambda b, t: (0,) * nd)


def _residual_specs(dual, rows=TM, skip=0):
    if not dual:
        return [_tok_spec(D, rows, skip)]
    per = CTX // rows
    return [pl.BlockSpec((1, rows, D), lambda b, t: (b, jnp.minimum(t, per - 1), 0)),
            pl.BlockSpec((1, rows, D), lambda b, t: (b, jnp.maximum(t - per, 0), 0))]


def _wide_specs(dual):
    n = TP // TM
    if not dual:
        return [pl.BlockSpec((1, TM, D), functools.partial(lambda j, b, t: (b, n * t + j, 0), j)) for j in range(n)]
    last = lambda j, b, t: (b, jnp.maximum(n * t + j - CTX // TM, 0), 0)
    return ([pl.BlockSpec((1, CTX, D), lambda b, t: (b, 0, 0))]
            + [pl.BlockSpec((1, TM, D), functools.partial(last, j)) for j in range(n)])


def _wide_modnorm(refs, dual, mod_ref, gpre_ref):
    t = pl.program_id(1)
    blocks = [r[0] for r in refs[1:]] if dual else [r[0] for r in refs]
    if dual:
        blocks[0] = jnp.where(t == 0, refs[0][0], blocks[0])
    x = jnp.concatenate(blocks, axis=0)
    is_ctx = jnp.logical_and(lax.broadcasted_iota(jnp.int32, (TP, 1), 0) < CTX, t == 0)
    mc, mx = mod_ref[0, 0], mod_ref[1, 0]
    scale = jnp.where(is_ctx, mc[1:2], mx[1:2])
    shift = jnp.where(is_ctx, mc[0:1], mx[0:1])
    return _rms(x) * gpre_ref[...] * (1.0 + scale) + shift


def _both_mod_spec():
    return pl.BlockSpec((2, 1, 3, D), lambda b, t: (0, b, 0, 0))


def _resident(shape):
    nd = len(shape)
    return pl.BlockSpec(shape, lambda b, t: (0,) * nd, pipeline_mode=pl.Buffered(1))


def _residual_tile(refs, dual, rows=TM):
    if not dual:
        return refs[0][0]
    return jnp.where(pl.program_id(1) < CTX // rows, refs[0][0], refs[1][0])


def _finish(o, gate_ref, wout_ref, x, mod_ref, gpost_ref):
    a = o.astype(F32) * _silu(gate_ref[0].astype(F32))
    y = _dot(a.astype(BF16), wout_ref[...])
    return x + mod_ref[0, 0][2:3] * (_rms(y) * gpost_ref[...])


def _mla_proj_kernel(*refs, dual):
    nx = TP // TM + (1 if dual else 0)
    (mod_ref, gpre_ref, win_ref, gq_ref, wq_ref, gkv_ref, wkv_ref,
     ra_ref, rb_ref, rc_ref, q_ref, k_ref, v_ref, gate_ref) = refs[nx:]
    h = _wide_modnorm(refs[:nx], dual, mod_ref, gpre_ref)
    p = _dot(h.astype(BF16), win_ref[...])
    c0, c1, c2 = A_Q_RANK, A_Q_RANK + A_KV_RANK, A_Q_RANK + A_KV_RANK + LANE
    gate_ref[0] = p[:, c2:].astype(BF16)
    qn = (_rms(p[:, :c0]) * gq_ref[...]).astype(BF16)
    kvn = (_rms(p[:, c0:c1]) * gkv_ref[...]).astype(BF16)
    ra, rb, rc = ra_ref[...], rb_ref[...], rc_ref[...]
    kr = _rope128(p[:, c1:c2], ra, rb, rc).astype(BF16)
    for hd in range(A_HEADS):
        q = _dot(qn, wq_ref[:, hd * A_HEAD_PAD:(hd + 1) * A_HEAD_PAD])
        q_ref[0, hd, :, 0:LANE] = q[:, 0:LANE].astype(BF16)
        q_ref[0, hd, :, LANE:2 * LANE] = _rope128(q[:, LANE:2 * LANE], ra, rb, rc).astype(BF16)
        kv = _dot(kvn, wkv_ref[:, hd * A_HEAD_PAD:(hd + 1) * A_HEAD_PAD])
        k_ref[0, hd, :, 0:LANE] = kv[:, 0:A_NOPE].astype(BF16)
        k_ref[0, hd, :, LANE:2 * LANE] = kr
        v_ref[0, hd] = kv[:, A_NOPE:].astype(BF16)


def _mla_proj(xsrc, modl, gpre, win, gq, wq, gkv, wkv, tabs):
    dual = len(xsrc) == 2
    bsz = xsrc[0].shape[0]
    t = tabs[0].shape[0]
    xargs = [xsrc[0]] + [xsrc[1]] * (TP // TM) if dual else [xsrc[0]] * (TP // TM)
    head = lambda w: pl.BlockSpec((1, A_HEADS, TP, w), lambda b, i: (b, 0, i, 0))
    tab = pl.BlockSpec((TP, LANE), lambda b, i: (i, 0))
    return pl.pallas_call(
        functools.partial(_mla_proj_kernel, dual=dual),
        grid=(bsz, t // TP),
        in_specs=[*_wide_specs(dual), _both_mod_spec(), _const_spec((1, D)), _resident(win.shape),
                  _const_spec((1, A_Q_RANK)), _resident(wq.shape),
                  _const_spec((1, A_KV_RANK)), _resident(wkv.shape), tab, tab, tab],
        out_specs=[head(A_HEAD_PAD), head(A_HEAD_PAD), head(A_VDIM), _tok_spec(A_HEADS * A_VDIM, TP)],
        out_shape=[jax.ShapeDtypeStruct((bsz, A_HEADS, t, A_HEAD_PAD), BF16),
                   jax.ShapeDtypeStruct((bsz, A_HEADS, t, A_HEAD_PAD), BF16),
                   jax.ShapeDtypeStruct((bsz, A_HEADS, t, A_VDIM), BF16),
                   jax.ShapeDtypeStruct((bsz, t, A_HEADS * A_VDIM), BF16)],
        compiler_params=_cparams(2),
        name="mla_proj",
    )(*xargs, modl, gpre, win, gq, wq, gkv, wkv, *tabs)


def _mla_attn_kernel(*refs, dual, latents_only):
    nx = 2 if dual else 1
    q_ref, k_ref, v_ref, gate_ref, wout_ref = refs[:5]
    mod_ref, gpost_ref, xo_ref, o_ref = refs[5 + nx:]

    def attend(nk):
        s_next = _dot_nt(q_ref[0, 0], k_ref[0, 0, :nk, :])
        for hd in range(A_HEADS):
            s = s_next
            if hd + 1 < A_HEADS:
                s_next = _dot_nt(q_ref[0, hd + 1], k_ref[0, hd + 1, :nk, :])
            m = jnp.max(s, axis=-1, keepdims=True)
            p = jnp.exp2(s - m)
            l = jnp.sum(p, axis=-1, keepdims=True)
            o = _dot(p.astype(BF16), v_ref[0, hd, :nk, :]) / l
            o_ref[:, hd * A_VDIM:(hd + 1) * A_VDIM] = o.astype(BF16)

    t_all = k_ref.shape[2]
    if latents_only:
        attend(t_all)
    else:
        is_ctx = pl.program_id(1) == 0
        pl.when(is_ctx)(lambda: attend(CTX))
        pl.when(jnp.logical_not(is_ctx))(lambda: attend(t_all))
    x = _residual_tile(refs[5:5 + nx], dual)
    xo_ref[0] = _finish(o_ref[...], gate_ref, wout_ref, x, mod_ref, gpost_ref)


def _mla_attn(q, k, v, gate, wout, xsrc, modl, gpost, latents_only):
    dual = len(xsrc) == 2
    bsz, nh, t, _ = q.shape
    skip = CTX // TM if latents_only else 0
    nt = t // TM - skip
    return pl.pallas_call(
        functools.partial(_mla_attn_kernel, dual=dual, latents_only=latents_only),
        grid=(bsz, nt),
        in_specs=[pl.BlockSpec((1, nh, TM, A_HEAD_PAD), lambda b, i: (b, 0, i + skip, 0)),
                  pl.BlockSpec((1, nh, t, A_HEAD_PAD), lambda b, i: (b, 0, 0, 0)),
                  pl.BlockSpec((1, nh, t, A_VDIM), lambda b, i: (b, 0, 0, 0)),
                  _tok_spec(nh * A_VDIM, skip=skip), _const_spec(wout.shape),
                  *_residual_specs(dual, skip=skip), _mod_spec(skip=skip), _const_spec((1, D))],
        out_specs=_tok_spec(D),
        out_shape=jax.ShapeDtypeStruct((bsz, nt * TM, D), F32),
        scratch_shapes=[pltpu.VMEM((TM, nh * A_VDIM), BF16)],
        input_output_aliases={} if (dual or latents_only) else {5: 0},
        compiler_params=_cparams(2),
        name="mla_attention_out",
    )(q, k, v, gate, wout, *xsrc, modl, gpost)


def _swa_proj_kernel(*refs):
    xrefs = refs[:TP // TM]
    mod_ref, gpre_ref, win_ref, ra_ref, rb_ref, rc_ref, q_ref, k_ref, v_ref, gate_ref = refs[TP // TM:]
    h = _wide_modnorm(xrefs, False, mod_ref, gpre_ref)
    p = _dot(h.astype(BF16), win_ref[...])
    ra, rb, rc = ra_ref[...], rb_ref[...], rc_ref[...]
    qw, kw = B_HEADS * B_HDIM, B_KV_HEADS * B_HDIM
    lane = lax.broadcasted_iota(jnp.int32, (1, LANE), 1)
    lo = lane < B_HDIM
    ones_col = jnp.broadcast_to(jnp.where(lane == 0, 1.0, 0.0), (TP, LANE)).astype(BF16)
    for j in range(B_KV_HEADS // 2):
        for g in range(B_GROUP):
            s = j * B_GROUP + g
            r = _rope128(p[:, s * LANE:(s + 1) * LANE], ra, rb, rc)
            for half in range(2):
                d = ((2 * j + half) * B_GROUP + g) * LANE
                keep = lo if half == 0 else jnp.logical_not(lo)
                q_ref[0, :, d:d + LANE] = jnp.where(keep, r, 0.0).astype(BF16)
        o = qw + j * LANE
        k_ref[0, :, j * LANE:(j + 1) * LANE] = _rope128(p[:, o:o + LANE], ra, rb, rc).astype(BF16)
        v_ref[0, :, 2 * j * LANE:(2 * j + 1) * LANE] = p[:, o + kw:o + kw + LANE].astype(BF16)
        v_ref[0, :, (2 * j + 1) * LANE:(2 * j + 2) * LANE] = ones_col
    gate_ref[0] = p[:, qw + 2 * kw:].astype(BF16)


def _swa_proj(xs, modl, gpre, win, tabs):
    bsz, t, _ = xs.shape
    qw, kw = B_HEADS * B_HDIM, B_KV_HEADS * B_HDIM
    tab = pl.BlockSpec((TP, LANE), lambda b, i: (i, 0))
    return pl.pallas_call(
        _swa_proj_kernel,
        grid=(bsz, t // TP),
        in_specs=[*_wide_specs(False), _both_mod_spec(), _const_spec((1, D)), _resident(win.shape), tab, tab, tab],
        out_specs=[_tok_spec(2 * qw, TP), _tok_spec(kw, TP), _tok_spec(2 * kw, TP), _tok_spec(qw, TP)],
        out_shape=[jax.ShapeDtypeStruct((bsz, t, 2 * qw), BF16),
                   jax.ShapeDtypeStruct((bsz, t, kw), BF16),
                   jax.ShapeDtypeStruct((bsz, t, 2 * kw), BF16),
                   jax.ShapeDtypeStruct((bsz, t, qw), BF16)],
        compiler_params=_cparams(2),
        name="swa_proj",
    )(*[xs] * (TP // TM), modl, gpre, win, *tabs)


def _swa_attn_kernel(q_ref, k_ref, v_ref, sink_ref, gate_ref, wout_ref, x_ref, mod_ref, gpost_ref, xo_ref, o_ref):
    i = pl.program_id(1)
    n_lat = k_ref.shape[1] - CTX
    band = 3 * B_BLOCK
    lo = lax.broadcasted_iota(jnp.int32, (1, LANE), 1) < B_HDIM

    def run(with_band):
        chains = [(sub, j, half) for sub in range(TM // B_BLOCK) for j in range(B_KV_HEADS // 2) for half in range(2)]
        window, keys = {}, {}

        def block_window(sub):
            if sub not in window:
                li = (i - CTX // TM) * (TM // B_BLOCK) + sub
                start = jnp.clip((li - 1) * B_BLOCK, 0, n_lat - band)
                off = li * B_BLOCK - start
                d0 = (lax.broadcasted_iota(jnp.int32, (B_BLOCK, band), 0)
                      - lax.broadcasted_iota(jnp.int32, (B_BLOCK, band), 1))
                mask = jnp.abs(d0 + off) <= B_WINDOW
                window[sub] = (mask, pl.multiple_of(CTX + start, B_BLOCK))
            return window[sub]

        def block_keys(sub, j):
            if (sub, j) not in keys:
                kcols = slice(j * LANE, (j + 1) * LANE)
                vcols = slice(2 * j * LANE, (2 * j + 2) * LANE)
                kk = k_ref[0, 0:CTX, kcols]
                vv = v_ref[0, 0:CTX, vcols]
                if with_band:
                    kstart = block_window(sub)[1]
                    kk = jnp.concatenate([kk, k_ref[0, pl.ds(kstart, band), kcols]], axis=0)
                    vv = jnp.concatenate([vv, v_ref[0, pl.ds(kstart, band), vcols]], axis=0)
                keys[(sub, j)] = (kk, vv)
            return keys[(sub, j)]

        def scores(chain):
            sub, j, half = chain
            hk = 2 * j + half
            r0 = sub * B_BLOCK
            qs = jnp.concatenate(
                [q_ref[0, r0:r0 + B_BLOCK, (hk * B_GROUP + g) * LANE:(hk * B_GROUP + g + 1) * LANE]
                 for g in range(B_GROUP)], axis=0)
            return _dot_nt(qs, block_keys(sub, j)[0])

        outs = []
        s_next = scores(chains[0])
        for n, (sub, j, half) in enumerate(chains):
            s = s_next
            if n + 1 < len(chains):
                s_next = scores(chains[n + 1])
            sk = sink_ref[2 * j + half]
            if with_band:
                mask = block_window(sub)[0]
                s_b = jnp.concatenate(
                    [jnp.where(mask, s[g * B_BLOCK:(g + 1) * B_BLOCK, CTX:], NEG_INF) for g in range(B_GROUP)],
                    axis=0)
                s = jnp.concatenate([s[:, :CTX], s_b], axis=1)
            m = jnp.maximum(jnp.max(s, axis=-1, keepdims=True), sk)
            p = jnp.exp2(s - m)
            pv = _dot(p.astype(BF16), block_keys(sub, j)[1])
            l = pv[:, LANE:LANE + 1] + jnp.exp2(sk - m)
            outs.append(pv[:, :LANE] / l)
            if half == 1:
                comb = jnp.where(lo, outs[-2], outs[-1])
                r0 = sub * B_BLOCK
                for g in range(B_GROUP):
                    s_out = j * B_GROUP + g
                    o_ref[r0:r0 + B_BLOCK, s_out * LANE:(s_out + 1) * LANE] = (
                        comb[g * B_BLOCK:(g + 1) * B_BLOCK].astype(BF16))

    is_ctx = i < CTX // TM
    pl.when(is_ctx)(lambda: run(False))
    pl.when(jnp.logical_not(is_ctx))(lambda: run(True))
    xo_ref[0] = _finish(o_ref[...], gate_ref, wout_ref, x_ref[0], mod_ref, gpost_ref)


def _swa_attn(q, k, v, sink_cols, gate, wout, xs, modl, gpost):
    bsz, t, qw2 = q.shape
    kw = k.shape[-1]
    qw = qw2 // 2
    return pl.pallas_call(
        _swa_attn_kernel,
        grid=(bsz, t // TM),
        in_specs=[pl.BlockSpec((1, TM, qw2), lambda b, i: (b, i, 0)),
                  pl.BlockSpec((1, t, kw), lambda b, i: (b, 0, 0)),
                  pl.BlockSpec((1, t, 2 * kw), lambda b, i: (b, 0, 0)),
                  pl.BlockSpec(sink_cols.shape, lambda b, i: (0, 0, 0)),
                  _tok_spec(qw), _const_spec(wout.shape), _tok_spec(D), _mod_spec(), _const_spec((1, D))],
        out_specs=_tok_spec(D),
        out_shape=jax.ShapeDtypeStruct(xs.shape, F32),
        scratch_shapes=[pltpu.VMEM((TM, qw), BF16)],
        input_output_aliases={6: 0},
        compiler_params=_cparams(2),
        name="swa_attention_out",
    )(q, k, v, sink_cols, gate, wout, xs, modl, gpost)


RADIX = 4
QP = TM // RADIX
TILE_ORDER = (3, 1, 2, 0)


def _hyena_proj_kernel(xp_ref, x_ref, xn_ref, mod_ref, gpre_ref, win_ref, cw_ref, cb_ref, perm_ref, *out_refs):
    u_refs, g_refs = out_refs[:RADIX], out_refs[RADIX:]
    t = pl.program_id(1)
    nt = pl.num_programs(1)
    hh = _dot(perm_ref[...], _modnorm(x_ref[0], mod_ref, gpre_ref).astype(BF16))
    h = jnp.concatenate([_modnorm(xp_ref[0], mod_ref, gpre_ref), hh, _modnorm(xn_ref[0], mod_ref, gpre_ref)], axis=0)
    p = _dot(h.astype(BF16), win_ref[...])
    cwid = 3 * C_WIDTH
    u = p[:, :cwid]
    start = {rho: HALO + k * QP for k, rho in enumerate(TILE_ORDER)}
    cls = {rho: u[start[rho]:start[rho] + QP] for rho in range(RADIX)}
    for rho in range(RADIX):
        g_refs[rho][0] = p[start[rho]:start[rho] + QP, cwid:].astype(BF16)
    before0 = pltpu.roll(u[0:HALO + QP], 1, 0)[HALO:]
    after3 = pltpu.roll(u[start[0]:], QP + HALO - 1, 0)[0:QP]
    r = lax.broadcasted_iota(jnp.int32, (QP, 1), 0)
    before0 = jnp.where(jnp.logical_and(r == 0, t <= 1), 0.0, before0)
    after3 = jnp.where(jnp.logical_and(r == QP - 1, jnp.logical_or(t == 0, t == nt - 1)), 0.0, after3)
    cw = cw_ref[...]
    cb = cb_ref[...]
    left = {0: before0, 1: cls[0], 2: cls[1], 3: cls[2]}
    right = {0: cls[1], 1: cls[2], 2: cls[3], 3: after3}
    for rho in range(RADIX):
        u_refs[rho][0] = (cb + left[rho] * cw[0:1] + cls[rho] * cw[1:2] + right[rho] * cw[2:3]).astype(BF16)


def _class_spec(width):
    return pl.BlockSpec((1, QP, width), lambda b, t: (b, t, 0))


def _hyena_proj(xs, modl, gpre, win, conv_w, conv_b):
    bsz, t, _ = xs.shape
    nt = t // TM
    per = TM // HALO
    last = t // HALO - 1
    part = lambda w: jax.ShapeDtypeStruct((bsz, t // RADIX, w), BF16)
    return pl.pallas_call(
        _hyena_proj_kernel,
        grid=(bsz, nt),
        in_specs=[pl.BlockSpec((1, HALO, D), lambda b, i: (b, jnp.maximum(i * per - 1, 0), 0)),
                  _tok_spec(D),
                  pl.BlockSpec((1, HALO, D), lambda b, i: (b, jnp.minimum((i + 1) * per, last), 0)),
                  _mod_spec(), _const_spec((1, D)), _const_spec(win.shape),
                  _const_spec(conv_w.shape), _const_spec(conv_b.shape), _const_spec((TM, TM))],
        out_specs=[_class_spec(3 * C_WIDTH)] * RADIX + [_class_spec(C_WIDTH)] * RADIX,
        out_shape=[part(3 * C_WIDTH)] * RADIX + [part(C_WIDTH)] * RADIX,
        compiler_params=_cparams(2),
        name="hyena_proj",
    )(xs, xs, xs, modl, gpre, win, conv_w, conv_b, _class_perm(TILE_ORDER))


def _class_perm(order):
    r = np.arange(QP)
    src = np.concatenate([RADIX * r + rho for rho in order])
    m = np.zeros((TM, TM), np.float32)
    m[np.arange(TM), src] = 1.0
    return jnp.asarray(m, BF16)


def _hyena_out_kernel(*refs):
    o_refs, g_refs = refs[:RADIX], refs[RADIX:2 * RADIX]
    wout_ref, x_ref, mod_ref, gpost_ref, perm_ref, xo_ref = refs[2 * RADIX:]
    a = jnp.concatenate([o_refs[rho][0].astype(F32) * _silu(g_refs[rho][0].astype(F32)) for rho in range(RADIX)],
                        axis=0)
    a = _dot(perm_ref[...], a.astype(BF16)).astype(BF16)
    y = _dot(a, wout_ref[...])
    xo_ref[0] = x_ref[0] + mod_ref[0, 0][2:3] * (_rms(y) * gpost_ref[...])


def _hyena_out_proj(os, gs, wout, xs, modl, gpost):
    bsz, t, _ = xs.shape
    w = os[0].shape[-1]
    return pl.pallas_call(
        _hyena_out_kernel,
        grid=(bsz, t // TM),
        in_specs=[_class_spec(w)] * (2 * RADIX) + [_const_spec(wout.shape), _tok_spec(D), _mod_spec(),
                                                   _const_spec((1, D)), _const_spec((TM, TM))],
        out_specs=_tok_spec(D),
        out_shape=jax.ShapeDtypeStruct(xs.shape, F32),
        input_output_aliases={2 * RADIX + 1: 0},
        compiler_params=_cparams(2),
        name="hyena_out_proj_residual",
    )(*os, *gs, wout, xs, modl, gpost, _class_perm(range(RADIX)).T)


def _filter_kernel(z_ref, w1_ref, b1_ref, fr_ref, w2_ref, b2_ref, w3_ref, dl_ref, o_ref):
    hp = lax.Precision.HIGHEST
    z = z_ref[...]
    fr = fr_ref[...]
    h = jnp.sin(fr * (jnp.dot(z, w1_ref[...], precision=hp, preferred_element_type=F32) + b1_ref[...]))
    h = jnp.sin(fr * (jnp.dot(h, w2_ref[...], precision=hp, preferred_element_type=F32) + b2_ref[...]))
    h = jnp.dot(h, w3_ref[...], precision=hp, preferred_element_type=F32)
    o_ref[...] = (h * jnp.exp(-z[:, 0:1] * dl_ref[...])).astype(o_ref.dtype)


SPEC_BLOCK = 512


def _filter_columns():
    o, cb, d, cc = np.meshgrid(np.arange(2), np.arange(C_WIDTH // SPEC_BLOCK), np.arange(2), np.arange(SPEC_BLOCK),
                               indexing="ij")
    return (d * 2 * C_WIDTH + o * C_WIDTH + cb * SPEC_BLOCK + cc).reshape(-1)


def _filters(n, w1, b1, fr, w2, b2, w3):
    t = np.linspace(0.0, 1.0, n, dtype=np.float32)[:, None]
    w = ((2.0 * math.pi / n) * np.arange(n, dtype=np.float32))[:, None].astype(np.float32)
    bands = np.linspace(1e-4, C_BANDS - 1, C_BANDS, dtype=np.float32)[None, :]
    z = np.zeros((n, LANE), np.float32)
    z[:, 0:1] = t
    z[:, 1:1 + C_BANDS] = np.cos(bands * w)
    z[:, 1 + C_BANDS:C_EMB] = -np.sin(bands * w)
    deltas = np.abs(np.linspace(C_MIN_DECAY, C_MAX_DECAY, C_WIDTH, dtype=np.float32))
    cols = _filter_columns()
    dl = deltas[cols % C_WIDTH][None, :]
    w3 = w3[:, cols]

    def pad(a, r, c):
        return jnp.zeros((r, c), F32).at[:a.shape[0], :a.shape[1]].set(a)

    tn = min(n, TM)
    nout = 4 * C_WIDTH
    cs = lambda shape: pl.BlockSpec(shape, lambda i: (0, 0))
    return pl.pallas_call(
        _filter_kernel,
        grid=(n // tn,),
        in_specs=[pl.BlockSpec((tn, LANE), lambda i: (i, 0)), cs((LANE, LANE)), cs((1, LANE)), cs((1, LANE)),
                  cs((LANE, LANE)), cs((1, LANE)), cs((LANE, nout)), cs((1, nout))],
        out_specs=pl.BlockSpec((tn, nout), lambda i: (i, 0)),
        out_shape=jax.ShapeDtypeStruct((n, nout), BF16),
        compiler_params=_cparams(1),
        name="hyena_filter_mlp",
    )(jnp.asarray(z), pad(w1, LANE, LANE), pad(b1[None], 1, LANE), pad(fr[None], 1, LANE),
      pad(w2, LANE, LANE), pad(b2[None], 1, LANE), pad(w3, LANE, nout), jnp.asarray(dl))


def _dft_matrix(n):
    f = np.arange(n, dtype=np.int64)[:, None]
    s = np.arange(n, dtype=np.int64)[None, :]
    ang = (2.0 * np.pi / (2 * n)) * ((f * s) % (2 * n)).astype(np.float64)
    cos = np.cos(ang)
    sin = np.sin(ang)
    sin[0, :] = np.where(np.arange(n) % 2 == 0, 1.0, -1.0)
    return cos.astype(np.float32), sin.astype(np.float32)


def _spectrum_kernel(a_ref, b_ref, re_ref, im_ref, *, scale):
    acc = _dot(a_ref[...], b_ref[...])
    fwd, bwd = acc[:, :SPEC_BLOCK], acc[:, SPEC_BLOCK:]
    re_ref[...] = (fwd + bwd) * scale
    im_ref[...] = (bwd - fwd) * scale


def _spectrum(a, b, bm, scale):
    m, k = a.shape
    nb = b.shape[1] // (2 * SPEC_BLOCK)
    assert m % bm == 0
    out = jax.ShapeDtypeStruct((m, nb * SPEC_BLOCK), F32)
    o_spec = pl.BlockSpec((bm, SPEC_BLOCK), lambda i, j: (i, j))
    return pl.pallas_call(
        functools.partial(_spectrum_kernel, scale=scale),
        grid=(m // bm, nb),
        in_specs=[pl.BlockSpec((bm, k), lambda i, j: (i, 0)), pl.BlockSpec((k, 2 * SPEC_BLOCK), lambda i, j: (0, j))],
        out_specs=[o_spec, o_spec],
        out_shape=[out, out],
        compiler_params=_cparams(2),
        name="filter_spectrum",
    )(a, b)


_PHASE = [(math.cos(math.pi * rho / 4), math.sin(math.pi * rho / 4),
           math.cos(3 * math.pi * rho / 4), math.sin(3 * math.pi * rho / 4)) for rho in range(RADIX)]


def _cmul(xr, xs, kr, ki):
    return xr * kr + xs * ki, xs * kr - xr * ki


def _hyena_conv_kernel(*refs, row0, q, aliased):
    nu = 3 * RADIX
    u_refs = refs[:nu]
    m_refs = refs[nu:nu + RADIX]
    mt_refs = refs[nu + RADIX:nu + 2 * RADIX]
    k_refs = refs[nu + 2 * RADIX:nu + 2 * RADIX + 8]
    ksr_ref, ksi_ref, fb_ref = refs[nu + 2 * RADIX + 8:nu + 2 * RADIX + 11]
    rest = refs[nu + 2 * RADIX + 11 + (RADIX if aliased else 0):]
    o_refs, z_refs = rest[:RADIX], rest[RADIX:]
    o = pl.program_id(2)
    rows = slice(row0, row0 + q)

    @pl.when(o == 0)
    def _():
        for rho in range(RADIX):
            z_refs[rho][...] = u_refs[3 * rho][0, rows, :]

    ksr, ksi = ksr_ref[...], ksi_ref[...]
    ksp = jnp.concatenate([ksr[0:3], ksi[5:6], ksr[3:4], ksi[6:7], ksr[4:5], ksi[7:8]], axis=0)
    first = lax.broadcasted_iota(jnp.int32, (8, 1), 0) == 0
    nblk = m_refs[0].shape[0]
    fbs = m_refs[0].shape[1] // 2

    def forward(j):
        return [_dot(m_refs[rho][j], z_refs[rho][...]) for rho in range(RADIX)]

    def spectral(j, fwd):
        c = [g[:fbs] for g in fwd]
        sn = [g[fbs:] for g in fwd]
        ar, as_, br, bs = c[0] + c[2], sn[0] + sn[2], c[1] + c[3], sn[1] + sn[3]
        cr, cs, dr, ds = c[0] - c[2], sn[0] - sn[2], c[1] - c[3], sn[1] - sn[3]
        k = [r[j * fbs:(j + 1) * fbs, :] for r in k_refs]
        y1r, y1s = _cmul(ar + br, as_ + bs, k[0], k[1])
        y2r, y2s = _cmul(cr - ds, cs + dr, k[2], k[3])
        y3r, y3s = _cmul(ar - br, bs - as_, k[4], k[5])
        y4r, y4s = _cmul(cr + ds, dr - cs, k[6], k[7])
        pr, ps, qr, qs = y1r + y3r, y1s - y3s, y1r - y3r, y1s + y3s
        rr, rs, tr, ts = y2r + y4r, y2s - y4s, y2r - y4r, y2s + y4s
        wr = [pr + rr, qr + ts, pr - rr, qr - ts]
        ws = [ps + rs, qs - tr, ps - rs, qs + tr]
        if j == 0:
            c0 = [x[0:8] for x in c]
            a0 = [x[0:8] for x in sn]
            y0 = (c0[0] + c0[1] + c0[2] + c0[3]) * ksp[0:1]
            yn = (c0[0] - c0[1] + c0[2] - c0[3]) * ksp[1:2]
            yhr, yhs = _cmul(c0[0] - c0[2], c0[1] - c0[3], ksp[2:3], ksp[3:4])
            xqr = sum(_PHASE[rho][0] * a0[rho] for rho in range(1, RADIX)) + a0[0]
            xqs = sum(_PHASE[rho][1] * a0[rho] for rho in range(1, RADIX))
            xgr = sum(_PHASE[rho][2] * a0[rho] for rho in range(1, RADIX)) + a0[0]
            xgs = sum(_PHASE[rho][3] * a0[rho] for rho in range(1, RADIX))
            yqr, yqs = _cmul(xqr, xqs, ksp[4:5], ksp[5:6])
            ygr, ygs = _cmul(xgr, xgs, ksp[6:7], ksp[7:8])
            turn = [yhr, yhs, -yhr, -yhs]
            for rho in range(RADIX):
                pc, psn, gc, gsn = _PHASE[rho]
                cos0 = y0 + (yn if rho % 2 == 0 else -yn) + turn[rho]
                sin0 = yqr * pc + yqs * psn + ygr * gc + ygs * gsn
                wr[rho] = jnp.concatenate([jnp.where(first, cos0, wr[rho][0:8]), wr[rho][8:]], axis=0)
                ws[rho] = jnp.concatenate([jnp.where(first, sin0, ws[rho][0:8]), ws[rho][8:]], axis=0)
        return [jnp.concatenate([wr[rho], ws[rho]], axis=0).astype(BF16) for rho in range(RADIX)]

    acc = [None] * RADIX
    fwd_next = forward(0)
    for j in range(nblk):
        fwd = fwd_next
        if j + 1 < nblk:
            fwd_next = forward(j + 1)
        w = spectral(j, fwd)
        for rho in range(RADIX):
            part = _dot(mt_refs[rho][j], w[rho])
            acc[rho] = part if acc[rho] is None else acc[rho] + part
    f_out = [acc[rho] + z_refs[rho][...].astype(F32) * fb_ref[0] for rho in range(RADIX)]

    @pl.when(o == 0)
    def _():
        for rho in range(RADIX):
            z_refs[rho][...] = (u_refs[3 * rho + 1][0, rows, :].astype(F32) * f_out[rho]).astype(BF16)

    @pl.when(o == 1)
    def _():
        for rho in range(RADIX):
            if row0 > 0:
                o_refs[rho][0, 0:row0, :] = jnp.zeros((row0, o_refs[rho].shape[2]), BF16)
            o_refs[rho][0, rows, :] = (u_refs[3 * rho + 2][0, rows, :].astype(F32) * f_out[rho]).astype(BF16)


def _hyena_conv(us, mats, spec_re, spec_im, fbias, *, row0, q, block_rows, tc, prev=None):
    bsz, t4, _ = us[0].shape
    nct = C_WIDTH // tc
    u_spec = lambda which: pl.BlockSpec((1, block_rows, tc), lambda b, c, o: (b, 0, which * nct + c))
    m_spec = pl.BlockSpec(mats[0].shape, lambda b, c, o: (0, 0, 0))
    mt_spec = pl.BlockSpec(mats[RADIX].shape, lambda b, c, o: (0, 0, 0))
    p_spec = lambda blk: pl.BlockSpec((q, tc), functools.partial(lambda g, b, c, o: (g, o * nct + c), blk))
    s_spec = pl.BlockSpec((8, tc), lambda b, c, o: (q, o * nct + c))
    in_specs = ([u_spec(w) for _ in range(RADIX) for w in range(3)] + [m_spec] * RADIX + [mt_spec] * RADIX
                + [p_spec(blk) for g in range(4) for blk in (g, 4 + g)] + [s_spec, s_spec]
                + [pl.BlockSpec((1, 1, tc), lambda b, c, o: (o, 0, c))])
    args = ([u for u in us for _ in range(3)] + list(mats) + [spec_re, spec_im] * 4 + [spec_re, spec_im, fbias])
    aliases = {}
    if prev is not None:
        in_specs += [pl.BlockSpec(memory_space=pl.ANY)] * RADIX
        aliases = {len(args) + rho: rho for rho in range(RADIX)}
        args += list(prev)
    o_spec = pl.BlockSpec((1, block_rows, tc), lambda b, c, o: (b, 0, c))
    return pl.pallas_call(
        functools.partial(_hyena_conv_kernel, row0=row0, q=q, aliased=prev is not None),
        grid=(bsz, nct, 2),
        in_specs=in_specs,
        out_specs=[o_spec] * RADIX,
        out_shape=[jax.ShapeDtypeStruct((bsz, t4, C_WIDTH), BF16)] * RADIX,
        scratch_shapes=[pltpu.VMEM((q, tc), BF16)] * RADIX,
        input_output_aliases=aliases,
        compiler_params=_cparams(3),
        name="hyena_long_conv_q%d" % q,
    )(*args)


def _radix4_matrices(n):
    q = n // RADIX
    fbs = min(q, LANE)
    nblk = q // fbs
    f = np.arange(q, dtype=np.int64)[:, None]
    r = np.arange(q, dtype=np.int64)[None, :]
    alt = np.where(np.arange(q) % 2 == 0, 1.0, -1.0)
    mats = []
    for rho in range(RADIX):
        ang = (2.0 * np.pi / (2 * n)) * ((f * (RADIX * r + rho)) % (2 * n)).astype(np.float64)
        cos, sin = np.cos(ang), np.sin(ang)
        sin[0, :] = alt
        mats.append(np.concatenate([cos.reshape(nblk, fbs, q), sin.reshape(nblk, fbs, q)], axis=1).astype(np.float32))
    tr = lambda m: np.ascontiguousarray(np.transpose(m, (0, 2, 1)))
    return tuple(jnp.asarray(m, BF16) for m in mats) + tuple(jnp.asarray(tr(m), BF16) for m in mats)


def _hyena_spectra(n, filt):
    h, q = n // 2, n // RADIX
    cos, sin = _dft_matrix(n)
    f = np.arange(q)
    groups = [f, h + f, np.maximum(n - f, 1) % n, h - f]
    special = [0.5 * cos[0:1], 0.5 * sin[0:1], cos[h:h + 1], cos[q:q + 1], cos[h + q:h + q + 1],
               sin[h:h + 1], sin[q:q + 1], sin[h + q:h + q + 1]]
    rows = np.concatenate([cos[g] for g in groups] + [sin[g] for g in groups] + special
                          + [np.zeros((LANE - 8, n), np.float32)], axis=0)
    nr = 8 * q + LANE
    return _spectrum(jnp.asarray(rows, BF16), filt, 3 * LANE if nr % (3 * LANE) == 0 else nr, 2.0 / (2 * n))


def _rope_tables(seq, layout):
    rows = seq // GRID_W
    row = np.repeat(np.arange(rows, dtype=np.float32), GRID_W)
    col = np.tile(np.arange(GRID_W, dtype=np.float32), rows)
    per_axis = 32
    inv = (ROPE_BASE ** (-np.arange(0, per_axis, 2, dtype=np.float32) / per_axis)).astype(np.float32)
    ang = np.concatenate([row[:, None] * inv, col[:, None] * inv], axis=-1)
    cos = np.concatenate([np.ones((CTX, 32), np.float32), np.cos(ang)], axis=0)
    sin = np.concatenate([np.zeros((CTX, 32), np.float32), np.sin(ang)], axis=0)
    one, zero = np.ones_like(cos), np.zeros_like(cos)
    if layout == "mla":
        a = [cos, cos, one, one]
        b = [zero, sin, zero, zero]
        c = [-sin, zero, zero, zero]
    else:
        a = [cos, cos, cos, cos]
        b = [zero, sin, zero, sin]
        c = [-sin, zero, -sin, zero]
    return tuple(jnp.asarray(np.concatenate(p, axis=1), F32) for p in (a, b, c))


def _swa_head_perm():
    cols = []
    for j in range(B_KV_HEADS // 2):
        for g in range(B_GROUP):
            for hk in (2 * j, 2 * j + 1):
                h = hk * B_GROUP + g
                cols.extend(range(h * B_HDIM, (h + 1) * B_HDIM))
    return np.asarray(cols, np.int32)


def _mla_weights(w_in, w_q, w_kv):
    c1 = A_Q_RANK + A_KV_RANK
    zpad = jnp.zeros((D, LANE - A_ROPE), F32)
    win = jnp.concatenate([w_in[:, :c1 + A_ROPE], zpad, w_in[:, c1 + A_ROPE:]], axis=1)
    qscale = (A_NOPE + A_ROPE) ** -0.5 * math.log2(math.e)
    wq = w_q.reshape(A_Q_RANK, A_HEADS, A_NOPE + A_ROPE) * qscale
    wq = jnp.concatenate([wq, jnp.zeros((A_Q_RANK, A_HEADS, A_HEAD_PAD - A_NOPE - A_ROPE), F32)], axis=-1)
    wq = wq.reshape(A_Q_RANK, A_HEADS * A_HEAD_PAD)
    return win.astype(BF16), wq.astype(BF16), w_kv.astype(BF16)


def kernel(x, c, ctx, c_ctx, w_mod, b_mod, g_pre, g_post, a_w_in, a_g_q, a_w_q, a_g_kv, a_w_kv, a_w_out, b_w_in, b_sink, b_w_out, c_w_in, c_conv_w, c_conv_b, c_f_w1, c_f_b1, c_f_freq, c_f_w2, c_f_b2, c_f_w3, c_filt_bias, c_w_out):
    bsz, seq, _ = x.shape
    assert ctx.shape[1] == CTX and (CTX + seq) % TP == 0 and seq % GRID_W == 0
    xsrc = (ctx, x)

    pad_rows = (-(bsz + 1)) % 8
    cond = jnp.concatenate([c, c_ctx[None], jnp.zeros((pad_rows, D), F32)], axis=0)
    mod = _modulation(cond, w_mod, b_mod)

    tabs_mla = _rope_tables(seq, "mla")
    tabs_swa = _rope_tables(seq, "swa")

    for layer in range(DEPTH):
        kind, j = layer % 3, layer // 3
        mx = mod[layer, :bsz].reshape(bsz, 3, D)
        mc = jnp.broadcast_to(mod[layer, bsz].reshape(1, 3, D), (bsz, 3, D))
        modl = jnp.stack([mc, mx], axis=0)
        gpre = g_pre[layer][None]
        gpost = g_post[layer][None]
        if kind == 0:
            win, wq, wkv = _mla_weights(a_w_in[j], a_w_q[j], a_w_kv[j])
            q, k, v, gate = _mla_proj(xsrc, modl, gpre, win, a_g_q[j][None], wq, a_g_kv[j][None], wkv, tabs_mla)
            xs = _mla_attn(q, k, v, gate, a_w_out[j].astype(BF16), xsrc, modl, gpost,
                           latents_only=layer == DEPTH - 1)
        elif kind == 1:
            xs, = xsrc
            perm = _swa_head_perm()
            qw, kw = B_HEADS * B_HDIM, B_KV_HEADS * B_HDIM
            w = b_w_in[j]
            win = jnp.concatenate([w[:, :qw][:, perm] * (B_HDIM ** -0.5 * math.log2(math.e)), w[:, qw:qw + 2 * kw],
                                   w[:, qw + 2 * kw:][:, perm]], axis=1).astype(BF16)
            q, k, v, gate = _swa_proj(xs, modl, gpre, win, tabs_swa)
            sink = (b_sink[j].astype(F32) * math.log2(math.e)).reshape(B_KV_HEADS, B_GROUP, 1, 1)
            sink_cols = jnp.broadcast_to(sink, (B_KV_HEADS, B_GROUP, B_BLOCK, 1)).reshape(B_KV_HEADS, B_GROUP * B_BLOCK, 1)
            xs = _swa_attn(q, k, v, sink_cols, gate, b_w_out[j][perm, :].astype(BF16), xs, modl, gpost)
        else:
            xs, = xsrc
            parts = _hyena_proj(xs, modl, gpre, c_w_in[j].astype(BF16), c_conv_w[j], c_conv_b[j][None])
            us, gs = parts[:RADIX], parts[RADIX:]
            fargs = (c_f_w1[j], c_f_b1[j], c_f_freq[j], c_f_w2[j], c_f_b2[j], c_f_w3[j])
            fbias = c_filt_bias[j].reshape(2, 1, C_WIDTH)
            os = _hyena_conv(us, _radix4_matrices(seq), *_hyena_spectra(seq, _filters(seq, *fargs)), fbias,
                             row0=CTX // RADIX, q=seq // RADIX, block_rows=(CTX + seq) // RADIX, tc=MXU_W)
            os = _hyena_conv(us, _radix4_matrices(CTX), *_hyena_spectra(CTX, _filters(CTX, *fargs)), fbias,
                             row0=0, q=CTX // RADIX, block_rows=CTX // RADIX, tc=C_WIDTH, prev=os)
            xs = _hyena_out_proj(os, gs, c_w_out[j].astype(BF16), xs, modl, gpost)
        xsrc = (xs,)
    return xs
```

```python
import functools
import math

import numpy as np
import jax
import jax.numpy as jnp
from jax import lax
from jax.experimental import pallas as pl
from jax.experimental.pallas import tpu as pltpu

F32 = jnp.float32
BF16 = jnp.bfloat16

D = 1024
DEPTH = 4
GRID_W = 64
CTX = 256
NORM_EPS = 1e-6
ROPE_BASE = 10000.0
NEG_INF = -1e30

A_HEADS = 8
A_Q_RANK = 512
A_KV_RANK = 256
A_NOPE = 128
A_ROPE = 64
A_VDIM = 128
A_HEAD_PAD = 256

B_HEADS = 16
B_KV_HEADS = 4
B_GROUP = 4
B_HDIM = 64
B_WINDOW = 128
B_BLOCK = 128

C_WIDTH = 1024
C_BANDS = 16
C_EMB = 1 + 2 * C_BANDS
C_FFN = 64
C_MIN_DECAY = math.log(1e-2) / 1.5
C_MAX_DECAY = math.log(1e-2) / 0.3

LANE = 128
MXU_W = 256
TM = 256
TP = 3 * TM
HALO = 8
VMEM_LIMIT = 56 * 1024 * 1024


def _cparams(n_axes):
    return pltpu.CompilerParams(dimension_semantics=("arbitrary",) * n_axes,
                                vmem_limit_bytes=VMEM_LIMIT)


def _rms(x):
    return x * lax.rsqrt(jnp.mean(x * x, axis=-1, keepdims=True) + NORM_EPS)


def _silu(g):
    return g / (1.0 + jnp.exp(-g))


def _dot(a, b):
    return jnp.dot(a, b, preferred_element_type=F32)


def _dot_nt(a, b):
    return lax.dot_general(a, b, (((1,), (1,)), ((), ())), preferred_element_type=F32)


def _rope128(x, a, b, c):
    return x * a + pltpu.roll(x, 32, 1) * b + pltpu.roll(x, LANE - 32, 1) * c


def _modnorm(x, mod_ref, gpre_ref):
    m = mod_ref[0, 0]
    return _rms(x) * gpre_ref[...] * (1.0 + m[1:2]) + m[0:1]


def _mod_kernel(c_ref, w_ref, b_ref, o_ref):
    a = _silu(c_ref[...])
    o_ref[0] = _dot(a.astype(BF16), w_ref[0].astype(BF16)) + b_ref[0]


def _modulation(cond, w_mod, b_mod):
    rows = cond.shape[0]
    return pl.pallas_call(
        _mod_kernel,
        grid=(DEPTH, 3),
        in_specs=[pl.BlockSpec((rows, D), lambda l, j: (0, 0)),
                  pl.BlockSpec((1, D, D), lambda l, j: (l, 0, j)),
                  pl.BlockSpec((1, 1, D), lambda l, j: (l, 0, j))],
        out_specs=pl.BlockSpec((1, rows, D), lambda l, j: (l, 0, j)),
        out_shape=jax.ShapeDtypeStruct((DEPTH, rows, 3 * D), F32),
        compiler_params=_cparams(2),
        name="adaln_modulation",
    )(cond, w_mod, b_mod.reshape(DEPTH, 1, 3 * D))


def _tok_spec(width, rows=TM, skip=0):
    return pl.BlockSpec((1, rows, width), lambda b, t: (b, t + skip, 0))


def _mod_spec(ctx_tiles=1, skip=0):
    return pl.BlockSpec((1, 1, 3, D), lambda b, t: (jnp.minimum((t + skip) // ctx_tiles, 1), b, 0, 0))


def _const_spec(shape):
    nd = len(shape)
    return pl.BlockSpec(shape, lambda b, t: (0,) * nd)


def _residual_specs(dual, rows=TM, skip=0):
    if not dual:
        return [_tok_spec(D, rows, skip)]
    per = CTX // rows
    return [pl.BlockSpec((1, rows, D), lambda b, t: (b, jnp.minimum(t, per - 1), 0)),
            pl.BlockSpec((1, rows, D), lambda b, t: (b, jnp.maximum(t - per, 0), 0))]


def _wide_specs(dual):
    n = TP // TM
    if not dual:
        return [pl.BlockSpec((1, TM, D), functools.partial(lambda j, b, t: (b, n * t + j, 0), j)) for j in range(n)]
    last = lambda j, b, t: (b, jnp.maximum(n * t + j - CTX // TM, 0), 0)
    return ([pl.BlockSpec((1, CTX, D), lambda b, t: (b, 0, 0))]
            + [pl.BlockSpec((1, TM, D), functools.partial(last, j)) for j in range(n)])


def _wide_modnorm(refs, dual, mod_ref, gpre_ref):
    t = pl.program_id(1)
    blocks = [r[0] for r in refs[1:]] if dual else [r[0] for r in refs]
    if dual:
        blocks[0] = jnp.where(t == 0, refs[0][0], blocks[0])
    x = jnp.concatenate(blocks, axis=0)
    is_ctx = jnp.logical_and(lax.broadcasted_iota(jnp.int32, (TP, 1), 0) < CTX, t == 0)
    mc, mx = mod_ref[0, 0], mod_ref[1, 0]
    scale = jnp.where(is_ctx, mc[1:2], mx[1:2])
    shift = jnp.where(is_ctx, mc[0:1], mx[0:1])
    return _rms(x) * gpre_ref[...] * (1.0 + scale) + shift


def _both_mod_spec():
    return pl.BlockSpec((2, 1, 3, D), lambda b, t: (0, b, 0, 0))


def _resident(shape):
    nd = len(shape)
    return pl.BlockSpec(shape, lambda b, t: (0,) * nd, pipeline_mode=pl.Buffered(1))


def _residual_tile(refs, dual, rows=TM):
    if not dual:
        return refs[0][0]
    return jnp.where(pl.program_id(1) < CTX // rows, refs[0][0], refs[1][0])


def _finish(o, gate_ref, wout_ref, x, mod_ref, gpost_ref):
    a = o.astype(F32) * _silu(gate_ref[0].astype(F32))
    y = _dot(a.astype(BF16), wout_ref[...])
    return x + mod_ref[0, 0][2:3] * (_rms(y) * gpost_ref[...])


def _mla_proj_kernel(*refs, dual):
    nx = TP // TM + (1 if dual else 0)
    (mod_ref, gpre_ref, win_ref, gq_ref, wq_ref, gkv_ref, wkv_ref,
     ra_ref, rb_ref, rc_ref, q_ref, k_ref, v_ref, gate_ref) = refs[nx:]
    h = _wide_modnorm(refs[:nx], dual, mod_ref, gpre_ref)
    p = _dot(h.astype(BF16), win_ref[...])
    c0, c1, c2 = A_Q_RANK, A_Q_RANK + A_KV_RANK, A_Q_RANK + A_KV_RANK + LANE
    gate_ref[0] = p[:, c2:].astype(BF16)
    qn = (_rms(p[:, :c0]) * gq_ref[...]).astype(BF16)
    kvn = (_rms(p[:, c0:c1]) * gkv_ref[...]).astype(BF16)
    ra, rb, rc = ra_ref[...], rb_ref[...], rc_ref[...]
    kr = _rope128(p[:, c1:c2], ra, rb, rc).astype(BF16)
    for hd in range(A_HEADS):
        q = _dot(qn, wq_ref[:, hd * A_HEAD_PAD:(hd + 1) * A_HEAD_PAD])
        q_ref[0, hd, :, 0:LANE] = q[:, 0:LANE].astype(BF16)
        q_ref[0, hd, :, LANE:2 * LANE] = _rope128(q[:, LANE:2 * LANE], ra, rb, rc).astype(BF16)
        kv = _dot(kvn, wkv_ref[:, hd * A_HEAD_PAD:(hd + 1) * A_HEAD_PAD])
        k_ref[0, hd, :, 0:LANE] = kv[:, 0:A_NOPE].astype(BF16)
        k_ref[0, hd, :, LANE:2 * LANE] = kr
        v_ref[0, hd] = kv[:, A_NOPE:].astype(BF16)


def _mla_proj(xsrc, modl, gpre, win, gq, wq, gkv, wkv, tabs):
    dual = len(xsrc) == 2
    bsz = xsrc[0].shape[0]
    t = tabs[0].shape[0]
    xargs = [xsrc[0]] + [xsrc[1]] * (TP // TM) if dual else [xsrc[0]] * (TP // TM)
    head = lambda w: pl.BlockSpec((1, A_HEADS, TP, w), lambda b, i: (b, 0, i, 0))
    tab = pl.BlockSpec((TP, LANE), lambda b, i: (i, 0))
    return pl.pallas_call(
        functools.partial(_mla_proj_kernel, dual=dual),
        grid=(bsz, t // TP),
        in_specs=[*_wide_specs(dual), _both_mod_spec(), _const_spec((1, D)), _resident(win.shape),
                  _const_spec((1, A_Q_RANK)), _resident(wq.shape),
                  _const_spec((1, A_KV_RANK)), _resident(wkv.shape), tab, tab, tab],
        out_specs=[head(A_HEAD_PAD), head(A_HEAD_PAD), head(A_VDIM), _tok_spec(A_HEADS * A_VDIM, TP)],
        out_shape=[jax.ShapeDtypeStruct((bsz, A_HEADS, t, A_HEAD_PAD), BF16),
                   jax.ShapeDtypeStruct((bsz, A_HEADS, t, A_HEAD_PAD), BF16),
                   jax.ShapeDtypeStruct((bsz, A_HEADS, t, A_VDIM), BF16),
                   jax.ShapeDtypeStruct((bsz, t, A_HEADS * A_VDIM), BF16)],
        compiler_params=_cparams(2),
        name="mla_proj",
    )(*xargs, modl, gpre, win, gq, wq, gkv, wkv, *tabs)


def _mla_attn_kernel(*refs, dual, latents_only):
    nx = 2 if dual else 1
    q_ref, k_ref, v_ref, gate_ref, wout_ref = refs[:5]
    mod_ref, gpost_ref, xo_ref, o_ref = refs[5 + nx:]

    def attend(nk):
        s_next = _dot_nt(q_ref[0, 0], k_ref[0, 0, :nk, :])
        for hd in range(A_HEADS):
            s = s_next
            if hd + 1 < A_HEADS:
                s_next = _dot_nt(q_ref[0, hd + 1], k_ref[0, hd + 1, :nk, :])
            m = jnp.broadcast_to(jnp.max(s, axis=-1, keepdims=True), (TM, LANE))
            p = jnp.concatenate([jnp.exp2(s[:, c:c + LANE] - m) for c in range(0, nk, LANE)], axis=1)
            l = jnp.sum(p, axis=-1, keepdims=True)
            o = _dot(p.astype(BF16), v_ref[0, hd, :nk, :]) / l
            o_ref[:, hd * A_VDIM:(hd + 1) * A_VDIM] = o.astype(BF16)

    t_all = k_ref.shape[2]
    if latents_only:
        attend(t_all)
    else:
        is_ctx = pl.program_id(1) == 0
        pl.when(is_ctx)(lambda: attend(CTX))
        pl.when(jnp.logical_not(is_ctx))(lambda: attend(t_all))
    x = _residual_tile(refs[5:5 + nx], dual)
    xo_ref[0] = _finish(o_ref[...], gate_ref, wout_ref, x, mod_ref, gpost_ref)


def _mla_attn(q, k, v, gate, wout, xsrc, modl, gpost, latents_only):
    dual = len(xsrc) == 2
    bsz, nh, t, _ = q.shape
    skip = CTX // TM if latents_only else 0
    nt = t // TM - skip
    return pl.pallas_call(
        functools.partial(_mla_attn_kernel, dual=dual, latents_only=latents_only),
        grid=(bsz, nt),
        in_specs=[pl.BlockSpec((1, nh, TM, A_HEAD_PAD), lambda b, i: (b, 0, i + skip, 0)),
                  pl.BlockSpec((1, nh, t, A_HEAD_PAD), lambda b, i: (b, 0, 0, 0)),
                  pl.BlockSpec((1, nh, t, A_VDIM), lambda b, i: (b, 0, 0, 0)),
                  _tok_spec(nh * A_VDIM, skip=skip), _const_spec(wout.shape),
                  *_residual_specs(dual, skip=skip), _mod_spec(skip=skip), _const_spec((1, D))],
        out_specs=_tok_spec(D),
        out_shape=jax.ShapeDtypeStruct((bsz, nt * TM, D), F32),
        scratch_shapes=[pltpu.VMEM((TM, nh * A_VDIM), BF16)],
        input_output_aliases={} if (dual or latents_only) else {5: 0},
        compiler_params=_cparams(2),
        name="mla_attention_out",
    )(q, k, v, gate, wout, *xsrc, modl, gpost)


def _swa_proj_kernel(*refs):
    xrefs = refs[:TP // TM]
    mod_ref, gpre_ref, win_ref, ra_ref, rb_ref, rc_ref, q_ref, k_ref, v_ref, gate_ref = refs[TP // TM:]
    h = _wide_modnorm(xrefs, False, mod_ref, gpre_ref)
    p = _dot(h.astype(BF16), win_ref[...])
    ra, rb, rc = ra_ref[...], rb_ref[...], rc_ref[...]
    qw, kw = B_HEADS * B_HDIM, B_KV_HEADS * B_HDIM
    lane = lax.broadcasted_iota(jnp.int32, (1, LANE), 1)
    lo = lane < B_HDIM
    ones_blk = jnp.ones((TP, LANE), BF16)
    for j in range(B_KV_HEADS // 2):
        for g in range(B_GROUP):
            s = j * B_GROUP + g
            r = _rope128(p[:, s * LANE:(s + 1) * LANE], ra, rb, rc)
            for half in range(2):
                d = ((2 * j + half) * B_GROUP + g) * LANE
                keep = lo if half == 0 else jnp.logical_not(lo)
                q_ref[0, :, d:d + LANE] = jnp.where(keep, r, 0.0).astype(BF16)
        o = qw + j * LANE
        k_ref[0, :, j * LANE:(j + 1) * LANE] = _rope128(p[:, o:o + LANE], ra, rb, rc).astype(BF16)
        v_ref[0, :, 2 * j * LANE:(2 * j + 1) * LANE] = p[:, o + kw:o + kw + LANE].astype(BF16)
        v_ref[0, :, (2 * j + 1) * LANE:(2 * j + 2) * LANE] = ones_blk
    gate_ref[0] = p[:, qw + 2 * kw:].astype(BF16)


def _swa_proj(xs, modl, gpre, win, tabs):
    bsz, t, _ = xs.shape
    qw, kw = B_HEADS * B_HDIM, B_KV_HEADS * B_HDIM
    tab = pl.BlockSpec((TP, LANE), lambda b, i: (i, 0))
    return pl.pallas_call(
        _swa_proj_kernel,
        grid=(bsz, t // TP),
        in_specs=[*_wide_specs(False), _both_mod_spec(), _const_spec((1, D)), _resident(win.shape), tab, tab, tab],
        out_specs=[_tok_spec(2 * qw, TP), _tok_spec(kw, TP), _tok_spec(2 * kw, TP), _tok_spec(qw, TP)],
        out_shape=[jax.ShapeDtypeStruct((bsz, t, 2 * qw), BF16),
                   jax.ShapeDtypeStruct((bsz, t, kw), BF16),
                   jax.ShapeDtypeStruct((bsz, t, 2 * kw), BF16),
                   jax.ShapeDtypeStruct((bsz, t, qw), BF16)],
        compiler_params=_cparams(2),
        name="swa_proj",
    )(*[xs] * (TP // TM), modl, gpre, win, *tabs)


def _swa_attn_kernel(q_ref, k_ref, v_ref, sink_ref, gate_ref, wout_ref, x_ref, mod_ref, gpost_ref, xo_ref, o_ref):
    i = pl.program_id(1)
    n_lat = k_ref.shape[1] - CTX
    band = 3 * B_BLOCK
    lo = lax.broadcasted_iota(jnp.int32, (1, LANE), 1) < B_HDIM

    def run(with_band):
        chains = [(sub, j, half) for sub in range(TM // B_BLOCK) for j in range(B_KV_HEADS // 2) for half in range(2)]
        window, keys = {}, {}

        def block_window(sub):
            if sub not in window:
                li = (i - CTX // TM) * (TM // B_BLOCK) + sub
                start = jnp.clip((li - 1) * B_BLOCK, 0, n_lat - band)
                off = li * B_BLOCK - start
                d0 = (lax.broadcasted_iota(jnp.int32, (B_BLOCK, band), 0)
                      - lax.broadcasted_iota(jnp.int32, (B_BLOCK, band), 1))
                mask = jnp.abs(d0 + off) <= B_WINDOW
                window[sub] = (mask, pl.multiple_of(CTX + start, B_BLOCK))
            return window[sub]

        def block_keys(sub, j):
            if (sub, j) not in keys:
                kcols = slice(j * LANE, (j + 1) * LANE)
                vcols = slice(2 * j * LANE, (2 * j + 2) * LANE)
                kk = k_ref[0, 0:CTX, kcols]
                vv = v_ref[0, 0:CTX, vcols]
                if with_band:
                    kstart = block_window(sub)[1]
                    kk = jnp.concatenate([kk, k_ref[0, pl.ds(kstart, band), kcols]], axis=0)
                    vv = jnp.concatenate([vv, v_ref[0, pl.ds(kstart, band), vcols]], axis=0)
                keys[(sub, j)] = (kk, vv)
            return keys[(sub, j)]

        def scores(chain):
            sub, j, half = chain
            hk = 2 * j + half
            r0 = sub * B_BLOCK
            qs = jnp.concatenate(
                [q_ref[0, r0:r0 + B_BLOCK, (hk * B_GROUP + g) * LANE:(hk * B_GROUP + g + 1) * LANE]
                 for g in range(B_GROUP)], axis=0)
            return _dot_nt(qs, block_keys(sub, j)[0])

        outs = []
        s_next = scores(chains[0])
        for n, (sub, j, half) in enumerate(chains):
            s = s_next
            if n + 1 < len(chains):
                s_next = scores(chains[n + 1])
            sk = sink_ref[2 * j + half]
            if with_band:
                mask = block_window(sub)[0]
                s_b = jnp.concatenate(
                    [jnp.where(mask, s[g * B_BLOCK:(g + 1) * B_BLOCK, CTX:], NEG_INF) for g in range(B_GROUP)],
                    axis=0)
                s = jnp.concatenate([s[:, :CTX], s_b], axis=1)
            m = jnp.maximum(jnp.broadcast_to(jnp.max(s, axis=-1, keepdims=True), sk.shape), sk)
            p = jnp.concatenate([jnp.exp2(s[:, c:c + LANE] - m) for c in range(0, s.shape[1], LANE)], axis=1)
            pv = _dot(p.astype(BF16), block_keys(sub, j)[1])
            l = pv[:, LANE:] + jnp.exp2(sk - m)
            outs.append(pv[:, :LANE] / l)
            if half == 1:
                comb = jnp.where(lo, outs[-2], outs[-1])
                r0 = sub * B_BLOCK
                for g in range(B_GROUP):
                    s_out = j * B_GROUP + g
                    o_ref[r0:r0 + B_BLOCK, s_out * LANE:(s_out + 1) * LANE] = (
                        comb[g * B_BLOCK:(g + 1) * B_BLOCK].astype(BF16))

    is_ctx = i < CTX // TM
    pl.when(is_ctx)(lambda: run(False))
    pl.when(jnp.logical_not(is_ctx))(lambda: run(True))
    xo_ref[0] = _finish(o_ref[...], gate_ref, wout_ref, x_ref[0], mod_ref, gpost_ref)


def _swa_attn(q, k, v, sink_cols, gate, wout, xs, modl, gpost):
    bsz, t, qw2 = q.shape
    kw = k.shape[-1]
    qw = qw2 // 2
    return pl.pallas_call(
        _swa_attn_kernel,
        grid=(bsz, t // TM),
        in_specs=[pl.BlockSpec((1, TM, qw2), lambda b, i: (b, i, 0)),
                  pl.BlockSpec((1, t, kw), lambda b, i: (b, 0, 0)),
                  pl.BlockSpec((1, t, 2 * kw), lambda b, i: (b, 0, 0)),
                  pl.BlockSpec(sink_cols.shape, lambda b, i: (0, 0, 0)),
                  _tok_spec(qw), _const_spec(wout.shape), _tok_spec(D), _mod_spec(), _const_spec((1, D))],
        out_specs=_tok_spec(D),
        out_shape=jax.ShapeDtypeStruct(xs.shape, F32),
        scratch_shapes=[pltpu.VMEM((TM, qw), BF16)],
        input_output_aliases={6: 0},
        compiler_params=_cparams(2),
        name="swa_attention_out",
    )(q, k, v, sink_cols, gate, wout, xs, modl, gpost)


RADIX = 4
QP = TM // RADIX
TILE_ORDER = (3, 1, 2, 0)


def _hyena_proj_kernel(xp_ref, x_ref, xn_ref, mod_ref, gpre_ref, win_ref, cw_ref, cb_ref, perm_ref, *out_refs):
    u_refs, g_refs = out_refs[:RADIX], out_refs[RADIX:]
    t = pl.program_id(1)
    nt = pl.num_programs(1)
    hh = _dot(perm_ref[...], _modnorm(x_ref[0], mod_ref, gpre_ref).astype(BF16))
    h = jnp.concatenate([_modnorm(xp_ref[0], mod_ref, gpre_ref), hh, _modnorm(xn_ref[0], mod_ref, gpre_ref)], axis=0)
    p = _dot(h.astype(BF16), win_ref[...])
    cwid = 3 * C_WIDTH
    u = p[:, :cwid]
    start = {rho: HALO + k * QP for k, rho in enumerate(TILE_ORDER)}
    cls = {rho: u[start[rho]:start[rho] + QP] for rho in range(RADIX)}
    for rho in range(RADIX):
        g_refs[rho][0] = p[start[rho]:start[rho] + QP, cwid:].astype(BF16)
    before0 = pltpu.roll(u[0:HALO + QP], 1, 0)[HALO:]
    after3 = pltpu.roll(u[start[0]:], QP + HALO - 1, 0)[0:QP]
    r = lax.broadcasted_iota(jnp.int32, (QP, 1), 0)
    before0 = jnp.where(jnp.logical_and(r == 0, t <= 1), 0.0, before0)
    after3 = jnp.where(jnp.logical_and(r == QP - 1, jnp.logical_or(t == 0, t == nt - 1)), 0.0, after3)
    cw = cw_ref[...]
    cb = cb_ref[...]
    left = {0: before0, 1: cls[0], 2: cls[1], 3: cls[2]}
    right = {0: cls[1], 1: cls[2], 2: cls[3], 3: after3}
    for rho in range(RADIX):
        u_refs[rho][0] = (cb + left[rho] * cw[0:1] + cls[rho] * cw[1:2] + right[rho] * cw[2:3]).astype(BF16)


def _class_spec(width):
    return pl.BlockSpec((1, QP, width), lambda b, t: (b, t, 0))


def _hyena_proj(xs, modl, gpre, win, conv_w, conv_b):
    bsz, t, _ = xs.shape
    nt = t // TM
    per = TM // HALO
    last = t // HALO - 1
    part = lambda w: jax.ShapeDtypeStruct((bsz, t // RADIX, w), BF16)
    return pl.pallas_call(
        _hyena_proj_kernel,
        grid=(bsz, nt),
        in_specs=[pl.BlockSpec((1, HALO, D), lambda b, i: (b, jnp.maximum(i * per - 1, 0), 0)),
                  _tok_spec(D),
                  pl.BlockSpec((1, HALO, D), lambda b, i: (b, jnp.minimum((i + 1) * per, last), 0)),
                  _mod_spec(), _const_spec((1, D)), _const_spec(win.shape),
                  _const_spec(conv_w.shape), _const_spec(conv_b.shape), _const_spec((TM, TM))],
        out_specs=[_class_spec(3 * C_WIDTH)] * RADIX + [_class_spec(C_WIDTH)] * RADIX,
        out_shape=[part(3 * C_WIDTH)] * RADIX + [part(C_WIDTH)] * RADIX,
        compiler_params=_cparams(2),
        name="hyena_proj",
    )(xs, xs, xs, modl, gpre, win, conv_w, conv_b, _class_perm(TILE_ORDER))


def _class_perm(order):
    r = np.arange(QP)
    src = np.concatenate([RADIX * r + rho for rho in order])
    m = np.zeros((TM, TM), np.float32)
    m[np.arange(TM), src] = 1.0
    return jnp.asarray(m, BF16)


def _hyena_out_kernel(*refs):
    o_refs, g_refs = refs[:RADIX], refs[RADIX:2 * RADIX]
    wout_ref, x_ref, mod_ref, gpost_ref, perm_ref, xo_ref = refs[2 * RADIX:]
    a = jnp.concatenate([o_refs[rho][0].astype(F32) * _silu(g_refs[rho][0].astype(F32)) for rho in range(RADIX)],
                        axis=0)
    a = _dot(perm_ref[...], a.astype(BF16)).astype(BF16)
    y = _dot(a, wout_ref[...])
    xo_ref[0] = x_ref[0] + mod_ref[0, 0][2:3] * (_rms(y) * gpost_ref[...])


def _hyena_out_proj(os, gs, wout, xs, modl, gpost):
    bsz, t, _ = xs.shape
    w = os[0].shape[-1]
    return pl.pallas_call(
        _hyena_out_kernel,
        grid=(bsz, t // TM),
        in_specs=[_class_spec(w)] * (2 * RADIX) + [_const_spec(wout.shape), _tok_spec(D), _mod_spec(),
                                                   _const_spec((1, D)), _const_spec((TM, TM))],
        out_specs=_tok_spec(D),
        out_shape=jax.ShapeDtypeStruct(xs.shape, F32),
        input_output_aliases={2 * RADIX + 1: 0},
        compiler_params=_cparams(2),
        name="hyena_out_proj_residual",
    )(*os, *gs, wout, xs, modl, gpost, _class_perm(range(RADIX)).T)


def _filter_kernel(z_ref, w1_ref, b1_ref, fr_ref, w2_ref, b2_ref, w3_ref, dl_ref, o_ref):
    hp = lax.Precision.HIGHEST
    z = z_ref[...]
    fr = fr_ref[...]
    h = jnp.sin(fr * (jnp.dot(z, w1_ref[...], precision=hp, preferred_element_type=F32) + b1_ref[...]))
    h = jnp.sin(fr * (jnp.dot(h, w2_ref[...], precision=hp, preferred_element_type=F32) + b2_ref[...]))
    h = jnp.dot(h, w3_ref[...], precision=hp, preferred_element_type=F32)
    o_ref[...] = (h * jnp.exp(-z[:, 0:1] * dl_ref[...])).astype(o_ref.dtype)


SPEC_BLOCK = 512


def _filter_columns():
    o, cb, d, cc = np.meshgrid(np.arange(2), np.arange(C_WIDTH // SPEC_BLOCK), np.arange(2), np.arange(SPEC_BLOCK),
                               indexing="ij")
    return (d * 2 * C_WIDTH + o * C_WIDTH + cb * SPEC_BLOCK + cc).reshape(-1)


def _filters(n, w1, b1, fr, w2, b2, w3):
    t = np.linspace(0.0, 1.0, n, dtype=np.float32)[:, None]
    w = ((2.0 * math.pi / n) * np.arange(n, dtype=np.float32))[:, None].astype(np.float32)
    bands = np.linspace(1e-4, C_BANDS - 1, C_BANDS, dtype=np.float32)[None, :]
    z = np.zeros((n, LANE), np.float32)
    z[:, 0:1] = t
    z[:, 1:1 + C_BANDS] = np.cos(bands * w)
    z[:, 1 + C_BANDS:C_EMB] = -np.sin(bands * w)
    deltas = np.abs(np.linspace(C_MIN_DECAY, C_MAX_DECAY, C_WIDTH, dtype=np.float32))
    cols = _filter_columns()
    dl = deltas[cols % C_WIDTH][None, :]
    w3 = w3[:, cols]

    def pad(a, r, c):
        return jnp.zeros((r, c), F32).at[:a.shape[0], :a.shape[1]].set(a)

    tn = min(n, TM)
    nout = 4 * C_WIDTH
    cs = lambda shape: pl.BlockSpec(shape, lambda i: (0, 0))
    return pl.pallas_call(
        _filter_kernel,
        grid=(n // tn,),
        in_specs=[pl.BlockSpec((tn, LANE), lambda i: (i, 0)), cs((LANE, LANE)), cs((1, LANE)), cs((1, LANE)),
                  cs((LANE, LANE)), cs((1, LANE)), cs((LANE, nout)), cs((1, nout))],
        out_specs=pl.BlockSpec((tn, nout), lambda i: (i, 0)),
        out_shape=jax.ShapeDtypeStruct((n, nout), BF16),
        compiler_params=_cparams(1),
        name="hyena_filter_mlp",
    )(jnp.asarray(z), pad(w1, LANE, LANE), pad(b1[None], 1, LANE), pad(fr[None], 1, LANE),
      pad(w2, LANE, LANE), pad(b2[None], 1, LANE), pad(w3, LANE, nout), jnp.asarray(dl))


def _dft_matrix(n):
    f = np.arange(n, dtype=np.int64)[:, None]
    s = np.arange(n, dtype=np.int64)[None, :]
    ang = (2.0 * np.pi / (2 * n)) * ((f * s) % (2 * n)).astype(np.float64)
    cos = np.cos(ang)
    sin = np.sin(ang)
    sin[0, :] = np.where(np.arange(n) % 2 == 0, 1.0, -1.0)
    return cos.astype(np.float32), sin.astype(np.float32)


def _spectrum_kernel(a_ref, b_ref, re_ref, im_ref, *, scale):
    acc = _dot(a_ref[...], b_ref[...])
    fwd, bwd = acc[:, :SPEC_BLOCK], acc[:, SPEC_BLOCK:]
    re_ref[...] = (fwd + bwd) * scale
    im_ref[...] = (bwd - fwd) * scale


def _spectrum(a, b, bm, scale):
    m, k = a.shape
    nb = b.shape[1] // (2 * SPEC_BLOCK)
    assert m % bm == 0
    out = jax.ShapeDtypeStruct((m, nb * SPEC_BLOCK), F32)
    o_spec = pl.BlockSpec((bm, SPEC_BLOCK), lambda i, j: (i, j))
    return pl.pallas_call(
        functools.partial(_spectrum_kernel, scale=scale),
        grid=(m // bm, nb),
        in_specs=[pl.BlockSpec((bm, k), lambda i, j: (i, 0)), pl.BlockSpec((k, 2 * SPEC_BLOCK), lambda i, j: (0, j))],
        out_specs=[o_spec, o_spec],
        out_shape=[out, out],
        compiler_params=_cparams(2),
        name="filter_spectrum",
    )(a, b)


_PHASE = [(math.cos(math.pi * rho / 4), math.sin(math.pi * rho / 4),
           math.cos(3 * math.pi * rho / 4), math.sin(3 * math.pi * rho / 4)) for rho in range(RADIX)]


def _cmul(xr, xs, kr, ki):
    return xr * kr + xs * ki, xs * kr - xr * ki


def _hyena_conv_kernel(*refs, row0, q, aliased):
    nu = 3 * RADIX
    u_refs = refs[:nu]
    m_refs = refs[nu:nu + RADIX]
    mt_refs = refs[nu + RADIX:nu + 2 * RADIX]
    k_refs = refs[nu + 2 * RADIX:nu + 2 * RADIX + 8]
    ksr_ref, ksi_ref, fb_ref = refs[nu + 2 * RADIX + 8:nu + 2 * RADIX + 11]
    rest = refs[nu + 2 * RADIX + 11 + (RADIX if aliased else 0):]
    o_refs, z_refs = rest[:RADIX], rest[RADIX:]
    o = pl.program_id(2)
    rows = slice(row0, row0 + q)

    @pl.when(o == 0)
    def _():
        for rho in range(RADIX):
            z_refs[rho][...] = u_refs[3 * rho][0, rows, :]

    ksr, ksi = ksr_ref[...], ksi_ref[...]
    ksp = jnp.concatenate([ksr[0:3], ksi[5:6], ksr[3:4], ksi[6:7], ksr[4:5], ksi[7:8]], axis=0)
    first = lax.broadcasted_iota(jnp.int32, (8, 1), 0) == 0
    nblk = m_refs[0].shape[0]
    fbs = m_refs[0].shape[1] // 2

    def forward(j):
        return [_dot(m_refs[rho][j], z_refs[rho][...]) for rho in range(RADIX)]

    def spectral(j, fwd):
        c = [g[:fbs] for g in fwd]
        sn = [g[fbs:] for g in fwd]
        ar, as_, br, bs = c[0] + c[2], sn[0] + sn[2], c[1] + c[3], sn[1] + sn[3]
        cr, cs, dr, ds = c[0] - c[2], sn[0] - sn[2], c[1] - c[3], sn[1] - sn[3]
        k = [r[j * fbs:(j + 1) * fbs, :] for r in k_refs]
        y1r, y1s = _cmul(ar + br, as_ + bs, k[0], k[1])
        y2r, y2s = _cmul(cr - ds, cs + dr, k[2], k[3])
        y3r, y3s = _cmul(ar - br, bs - as_, k[4], k[5])
        y4r, y4s = _cmul(cr + ds, dr - cs, k[6], k[7])
        pr, ps, qr, qs = y1r + y3r, y1s - y3s, y1r - y3r, y1s + y3s
        rr, rs, tr, ts = y2r + y4r, y2s - y4s, y2r - y4r, y2s + y4s
        wr = [pr + rr, qr + ts, pr - rr, qr - ts]
        ws = [ps + rs, qs - tr, ps - rs, qs + tr]
        if j == 0:
            c0 = [x[0:8] for x in c]
            a0 = [x[0:8] for x in sn]
            y0 = (c0[0] + c0[1] + c0[2] + c0[3]) * ksp[0:1]
            yn = (c0[0] - c0[1] + c0[2] - c0[3]) * ksp[1:2]
            yhr, yhs = _cmul(c0[0] - c0[2], c0[1] - c0[3], ksp[2:3], ksp[3:4])
            xqr = sum(_PHASE[rho][0] * a0[rho] for rho in range(1, RADIX)) + a0[0]
            xqs = sum(_PHASE[rho][1] * a0[rho] for rho in range(1, RADIX))
            xgr = sum(_PHASE[rho][2] * a0[rho] for rho in range(1, RADIX)) + a0[0]
            xgs = sum(_PHASE[rho][3] * a0[rho] for rho in range(1, RADIX))
            yqr, yqs = _cmul(xqr, xqs, ksp[4:5], ksp[5:6])
            ygr, ygs = _cmul(xgr, xgs, ksp[6:7], ksp[7:8])
            turn = [yhr, yhs, -yhr, -yhs]
            for rho in range(RADIX):
                pc, psn, gc, gsn = _PHASE[rho]
                cos0 = y0 + (yn if rho % 2 == 0 else -yn) + turn[rho]
                sin0 = yqr * pc + yqs * psn + ygr * gc + ygs * gsn
                wr[rho] = jnp.concatenate([jnp.where(first, cos0, wr[rho][0:8]), wr[rho][8:]], axis=0)
                ws[rho] = jnp.concatenate([jnp.where(first, sin0, ws[rho][0:8]), ws[rho][8:]], axis=0)
        return [jnp.concatenate([wr[rho], ws[rho]], axis=0).astype(BF16) for rho in range(RADIX)]

    acc = [None] * RADIX
    fwd_next = forward(0)
    for j in range(nblk):
        fwd = fwd_next
        if j + 1 < nblk:
            fwd_next = forward(j + 1)
        w = spectral(j, fwd)
        for rho in range(RADIX):
            part = _dot(mt_refs[rho][j], w[rho])
            acc[rho] = part if acc[rho] is None else acc[rho] + part
    f_out = [acc[rho] + z_refs[rho][...].astype(F32) * fb_ref[0] for rho in range(RADIX)]

    @pl.when(o == 0)
    def _():
        for rho in range(RADIX):
            z_refs[rho][...] = (u_refs[3 * rho + 1][0, rows, :].astype(F32) * f_out[rho]).astype(BF16)

    @pl.when(o == 1)
    def _():
        for rho in range(RADIX):
            if row0 > 0:
                o_refs[rho][0, 0:row0, :] = jnp.zeros((row0, o_refs[rho].shape[2]), BF16)
            o_refs[rho][0, rows, :] = (u_refs[3 * rho + 2][0, rows, :].astype(F32) * f_out[rho]).astype(BF16)


def _hyena_conv(us, mats, spec_re, spec_im, fbias, *, row0, q, block_rows, tc, prev=None):
    bsz, t4, _ = us[0].shape
    nct = C_WIDTH // tc
    u_spec = lambda which: pl.BlockSpec((1, block_rows, tc), lambda b, c, o: (b, 0, which * nct + c))
    m_spec = pl.BlockSpec(mats[0].shape, lambda b, c, o: (0, 0, 0))
    mt_spec = pl.BlockSpec(mats[RADIX].shape, lambda b, c, o: (0, 0, 0))
    p_spec = lambda blk: pl.BlockSpec((q, tc), functools.partial(lambda g, b, c, o: (g, o * nct + c), blk))
    s_spec = pl.BlockSpec((8, tc), lambda b, c, o: (q, o * nct + c))
    in_specs = ([u_spec(w) for _ in range(RADIX) for w in range(3)] + [m_spec] * RADIX + [mt_spec] * RADIX
                + [p_spec(blk) for g in range(4) for blk in (g, 4 + g)] + [s_spec, s_spec]
                + [pl.BlockSpec((1, 1, tc), lambda b, c, o: (o, 0, c))])
    args = ([u for u in us for _ in range(3)] + list(mats) + [spec_re, spec_im] * 4 + [spec_re, spec_im, fbias])
    aliases = {}
    if prev is not None:
        in_specs += [pl.BlockSpec(memory_space=pl.ANY)] * RADIX
        aliases = {len(args) + rho: rho for rho in range(RADIX)}
        args += list(prev)
    o_spec = pl.BlockSpec((1, block_rows, tc), lambda b, c, o: (b, 0, c))
    return pl.pallas_call(
        functools.partial(_hyena_conv_kernel, row0=row0, q=q, aliased=prev is not None),
        grid=(bsz, nct, 2),
        in_specs=in_specs,
        out_specs=[o_spec] * RADIX,
        out_shape=[jax.ShapeDtypeStruct((bsz, t4, C_WIDTH), BF16)] * RADIX,
        scratch_shapes=[pltpu.VMEM((q, tc), BF16)] * RADIX,
        input_output_aliases=aliases,
        compiler_params=_cparams(3),
        name="hyena_long_conv_q%d" % q,
    )(*args)


def _radix4_matrices(n):
    q = n // RADIX
    fbs = min(q, LANE)
    nblk = q // fbs
    f = np.arange(q, dtype=np.int64)[:, None]
    r = np.arange(q, dtype=np.int64)[None, :]
    alt = np.where(np.arange(q) % 2 == 0, 1.0, -1.0)
    mats = []
    for rho in range(RADIX):
        ang = (2.0 * np.pi / (2 * n)) * ((f * (RADIX * r + rho)) % (2 * n)).astype(np.float64)
        cos, sin = np.cos(ang), np.sin(ang)
        sin[0, :] = alt
        mats.append(np.concatenate([cos.reshape(nblk, fbs, q), sin.reshape(nblk, fbs, q)], axis=1).astype(np.float32))
    tr = lambda m: np.ascontiguousarray(np.transpose(m, (0, 2, 1)))
    return tuple(jnp.asarray(m, BF16) for m in mats) + tuple(jnp.asarray(tr(m), BF16) for m in mats)


def _hyena_spectra(n, filt):
    h, q = n // 2, n // RADIX
    cos, sin = _dft_matrix(n)
    f = np.arange(q)
    groups = [f, h + f, np.maximum(n - f, 1) % n, h - f]
    special = [0.5 * cos[0:1], 0.5 * sin[0:1], cos[h:h + 1], cos[q:q + 1], cos[h + q:h + q + 1],
               sin[h:h + 1], sin[q:q + 1], sin[h + q:h + q + 1]]
    rows = np.concatenate([cos[g] for g in groups] + [sin[g] for g in groups] + special
                          + [np.zeros((LANE - 8, n), np.float32)], axis=0)
    nr = 8 * q + LANE
    return _spectrum(jnp.asarray(rows, BF16), filt, 3 * LANE if nr % (3 * LANE) == 0 else nr, 2.0 / (2 * n))


def _rope_tables(seq, layout):
    rows = seq // GRID_W
    row = np.repeat(np.arange(rows, dtype=np.float32), GRID_W)
    col = np.tile(np.arange(GRID_W, dtype=np.float32), rows)
    per_axis = 32
    inv = (ROPE_BASE ** (-np.arange(0, per_axis, 2, dtype=np.float32) / per_axis)).astype(np.float32)
    ang = np.concatenate([row[:, None] * inv, col[:, None] * inv], axis=-1)
    cos = np.concatenate([np.ones((CTX, 32), np.float32), np.cos(ang)], axis=0)
    sin = np.concatenate([np.zeros((CTX, 32), np.float32), np.sin(ang)], axis=0)
    one, zero = np.ones_like(cos), np.zeros_like(cos)
    if layout == "mla":
        a = [cos, cos, one, one]
        b = [zero, sin, zero, zero]
        c = [-sin, zero, zero, zero]
    else:
        a = [cos, cos, cos, cos]
        b = [zero, sin, zero, sin]
        c = [-sin, zero, -sin, zero]
    return tuple(jnp.asarray(np.concatenate(p, axis=1), F32) for p in (a, b, c))


def _swa_head_perm():
    cols = []
    for j in range(B_KV_HEADS // 2):
        for g in range(B_GROUP):
            for hk in (2 * j, 2 * j + 1):
                h = hk * B_GROUP + g
                cols.extend(range(h * B_HDIM, (h + 1) * B_HDIM))
    return np.asarray(cols, np.int32)


def _mla_weights(w_in, w_q, w_kv):
    c1 = A_Q_RANK + A_KV_RANK
    zpad = jnp.zeros((D, LANE - A_ROPE), F32)
    win = jnp.concatenate([w_in[:, :c1 + A_ROPE], zpad, w_in[:, c1 + A_ROPE:]], axis=1)
    qscale = (A_NOPE + A_ROPE) ** -0.5 * math.log2(math.e)
    wq = w_q.reshape(A_Q_RANK, A_HEADS, A_NOPE + A_ROPE) * qscale
    wq = jnp.concatenate([wq, jnp.zeros((A_Q_RANK, A_HEADS, A_HEAD_PAD - A_NOPE - A_ROPE), F32)], axis=-1)
    wq = wq.reshape(A_Q_RANK, A_HEADS * A_HEAD_PAD)
    return win.astype(BF16), wq.astype(BF16), w_kv.astype(BF16)


def kernel(x, c, ctx, c_ctx, w_mod, b_mod, g_pre, g_post, a_w_in, a_g_q, a_w_q, a_g_kv, a_w_kv, a_w_out, b_w_in, b_sink, b_w_out, c_w_in, c_conv_w, c_conv_b, c_f_w1, c_f_b1, c_f_freq, c_f_w2, c_f_b2, c_f_w3, c_filt_bias, c_w_out):
    bsz, seq, _ = x.shape
    assert ctx.shape[1] == CTX and (CTX + seq) % TP == 0 and seq % GRID_W == 0
    xsrc = (ctx, x)

    pad_rows = (-(bsz + 1)) % 8
    cond = jnp.concatenate([c, c_ctx[None], jnp.zeros((pad_rows, D), F32)], axis=0)
    mod = _modulation(cond, w_mod, b_mod)

    tabs_mla = _rope_tables(seq, "mla")
    tabs_swa = _rope_tables(seq, "swa")

    for layer in range(DEPTH):
        kind, j = layer % 3, layer // 3
        mx = mod[layer, :bsz].reshape(bsz, 3, D)
        mc = jnp.broadcast_to(mod[layer, bsz].reshape(1, 3, D), (bsz, 3, D))
        modl = jnp.stack([mc, mx], axis=0)
        gpre = g_pre[layer][None]
        gpost = g_post[layer][None]
        if kind == 0:
            win, wq, wkv = _mla_weights(a_w_in[j], a_w_q[j], a_w_kv[j])
            q, k, v, gate = _mla_proj(xsrc, modl, gpre, win, a_g_q[j][None], wq, a_g_kv[j][None], wkv, tabs_mla)
            xs = _mla_attn(q, k, v, gate, a_w_out[j].astype(BF16), xsrc, modl, gpost,
                           latents_only=layer == DEPTH - 1)
        elif kind == 1:
            xs, = xsrc
            perm = _swa_head_perm()
            qw, kw = B_HEADS * B_HDIM, B_KV_HEADS * B_HDIM
            w = b_w_in[j]
            win = jnp.concatenate([w[:, :qw][:, perm] * (B_HDIM ** -0.5 * math.log2(math.e)), w[:, qw:qw + 2 * kw],
                                   w[:, qw + 2 * kw:][:, perm]], axis=1).astype(BF16)
            q, k, v, gate = _swa_proj(xs, modl, gpre, win, tabs_swa)
            sink = (b_sink[j].astype(F32) * math.log2(math.e)).reshape(B_KV_HEADS, B_GROUP, 1, 1)
            sink_cols = jnp.broadcast_to(sink, (B_KV_HEADS, B_GROUP, B_BLOCK, LANE)).reshape(B_KV_HEADS, B_GROUP * B_BLOCK, LANE)
            xs = _swa_attn(q, k, v, sink_cols, gate, b_w_out[j][perm, :].astype(BF16), xs, modl, gpost)
        else:
            xs, = xsrc
            parts = _hyena_proj(xs, modl, gpre, c_w_in[j].astype(BF16), c_conv_w[j], c_conv_b[j][None])
            us, gs = parts[:RADIX], parts[RADIX:]
            fargs = (c_f_w1[j], c_f_b1[j], c_f_freq[j], c_f_w2[j], c_f_b2[j], c_f_w3[j])
            fbias = c_filt_bias[j].reshape(2, 1, C_WIDTH)
            os = _hyena_conv(us, _radix4_matrices(seq), *_hyena_spectra(seq, _filters(seq, *fargs)), fbias,
                             row0=CTX // RADIX, q=seq // RADIX, block_rows=(CTX + seq) // RADIX, tc=MXU_W)
            os = _hyena_conv(us, _radix4_matrices(CTX), *_hyena_spectra(CTX, _filters(CTX, *fargs)), fbias,
                             row0=0, q=CTX // RADIX, block_rows=CTX // RADIX, tc=C_WIDTH, prev=os)
            xs = _hyena_out_proj(os, gs, c_w_out[j].astype(BF16), xs, modl, gpost)
        xsrc = (xs,)
    return xs
```

```python
import functools
import math

import numpy as np
import jax
import jax.numpy as jnp
from jax import lax
from jax.experimental import pallas as pl
from jax.experimental.pallas import tpu as pltpu

F32 = jnp.float32
BF16 = jnp.bfloat16

D = 1024
DEPTH = 4
GRID_W = 64
CTX = 256
NORM_EPS = 1e-6
ROPE_BASE = 10000.0
NEG_INF = -1e30

A_HEADS = 8
A_Q_RANK = 512
A_KV_RANK = 256
A_NOPE = 128
A_ROPE = 64
A_VDIM = 128
A_HEAD_PAD = 256

B_HEADS = 16
B_KV_HEADS = 4
B_GROUP = 4
B_HDIM = 64
B_WINDOW = 128
B_BLOCK = 128

C_WIDTH = 1024
C_BANDS = 16
C_EMB = 1 + 2 * C_BANDS
C_FFN = 64
C_MIN_DECAY = math.log(1e-2) / 1.5
C_MAX_DECAY = math.log(1e-2) / 0.3

LANE = 128
MXU_W = 256
TM = 256
TP = 3 * TM
HALO = 8
VMEM_LIMIT = 56 * 1024 * 1024


def _cparams(n_axes):
    return pltpu.CompilerParams(dimension_semantics=("arbitrary",) * n_axes,
                                vmem_limit_bytes=VMEM_LIMIT)


def _rms(x):
    return x * lax.rsqrt(jnp.mean(x * x, axis=-1, keepdims=True) + NORM_EPS)


def _silu(g):
    return g / (1.0 + jnp.exp(-g))


def _dot(a, b):
    return jnp.dot(a, b, preferred_element_type=F32)


def _dot_nt(a, b):
    return lax.dot_general(a, b, (((1,), (1,)), ((), ())), preferred_element_type=F32)


def _rope128(x, a, b, c):
    return x * a + pltpu.roll(x, 32, 1) * b + pltpu.roll(x, LANE - 32, 1) * c


def _modnorm(x, mod_ref, gpre_ref):
    m = mod_ref[0, 0]
    return _rms(x) * gpre_ref[...] * (1.0 + m[1:2]) + m[0:1]


def _mod_kernel(c_ref, w_ref, b_ref, o_ref):
    a = _silu(c_ref[...])
    o_ref[0] = _dot(a.astype(BF16), w_ref[0].astype(BF16)) + b_ref[0]


def _modulation(cond, w_mod, b_mod):
    rows = cond.shape[0]
    return pl.pallas_call(
        _mod_kernel,
        grid=(DEPTH, 3),
        in_specs=[pl.BlockSpec((rows, D), lambda l, j: (0, 0)),
                  pl.BlockSpec((1, D, D), lambda l, j: (l, 0, j)),
                  pl.BlockSpec((1, 1, D), lambda l, j: (l, 0, j))],
        out_specs=pl.BlockSpec((1, rows, D), lambda l, j: (l, 0, j)),
        out_shape=jax.ShapeDtypeStruct((DEPTH, rows, 3 * D), F32),
        compiler_params=_cparams(2),
        name="adaln_modulation",
    )(cond, w_mod, b_mod.reshape(DEPTH, 1, 3 * D))


def _tok_spec(width, rows=TM, skip=0):
    return pl.BlockSpec((1, rows, width), lambda b, t: (b, t + skip, 0))


def _mod_spec(ctx_tiles=1, skip=0):
    return pl.BlockSpec((1, 1, 3, D), lambda b, t: (jnp.minimum((t + skip) // ctx_tiles, 1), b, 0, 0))


def _const_spec(shape):
    nd = len(shape)
    return pl.BlockSpec(shape, lambda b, t: (0,) * nd)


def _residual_specs(dual, rows=TM, skip=0):
    if not dual:
        return [_tok_spec(D, rows, skip)]
    per = CTX // rows
    return [pl.BlockSpec((1, rows, D), lambda b, t: (b, jnp.minimum(t, per - 1), 0)),
            pl.BlockSpec((1, rows, D), lambda b, t: (b, jnp.maximum(t - per, 0), 0))]


def _wide_specs(dual):
    n = TP // TM
    if not dual:
        return [pl.BlockSpec((1, TM, D), functools.partial(lambda j, b, t: (b, n * t + j, 0), j)) for j in range(n)]
    last = lambda j, b, t: (b, jnp.maximum(n * t + j - CTX // TM, 0), 0)
    return ([pl.BlockSpec((1, CTX, D), lambda b, t: (b, 0, 0))]
            + [pl.BlockSpec((1, TM, D), functools.partial(last, j)) for j in range(n)])


def _wide_modnorm(refs, dual, mod_ref, gpre_ref):
    t = pl.program_id(1)
    blocks = [r[0] for r in refs[1:]] if dual else [r[0] for r in refs]
    if dual:
        blocks[0] = jnp.where(t == 0, refs[0][0], blocks[0])
    x = jnp.concatenate(blocks, axis=0)
    is_ctx = jnp.logical_and(lax.broadcasted_iota(jnp.int32, (TP, 1), 0) < CTX, t == 0)
    mc, mx = mod_ref[0, 0], mod_ref[1, 0]
    scale = jnp.where(is_ctx, mc[1:2], mx[1:2])
    shift = jnp.where(is_ctx, mc[0:1], mx[0:1])
    return _rms(x) * gpre_ref[...] * (1.0 + scale) + shift


def _both_mod_spec():
    return pl.BlockSpec((2, 1, 3, D), lambda b, t: (0, b, 0, 0))


def _resident(shape):
    nd = len(shape)
    return pl.BlockSpec(shape, lambda b, t: (0,) * nd, pipeline_mode=pl.Buffered(1))


def _residual_tile(refs, dual, rows=TM):
    if not dual:
        return refs[0][0]
    return jnp.where(pl.program_id(1) < CTX // rows, refs[0][0], refs[1][0])


def _finish(o, gate_ref, wout_ref, x, mod_ref, gpost_ref):
    a = o.astype(F32) * _silu(gate_ref[0].astype(F32))
    y = _dot(a.astype(BF16), wout_ref[...])
    return x + mod_ref[0, 0][2:3] * (_rms(y) * gpost_ref[...])


def _mla_proj_kernel(*refs, dual):
    nx = TP // TM + (1 if dual else 0)
    (mod_ref, gpre_ref, win_ref, gq_ref, wq_ref, gkv_ref, wkv_ref,
     ra_ref, rb_ref, rc_ref, q_ref, k_ref, v_ref, gate_ref) = refs[nx:]
    h = _wide_modnorm(refs[:nx], dual, mod_ref, gpre_ref)
    p = _dot(h.astype(BF16), win_ref[...])
    c0, c1, c2 = A_Q_RANK, A_Q_RANK + A_KV_RANK, A_Q_RANK + A_KV_RANK + LANE
    gate_ref[0] = p[:, c2:].astype(BF16)
    qn = (_rms(p[:, :c0]) * gq_ref[...]).astype(BF16)
    kvn = (_rms(p[:, c0:c1]) * gkv_ref[...]).astype(BF16)
    ra, rb, rc = ra_ref[...], rb_ref[...], rc_ref[...]
    kr = _rope128(p[:, c1:c2], ra, rb, rc).astype(BF16)
    for hd in range(A_HEADS):
        q = _dot(qn, wq_ref[:, hd * A_HEAD_PAD:(hd + 1) * A_HEAD_PAD])
        q_ref[0, hd, :, 0:LANE] = q[:, 0:LANE].astype(BF16)
        q_ref[0, hd, :, LANE:2 * LANE] = _rope128(q[:, LANE:2 * LANE], ra, rb, rc).astype(BF16)
        kv = _dot(kvn, wkv_ref[:, hd * A_HEAD_PAD:(hd + 1) * A_HEAD_PAD])
        k_ref[0, hd, :, 0:LANE] = kv[:, 0:A_NOPE].astype(BF16)
        k_ref[0, hd, :, LANE:2 * LANE] = kr
        v_ref[0, hd] = kv[:, A_NOPE:].astype(BF16)


def _mla_proj(xsrc, modl, gpre, win, gq, wq, gkv, wkv, tabs):
    dual = len(xsrc) == 2
    bsz = xsrc[0].shape[0]
    t = tabs[0].shape[0]
    xargs = [xsrc[0]] + [xsrc[1]] * (TP // TM) if dual else [xsrc[0]] * (TP // TM)
    head = lambda w: pl.BlockSpec((1, A_HEADS, TP, w), lambda b, i: (b, 0, i, 0))
    tab = pl.BlockSpec((TP, LANE), lambda b, i: (i, 0))
    return pl.pallas_call(
        functools.partial(_mla_proj_kernel, dual=dual),
        grid=(bsz, t // TP),
        in_specs=[*_wide_specs(dual), _both_mod_spec(), _const_spec((1, D)), _resident(win.shape),
                  _const_spec((1, A_Q_RANK)), _resident(wq.shape),
                  _const_spec((1, A_KV_RANK)), _resident(wkv.shape), tab, tab, tab],
        out_specs=[head(A_HEAD_PAD), head(A_HEAD_PAD), head(A_VDIM), _tok_spec(A_HEADS * A_VDIM, TP)],
        out_shape=[jax.ShapeDtypeStruct((bsz, A_HEADS, t, A_HEAD_PAD), BF16),
                   jax.ShapeDtypeStruct((bsz, A_HEADS, t, A_HEAD_PAD), BF16),
                   jax.ShapeDtypeStruct((bsz, A_HEADS, t, A_VDIM), BF16),
                   jax.ShapeDtypeStruct((bsz, t, A_HEADS * A_VDIM), BF16)],
        compiler_params=_cparams(2),
        name="mla_proj",
    )(*xargs, modl, gpre, win, gq, wq, gkv, wkv, *tabs)


def _mla_attn_kernel(*refs, dual, latents_only):
    nx = 2 if dual else 1
    q_ref, k_ref, v_ref, gate_ref, wout_ref = refs[:5]
    mod_ref, gpost_ref, xo_ref, o_ref = refs[5 + nx:]

    def attend(nk):
        s_next = _dot_nt(q_ref[0, 0], k_ref[0, 0, :nk, :])
        for hd in range(A_HEADS):
            s = s_next
            if hd + 1 < A_HEADS:
                s_next = _dot_nt(q_ref[0, hd + 1], k_ref[0, hd + 1, :nk, :])
            m = jnp.broadcast_to(jnp.max(s, axis=-1, keepdims=True), (TM, LANE))
            p = jnp.concatenate([jnp.exp2(s[:, c:c + LANE] - m) for c in range(0, nk, LANE)], axis=1)
            vx = jnp.concatenate([v_ref[0, hd, :nk, :], jnp.ones((nk, LANE), BF16)], axis=1)
            pv = _dot(p.astype(BF16), vx)
            o_ref[:, hd * A_VDIM:(hd + 1) * A_VDIM] = (pv[:, :A_VDIM] / pv[:, A_VDIM:]).astype(BF16)

    t_all = k_ref.shape[2]
    if latents_only:
        attend(t_all)
    else:
        is_ctx = pl.program_id(1) == 0
        pl.when(is_ctx)(lambda: attend(CTX))
        pl.when(jnp.logical_not(is_ctx))(lambda: attend(t_all))
    x = _residual_tile(refs[5:5 + nx], dual)
    xo_ref[0] = _finish(o_ref[...], gate_ref, wout_ref, x, mod_ref, gpost_ref)


def _mla_attn(q, k, v, gate, wout, xsrc, modl, gpost, latents_only):
    dual = len(xsrc) == 2
    bsz, nh, t, _ = q.shape
    skip = CTX // TM if latents_only else 0
    nt = t // TM - skip
    return pl.pallas_call(
        functools.partial(_mla_attn_kernel, dual=dual, latents_only=latents_only),
        grid=(bsz, nt),
        in_specs=[pl.BlockSpec((1, nh, TM, A_HEAD_PAD), lambda b, i: (b, 0, i + skip, 0)),
                  pl.BlockSpec((1, nh, t, A_HEAD_PAD), lambda b, i: (b, 0, 0, 0)),
                  pl.BlockSpec((1, nh, t, A_VDIM), lambda b, i: (b, 0, 0, 0)),
                  _tok_spec(nh * A_VDIM, skip=skip), _const_spec(wout.shape),
                  *_residual_specs(dual, skip=skip), _mod_spec(skip=skip), _const_spec((1, D))],
        out_specs=_tok_spec(D),
        out_shape=jax.ShapeDtypeStruct((bsz, nt * TM, D), F32),
        scratch_shapes=[pltpu.VMEM((TM, nh * A_VDIM), BF16)],
        input_output_aliases={} if (dual or latents_only) else {5: 0},
        compiler_params=_cparams(2),
        name="mla_attention_out",
    )(q, k, v, gate, wout, *xsrc, modl, gpost)


def _swa_proj_kernel(*refs):
    xrefs = refs[:TP // TM]
    mod_ref, gpre_ref, win_ref, ra_ref, rb_ref, rc_ref, q_ref, k_ref, v_ref, gate_ref = refs[TP // TM:]
    h = _wide_modnorm(xrefs, False, mod_ref, gpre_ref)
    p = _dot(h.astype(BF16), win_ref[...])
    ra, rb, rc = ra_ref[...], rb_ref[...], rc_ref[...]
    qw, kw = B_HEADS * B_HDIM, B_KV_HEADS * B_HDIM
    lane = lax.broadcasted_iota(jnp.int32, (1, LANE), 1)
    lo = lane < B_HDIM
    ones_blk = jnp.ones((TP, LANE), BF16)
    for j in range(B_KV_HEADS // 2):
        for g in range(B_GROUP):
            s = j * B_GROUP + g
            r = _rope128(p[:, s * LANE:(s + 1) * LANE], ra, rb, rc)
            for half in range(2):
                d = ((2 * j + half) * B_GROUP + g) * LANE
                keep = lo if half == 0 else jnp.logical_not(lo)
                q_ref[0, :, d:d + LANE] = jnp.where(keep, r, 0.0).astype(BF16)
        o = qw + j * LANE
        k_ref[0, :, j * LANE:(j + 1) * LANE] = _rope128(p[:, o:o + LANE], ra, rb, rc).astype(BF16)
        v_ref[0, :, 2 * j * LANE:(2 * j + 1) * LANE] = p[:, o + kw:o + kw + LANE].astype(BF16)
        v_ref[0, :, (2 * j + 1) * LANE:(2 * j + 2) * LANE] = ones_blk
    gate_ref[0] = p[:, qw + 2 * kw:].astype(BF16)


def _swa_proj(xs, modl, gpre, win, tabs):
    bsz, t, _ = xs.shape
    qw, kw = B_HEADS * B_HDIM, B_KV_HEADS * B_HDIM
    tab = pl.BlockSpec((TP, LANE), lambda b, i: (i, 0))
    return pl.pallas_call(
        _swa_proj_kernel,
        grid=(bsz, t // TP),
        in_specs=[*_wide_specs(False), _both_mod_spec(), _const_spec((1, D)), _resident(win.shape), tab, tab, tab],
        out_specs=[_tok_spec(2 * qw, TP), _tok_spec(kw, TP), _tok_spec(2 * kw, TP), _tok_spec(qw, TP)],
        out_shape=[jax.ShapeDtypeStruct((bsz, t, 2 * qw), BF16),
                   jax.ShapeDtypeStruct((bsz, t, kw), BF16),
                   jax.ShapeDtypeStruct((bsz, t, 2 * kw), BF16),
                   jax.ShapeDtypeStruct((bsz, t, qw), BF16)],
        compiler_params=_cparams(2),
        name="swa_proj",
    )(*[xs] * (TP // TM), modl, gpre, win, *tabs)


def _swa_attn_kernel(q_ref, k_ref, v_ref, sink_ref, gate_ref, wout_ref, x_ref, mod_ref, gpost_ref, xo_ref, o_ref):
    i = pl.program_id(1)
    n_lat = k_ref.shape[1] - CTX
    band = 3 * B_BLOCK
    lo = lax.broadcasted_iota(jnp.int32, (1, LANE), 1) < B_HDIM

    def run(with_band):
        chains = [(sub, j, half) for sub in range(TM // B_BLOCK) for j in range(B_KV_HEADS // 2) for half in range(2)]
        window, keys = {}, {}

        def block_window(sub):
            if sub not in window:
                li = (i - CTX // TM) * (TM // B_BLOCK) + sub
                start = jnp.clip((li - 1) * B_BLOCK, 0, n_lat - band)
                off = li * B_BLOCK - start
                d0 = (lax.broadcasted_iota(jnp.int32, (B_BLOCK, band), 0)
                      - lax.broadcasted_iota(jnp.int32, (B_BLOCK, band), 1))
                mask = jnp.abs(d0 + off) <= B_WINDOW
                window[sub] = (mask, pl.multiple_of(CTX + start, B_BLOCK))
            return window[sub]

        def block_keys(sub, j):
            if (sub, j) not in keys:
                kcols = slice(j * LANE, (j + 1) * LANE)
                vcols = slice(2 * j * LANE, (2 * j + 2) * LANE)
                kk = k_ref[0, 0:CTX, kcols]
                vv = v_ref[0, 0:CTX, vcols]
                if with_band:
                    kstart = block_window(sub)[1]
                    kk = jnp.concatenate([kk, k_ref[0, pl.ds(kstart, band), kcols]], axis=0)
                    vv = jnp.concatenate([vv, v_ref[0, pl.ds(kstart, band), vcols]], axis=0)
                keys[(sub, j)] = (kk, vv)
            return keys[(sub, j)]

        def scores(chain):
            sub, j, half = chain
            hk = 2 * j + half
            r0 = sub * B_BLOCK
            qs = jnp.concatenate(
                [q_ref[0, r0:r0 + B_BLOCK, (hk * B_GROUP + g) * LANE:(hk * B_GROUP + g + 1) * LANE]
                 for g in range(B_GROUP)], axis=0)
            return _dot_nt(qs, block_keys(sub, j)[0])

        outs = []
        s_next = scores(chains[0])
        for n, (sub, j, half) in enumerate(chains):
            s = s_next
            if n + 1 < len(chains):
                s_next = scores(chains[n + 1])
            sk = sink_ref[2 * j + half]
            if with_band:
                mask = block_window(sub)[0]
                s_b = jnp.concatenate(
                    [jnp.where(mask, s[g * B_BLOCK:(g + 1) * B_BLOCK, CTX:], NEG_INF) for g in range(B_GROUP)],
                    axis=0)
                s = jnp.concatenate([s[:, :CTX], s_b], axis=1)
            m = jnp.maximum(jnp.broadcast_to(jnp.max(s, axis=-1, keepdims=True), sk.shape), sk)
            p = jnp.concatenate([jnp.exp2(s[:, c:c + LANE] - m) for c in range(0, s.shape[1], LANE)], axis=1)
            pv = _dot(p.astype(BF16), block_keys(sub, j)[1])
            l = pv[:, LANE:] + jnp.exp2(sk - m)
            outs.append(pv[:, :LANE] / l)
            if half == 1:
                comb = jnp.where(lo, outs[-2], outs[-1])
                r0 = sub * B_BLOCK
                for g in range(B_GROUP):
                    s_out = j * B_GROUP + g
                    o_ref[r0:r0 + B_BLOCK, s_out * LANE:(s_out + 1) * LANE] = (
                        comb[g * B_BLOCK:(g + 1) * B_BLOCK].astype(BF16))

    is_ctx = i < CTX // TM
    pl.when(is_ctx)(lambda: run(False))
    pl.when(jnp.logical_not(is_ctx))(lambda: run(True))
    xo_ref[0] = _finish(o_ref[...], gate_ref, wout_ref, x_ref[0], mod_ref, gpost_ref)


def _swa_attn(q, k, v, sink_cols, gate, wout, xs, modl, gpost):
    bsz, t, qw2 = q.shape
    kw = k.shape[-1]
    qw = qw2 // 2
    return pl.pallas_call(
        _swa_attn_kernel,
        grid=(bsz, t // TM),
        in_specs=[pl.BlockSpec((1, TM, qw2), lambda b, i: (b, i, 0)),
                  pl.BlockSpec((1, t, kw), lambda b, i: (b, 0, 0)),
                  pl.BlockSpec((1, t, 2 * kw), lambda b, i: (b, 0, 0)),
                  pl.BlockSpec(sink_cols.shape, lambda b, i: (0, 0, 0)),
                  _tok_spec(qw), _const_spec(wout.shape), _tok_spec(D), _mod_spec(), _const_spec((1, D))],
        out_specs=_tok_spec(D),
        out_shape=jax.ShapeDtypeStruct(xs.shape, F32),
        scratch_shapes=[pltpu.VMEM((TM, qw), BF16)],
        input_output_aliases={6: 0},
        compiler_params=_cparams(2),
        name="swa_attention_out",
    )(q, k, v, sink_cols, gate, wout, xs, modl, gpost)


RADIX = 4
QP = TM // RADIX
TILE_ORDER = (3, 1, 2, 0)


def _hyena_proj_kernel(xp_ref, x_ref, xn_ref, mod_ref, gpre_ref, win_ref, cw_ref, cb_ref, perm_ref, *out_refs):
    u_refs, g_refs = out_refs[:RADIX], out_refs[RADIX:]
    t = pl.program_id(1)
    nt = pl.num_programs(1)
    hh = _dot(perm_ref[...], _modnorm(x_ref[0], mod_ref, gpre_ref).astype(BF16))
    h = jnp.concatenate([_modnorm(xp_ref[0], mod_ref, gpre_ref), hh, _modnorm(xn_ref[0], mod_ref, gpre_ref)], axis=0)
    p = _dot(h.astype(BF16), win_ref[...])
    cwid = 3 * C_WIDTH
    u = p[:, :cwid]
    start = {rho: HALO + k * QP for k, rho in enumerate(TILE_ORDER)}
    cls = {rho: u[start[rho]:start[rho] + QP] for rho in range(RADIX)}
    for rho in range(RADIX):
        g_refs[rho][0] = p[start[rho]:start[rho] + QP, cwid:].astype(BF16)
    before0 = pltpu.roll(u[0:HALO + QP], 1, 0)[HALO:]
    after3 = pltpu.roll(u[start[0]:], QP + HALO - 1, 0)[0:QP]
    r = lax.broadcasted_iota(jnp.int32, (QP, 1), 0)
    before0 = jnp.where(jnp.logical_and(r == 0, t <= 1), 0.0, before0)
    after3 = jnp.where(jnp.logical_and(r == QP - 1, jnp.logical_or(t == 0, t == nt - 1)), 0.0, after3)
    cw = cw_ref[...]
    cb = cb_ref[...]
    left = {0: before0, 1: cls[0], 2: cls[1], 3: cls[2]}
    right = {0: cls[1], 1: cls[2], 2: cls[3], 3: after3}
    for rho in range(RADIX):
        u_refs[rho][0] = (cb + left[rho] * cw[0:1] + cls[rho] * cw[1:2] + right[rho] * cw[2:3]).astype(BF16)


def _class_spec(width):
    return pl.BlockSpec((1, QP, width), lambda b, t: (b, t, 0))


def _hyena_proj(xs, modl, gpre, win, conv_w, conv_b):
    bsz, t, _ = xs.shape
    nt = t // TM
    per = TM // HALO
    last = t // HALO - 1
    part = lambda w: jax.ShapeDtypeStruct((bsz, t // RADIX, w), BF16)
    return pl.pallas_call(
        _hyena_proj_kernel,
        grid=(bsz, nt),
        in_specs=[pl.BlockSpec((1, HALO, D), lambda b, i: (b, jnp.maximum(i * per - 1, 0), 0)),
                  _tok_spec(D),
                  pl.BlockSpec((1, HALO, D), lambda b, i: (b, jnp.minimum((i + 1) * per, last), 0)),
                  _mod_spec(), _const_spec((1, D)), _const_spec(win.shape),
                  _const_spec(conv_w.shape), _const_spec(conv_b.shape), _const_spec((TM, TM))],
        out_specs=[_class_spec(3 * C_WIDTH)] * RADIX + [_class_spec(C_WIDTH)] * RADIX,
        out_shape=[part(3 * C_WIDTH)] * RADIX + [part(C_WIDTH)] * RADIX,
        compiler_params=_cparams(2),
        name="hyena_proj",
    )(xs, xs, xs, modl, gpre, win, conv_w, conv_b, _class_perm(TILE_ORDER))


def _class_perm(order):
    r = np.arange(QP)
    src = np.concatenate([RADIX * r + rho for rho in order])
    m = np.zeros((TM, TM), np.float32)
    m[np.arange(TM), src] = 1.0
    return jnp.asarray(m, BF16)


def _hyena_out_kernel(*refs):
    o_refs, g_refs = refs[:RADIX], refs[RADIX:2 * RADIX]
    wout_ref, x_ref, mod_ref, gpost_ref, perm_ref, xo_ref = refs[2 * RADIX:]
    a = jnp.concatenate([o_refs[rho][0].astype(F32) * _silu(g_refs[rho][0].astype(F32)) for rho in range(RADIX)],
                        axis=0)
    a = _dot(perm_ref[...], a.astype(BF16)).astype(BF16)
    y = _dot(a, wout_ref[...])
    xo_ref[0] = x_ref[0] + mod_ref[0, 0][2:3] * (_rms(y) * gpost_ref[...])


def _hyena_out_proj(os, gs, wout, xs, modl, gpost):
    bsz, t, _ = xs.shape
    w = os[0].shape[-1]
    return pl.pallas_call(
        _hyena_out_kernel,
        grid=(bsz, t // TM),
        in_specs=[_class_spec(w)] * (2 * RADIX) + [_const_spec(wout.shape), _tok_spec(D), _mod_spec(),
                                                   _const_spec((1, D)), _const_spec((TM, TM))],
        out_specs=_tok_spec(D),
        out_shape=jax.ShapeDtypeStruct(xs.shape, F32),
        input_output_aliases={2 * RADIX + 1: 0},
        compiler_params=_cparams(2),
        name="hyena_out_proj_residual",
    )(*os, *gs, wout, xs, modl, gpost, _class_perm(range(RADIX)).T)


def _filter_kernel(z_ref, w1_ref, b1_ref, fr_ref, w2_ref, b2_ref, w3_ref, dl_ref, o_ref):
    hp = lax.Precision.HIGHEST
    z = z_ref[...]
    fr = fr_ref[...]
    h = jnp.sin(fr * (jnp.dot(z, w1_ref[...], precision=hp, preferred_element_type=F32) + b1_ref[...]))
    h = jnp.sin(fr * (jnp.dot(h, w2_ref[...], precision=hp, preferred_element_type=F32) + b2_ref[...]))
    h = jnp.dot(h, w3_ref[...], precision=hp, preferred_element_type=F32)
    o_ref[...] = (h * jnp.exp(-z[:, 0:1] * dl_ref[...])).astype(o_ref.dtype)


SPEC_BLOCK = 512


def _filter_columns():
    o, cb, d, cc = np.meshgrid(np.arange(2), np.arange(C_WIDTH // SPEC_BLOCK), np.arange(2), np.arange(SPEC_BLOCK),
                               indexing="ij")
    return (d * 2 * C_WIDTH + o * C_WIDTH + cb * SPEC_BLOCK + cc).reshape(-1)


def _filters(n, w1, b1, fr, w2, b2, w3):
    t = np.linspace(0.0, 1.0, n, dtype=np.float32)[:, None]
    w = ((2.0 * math.pi / n) * np.arange(n, dtype=np.float32))[:, None].astype(np.float32)
    bands = np.linspace(1e-4, C_BANDS - 1, C_BANDS, dtype=np.float32)[None, :]
    z = np.zeros((n, LANE), np.float32)
    z[:, 0:1] = t
    z[:, 1:1 + C_BANDS] = np.cos(bands * w)
    z[:, 1 + C_BANDS:C_EMB] = -np.sin(bands * w)
    deltas = np.abs(np.linspace(C_MIN_DECAY, C_MAX_DECAY, C_WIDTH, dtype=np.float32))
    cols = _filter_columns()
    dl = deltas[cols % C_WIDTH][None, :]
    w3 = w3[:, cols]

    def pad(a, r, c):
        return jnp.zeros((r, c), F32).at[:a.shape[0], :a.shape[1]].set(a)

    tn = min(n, TM)
    nout = 4 * C_WIDTH
    cs = lambda shape: pl.BlockSpec(shape, lambda i: (0, 0))
    return pl.pallas_call(
        _filter_kernel,
        grid=(n // tn,),
        in_specs=[pl.BlockSpec((tn, LANE), lambda i: (i, 0)), cs((LANE, LANE)), cs((1, LANE)), cs((1, LANE)),
                  cs((LANE, LANE)), cs((1, LANE)), cs((LANE, nout)), cs((1, nout))],
        out_specs=pl.BlockSpec((tn, nout), lambda i: (i, 0)),
        out_shape=jax.ShapeDtypeStruct((n, nout), BF16),
        compiler_params=_cparams(1),
        name="hyena_filter_mlp",
    )(jnp.asarray(z), pad(w1, LANE, LANE), pad(b1[None], 1, LANE), pad(fr[None], 1, LANE),
      pad(w2, LANE, LANE), pad(b2[None], 1, LANE), pad(w3, LANE, nout), jnp.asarray(dl))


def _dft_matrix(n):
    f = np.arange(n, dtype=np.int64)[:, None]
    s = np.arange(n, dtype=np.int64)[None, :]
    ang = (2.0 * np.pi / (2 * n)) * ((f * s) % (2 * n)).astype(np.float64)
    cos = np.cos(ang)
    sin = np.sin(ang)
    sin[0, :] = np.where(np.arange(n) % 2 == 0, 1.0, -1.0)
    return cos.astype(np.float32), sin.astype(np.float32)


def _spectrum_kernel(a_ref, b_ref, re_ref, im_ref, *, scale):
    acc = _dot(a_ref[...], b_ref[...])
    fwd, bwd = acc[:, :SPEC_BLOCK], acc[:, SPEC_BLOCK:]
    re_ref[...] = (fwd + bwd) * scale
    im_ref[...] = (bwd - fwd) * scale


def _spectrum(a, b, bm, scale):
    m, k = a.shape
    nb = b.shape[1] // (2 * SPEC_BLOCK)
    assert m % bm == 0
    out = jax.ShapeDtypeStruct((m, nb * SPEC_BLOCK), F32)
    o_spec = pl.BlockSpec((bm, SPEC_BLOCK), lambda i, j: (i, j))
    return pl.pallas_call(
        functools.partial(_spectrum_kernel, scale=scale),
        grid=(m // bm, nb),
        in_specs=[pl.BlockSpec((bm, k), lambda i, j: (i, 0)), pl.BlockSpec((k, 2 * SPEC_BLOCK), lambda i, j: (0, j))],
        out_specs=[o_spec, o_spec],
        out_shape=[out, out],
        compiler_params=_cparams(2),
        name="filter_spectrum",
    )(a, b)


_PHASE = [(math.cos(math.pi * rho / 4), math.sin(math.pi * rho / 4),
           math.cos(3 * math.pi * rho / 4), math.sin(3 * math.pi * rho / 4)) for rho in range(RADIX)]


def _cmul(xr, xs, kr, ki):
    return xr * kr + xs * ki, xs * kr - xr * ki


def _hyena_conv_kernel(*refs, row0, q, aliased):
    nu = 3 * RADIX
    u_refs = refs[:nu]
    m_refs = refs[nu:nu + RADIX]
    mt_refs = refs[nu + RADIX:nu + 2 * RADIX]
    k_refs = refs[nu + 2 * RADIX:nu + 2 * RADIX + 8]
    ksr_ref, ksi_ref, fb_ref = refs[nu + 2 * RADIX + 8:nu + 2 * RADIX + 11]
    rest = refs[nu + 2 * RADIX + 11 + (RADIX if aliased else 0):]
    o_refs, z_refs = rest[:RADIX], rest[RADIX:]
    o = pl.program_id(2)
    rows = slice(row0, row0 + q)

    @pl.when(o == 0)
    def _():
        for rho in range(RADIX):
            z_refs[rho][...] = u_refs[3 * rho][0, rows, :]

    ksr, ksi = ksr_ref[...], ksi_ref[...]
    ksp = jnp.concatenate([ksr[0:3], ksi[5:6], ksr[3:4], ksi[6:7], ksr[4:5], ksi[7:8]], axis=0)
    first = lax.broadcasted_iota(jnp.int32, (8, 1), 0) == 0
    nblk = m_refs[0].shape[0]
    fbs = m_refs[0].shape[1] // 2

    def forward(j):
        return [_dot(m_refs[rho][j], z_refs[rho][...]) for rho in range(RADIX)]

    def spectral(j, fwd):
        c = [g[:fbs] for g in fwd]
        sn = [g[fbs:] for g in fwd]
        ar, as_, br, bs = c[0] + c[2], sn[0] + sn[2], c[1] + c[3], sn[1] + sn[3]
        cr, cs, dr, ds = c[0] - c[2], sn[0] - sn[2], c[1] - c[3], sn[1] - sn[3]
        k = [r[j * fbs:(j + 1) * fbs, :] for r in k_refs]
        y1r, y1s = _cmul(ar + br, as_ + bs, k[0], k[1])
        y2r, y2s = _cmul(cr - ds, cs + dr, k[2], k[3])
        y3r, y3s = _cmul(ar - br, bs - as_, k[4], k[5])
        y4r, y4s = _cmul(cr + ds, dr - cs, k[6], k[7])
        pr, ps, qr, qs = y1r + y3r, y1s - y3s, y1r - y3r, y1s + y3s
        rr, rs, tr, ts = y2r + y4r, y2s - y4s, y2r - y4r, y2s + y4s
        wr = [pr + rr, qr + ts, pr - rr, qr - ts]
        ws = [ps + rs, qs - tr, ps - rs, qs + tr]
        if j == 0:
            c0 = [x[0:8] for x in c]
            a0 = [x[0:8] for x in sn]
            y0 = (c0[0] + c0[1] + c0[2] + c0[3]) * ksp[0:1]
            yn = (c0[0] - c0[1] + c0[2] - c0[3]) * ksp[1:2]
            yhr, yhs = _cmul(c0[0] - c0[2], c0[1] - c0[3], ksp[2:3], ksp[3:4])
            xqr = sum(_PHASE[rho][0] * a0[rho] for rho in range(1, RADIX)) + a0[0]
            xqs = sum(_PHASE[rho][1] * a0[rho] for rho in range(1, RADIX))
            xgr = sum(_PHASE[rho][2] * a0[rho] for rho in range(1, RADIX)) + a0[0]
            xgs = sum(_PHASE[rho][3] * a0[rho] for rho in range(1, RADIX))
            yqr, yqs = _cmul(xqr, xqs, ksp[4:5], ksp[5:6])
            ygr, ygs = _cmul(xgr, xgs, ksp[6:7], ksp[7:8])
            turn = [yhr, yhs, -yhr, -yhs]
            for rho in range(RADIX):
                pc, psn, gc, gsn = _PHASE[rho]
                cos0 = y0 + (yn if rho % 2 == 0 else -yn) + turn[rho]
                sin0 = yqr * pc + yqs * psn + ygr * gc + ygs * gsn
                wr[rho] = jnp.concatenate([jnp.where(first, cos0, wr[rho][0:8]), wr[rho][8:]], axis=0)
                ws[rho] = jnp.concatenate([jnp.where(first, sin0, ws[rho][0:8]), ws[rho][8:]], axis=0)
        return [jnp.concatenate([wr[rho], ws[rho]], axis=0).astype(BF16) for rho in range(RADIX)]

    acc = [None] * RADIX
    fwd_next = forward(0)
    for j in range(nblk):
        fwd = fwd_next
        if j + 1 < nblk:
            fwd_next = forward(j + 1)
        w = spectral(j, fwd)
        for rho in range(RADIX):
            part = _dot(mt_refs[rho][j], w[rho])
            acc[rho] = part if acc[rho] is None else acc[rho] + part
    f_out = [acc[rho] + z_refs[rho][...].astype(F32) * fb_ref[0] for rho in range(RADIX)]

    @pl.when(o == 0)
    def _():
        for rho in range(RADIX):
            z_refs[rho][...] = (u_refs[3 * rho + 1][0, rows, :].astype(F32) * f_out[rho]).astype(BF16)

    @pl.when(o == 1)
    def _():
        for rho in range(RADIX):
            if row0 > 0:
                o_refs[rho][0, 0:row0, :] = jnp.zeros((row0, o_refs[rho].shape[2]), BF16)
            o_refs[rho][0, rows, :] = (u_refs[3 * rho + 2][0, rows, :].astype(F32) * f_out[rho]).astype(BF16)


def _hyena_conv(us, mats, spec_re, spec_im, fbias, *, row0, q, block_rows, tc, prev=None):
    bsz, t4, _ = us[0].shape
    nct = C_WIDTH // tc
    u_spec = lambda which: pl.BlockSpec((1, block_rows, tc), lambda b, c, o: (b, 0, which * nct + c))
    m_spec = pl.BlockSpec(mats[0].shape, lambda b, c, o: (0, 0, 0))
    mt_spec = pl.BlockSpec(mats[RADIX].shape, lambda b, c, o: (0, 0, 0))
    p_spec = lambda blk: pl.BlockSpec((q, tc), functools.partial(lambda g, b, c, o: (g, o * nct + c), blk))
    s_spec = pl.BlockSpec((8, tc), lambda b, c, o: (q, o * nct + c))
    in_specs = ([u_spec(w) for _ in range(RADIX) for w in range(3)] + [m_spec] * RADIX + [mt_spec] * RADIX
                + [p_spec(blk) for g in range(4) for blk in (g, 4 + g)] + [s_spec, s_spec]
                + [pl.BlockSpec((1, 1, tc), lambda b, c, o: (o, 0, c))])
    args = ([u for u in us for _ in range(3)] + list(mats) + [spec_re, spec_im] * 4 + [spec_re, spec_im, fbias])
    aliases = {}
    if prev is not None:
        in_specs += [pl.BlockSpec(memory_space=pl.ANY)] * RADIX
        aliases = {len(args) + rho: rho for rho in range(RADIX)}
        args += list(prev)
    o_spec = pl.BlockSpec((1, block_rows, tc), lambda b, c, o: (b, 0, c))
    return pl.pallas_call(
        functools.partial(_hyena_conv_kernel, row0=row0, q=q, aliased=prev is not None),
        grid=(bsz, nct, 2),
        in_specs=in_specs,
        out_specs=[o_spec] * RADIX,
        out_shape=[jax.ShapeDtypeStruct((bsz, t4, C_WIDTH), BF16)] * RADIX,
        scratch_shapes=[pltpu.VMEM((q, tc), BF16)] * RADIX,
        input_output_aliases=aliases,
        compiler_params=_cparams(3),
        name="hyena_long_conv_q%d" % q,
    )(*args)


def _radix4_matrices(n):
    q = n // RADIX
    fbs = min(q, LANE)
    nblk = q // fbs
    f = np.arange(q, dtype=np.int64)[:, None]
    r = np.arange(q, dtype=np.int64)[None, :]
    alt = np.where(np.arange(q) % 2 == 0, 1.0, -1.0)
    mats = []
    for rho in range(RADIX):
        ang = (2.0 * np.pi / (2 * n)) * ((f * (RADIX * r + rho)) % (2 * n)).astype(np.float64)
        cos, sin = np.cos(ang), np.sin(ang)
        sin[0, :] = alt
        mats.append(np.concatenate([cos.reshape(nblk, fbs, q), sin.reshape(nblk, fbs, q)], axis=1).astype(np.float32))
    tr = lambda m: np.ascontiguousarray(np.transpose(m, (0, 2, 1)))
    return tuple(jnp.asarray(m, BF16) for m in mats) + tuple(jnp.asarray(tr(m), BF16) for m in mats)


def _hyena_spectra(n, filt):
    h, q = n // 2, n // RADIX
    cos, sin = _dft_matrix(n)
    f = np.arange(q)
    groups = [f, h + f, np.maximum(n - f, 1) % n, h - f]
    special = [0.5 * cos[0:1], 0.5 * sin[0:1], cos[h:h + 1], cos[q:q + 1], cos[h + q:h + q + 1],
               sin[h:h + 1], sin[q:q + 1], sin[h + q:h + q + 1]]
    rows = np.concatenate([cos[g] for g in groups] + [sin[g] for g in groups] + special
                          + [np.zeros((LANE - 8, n), np.float32)], axis=0)
    nr = 8 * q + LANE
    return _spectrum(jnp.asarray(rows, BF16), filt, 3 * LANE if nr % (3 * LANE) == 0 else nr, 2.0 / (2 * n))


def _rope_tables(seq, layout):
    rows = seq // GRID_W
    row = np.repeat(np.arange(rows, dtype=np.float32), GRID_W)
    col = np.tile(np.arange(GRID_W, dtype=np.float32), rows)
    per_axis = 32
    inv = (ROPE_BASE ** (-np.arange(0, per_axis, 2, dtype=np.float32) / per_axis)).astype(np.float32)
    ang = np.concatenate([row[:, None] * inv, col[:, None] * inv], axis=-1)
    cos = np.concatenate([np.ones((CTX, 32), np.float32), np.cos(ang)], axis=0)
    sin = np.concatenate([np.zeros((CTX, 32), np.float32), np.sin(ang)], axis=0)
    one, zero = np.ones_like(cos), np.zeros_like(cos)
    if layout == "mla":
        a = [cos, cos, one, one]
        b = [zero, sin, zero, zero]
        c = [-sin, zero, zero, zero]
    else:
        a = [cos, cos, cos, cos]
        b = [zero, sin, zero, sin]
        c = [-sin, zero, -sin, zero]
    return tuple(jnp.asarray(np.concatenate(p, axis=1), F32) for p in (a, b, c))


def _swa_head_perm():
    cols = []
    for j in range(B_KV_HEADS // 2):
        for g in range(B_GROUP):
            for hk in (2 * j, 2 * j + 1):
                h = hk * B_GROUP + g
                cols.extend(range(h * B_HDIM, (h + 1) * B_HDIM))
    return np.asarray(cols, np.int32)


def _mla_weights(w_in, w_q, w_kv):
    c1 = A_Q_RANK + A_KV_RANK
    zpad = jnp.zeros((D, LANE - A_ROPE), F32)
    win = jnp.concatenate([w_in[:, :c1 + A_ROPE], zpad, w_in[:, c1 + A_ROPE:]], axis=1)
    qscale = (A_NOPE + A_ROPE) ** -0.5 * math.log2(math.e)
    wq = w_q.reshape(A_Q_RANK, A_HEADS, A_NOPE + A_ROPE) * qscale
    wq = jnp.concatenate([wq, jnp.zeros((A_Q_RANK, A_HEADS, A_HEAD_PAD - A_NOPE - A_ROPE), F32)], axis=-1)
    wq = wq.reshape(A_Q_RANK, A_HEADS * A_HEAD_PAD)
    return win.astype(BF16), wq.astype(BF16), w_kv.astype(BF16)


def kernel(x, c, ctx, c_ctx, w_mod, b_mod, g_pre, g_post, a_w_in, a_g_q, a_w_q, a_g_kv, a_w_kv, a_w_out, b_w_in, b_sink, b_w_out, c_w_in, c_conv_w, c_conv_b, c_f_w1, c_f_b1, c_f_freq, c_f_w2, c_f_b2, c_f_w3, c_filt_bias, c_w_out):
    bsz, seq, _ = x.shape
    assert ctx.shape[1] == CTX and (CTX + seq) % TP == 0 and seq % GRID_W == 0
    xsrc = (ctx, x)

    pad_rows = (-(bsz + 1)) % 8
    cond = jnp.concatenate([c, c_ctx[None], jnp.zeros((pad_rows, D), F32)], axis=0)
    mod = _modulation(cond, w_mod, b_mod)

    tabs_mla = _rope_tables(seq, "mla")
    tabs_swa = _rope_tables(seq, "swa")

    for layer in range(DEPTH):
        kind, j = layer % 3, layer // 3
        mx = mod[layer, :bsz].reshape(bsz, 3, D)
        mc = jnp.broadcast_to(mod[layer, bsz].reshape(1, 3, D), (bsz, 3, D))
        modl = jnp.stack([mc, mx], axis=0)
        gpre = g_pre[layer][None]
        gpost = g_post[layer][None]
        if kind == 0:
            win, wq, wkv = _mla_weights(a_w_in[j], a_w_q[j], a_w_kv[j])
            q, k, v, gate = _mla_proj(xsrc, modl, gpre, win, a_g_q[j][None], wq, a_g_kv[j][None], wkv, tabs_mla)
            xs = _mla_attn(q, k, v, gate, a_w_out[j].astype(BF16), xsrc, modl, gpost,
                           latents_only=layer == DEPTH - 1)
        elif kind == 1:
            xs, = xsrc
            perm = _swa_head_perm()
            qw, kw = B_HEADS * B_HDIM, B_KV_HEADS * B_HDIM
            w = b_w_in[j]
            win = jnp.concatenate([w[:, :qw][:, perm] * (B_HDIM ** -0.5 * math.log2(math.e)), w[:, qw:qw + 2 * kw],
                                   w[:, qw + 2 * kw:][:, perm]], axis=1).astype(BF16)
            q, k, v, gate = _swa_proj(xs, modl, gpre, win, tabs_swa)
            sink = (b_sink[j].astype(F32) * math.log2(math.e)).reshape(B_KV_HEADS, B_GROUP, 1, 1)
            sink_cols = jnp.broadcast_to(sink, (B_KV_HEADS, B_GROUP, B_BLOCK, LANE)).reshape(B_KV_HEADS, B_GROUP * B_BLOCK, LANE)
            xs = _swa_attn(q, k, v, sink_cols, gate, b_w_out[j][perm, :].astype(BF16), xs, modl, gpost)
        else:
            xs, = xsrc
            parts = _hyena_proj(xs, modl, gpre, c_w_in[j].astype(BF16), c_conv_w[j], c_conv_b[j][None])
            us, gs = parts[:RADIX], parts[RADIX:]
            fargs = (c_f_w1[j], c_f_b1[j], c_f_freq[j], c_f_w2[j], c_f_b2[j], c_f_w3[j])
            fbias = c_filt_bias[j].reshape(2, 1, C_WIDTH)
            os = _hyena_conv(us, _radix4_matrices(seq), *_hyena_spectra(seq, _filters(seq, *fargs)), fbias,
                             row0=CTX // RADIX, q=seq // RADIX, block_rows=(CTX + seq) // RADIX, tc=MXU_W)
            os = _hyena_conv(us, _radix4_matrices(CTX), *_hyena_spectra(CTX, _filters(CTX, *fargs)), fbias,
                             row0=0, q=CTX // RADIX, block_rows=CTX // RADIX, tc=C_WIDTH, prev=os)
            xs = _hyena_out_proj(os, gs, c_w_out[j].astype(BF16), xs, modl, gpost)
        xsrc = (xs,)
    return xs
```

```python
import functools
import math

import numpy as np
import jax
import jax.numpy as jnp
from jax import lax
from jax.experimental import pallas as pl
from jax.experimental.pallas import tpu as pltpu

F32 = jnp.float32
BF16 = jnp.bfloat16

D = 1024
DEPTH = 4
GRID_W = 64
CTX = 256
NORM_EPS = 1e-6
ROPE_BASE = 10000.0
NEG_INF = -1e30

A_HEADS = 8
A_Q_RANK = 512
A_KV_RANK = 256
A_NOPE = 128
A_ROPE = 64
A_VDIM = 128
A_HEAD_PAD = 256

B_HEADS = 16
B_KV_HEADS = 4
B_GROUP = 4
B_HDIM = 64
B_WINDOW = 128
B_BLOCK = 128

C_WIDTH = 1024
C_BANDS = 16
C_EMB = 1 + 2 * C_BANDS
C_FFN = 64
C_MIN_DECAY = math.log(1e-2) / 1.5
C_MAX_DECAY = math.log(1e-2) / 0.3

LANE = 128
MXU_W = 256
TM = 256
TP = 3 * TM
HALO = 8
VMEM_LIMIT = 56 * 1024 * 1024


def _cparams(n_axes):
    return pltpu.CompilerParams(dimension_semantics=("arbitrary",) * n_axes,
                                vmem_limit_bytes=VMEM_LIMIT)


def _rms(x):
    return x * lax.rsqrt(jnp.mean(x * x, axis=-1, keepdims=True) + NORM_EPS)


def _silu(g):
    return g / (1.0 + jnp.exp(-g))


def _dot(a, b):
    return jnp.dot(a, b, preferred_element_type=F32)


def _dot_nt(a, b):
    return lax.dot_general(a, b, (((1,), (1,)), ((), ())), preferred_element_type=F32)


def _rope128(x, a, b, c):
    return x * a + pltpu.roll(x, 32, 1) * b + pltpu.roll(x, LANE - 32, 1) * c


def _modnorm(x, mod_ref, gpre_ref):
    m = mod_ref[0, 0]
    return _rms(x) * gpre_ref[...] * (1.0 + m[1:2]) + m[0:1]


def _mod_kernel(c_ref, w_ref, b_ref, o_ref):
    a = _silu(c_ref[...])
    o_ref[0] = _dot(a.astype(BF16), w_ref[0].astype(BF16)) + b_ref[0]


def _modulation(cond, w_mod, b_mod):
    rows = cond.shape[0]
    return pl.pallas_call(
        _mod_kernel,
        grid=(DEPTH, 3),
        in_specs=[pl.BlockSpec((rows, D), lambda l, j: (0, 0)),
                  pl.BlockSpec((1, D, D), lambda l, j: (l, 0, j)),
                  pl.BlockSpec((1, 1, D), lambda l, j: (l, 0, j))],
        out_specs=pl.BlockSpec((1, rows, D), lambda l, j: (l, 0, j)),
        out_shape=jax.ShapeDtypeStruct((DEPTH, rows, 3 * D), F32),
        compiler_params=_cparams(2),
        name="adaln_modulation",
    )(cond, w_mod, b_mod.reshape(DEPTH, 1, 3 * D))


def _tok_spec(width, rows=TM, skip=0):
    return pl.BlockSpec((1, rows, width), lambda b, t: (b, t + skip, 0))


def _mod_spec(ctx_tiles=1, skip=0):
    return pl.BlockSpec((1, 1, 3, D), lambda b, t: (jnp.minimum((t + skip) // ctx_tiles, 1), b, 0, 0))


def _const_spec(shape):
    nd = len(shape)
    return pl.BlockSpec(shape, lambda b, t: (0,) * nd)


def _residual_specs(dual, rows=TM, skip=0):
    if not dual:
        return [_tok_spec(D, rows, skip)]
    per = CTX // rows
    return [pl.BlockSpec((1, rows, D), lambda b, t: (b, jnp.minimum(t, per - 1), 0)),
            pl.BlockSpec((1, rows, D), lambda b, t: (b, jnp.maximum(t - per, 0), 0))]


def _wide_specs(dual):
    n = TP // TM
    if not dual:
        return [pl.BlockSpec((1, TM, D), functools.partial(lambda j, b, t: (b, n * t + j, 0), j)) for j in range(n)]
    last = lambda j, b, t: (b, jnp.maximum(n * t + j - CTX // TM, 0), 0)
    return ([pl.BlockSpec((1, CTX, D), lambda b, t: (b, 0, 0))]
            + [pl.BlockSpec((1, TM, D), functools.partial(last, j)) for j in range(n)])


def _wide_modnorm(refs, dual, mod_ref, gpre_ref):
    t = pl.program_id(1)
    blocks = [r[0] for r in refs[1:]] if dual else [r[0] for r in refs]
    if dual:
        blocks[0] = jnp.where(t == 0, refs[0][0], blocks[0])
    x = jnp.concatenate(blocks, axis=0)
    is_ctx = jnp.logical_and(lax.broadcasted_iota(jnp.int32, (TP, 1), 0) < CTX, t == 0)
    mc, mx = mod_ref[0, 0], mod_ref[1, 0]
    scale = jnp.where(is_ctx, mc[1:2], mx[1:2])
    shift = jnp.where(is_ctx, mc[0:1], mx[0:1])
    return _rms(x) * gpre_ref[...] * (1.0 + scale) + shift


def _both_mod_spec():
    return pl.BlockSpec((2, 1, 3, D), lambda b, t: (0, b, 0, 0))


def _resident(shape):
    nd = len(shape)
    return pl.BlockSpec(shape, lambda b, t: (0,) * nd, pipeline_mode=pl.Buffered(1))


def _residual_tile(refs, dual, rows=TM):
    if not dual:
        return refs[0][0]
    return jnp.where(pl.program_id(1) < CTX // rows, refs[0][0], refs[1][0])


def _finish(o, gate_ref, wout_ref, x, mod_ref, gpost_ref):
    a = o.astype(F32) * _silu(gate_ref[0].astype(F32))
    y = _dot(a.astype(BF16), wout_ref[...])
    return x + mod_ref[0, 0][2:3] * (_rms(y) * gpost_ref[...])


def _mla_proj_kernel(*refs, dual):
    nx = TP // TM + (1 if dual else 0)
    (mod_ref, gpre_ref, win_ref, gq_ref, wq_ref, gkv_ref, wkv_ref,
     ra_ref, rb_ref, rc_ref, q_ref, k_ref, v_ref, gate_ref) = refs[nx:]
    h = _wide_modnorm(refs[:nx], dual, mod_ref, gpre_ref)
    p = _dot(h.astype(BF16), win_ref[...])
    c0, c1, c2 = A_Q_RANK, A_Q_RANK + A_KV_RANK, A_Q_RANK + A_KV_RANK + LANE
    gate_ref[0] = p[:, c2:].astype(BF16)
    qn = (_rms(p[:, :c0]) * gq_ref[...]).astype(BF16)
    kvn = (_rms(p[:, c0:c1]) * gkv_ref[...]).astype(BF16)
    ra, rb, rc = ra_ref[...], rb_ref[...], rc_ref[...]
    kr = _rope128(p[:, c1:c2], ra, rb, rc).astype(BF16)
    qa = _dot(qn, wq_ref[...])
    lo = lax.broadcasted_iota(jnp.int32, (1, LANE), 1) < A_ROPE
    nope_w = A_HEADS * A_NOPE
    for pr in range(A_HEADS // 2):
        r = _rope128(qa[:, nope_w + pr * LANE:nope_w + (pr + 1) * LANE], ra, rb, rc)
        q_ref[0, 2 * pr, :, LANE:2 * LANE] = jnp.where(lo, r, 0.0).astype(BF16)
        q_ref[0, 2 * pr + 1, :, LANE:2 * LANE] = jnp.where(lo, pltpu.roll(r, LANE - A_ROPE, 1), 0.0).astype(BF16)
    for hd in range(A_HEADS):
        q_ref[0, hd, :, 0:LANE] = qa[:, hd * A_NOPE:(hd + 1) * A_NOPE].astype(BF16)
        kv = _dot(kvn, wkv_ref[:, hd * A_HEAD_PAD:(hd + 1) * A_HEAD_PAD])
        k_ref[0, hd, :, 0:LANE] = kv[:, 0:A_NOPE].astype(BF16)
        k_ref[0, hd, :, LANE:2 * LANE] = kr
        v_ref[0, hd] = kv[:, A_NOPE:].astype(BF16)


def _mla_proj(xsrc, modl, gpre, win, gq, wq, gkv, wkv, tabs):
    dual = len(xsrc) == 2
    bsz = xsrc[0].shape[0]
    t = tabs[0].shape[0]
    xargs = [xsrc[0]] + [xsrc[1]] * (TP // TM) if dual else [xsrc[0]] * (TP // TM)
    head = lambda w: pl.BlockSpec((1, A_HEADS, TP, w), lambda b, i: (b, 0, i, 0))
    tab = pl.BlockSpec((TP, LANE), lambda b, i: (i, 0))
    return pl.pallas_call(
        functools.partial(_mla_proj_kernel, dual=dual),
        grid=(bsz, t // TP),
        in_specs=[*_wide_specs(dual), _both_mod_spec(), _const_spec((1, D)), _resident(win.shape),
                  _const_spec((1, A_Q_RANK)), _resident(wq.shape),
                  _const_spec((1, A_KV_RANK)), _resident(wkv.shape), tab, tab, tab],
        out_specs=[head(A_HEAD_PAD), head(A_HEAD_PAD), head(A_VDIM), _tok_spec(A_HEADS * A_VDIM, TP)],
        out_shape=[jax.ShapeDtypeStruct((bsz, A_HEADS, t, A_HEAD_PAD), BF16),
                   jax.ShapeDtypeStruct((bsz, A_HEADS, t, A_HEAD_PAD), BF16),
                   jax.ShapeDtypeStruct((bsz, A_HEADS, t, A_VDIM), BF16),
                   jax.ShapeDtypeStruct((bsz, t, A_HEADS * A_VDIM), BF16)],
        compiler_params=_cparams(2),
        name="mla_proj",
    )(*xargs, modl, gpre, win, gq, wq, gkv, wkv, *tabs)


def _mla_attn_kernel(*refs, dual, latents_only):
    nx = 2 if dual else 1
    q_ref, k_ref, v_ref, gate_ref, wout_ref = refs[:5]
    mod_ref, gpost_ref, xo_ref, o_ref = refs[5 + nx:]

    def attend(nk):
        s_next = _dot_nt(q_ref[0, 0], k_ref[0, 0, :nk, :])
        for hd in range(A_HEADS):
            s = s_next
            if hd + 1 < A_HEADS:
                s_next = _dot_nt(q_ref[0, hd + 1], k_ref[0, hd + 1, :nk, :])
            m = jnp.broadcast_to(jnp.max(s, axis=-1, keepdims=True), (TM, LANE))
            p = jnp.concatenate([jnp.exp2(s[:, c:c + LANE] - m) for c in range(0, nk, LANE)], axis=1)
            vx = jnp.concatenate([v_ref[0, hd, :nk, :], jnp.ones((nk, LANE), BF16)], axis=1)
            pv = _dot(p.astype(BF16), vx)
            o_ref[:, hd * A_VDIM:(hd + 1) * A_VDIM] = (pv[:, :A_VDIM] / pv[:, A_VDIM:]).astype(BF16)

    t_all = k_ref.shape[2]
    if latents_only:
        attend(t_all)
    else:
        is_ctx = pl.program_id(1) == 0
        pl.when(is_ctx)(lambda: attend(CTX))
        pl.when(jnp.logical_not(is_ctx))(lambda: attend(t_all))
    x = _residual_tile(refs[5:5 + nx], dual)
    xo_ref[0] = _finish(o_ref[...], gate_ref, wout_ref, x, mod_ref, gpost_ref)


def _mla_attn(q, k, v, gate, wout, xsrc, modl, gpost, latents_only):
    dual = len(xsrc) == 2
    bsz, nh, t, _ = q.shape
    skip = CTX // TM if latents_only else 0
    nt = t // TM - skip
    return pl.pallas_call(
        functools.partial(_mla_attn_kernel, dual=dual, latents_only=latents_only),
        grid=(bsz, nt),
        in_specs=[pl.BlockSpec((1, nh, TM, A_HEAD_PAD), lambda b, i: (b, 0, i + skip, 0)),
                  pl.BlockSpec((1, nh, t, A_HEAD_PAD), lambda b, i: (b, 0, 0, 0)),
                  pl.BlockSpec((1, nh, t, A_VDIM), lambda b, i: (b, 0, 0, 0)),
                  _tok_spec(nh * A_VDIM, skip=skip), _const_spec(wout.shape),
                  *_residual_specs(dual, skip=skip), _mod_spec(skip=skip), _const_spec((1, D))],
        out_specs=_tok_spec(D),
        out_shape=jax.ShapeDtypeStruct((bsz, nt * TM, D), F32),
        scratch_shapes=[pltpu.VMEM((TM, nh * A_VDIM), BF16)],
        input_output_aliases={} if (dual or latents_only) else {5: 0},
        compiler_params=_cparams(2),
        name="mla_attention_out",
    )(q, k, v, gate, wout, *xsrc, modl, gpost)


def _swa_proj_kernel(*refs):
    xrefs = refs[:TP // TM]
    mod_ref, gpre_ref, win_ref, ra_ref, rb_ref, rc_ref, q_ref, k_ref, v_ref, gate_ref = refs[TP // TM:]
    h = _wide_modnorm(xrefs, False, mod_ref, gpre_ref)
    p = _dot(h.astype(BF16), win_ref[...])
    ra, rb, rc = ra_ref[...], rb_ref[...], rc_ref[...]
    qw, kw = B_HEADS * B_HDIM, B_KV_HEADS * B_HDIM
    lane = lax.broadcasted_iota(jnp.int32, (1, LANE), 1)
    lo = lane < B_HDIM
    ones_blk = jnp.ones((TP, LANE), BF16)
    for j in range(B_KV_HEADS // 2):
        for g in range(B_GROUP):
            s = j * B_GROUP + g
            r = _rope128(p[:, s * LANE:(s + 1) * LANE], ra, rb, rc)
            for half in range(2):
                d = ((2 * j + half) * B_GROUP + g) * LANE
                keep = lo if half == 0 else jnp.logical_not(lo)
                q_ref[0, :, d:d + LANE] = jnp.where(keep, r, 0.0).astype(BF16)
        o = qw + j * LANE
        k_ref[0, :, j * LANE:(j + 1) * LANE] = _rope128(p[:, o:o + LANE], ra, rb, rc).astype(BF16)
        v_ref[0, :, 2 * j * LANE:(2 * j + 1) * LANE] = p[:, o + kw:o + kw + LANE].astype(BF16)
        v_ref[0, :, (2 * j + 1) * LANE:(2 * j + 2) * LANE] = ones_blk
    gate_ref[0] = p[:, qw + 2 * kw:].astype(BF16)


def _swa_proj(xs, modl, gpre, win, tabs):
    bsz, t, _ = xs.shape
    qw, kw = B_HEADS * B_HDIM, B_KV_HEADS * B_HDIM
    tab = pl.BlockSpec((TP, LANE), lambda b, i: (i, 0))
    return pl.pallas_call(
        _swa_proj_kernel,
        grid=(bsz, t // TP),
        in_specs=[*_wide_specs(False), _both_mod_spec(), _const_spec((1, D)), _resident(win.shape), tab, tab, tab],
        out_specs=[_tok_spec(2 * qw, TP), _tok_spec(kw, TP), _tok_spec(2 * kw, TP), _tok_spec(qw, TP)],
        out_shape=[jax.ShapeDtypeStruct((bsz, t, 2 * qw), BF16),
                   jax.ShapeDtypeStruct((bsz, t, kw), BF16),
                   jax.ShapeDtypeStruct((bsz, t, 2 * kw), BF16),
                   jax.ShapeDtypeStruct((bsz, t, qw), BF16)],
        compiler_params=_cparams(2),
        name="swa_proj",
    )(*[xs] * (TP // TM), modl, gpre, win, *tabs)


def _swa_attn_kernel(q_ref, k_ref, v_ref, sink_ref, gate_ref, wout_ref, x_ref, mod_ref, gpost_ref, xo_ref, o_ref):
    i = pl.program_id(1)
    n_lat = k_ref.shape[1] - CTX
    band = 3 * B_BLOCK
    lo = lax.broadcasted_iota(jnp.int32, (1, LANE), 1) < B_HDIM

    def run(with_band):
        chains = [(sub, j, half) for sub in range(TM // B_BLOCK) for j in range(B_KV_HEADS // 2) for half in range(2)]
        window, keys = {}, {}

        def block_window(sub):
            if sub not in window:
                li = (i - CTX // TM) * (TM // B_BLOCK) + sub
                start = jnp.clip((li - 1) * B_BLOCK, 0, n_lat - band)
                off = li * B_BLOCK - start
                d0 = (lax.broadcasted_iota(jnp.int32, (B_BLOCK, band), 0)
                      - lax.broadcasted_iota(jnp.int32, (B_BLOCK, band), 1))
                mask = jnp.abs(d0 + off) <= B_WINDOW
                window[sub] = (mask, pl.multiple_of(CTX + start, B_BLOCK))
            return window[sub]

        def block_keys(sub, j):
            if (sub, j) not in keys:
                kcols = slice(j * LANE, (j + 1) * LANE)
                vcols = slice(2 * j * LANE, (2 * j + 2) * LANE)
                kk = k_ref[0, 0:CTX, kcols]
                vv = v_ref[0, 0:CTX, vcols]
                if with_band:
                    kstart = block_window(sub)[1]
                    kk = jnp.concatenate([kk, k_ref[0, pl.ds(kstart, band), kcols]], axis=0)
                    vv = jnp.concatenate([vv, v_ref[0, pl.ds(kstart, band), vcols]], axis=0)
                keys[(sub, j)] = (kk, vv)
            return keys[(sub, j)]

        def scores(chain):
            sub, j, half = chain
            hk = 2 * j + half
            r0 = sub * B_BLOCK
            qs = jnp.concatenate(
                [q_ref[0, r0:r0 + B_BLOCK, (hk * B_GROUP + g) * LANE:(hk * B_GROUP + g + 1) * LANE]
                 for g in range(B_GROUP)], axis=0)
            return _dot_nt(qs, block_keys(sub, j)[0])

        outs = []
        s_next = scores(chains[0])
        for n, (sub, j, half) in enumerate(chains):
            s = s_next
            if n + 1 < len(chains):
                s_next = scores(chains[n + 1])
            sk = sink_ref[2 * j + half]
            if with_band:
                mask = block_window(sub)[0]
                s_b = jnp.concatenate(
                    [jnp.where(mask, s[g * B_BLOCK:(g + 1) * B_BLOCK, CTX:], NEG_INF) for g in range(B_GROUP)],
                    axis=0)
                s = jnp.concatenate([s[:, :CTX], s_b], axis=1)
            m = jnp.maximum(jnp.broadcast_to(jnp.max(s, axis=-1, keepdims=True), sk.shape), sk)
            p = jnp.concatenate([jnp.exp2(s[:, c:c + LANE] - m) for c in range(0, s.shape[1], LANE)], axis=1)
            pv = _dot(p.astype(BF16), block_keys(sub, j)[1])
            l = pv[:, LANE:] + jnp.exp2(sk - m)
            outs.append(pv[:, :LANE] / l)
            if half == 1:
                comb = jnp.where(lo, outs[-2], outs[-1])
                r0 = sub * B_BLOCK
                for g in range(B_GROUP):
                    s_out = j * B_GROUP + g
                    o_ref[r0:r0 + B_BLOCK, s_out * LANE:(s_out + 1) * LANE] = (
                        comb[g * B_BLOCK:(g + 1) * B_BLOCK].astype(BF16))

    is_ctx = i < CTX // TM
    pl.when(is_ctx)(lambda: run(False))
    pl.when(jnp.logical_not(is_ctx))(lambda: run(True))
    xo_ref[0] = _finish(o_ref[...], gate_ref, wout_ref, x_ref[0], mod_ref, gpost_ref)


def _swa_attn(q, k, v, sink_cols, gate, wout, xs, modl, gpost):
    bsz, t, qw2 = q.shape
    kw = k.shape[-1]
    qw = qw2 // 2
    return pl.pallas_call(
        _swa_attn_kernel,
        grid=(bsz, t // TM),
        in_specs=[pl.BlockSpec((1, TM, qw2), lambda b, i: (b, i, 0)),
                  pl.BlockSpec((1, t, kw), lambda b, i: (b, 0, 0)),
                  pl.BlockSpec((1, t, 2 * kw), lambda b, i: (b, 0, 0)),
                  pl.BlockSpec(sink_cols.shape, lambda b, i: (0, 0, 0)),
                  _tok_spec(qw), _const_spec(wout.shape), _tok_spec(D), _mod_spec(), _const_spec((1, D))],
        out_specs=_tok_spec(D),
        out_shape=jax.ShapeDtypeStruct(xs.shape, F32),
        scratch_shapes=[pltpu.VMEM((TM, qw), BF16)],
        input_output_aliases={6: 0},
        compiler_params=_cparams(2),
        name="swa_attention_out",
    )(q, k, v, sink_cols, gate, wout, xs, modl, gpost)


RADIX = 4
QP = TM // RADIX
TILE_ORDER = (3, 1, 2, 0)


def _hyena_proj_kernel(xp_ref, x_ref, xn_ref, mod_ref, gpre_ref, win_ref, cw_ref, cb_ref, perm_ref, *out_refs):
    u_refs, g_refs = out_refs[:RADIX], out_refs[RADIX:]
    t = pl.program_id(1)
    nt = pl.num_programs(1)
    hh = _dot(perm_ref[...], _modnorm(x_ref[0], mod_ref, gpre_ref).astype(BF16))
    h = jnp.concatenate([_modnorm(xp_ref[0], mod_ref, gpre_ref), hh, _modnorm(xn_ref[0], mod_ref, gpre_ref)], axis=0)
    p = _dot(h.astype(BF16), win_ref[...])
    cwid = 3 * C_WIDTH
    u = p[:, :cwid]
    start = {rho: HALO + k * QP for k, rho in enumerate(TILE_ORDER)}
    cls = {rho: u[start[rho]:start[rho] + QP] for rho in range(RADIX)}
    for rho in range(RADIX):
        g_refs[rho][0] = p[start[rho]:start[rho] + QP, cwid:].astype(BF16)
    before0 = pltpu.roll(u[0:HALO + QP], 1, 0)[HALO:]
    after3 = pltpu.roll(u[start[0]:], QP + HALO - 1, 0)[0:QP]
    r = lax.broadcasted_iota(jnp.int32, (QP, 1), 0)
    before0 = jnp.where(jnp.logical_and(r == 0, t <= 1), 0.0, before0)
    after3 = jnp.where(jnp.logical_and(r == QP - 1, jnp.logical_or(t == 0, t == nt - 1)), 0.0, after3)
    cw = cw_ref[...]
    cb = cb_ref[...]
    left = {0: before0, 1: cls[0], 2: cls[1], 3: cls[2]}
    right = {0: cls[1], 1: cls[2], 2: cls[3], 3: after3}
    for rho in range(RADIX):
        u_refs[rho][0] = (cb + left[rho] * cw[0:1] + cls[rho] * cw[1:2] + right[rho] * cw[2:3]).astype(BF16)


def _class_spec(width):
    return pl.BlockSpec((1, QP, width), lambda b, t: (b, t, 0))


def _hyena_proj(xs, modl, gpre, win, conv_w, conv_b):
    bsz, t, _ = xs.shape
    nt = t // TM
    per = TM // HALO
    last = t // HALO - 1
    part = lambda w: jax.ShapeDtypeStruct((bsz, t // RADIX, w), BF16)
    return pl.pallas_call(
        _hyena_proj_kernel,
        grid=(bsz, nt),
        in_specs=[pl.BlockSpec((1, HALO, D), lambda b, i: (b, jnp.maximum(i * per - 1, 0), 0)),
                  _tok_spec(D),
                  pl.BlockSpec((1, HALO, D), lambda b, i: (b, jnp.minimum((i + 1) * per, last), 0)),
                  _mod_spec(), _const_spec((1, D)), _const_spec(win.shape),
                  _const_spec(conv_w.shape), _const_spec(conv_b.shape), _const_spec((TM, TM))],
        out_specs=[_class_spec(3 * C_WIDTH)] * RADIX + [_class_spec(C_WIDTH)] * RADIX,
        out_shape=[part(3 * C_WIDTH)] * RADIX + [part(C_WIDTH)] * RADIX,
        compiler_params=_cparams(2),
        name="hyena_proj",
    )(xs, xs, xs, modl, gpre, win, conv_w, conv_b, _class_perm(TILE_ORDER))


def _class_perm(order):
    r = np.arange(QP)
    src = np.concatenate([RADIX * r + rho for rho in order])
    m = np.zeros((TM, TM), np.float32)
    m[np.arange(TM), src] = 1.0
    return jnp.asarray(m, BF16)


def _hyena_out_kernel(*refs):
    o_refs, g_refs = refs[:RADIX], refs[RADIX:2 * RADIX]
    wout_ref, x_ref, mod_ref, gpost_ref, perm_ref, xo_ref = refs[2 * RADIX:]
    a = jnp.concatenate([o_refs[rho][0].astype(F32) * _silu(g_refs[rho][0].astype(F32)) for rho in range(RADIX)],
                        axis=0)
    a = _dot(perm_ref[...], a.astype(BF16)).astype(BF16)
    y = _dot(a, wout_ref[...])
    xo_ref[0] = x_ref[0] + mod_ref[0, 0][2:3] * (_rms(y) * gpost_ref[...])


def _hyena_out_proj(os, gs, wout, xs, modl, gpost):
    bsz, t, _ = xs.shape
    w = os[0].shape[-1]
    return pl.pallas_call(
        _hyena_out_kernel,
        grid=(bsz, t // TM),
        in_specs=[_class_spec(w)] * (2 * RADIX) + [_const_spec(wout.shape), _tok_spec(D), _mod_spec(),
                                                   _const_spec((1, D)), _const_spec((TM, TM))],
        out_specs=_tok_spec(D),
        out_shape=jax.ShapeDtypeStruct(xs.shape, F32),
        input_output_aliases={2 * RADIX + 1: 0},
        compiler_params=_cparams(2),
        name="hyena_out_proj_residual",
    )(*os, *gs, wout, xs, modl, gpost, _class_perm(range(RADIX)).T)


def _filter_kernel(z_ref, w1_ref, b1_ref, fr_ref, w2_ref, b2_ref, w3_ref, dl_ref, o_ref):
    hp = lax.Precision.HIGHEST
    z = z_ref[...]
    fr = fr_ref[...]
    h = jnp.sin(fr * (jnp.dot(z, w1_ref[...], precision=hp, preferred_element_type=F32) + b1_ref[...]))
    h = jnp.sin(fr * (jnp.dot(h, w2_ref[...], precision=hp, preferred_element_type=F32) + b2_ref[...]))
    h = jnp.dot(h, w3_ref[...], precision=hp, preferred_element_type=F32)
    o_ref[...] = (h * jnp.exp(-z[:, 0:1] * dl_ref[...])).astype(o_ref.dtype)


SPEC_BLOCK = 512


def _filter_columns():
    o, cb, d, cc = np.meshgrid(np.arange(2), np.arange(C_WIDTH // SPEC_BLOCK), np.arange(2), np.arange(SPEC_BLOCK),
                               indexing="ij")
    return (d * 2 * C_WIDTH + o * C_WIDTH + cb * SPEC_BLOCK + cc).reshape(-1)


def _filters(n, w1, b1, fr, w2, b2, w3):
    t = np.linspace(0.0, 1.0, n, dtype=np.float32)[:, None]
    w = ((2.0 * math.pi / n) * np.arange(n, dtype=np.float32))[:, None].astype(np.float32)
    bands = np.linspace(1e-4, C_BANDS - 1, C_BANDS, dtype=np.float32)[None, :]
    z = np.zeros((n, LANE), np.float32)
    z[:, 0:1] = t
    z[:, 1:1 + C_BANDS] = np.cos(bands * w)
    z[:, 1 + C_BANDS:C_EMB] = -np.sin(bands * w)
    deltas = np.abs(np.linspace(C_MIN_DECAY, C_MAX_DECAY, C_WIDTH, dtype=np.float32))
    cols = _filter_columns()
    dl = deltas[cols % C_WIDTH][None, :]
    w3 = w3[:, cols]

    def pad(a, r, c):
        return jnp.zeros((r, c), F32).at[:a.shape[0], :a.shape[1]].set(a)

    tn = min(n, TM)
    nout = 4 * C_WIDTH
    cs = lambda shape: pl.BlockSpec(shape, lambda i: (0, 0))
    return pl.pallas_call(
        _filter_kernel,
        grid=(n // tn,),
        in_specs=[pl.BlockSpec((tn, LANE), lambda i: (i, 0)), cs((LANE, LANE)), cs((1, LANE)), cs((1, LANE)),
                  cs((LANE, LANE)), cs((1, LANE)), cs((LANE, nout)), cs((1, nout))],
        out_specs=pl.BlockSpec((tn, nout), lambda i: (i, 0)),
        out_shape=jax.ShapeDtypeStruct((n, nout), BF16),
        compiler_params=_cparams(1),
        name="hyena_filter_mlp",
    )(jnp.asarray(z), pad(w1, LANE, LANE), pad(b1[None], 1, LANE), pad(fr[None], 1, LANE),
      pad(w2, LANE, LANE), pad(b2[None], 1, LANE), pad(w3, LANE, nout), jnp.asarray(dl))


def _dft_matrix(n):
    f = np.arange(n, dtype=np.int64)[:, None]
    s = np.arange(n, dtype=np.int64)[None, :]
    ang = (2.0 * np.pi / (2 * n)) * ((f * s) % (2 * n)).astype(np.float64)
    cos = np.cos(ang)
    sin = np.sin(ang)
    sin[0, :] = np.where(np.arange(n) % 2 == 0, 1.0, -1.0)
    return cos.astype(np.float32), sin.astype(np.float32)


def _spectrum_kernel(a_ref, b_ref, re_ref, im_ref, *, scale):
    acc = _dot(a_ref[...], b_ref[...])
    fwd, bwd = acc[:, :SPEC_BLOCK], acc[:, SPEC_BLOCK:]
    re_ref[...] = (fwd + bwd) * scale
    im_ref[...] = (bwd - fwd) * scale


def _spectrum(a, b, bm, scale):
    m, k = a.shape
    nb = b.shape[1] // (2 * SPEC_BLOCK)
    assert m % bm == 0
    out = jax.ShapeDtypeStruct((m, nb * SPEC_BLOCK), F32)
    o_spec = pl.BlockSpec((bm, SPEC_BLOCK), lambda i, j: (i, j))
    return pl.pallas_call(
        functools.partial(_spectrum_kernel, scale=scale),
        grid=(m // bm, nb),
        in_specs=[pl.BlockSpec((bm, k), lambda i, j: (i, 0)), pl.BlockSpec((k, 2 * SPEC_BLOCK), lambda i, j: (0, j))],
        out_specs=[o_spec, o_spec],
        out_shape=[out, out],
        compiler_params=_cparams(2),
        name="filter_spectrum",
    )(a, b)


_PHASE = [(math.cos(math.pi * rho / 4), math.sin(math.pi * rho / 4),
           math.cos(3 * math.pi * rho / 4), math.sin(3 * math.pi * rho / 4)) for rho in range(RADIX)]


def _cmul(xr, xs, kr, ki):
    return xr * kr + xs * ki, xs * kr - xr * ki


def _hyena_conv_kernel(*refs, row0, q, aliased):
    nu = 3 * RADIX
    u_refs = refs[:nu]
    m_refs = refs[nu:nu + RADIX]
    mt_refs = refs[nu + RADIX:nu + 2 * RADIX]
    k_refs = refs[nu + 2 * RADIX:nu + 2 * RADIX + 8]
    ksr_ref, ksi_ref, fb_ref = refs[nu + 2 * RADIX + 8:nu + 2 * RADIX + 11]
    rest = refs[nu + 2 * RADIX + 11 + (RADIX if aliased else 0):]
    o_refs, z_refs = rest[:RADIX], rest[RADIX:]
    o = pl.program_id(2)
    rows = slice(row0, row0 + q)

    @pl.when(o == 0)
    def _():
        for rho in range(RADIX):
            z_refs[rho][...] = u_refs[3 * rho][0, rows, :]

    ksr, ksi = ksr_ref[...], ksi_ref[...]
    ksp = jnp.concatenate([ksr[0:3], ksi[5:6], ksr[3:4], ksi[6:7], ksr[4:5], ksi[7:8]], axis=0)
    first = lax.broadcasted_iota(jnp.int32, (8, 1), 0) == 0
    nblk = m_refs[0].shape[0]
    fbs = m_refs[0].shape[1] // 2

    def forward(j):
        return [_dot(m_refs[rho][j], z_refs[rho][...]) for rho in range(RADIX)]

    def spectral(j, fwd):
        c = [g[:fbs] for g in fwd]
        sn = [g[fbs:] for g in fwd]
        ar, as_, br, bs = c[0] + c[2], sn[0] + sn[2], c[1] + c[3], sn[1] + sn[3]
        cr, cs, dr, ds = c[0] - c[2], sn[0] - sn[2], c[1] - c[3], sn[1] - sn[3]
        k = [r[j * fbs:(j + 1) * fbs, :] for r in k_refs]
        y1r, y1s = _cmul(ar + br, as_ + bs, k[0], k[1])
        y2r, y2s = _cmul(cr - ds, cs + dr, k[2], k[3])
        y3r, y3s = _cmul(ar - br, bs - as_, k[4], k[5])
        y4r, y4s = _cmul(cr + ds, dr - cs, k[6], k[7])
        pr, ps, qr, qs = y1r + y3r, y1s - y3s, y1r - y3r, y1s + y3s
        rr, rs, tr, ts = y2r + y4r, y2s - y4s, y2r - y4r, y2s + y4s
        wr = [pr + rr, qr + ts, pr - rr, qr - ts]
        ws = [ps + rs, qs - tr, ps - rs, qs + tr]
        if j == 0:
            c0 = [x[0:8] for x in c]
            a0 = [x[0:8] for x in sn]
            y0 = (c0[0] + c0[1] + c0[2] + c0[3]) * ksp[0:1]
            yn = (c0[0] - c0[1] + c0[2] - c0[3]) * ksp[1:2]
            yhr, yhs = _cmul(c0[0] - c0[2], c0[1] - c0[3], ksp[2:3], ksp[3:4])
            xqr = sum(_PHASE[rho][0] * a0[rho] for rho in range(1, RADIX)) + a0[0]
            xqs = sum(_PHASE[rho][1] * a0[rho] for rho in range(1, RADIX))
            xgr = sum(_PHASE[rho][2] * a0[rho] for rho in range(1, RADIX)) + a0[0]
            xgs = sum(_PHASE[rho][3] * a0[rho] for rho in range(1, RADIX))
            yqr, yqs = _cmul(xqr, xqs, ksp[4:5], ksp[5:6])
            ygr, ygs = _cmul(xgr, xgs, ksp[6:7], ksp[7:8])
            turn = [yhr, yhs, -yhr, -yhs]
            for rho in range(RADIX):
                pc, psn, gc, gsn = _PHASE[rho]
                cos0 = y0 + (yn if rho % 2 == 0 else -yn) + turn[rho]
                sin0 = yqr * pc + yqs * psn + ygr * gc + ygs * gsn
                wr[rho] = jnp.concatenate([jnp.where(first, cos0, wr[rho][0:8]), wr[rho][8:]], axis=0)
                ws[rho] = jnp.concatenate([jnp.where(first, sin0, ws[rho][0:8]), ws[rho][8:]], axis=0)
        return [jnp.concatenate([wr[rho], ws[rho]], axis=0).astype(BF16) for rho in range(RADIX)]

    acc = [None] * RADIX
    fwd_next = forward(0)
    for j in range(nblk):
        fwd = fwd_next
        if j + 1 < nblk:
            fwd_next = forward(j + 1)
        w = spectral(j, fwd)
        for rho in range(RADIX):
            part = _dot(mt_refs[rho][j], w[rho])
            acc[rho] = part if acc[rho] is None else acc[rho] + part
    f_out = [acc[rho] + z_refs[rho][...].astype(F32) * fb_ref[0] for rho in range(RADIX)]

    @pl.when(o == 0)
    def _():
        for rho in range(RADIX):
            z_refs[rho][...] = (u_refs[3 * rho + 1][0, rows, :].astype(F32) * f_out[rho]).astype(BF16)

    @pl.when(o == 1)
    def _():
        for rho in range(RADIX):
            if row0 > 0:
                o_refs[rho][0, 0:row0, :] = jnp.zeros((row0, o_refs[rho].shape[2]), BF16)
            o_refs[rho][0, rows, :] = (u_refs[3 * rho + 2][0, rows, :].astype(F32) * f_out[rho]).astype(BF16)


def _hyena_conv(us, mats, spec_re, spec_im, fbias, *, row0, q, block_rows, tc, prev=None):
    bsz, t4, _ = us[0].shape
    nct = C_WIDTH // tc
    u_spec = lambda which: pl.BlockSpec((1, block_rows, tc), lambda b, c, o: (b, 0, which * nct + c))
    m_spec = pl.BlockSpec(mats[0].shape, lambda b, c, o: (0, 0, 0))
    mt_spec = pl.BlockSpec(mats[RADIX].shape, lambda b, c, o: (0, 0, 0))
    p_spec = lambda blk: pl.BlockSpec((q, tc), functools.partial(lambda g, b, c, o: (g, o * nct + c), blk))
    s_spec = pl.BlockSpec((8, tc), lambda b, c, o: (q, o * nct + c))
    in_specs = ([u_spec(w) for _ in range(RADIX) for w in range(3)] + [m_spec] * RADIX + [mt_spec] * RADIX
                + [p_spec(blk) for g in range(4) for blk in (g, 4 + g)] + [s_spec, s_spec]
                + [pl.BlockSpec((1, 1, tc), lambda b, c, o: (o, 0, c))])
    args = ([u for u in us for _ in range(3)] + list(mats) + [spec_re, spec_im] * 4 + [spec_re, spec_im, fbias])
    aliases = {}
    if prev is not None:
        in_specs += [pl.BlockSpec(memory_space=pl.ANY)] * RADIX
        aliases = {len(args) + rho: rho for rho in range(RADIX)}
        args += list(prev)
    o_spec = pl.BlockSpec((1, block_rows, tc), lambda b, c, o: (b, 0, c))
    return pl.pallas_call(
        functools.partial(_hyena_conv_kernel, row0=row0, q=q, aliased=prev is not None),
        grid=(bsz, nct, 2),
        in_specs=in_specs,
        out_specs=[o_spec] * RADIX,
        out_shape=[jax.ShapeDtypeStruct((bsz, t4, C_WIDTH), BF16)] * RADIX,
        scratch_shapes=[pltpu.VMEM((q, tc), BF16)] * RADIX,
        input_output_aliases=aliases,
        compiler_params=_cparams(3),
        name="hyena_long_conv_q%d" % q,
    )(*args)


def _radix4_matrices(n):
    q = n // RADIX
    fbs = min(q, LANE)
    nblk = q // fbs
    f = np.arange(q, dtype=np.int64)[:, None]
    r = np.arange(q, dtype=np.int64)[None, :]
    alt = np.where(np.arange(q) % 2 == 0, 1.0, -1.0)
    mats = []
    for rho in range(RADIX):
        ang = (2.0 * np.pi / (2 * n)) * ((f * (RADIX * r + rho)) % (2 * n)).astype(np.float64)
        cos, sin = np.cos(ang), np.sin(ang)
        sin[0, :] = alt
        mats.append(np.concatenate([cos.reshape(nblk, fbs, q), sin.reshape(nblk, fbs, q)], axis=1).astype(np.float32))
    tr = lambda m: np.ascontiguousarray(np.transpose(m, (0, 2, 1)))
    return tuple(jnp.asarray(m, BF16) for m in mats) + tuple(jnp.asarray(tr(m), BF16) for m in mats)


def _hyena_spectra(n, filt):
    h, q = n // 2, n // RADIX
    cos, sin = _dft_matrix(n)
    f = np.arange(q)
    groups = [f, h + f, np.maximum(n - f, 1) % n, h - f]
    special = [0.5 * cos[0:1], 0.5 * sin[0:1], cos[h:h + 1], cos[q:q + 1], cos[h + q:h + q + 1],
               sin[h:h + 1], sin[q:q + 1], sin[h + q:h + q + 1]]
    rows = np.concatenate([cos[g] for g in groups] + [sin[g] for g in groups] + special
                          + [np.zeros((LANE - 8, n), np.float32)], axis=0)
    nr = 8 * q + LANE
    return _spectrum(jnp.asarray(rows, BF16), filt, 3 * LANE if nr % (3 * LANE) == 0 else nr, 2.0 / (2 * n))


def _rope_tables(seq):
    rows = seq // GRID_W
    row = np.repeat(np.arange(rows, dtype=np.float32), GRID_W)
    col = np.tile(np.arange(GRID_W, dtype=np.float32), rows)
    per_axis = 32
    inv = (ROPE_BASE ** (-np.arange(0, per_axis, 2, dtype=np.float32) / per_axis)).astype(np.float32)
    ang = np.concatenate([row[:, None] * inv, col[:, None] * inv], axis=-1)
    cos = np.concatenate([np.ones((CTX, 32), np.float32), np.cos(ang)], axis=0)
    sin = np.concatenate([np.zeros((CTX, 32), np.float32), np.sin(ang)], axis=0)
    zero = np.zeros_like(cos)
    parts = ([cos, cos, cos, cos], [zero, sin, zero, sin], [-sin, zero, -sin, zero])
    return tuple(jnp.asarray(np.concatenate(p, axis=1), F32) for p in parts)


def _swa_head_perm():
    cols = []
    for j in range(B_KV_HEADS // 2):
        for g in range(B_GROUP):
            for hk in (2 * j, 2 * j + 1):
                h = hk * B_GROUP + g
                cols.extend(range(h * B_HDIM, (h + 1) * B_HDIM))
    return np.asarray(cols, np.int32)


def _mla_weights(w_in, w_q, w_kv):
    c1 = A_Q_RANK + A_KV_RANK
    zpad = jnp.zeros((D, LANE - A_ROPE), F32)
    win = jnp.concatenate([w_in[:, :c1 + A_ROPE], zpad, w_in[:, c1 + A_ROPE:]], axis=1)
    qscale = (A_NOPE + A_ROPE) ** -0.5 * math.log2(math.e)
    wq = w_q.reshape(A_Q_RANK, A_HEADS, A_NOPE + A_ROPE) * qscale
    wq = jnp.concatenate([wq[:, :, :A_NOPE].reshape(A_Q_RANK, -1), wq[:, :, A_NOPE:].reshape(A_Q_RANK, -1)], axis=1)
    return win.astype(BF16), wq.astype(BF16), w_kv.astype(BF16)


def kernel(x, c, ctx, c_ctx, w_mod, b_mod, g_pre, g_post, a_w_in, a_g_q, a_w_q, a_g_kv, a_w_kv, a_w_out, b_w_in, b_sink, b_w_out, c_w_in, c_conv_w, c_conv_b, c_f_w1, c_f_b1, c_f_freq, c_f_w2, c_f_b2, c_f_w3, c_filt_bias, c_w_out):
    bsz, seq, _ = x.shape
    assert ctx.shape[1] == CTX and (CTX + seq) % TP == 0 and seq % GRID_W == 0
    xsrc = (ctx, x)

    pad_rows = (-(bsz + 1)) % 8
    cond = jnp.concatenate([c, c_ctx[None], jnp.zeros((pad_rows, D), F32)], axis=0)
    mod = _modulation(cond, w_mod, b_mod)

    tabs = _rope_tables(seq)

    for layer in range(DEPTH):
        kind, j = layer % 3, layer // 3
        mx = mod[layer, :bsz].reshape(bsz, 3, D)
        mc = jnp.broadcast_to(mod[layer, bsz].reshape(1, 3, D), (bsz, 3, D))
        modl = jnp.stack([mc, mx], axis=0)
        gpre = g_pre[layer][None]
        gpost = g_post[layer][None]
        if kind == 0:
            win, wq, wkv = _mla_weights(a_w_in[j], a_w_q[j], a_w_kv[j])
            q, k, v, gate = _mla_proj(xsrc, modl, gpre, win, a_g_q[j][None], wq, a_g_kv[j][None], wkv, tabs)
            xs = _mla_attn(q, k, v, gate, a_w_out[j].astype(BF16), xsrc, modl, gpost,
                           latents_only=layer == DEPTH - 1)
        elif kind == 1:
            xs, = xsrc
            perm = _swa_head_perm()
            qw, kw = B_HEADS * B_HDIM, B_KV_HEADS * B_HDIM
            w = b_w_in[j]
            win = jnp.concatenate([w[:, :qw][:, perm] * (B_HDIM ** -0.5 * math.log2(math.e)), w[:, qw:qw + 2 * kw],
                                   w[:, qw + 2 * kw:][:, perm]], axis=1).astype(BF16)
            q, k, v, gate = _swa_proj(xs, modl, gpre, win, tabs)
            sink = (b_sink[j].astype(F32) * math.log2(math.e)).reshape(B_KV_HEADS, B_GROUP, 1, 1)
            sink_cols = jnp.broadcast_to(sink, (B_KV_HEADS, B_GROUP, B_BLOCK, LANE)).reshape(B_KV_HEADS, B_GROUP * B_BLOCK, LANE)
            xs = _swa_attn(q, k, v, sink_cols, gate, b_w_out[j][perm, :].astype(BF16), xs, modl, gpost)
        else:
            xs, = xsrc
            parts = _hyena_proj(xs, modl, gpre, c_w_in[j].astype(BF16), c_conv_w[j], c_conv_b[j][None])
            us, gs = parts[:RADIX], parts[RADIX:]
            fargs = (c_f_w1[j], c_f_b1[j], c_f_freq[j], c_f_w2[j], c_f_b2[j], c_f_w3[j])
            fbias = c_filt_bias[j].reshape(2, 1, C_WIDTH)
            os = _hyena_conv(us, _radix4_matrices(seq), *_hyena_spectra(seq, _filters(seq, *fargs)), fbias,
                             row0=CTX // RADIX, q=seq // RADIX, block_rows=(CTX + seq) // RADIX, tc=MXU_W)
            os = _hyena_conv(us, _radix4_matrices(CTX), *_hyena_spectra(CTX, _filters(CTX, *fargs)), fbias,
                             row0=0, q=CTX // RADIX, block_rows=CTX // RADIX, tc=C_WIDTH, prev=os)
            xs = _hyena_out_proj(os, gs, c_w_out[j].astype(BF16), xs, modl, gpost)
        xsrc = (xs,)
    return xs
```

```python
import functools
import math

import numpy as np
import jax
import jax.numpy as jnp
from jax import lax
from jax.experimental import pallas as pl
from jax.experimental.pallas import tpu as pltpu

F32 = jnp.float32
BF16 = jnp.bfloat16

D = 1024
DEPTH = 4
GRID_W = 64
CTX = 256
NORM_EPS = 1e-6
ROPE_BASE = 10000.0
NEG_INF = -1e30

A_HEADS = 8
A_Q_RANK = 512
A_KV_RANK = 256
A_NOPE = 128
A_ROPE = 64
A_VDIM = 128
A_HEAD_PAD = 256

B_HEADS = 16
B_KV_HEADS = 4
B_GROUP = 4
B_HDIM = 64
B_WINDOW = 128
B_BLOCK = 128

C_WIDTH = 1024
C_BANDS = 16
C_EMB = 1 + 2 * C_BANDS
C_FFN = 64
C_MIN_DECAY = math.log(1e-2) / 1.5
C_MAX_DECAY = math.log(1e-2) / 0.3

LANE = 128
MXU_W = 256
TM = 256
TP = 3 * TM
HALO = 8
VMEM_LIMIT = 56 * 1024 * 1024


def _cparams(n_axes):
    return pltpu.CompilerParams(dimension_semantics=("arbitrary",) * n_axes,
                                vmem_limit_bytes=VMEM_LIMIT)


def _rms(x):
    return x * lax.rsqrt(jnp.mean(x * x, axis=-1, keepdims=True) + NORM_EPS)


def _silu(g):
    return g / (1.0 + jnp.exp(-g))


def _dot(a, b):
    return jnp.dot(a, b, preferred_element_type=F32)


def _dot_nt(a, b):
    return lax.dot_general(a, b, (((1,), (1,)), ((), ())), preferred_element_type=F32)


def _rope128(x, a, b, c):
    return x * a + pltpu.roll(x, 32, 1) * b + pltpu.roll(x, LANE - 32, 1) * c


def _modnorm(x, mod_ref, gpre_ref):
    m = mod_ref[0, 0]
    return _rms(x) * gpre_ref[...] * (1.0 + m[1:2]) + m[0:1]


def _mod_kernel(c_ref, w_ref, b_ref, o_ref):
    a = _silu(c_ref[...])
    o_ref[0] = _dot(a.astype(BF16), w_ref[0].astype(BF16)) + b_ref[0]


def _modulation(cond, w_mod, b_mod):
    rows = cond.shape[0]
    return pl.pallas_call(
        _mod_kernel,
        grid=(DEPTH, 3),
        in_specs=[pl.BlockSpec((rows, D), lambda l, j: (0, 0)),
                  pl.BlockSpec((1, D, D), lambda l, j: (l, 0, j)),
                  pl.BlockSpec((1, 1, D), lambda l, j: (l, 0, j))],
        out_specs=pl.BlockSpec((1, rows, D), lambda l, j: (l, 0, j)),
        out_shape=jax.ShapeDtypeStruct((DEPTH, rows, 3 * D), F32),
        compiler_params=_cparams(2),
        name="adaln_modulation",
    )(cond, w_mod, b_mod.reshape(DEPTH, 1, 3 * D))


def _tok_spec(width, rows=TM, skip=0):
    return pl.BlockSpec((1, rows, width), lambda b, t: (b, t + skip, 0))


def _mod_spec(skip=0):
    return pl.BlockSpec((1, 1, 3, D), lambda b, t: (jnp.minimum(t + skip, 1), b, 0, 0))


def _const_spec(shape):
    nd = len(shape)
    return pl.BlockSpec(shape, lambda b, t: (0,) * nd)


def _residual_specs(dual, skip=0):
    if not dual:
        return [_tok_spec(D, TM, skip)]
    return [pl.BlockSpec((1, CTX, D), lambda b, t: (b, 0, 0)),
            pl.BlockSpec((1, TM, D), lambda b, t: (b, jnp.maximum(t - CTX // TM, 0), 0))]


def _wide_specs(dual):
    n = TP // TM
    if not dual:
        return [pl.BlockSpec((1, TM, D), functools.partial(lambda j, b, t: (b, n * t + j, 0), j)) for j in range(n)]
    last = lambda j, b, t: (b, jnp.maximum(n * t + j - CTX // TM, 0), 0)
    return ([pl.BlockSpec((1, CTX, D), lambda b, t: (b, 0, 0))]
            + [pl.BlockSpec((1, TM, D), functools.partial(last, j)) for j in range(n)])


def _wide_modnorm(refs, dual, mod_ref, gpre_ref):
    t = pl.program_id(1)
    blocks = [r[0] for r in refs[1:]] if dual else [r[0] for r in refs]
    if dual:
        blocks[0] = jnp.where(t == 0, refs[0][0], blocks[0])
    x = jnp.concatenate(blocks, axis=0)
    is_ctx = jnp.logical_and(lax.broadcasted_iota(jnp.int32, (TP, 1), 0) < CTX, t == 0)
    mc, mx = mod_ref[0, 0], mod_ref[1, 0]
    scale = jnp.where(is_ctx, mc[1:2], mx[1:2])
    shift = jnp.where(is_ctx, mc[0:1], mx[0:1])
    return _rms(x) * gpre_ref[...] * (1.0 + scale) + shift


def _both_mod_spec():
    return pl.BlockSpec((2, 1, 3, D), lambda b, t: (0, b, 0, 0))


def _resident(shape):
    nd = len(shape)
    return pl.BlockSpec(shape, lambda b, t: (0,) * nd, pipeline_mode=pl.Buffered(1))


def _residual_tile(refs, dual):
    if not dual:
        return refs[0][0]
    return jnp.where(pl.program_id(1) < CTX // TM, refs[0][0], refs[1][0])


def _finish(o, gate_ref, wout_ref, x, mod_ref, gpost_ref):
    a = o.astype(F32) * _silu(gate_ref[0].astype(F32))
    y = _dot(a.astype(BF16), wout_ref[...])
    return x + mod_ref[0, 0][2:3] * (_rms(y) * gpost_ref[...])


def _mla_proj_kernel(*refs, dual):
    nx = TP // TM + (1 if dual else 0)
    (mod_ref, gpre_ref, win_ref, gq_ref, wq_ref, gkv_ref, wkv_ref,
     ra_ref, rb_ref, rc_ref, q_ref, k_ref, v_ref, gate_ref) = refs[nx:]
    h = _wide_modnorm(refs[:nx], dual, mod_ref, gpre_ref)
    p = _dot(h.astype(BF16), win_ref[...])
    c0, c1, c2 = A_Q_RANK, A_Q_RANK + A_KV_RANK, A_Q_RANK + A_KV_RANK + LANE
    gate_ref[0] = p[:, c2:].astype(BF16)
    qn = (_rms(p[:, :c0]) * gq_ref[...]).astype(BF16)
    kvn = (_rms(p[:, c0:c1]) * gkv_ref[...]).astype(BF16)
    ra, rb, rc = ra_ref[...], rb_ref[...], rc_ref[...]
    kr = _rope128(p[:, c1:c2], ra, rb, rc).astype(BF16)
    qa = _dot(qn, wq_ref[...])
    lo = lax.broadcasted_iota(jnp.int32, (1, LANE), 1) < A_ROPE
    nope_w = A_HEADS * A_NOPE
    for pr in range(A_HEADS // 2):
        r = _rope128(qa[:, nope_w + pr * LANE:nope_w + (pr + 1) * LANE], ra, rb, rc)
        q_ref[0, 2 * pr, :, LANE:2 * LANE] = jnp.where(lo, r, 0.0).astype(BF16)
        q_ref[0, 2 * pr + 1, :, LANE:2 * LANE] = jnp.where(lo, pltpu.roll(r, LANE - A_ROPE, 1), 0.0).astype(BF16)
    for hd in range(A_HEADS):
        q_ref[0, hd, :, 0:LANE] = qa[:, hd * A_NOPE:(hd + 1) * A_NOPE].astype(BF16)
        kv = _dot(kvn, wkv_ref[:, hd * (A_NOPE + A_VDIM):(hd + 1) * (A_NOPE + A_VDIM)])
        k_ref[0, hd, :, 0:LANE] = kv[:, 0:A_NOPE].astype(BF16)
        k_ref[0, hd, :, LANE:2 * LANE] = kr
        v_ref[0, hd] = kv[:, A_NOPE:].astype(BF16)


def _mla_proj(xsrc, modl, gpre, win, gq, wq, gkv, wkv, tabs):
    dual = len(xsrc) == 2
    bsz = xsrc[0].shape[0]
    t = tabs[0].shape[0]
    xargs = [xsrc[0]] + [xsrc[1]] * (TP // TM) if dual else [xsrc[0]] * (TP // TM)
    head = lambda w: pl.BlockSpec((1, A_HEADS, TP, w), lambda b, i: (b, 0, i, 0))
    tab = pl.BlockSpec((TP, LANE), lambda b, i: (i, 0))
    return pl.pallas_call(
        functools.partial(_mla_proj_kernel, dual=dual),
        grid=(bsz, t // TP),
        in_specs=[*_wide_specs(dual), _both_mod_spec(), _const_spec((1, D)), _resident(win.shape),
                  _const_spec((1, A_Q_RANK)), _resident(wq.shape),
                  _const_spec((1, A_KV_RANK)), _resident(wkv.shape), tab, tab, tab],
        out_specs=[head(A_HEAD_PAD), head(A_HEAD_PAD), head(A_VDIM), _tok_spec(A_HEADS * A_VDIM, TP)],
        out_shape=[jax.ShapeDtypeStruct((bsz, A_HEADS, t, A_HEAD_PAD), BF16),
                   jax.ShapeDtypeStruct((bsz, A_HEADS, t, A_HEAD_PAD), BF16),
                   jax.ShapeDtypeStruct((bsz, A_HEADS, t, A_VDIM), BF16),
                   jax.ShapeDtypeStruct((bsz, t, A_HEADS * A_VDIM), BF16)],
        compiler_params=_cparams(2),
        name="mla_proj",
    )(*xargs, modl, gpre, win, gq, wq, gkv, wkv, *tabs)


def _mla_attn_kernel(*refs, dual, latents_only):
    nx = 2 if dual else 1
    q_ref, k_ref, v_ref, gate_ref, wout_ref = refs[:5]
    mod_ref, gpost_ref, xo_ref, o_ref = refs[5 + nx:]

    def attend(nk):
        s_next = _dot_nt(q_ref[0, 0], k_ref[0, 0, :nk, :])
        for hd in range(A_HEADS):
            s = s_next
            if hd + 1 < A_HEADS:
                s_next = _dot_nt(q_ref[0, hd + 1], k_ref[0, hd + 1, :nk, :])
            m = jnp.broadcast_to(jnp.max(s, axis=-1, keepdims=True), (TM, LANE))
            p = jnp.concatenate([jnp.exp2(s[:, c:c + LANE] - m) for c in range(0, nk, LANE)], axis=1)
            vx = jnp.concatenate([v_ref[0, hd, :nk, :], jnp.ones((nk, LANE), BF16)], axis=1)
            pv = _dot(p.astype(BF16), vx)
            o_ref[:, hd * A_VDIM:(hd + 1) * A_VDIM] = (pv[:, :A_VDIM] / pv[:, A_VDIM:]).astype(BF16)

    t_all = k_ref.shape[2]
    if latents_only:
        attend(t_all)
    else:
        is_ctx = pl.program_id(1) == 0
        pl.when(is_ctx)(lambda: attend(CTX))
        pl.when(jnp.logical_not(is_ctx))(lambda: attend(t_all))
    x = _residual_tile(refs[5:5 + nx], dual)
    xo_ref[0] = _finish(o_ref[...], gate_ref, wout_ref, x, mod_ref, gpost_ref)


def _mla_attn(q, k, v, gate, wout, xsrc, modl, gpost, latents_only):
    dual = len(xsrc) == 2
    bsz, nh, t, _ = q.shape
    skip = CTX // TM if latents_only else 0
    nt = t // TM - skip
    return pl.pallas_call(
        functools.partial(_mla_attn_kernel, dual=dual, latents_only=latents_only),
        grid=(bsz, nt),
        in_specs=[pl.BlockSpec((1, nh, TM, A_HEAD_PAD), lambda b, i: (b, 0, i + skip, 0)),
                  pl.BlockSpec((1, nh, t, A_HEAD_PAD), lambda b, i: (b, 0, 0, 0)),
                  pl.BlockSpec((1, nh, t, A_VDIM), lambda b, i: (b, 0, 0, 0)),
                  _tok_spec(nh * A_VDIM, skip=skip), _const_spec(wout.shape),
                  *_residual_specs(dual, skip=skip), _mod_spec(skip=skip), _const_spec((1, D))],
        out_specs=_tok_spec(D),
        out_shape=jax.ShapeDtypeStruct((bsz, nt * TM, D), F32),
        scratch_shapes=[pltpu.VMEM((TM, nh * A_VDIM), BF16)],
        input_output_aliases={} if (dual or latents_only) else {5: 0},
        compiler_params=_cparams(2),
        name="mla_attention_out",
    )(q, k, v, gate, wout, *xsrc, modl, gpost)


def _swa_proj_kernel(*refs):
    xrefs = refs[:TP // TM]
    mod_ref, gpre_ref, win_ref, ra_ref, rb_ref, rc_ref, q_ref, k_ref, v_ref, gate_ref = refs[TP // TM:]
    h = _wide_modnorm(xrefs, False, mod_ref, gpre_ref)
    p = _dot(h.astype(BF16), win_ref[...])
    ra, rb, rc = ra_ref[...], rb_ref[...], rc_ref[...]
    qw, kw = B_HEADS * B_HDIM, B_KV_HEADS * B_HDIM
    lane = lax.broadcasted_iota(jnp.int32, (1, LANE), 1)
    lo = lane < B_HDIM
    ones_blk = jnp.ones((TP, LANE), BF16)
    for j in range(B_KV_HEADS // 2):
        for g in range(B_GROUP):
            s = j * B_GROUP + g
            r = _rope128(p[:, s * LANE:(s + 1) * LANE], ra, rb, rc)
            for half in range(2):
                d = ((2 * j + half) * B_GROUP + g) * LANE
                keep = lo if half == 0 else jnp.logical_not(lo)
                q_ref[0, :, d:d + LANE] = jnp.where(keep, r, 0.0).astype(BF16)
        o = qw + j * LANE
        k_ref[0, :, j * LANE:(j + 1) * LANE] = _rope128(p[:, o:o + LANE], ra, rb, rc).astype(BF16)
        v_ref[0, :, 2 * j * LANE:(2 * j + 1) * LANE] = p[:, o + kw:o + kw + LANE].astype(BF16)
        v_ref[0, :, (2 * j + 1) * LANE:(2 * j + 2) * LANE] = ones_blk
    gate_ref[0] = p[:, qw + 2 * kw:].astype(BF16)


def _swa_proj(xs, modl, gpre, win, tabs):
    bsz, t, _ = xs.shape
    qw, kw = B_HEADS * B_HDIM, B_KV_HEADS * B_HDIM
    tab = pl.BlockSpec((TP, LANE), lambda b, i: (i, 0))
    return pl.pallas_call(
        _swa_proj_kernel,
        grid=(bsz, t // TP),
        in_specs=[*_wide_specs(False), _both_mod_spec(), _const_spec((1, D)), _resident(win.shape), tab, tab, tab],
        out_specs=[_tok_spec(2 * qw, TP), _tok_spec(kw, TP), _tok_spec(2 * kw, TP), _tok_spec(qw, TP)],
        out_shape=[jax.ShapeDtypeStruct((bsz, t, 2 * qw), BF16),
                   jax.ShapeDtypeStruct((bsz, t, kw), BF16),
                   jax.ShapeDtypeStruct((bsz, t, 2 * kw), BF16),
                   jax.ShapeDtypeStruct((bsz, t, qw), BF16)],
        compiler_params=_cparams(2),
        name="swa_proj",
    )(*[xs] * (TP // TM), modl, gpre, win, *tabs)


def _swa_attn_kernel(q_ref, k_ref, v_ref, sink_ref, gate_ref, wout_ref, x_ref, mod_ref, gpost_ref, xo_ref, o_ref):
    i = pl.program_id(1)
    n_lat = k_ref.shape[1] - CTX
    band = 3 * B_BLOCK
    lo = lax.broadcasted_iota(jnp.int32, (1, LANE), 1) < B_HDIM

    def run(with_band):
        chains = [(sub, j, half) for sub in range(TM // B_BLOCK) for j in range(B_KV_HEADS // 2) for half in range(2)]
        window, keys = {}, {}

        def block_window(sub):
            if sub not in window:
                li = (i - CTX // TM) * (TM // B_BLOCK) + sub
                start = jnp.clip((li - 1) * B_BLOCK, 0, n_lat - band)
                off = li * B_BLOCK - start
                d0 = (lax.broadcasted_iota(jnp.int32, (B_BLOCK, band), 0)
                      - lax.broadcasted_iota(jnp.int32, (B_BLOCK, band), 1))
                mask = jnp.abs(d0 + off) <= B_WINDOW
                window[sub] = (mask, pl.multiple_of(CTX + start, B_BLOCK))
            return window[sub]

        def block_keys(sub, j):
            if (sub, j) not in keys:
                kcols = slice(j * LANE, (j + 1) * LANE)
                vcols = slice(2 * j * LANE, (2 * j + 2) * LANE)
                kk = k_ref[0, 0:CTX, kcols]
                vv = v_ref[0, 0:CTX, vcols]
                if with_band:
                    kstart = block_window(sub)[1]
                    kk = jnp.concatenate([kk, k_ref[0, pl.ds(kstart, band), kcols]], axis=0)
                    vv = jnp.concatenate([vv, v_ref[0, pl.ds(kstart, band), vcols]], axis=0)
                keys[(sub, j)] = (kk, vv)
            return keys[(sub, j)]

        def scores(chain):
            sub, j, half = chain
            hk = 2 * j + half
            r0 = sub * B_BLOCK
            qs = jnp.concatenate(
                [q_ref[0, r0:r0 + B_BLOCK, (hk * B_GROUP + g) * LANE:(hk * B_GROUP + g + 1) * LANE]
                 for g in range(B_GROUP)], axis=0)
            return _dot_nt(qs, block_keys(sub, j)[0])

        outs = []
        s_next = scores(chains[0])
        for n, (sub, j, half) in enumerate(chains):
            s = s_next
            if n + 1 < len(chains):
                s_next = scores(chains[n + 1])
            sk = sink_ref[2 * j + half]
            if with_band:
                mask = block_window(sub)[0]
                s_b = jnp.concatenate(
                    [jnp.where(mask, s[g * B_BLOCK:(g + 1) * B_BLOCK, CTX:], NEG_INF) for g in range(B_GROUP)],
                    axis=0)
                s = jnp.concatenate([s[:, :CTX], s_b], axis=1)
            m = jnp.maximum(jnp.broadcast_to(jnp.max(s, axis=-1, keepdims=True), sk.shape), sk)
            p = jnp.concatenate([jnp.exp2(s[:, c:c + LANE] - m) for c in range(0, s.shape[1], LANE)], axis=1)
            pv = _dot(p.astype(BF16), block_keys(sub, j)[1])
            l = pv[:, LANE:] + jnp.exp2(sk - m)
            outs.append(pv[:, :LANE] / l)
            if half == 1:
                comb = jnp.where(lo, outs[-2], outs[-1])
                r0 = sub * B_BLOCK
                for g in range(B_GROUP):
                    s_out = j * B_GROUP + g
                    o_ref[r0:r0 + B_BLOCK, s_out * LANE:(s_out + 1) * LANE] = (
                        comb[g * B_BLOCK:(g + 1) * B_BLOCK].astype(BF16))

    is_ctx = i < CTX // TM
    pl.when(is_ctx)(lambda: run(False))
    pl.when(jnp.logical_not(is_ctx))(lambda: run(True))
    xo_ref[0] = _finish(o_ref[...], gate_ref, wout_ref, x_ref[0], mod_ref, gpost_ref)


def _swa_attn(q, k, v, sink_cols, gate, wout, xs, modl, gpost):
    bsz, t, qw2 = q.shape
    kw = k.shape[-1]
    qw = qw2 // 2
    return pl.pallas_call(
        _swa_attn_kernel,
        grid=(bsz, t // TM),
        in_specs=[pl.BlockSpec((1, TM, qw2), lambda b, i: (b, i, 0)),
                  pl.BlockSpec((1, t, kw), lambda b, i: (b, 0, 0)),
                  pl.BlockSpec((1, t, 2 * kw), lambda b, i: (b, 0, 0)),
                  pl.BlockSpec(sink_cols.shape, lambda b, i: (0, 0, 0)),
                  _tok_spec(qw), _const_spec(wout.shape), _tok_spec(D), _mod_spec(), _const_spec((1, D))],
        out_specs=_tok_spec(D),
        out_shape=jax.ShapeDtypeStruct(xs.shape, F32),
        scratch_shapes=[pltpu.VMEM((TM, qw), BF16)],
        input_output_aliases={6: 0},
        compiler_params=_cparams(2),
        name="swa_attention_out",
    )(q, k, v, sink_cols, gate, wout, xs, modl, gpost)


RADIX = 4
QP = TM // RADIX
TILE_ORDER = (3, 1, 2, 0)


def _hyena_proj_kernel(xp_ref, x_ref, xn_ref, mod_ref, gpre_ref, win_ref, cw_ref, cb_ref, perm_ref, *out_refs):
    u_refs, g_refs = out_refs[:RADIX], out_refs[RADIX:]
    t = pl.program_id(1)
    nt = pl.num_programs(1)
    hh = _dot(perm_ref[...], _modnorm(x_ref[0], mod_ref, gpre_ref).astype(BF16))
    h = jnp.concatenate([_modnorm(xp_ref[0], mod_ref, gpre_ref), hh, _modnorm(xn_ref[0], mod_ref, gpre_ref)], axis=0)
    p = _dot(h.astype(BF16), win_ref[...])
    cwid = 3 * C_WIDTH
    u = p[:, :cwid]
    start = {rho: HALO + k * QP for k, rho in enumerate(TILE_ORDER)}
    cls = {rho: u[start[rho]:start[rho] + QP] for rho in range(RADIX)}
    for rho in range(RADIX):
        g_refs[rho][0] = p[start[rho]:start[rho] + QP, cwid:].astype(BF16)
    before0 = pltpu.roll(u[0:HALO + QP], 1, 0)[HALO:]
    after3 = pltpu.roll(u[start[0]:], QP + HALO - 1, 0)[0:QP]
    r = lax.broadcasted_iota(jnp.int32, (QP, 1), 0)
    before0 = jnp.where(jnp.logical_and(r == 0, t <= 1), 0.0, before0)
    after3 = jnp.where(jnp.logical_and(r == QP - 1, jnp.logical_or(t == 0, t == nt - 1)), 0.0, after3)
    cw = cw_ref[...]
    cb = cb_ref[...]
    left = {0: before0, 1: cls[0], 2: cls[1], 3: cls[2]}
    right = {0: cls[1], 1: cls[2], 2: cls[3], 3: after3}
    for rho in range(RADIX):
        u_refs[rho][0] = (cb + left[rho] * cw[0:1] + cls[rho] * cw[1:2] + right[rho] * cw[2:3]).astype(BF16)


def _class_spec(width):
    return pl.BlockSpec((1, QP, width), lambda b, t: (b, t, 0))


def _hyena_proj(xs, modl, gpre, win, conv_w, conv_b):
    bsz, t, _ = xs.shape
    nt = t // TM
    per = TM // HALO
    last = t // HALO - 1
    part = lambda w: jax.ShapeDtypeStruct((bsz, t // RADIX, w), BF16)
    return pl.pallas_call(
        _hyena_proj_kernel,
        grid=(bsz, nt),
        in_specs=[pl.BlockSpec((1, HALO, D), lambda b, i: (b, jnp.maximum(i * per - 1, 0), 0)),
                  _tok_spec(D),
                  pl.BlockSpec((1, HALO, D), lambda b, i: (b, jnp.minimum((i + 1) * per, last), 0)),
                  _mod_spec(), _const_spec((1, D)), _const_spec(win.shape),
                  _const_spec(conv_w.shape), _const_spec(conv_b.shape), _const_spec((TM, TM))],
        out_specs=[_class_spec(3 * C_WIDTH)] * RADIX + [_class_spec(C_WIDTH)] * RADIX,
        out_shape=[part(3 * C_WIDTH)] * RADIX + [part(C_WIDTH)] * RADIX,
        compiler_params=_cparams(2),
        name="hyena_proj",
    )(xs, xs, xs, modl, gpre, win, conv_w, conv_b, _class_perm(TILE_ORDER))


def _class_perm(order):
    r = np.arange(QP)
    src = np.concatenate([RADIX * r + rho for rho in order])
    m = np.zeros((TM, TM), np.float32)
    m[np.arange(TM), src] = 1.0
    return jnp.asarray(m, BF16)


def _hyena_out_kernel(*refs):
    o_refs, g_refs = refs[:RADIX], refs[RADIX:2 * RADIX]
    wout_ref, x_ref, mod_ref, gpost_ref, perm_ref, xo_ref = refs[2 * RADIX:]
    a = jnp.concatenate([o_refs[rho][0].astype(F32) * _silu(g_refs[rho][0].astype(F32)) for rho in range(RADIX)],
                        axis=0)
    a = _dot(perm_ref[...], a.astype(BF16)).astype(BF16)
    y = _dot(a, wout_ref[...])
    xo_ref[0] = x_ref[0] + mod_ref[0, 0][2:3] * (_rms(y) * gpost_ref[...])


def _hyena_out_proj(os, gs, wout, xs, modl, gpost):
    bsz, t, _ = xs.shape
    w = os[0].shape[-1]
    return pl.pallas_call(
        _hyena_out_kernel,
        grid=(bsz, t // TM),
        in_specs=[_class_spec(w)] * (2 * RADIX) + [_const_spec(wout.shape), _tok_spec(D), _mod_spec(),
                                                   _const_spec((1, D)), _const_spec((TM, TM))],
        out_specs=_tok_spec(D),
        out_shape=jax.ShapeDtypeStruct(xs.shape, F32),
        input_output_aliases={2 * RADIX + 1: 0},
        compiler_params=_cparams(2),
        name="hyena_out_proj_residual",
    )(*os, *gs, wout, xs, modl, gpost, _class_perm(range(RADIX)).T)


def _filter_kernel(z_ref, w1_ref, b1_ref, fr_ref, w2_ref, b2_ref, w3_ref, dl_ref, o_ref):
    hp = lax.Precision.HIGHEST
    z = z_ref[...]
    fr = fr_ref[...]
    h = jnp.sin(fr * (jnp.dot(z, w1_ref[...], precision=hp, preferred_element_type=F32) + b1_ref[...]))
    h = jnp.sin(fr * (jnp.dot(h, w2_ref[...], precision=hp, preferred_element_type=F32) + b2_ref[...]))
    h = jnp.dot(h, w3_ref[...], precision=hp, preferred_element_type=F32)
    o_ref[...] = (h * jnp.exp(-z[:, 0:1] * dl_ref[...])).astype(o_ref.dtype)


SPEC_BLOCK = 512


def _filter_columns():
    o, cb, d, cc = np.meshgrid(np.arange(2), np.arange(C_WIDTH // SPEC_BLOCK), np.arange(2), np.arange(SPEC_BLOCK),
                               indexing="ij")
    return (d * 2 * C_WIDTH + o * C_WIDTH + cb * SPEC_BLOCK + cc).reshape(-1)


def _filters(n, w1, b1, fr, w2, b2, w3):
    t = np.linspace(0.0, 1.0, n, dtype=np.float32)[:, None]
    w = ((2.0 * math.pi / n) * np.arange(n, dtype=np.float32))[:, None].astype(np.float32)
    bands = np.linspace(1e-4, C_BANDS - 1, C_BANDS, dtype=np.float32)[None, :]
    z = np.zeros((n, LANE), np.float32)
    z[:, 0:1] = t
    z[:, 1:1 + C_BANDS] = np.cos(bands * w)
    z[:, 1 + C_BANDS:C_EMB] = -np.sin(bands * w)
    deltas = np.abs(np.linspace(C_MIN_DECAY, C_MAX_DECAY, C_WIDTH, dtype=np.float32))
    cols = _filter_columns()
    dl = deltas[cols % C_WIDTH][None, :]
    w3 = w3[:, cols]

    def pad(a, r, c):
        return jnp.zeros((r, c), F32).at[:a.shape[0], :a.shape[1]].set(a)

    tn = min(n, TM)
    nout = 4 * C_WIDTH
    cs = lambda shape: pl.BlockSpec(shape, lambda i: (0, 0))
    return pl.pallas_call(
        _filter_kernel,
        grid=(n // tn,),
        in_specs=[pl.BlockSpec((tn, LANE), lambda i: (i, 0)), cs((LANE, LANE)), cs((1, LANE)), cs((1, LANE)),
                  cs((LANE, LANE)), cs((1, LANE)), cs((LANE, nout)), cs((1, nout))],
        out_specs=pl.BlockSpec((tn, nout), lambda i: (i, 0)),
        out_shape=jax.ShapeDtypeStruct((n, nout), BF16),
        compiler_params=_cparams(1),
        name="hyena_filter_mlp",
    )(jnp.asarray(z), pad(w1, LANE, LANE), pad(b1[None], 1, LANE), pad(fr[None], 1, LANE),
      pad(w2, LANE, LANE), pad(b2[None], 1, LANE), pad(w3, LANE, nout), jnp.asarray(dl))


def _dft_matrix(n):
    f = np.arange(n, dtype=np.int64)[:, None]
    s = np.arange(n, dtype=np.int64)[None, :]
    ang = (2.0 * np.pi / (2 * n)) * ((f * s) % (2 * n)).astype(np.float64)
    cos = np.cos(ang)
    sin = np.sin(ang)
    sin[0, :] = np.where(np.arange(n) % 2 == 0, 1.0, -1.0)
    return cos.astype(np.float32), sin.astype(np.float32)


def _spectrum_kernel(a_ref, b_ref, re_ref, im_ref, *, scale):
    acc = _dot(a_ref[...], b_ref[...])
    fwd, bwd = acc[:, :SPEC_BLOCK], acc[:, SPEC_BLOCK:]
    re_ref[...] = (fwd + bwd) * scale
    im_ref[...] = (bwd - fwd) * scale


def _spectrum(a, b, bm, scale):
    m, k = a.shape
    nb = b.shape[1] // (2 * SPEC_BLOCK)
    assert m % bm == 0
    out = jax.ShapeDtypeStruct((m, nb * SPEC_BLOCK), F32)
    o_spec = pl.BlockSpec((bm, SPEC_BLOCK), lambda i, j: (i, j))
    return pl.pallas_call(
        functools.partial(_spectrum_kernel, scale=scale),
        grid=(m // bm, nb),
        in_specs=[pl.BlockSpec((bm, k), lambda i, j: (i, 0)), pl.BlockSpec((k, 2 * SPEC_BLOCK), lambda i, j: (0, j))],
        out_specs=[o_spec, o_spec],
        out_shape=[out, out],
        compiler_params=_cparams(2),
        name="filter_spectrum",
    )(a, b)


_PHASE = [(math.cos(math.pi * rho / 4), math.sin(math.pi * rho / 4),
           math.cos(3 * math.pi * rho / 4), math.sin(3 * math.pi * rho / 4)) for rho in range(RADIX)]


def _cmul(xr, xs, kr, ki):
    return xr * kr + xs * ki, xs * kr - xr * ki


def _hyena_conv_kernel(*refs, row0, q, aliased):
    nu = 3 * RADIX
    u_refs = refs[:nu]
    m_refs = refs[nu:nu + RADIX]
    mt_refs = refs[nu + RADIX:nu + 2 * RADIX]
    k_refs = refs[nu + 2 * RADIX:nu + 2 * RADIX + 8]
    ksr_ref, ksi_ref, fb_ref = refs[nu + 2 * RADIX + 8:nu + 2 * RADIX + 11]
    rest = refs[nu + 2 * RADIX + 11 + (RADIX if aliased else 0):]
    o_refs, z_refs = rest[:RADIX], rest[RADIX:]
    o = pl.program_id(2)
    rows = slice(row0, row0 + q)

    @pl.when(o == 0)
    def _():
        for rho in range(RADIX):
            z_refs[rho][...] = u_refs[3 * rho][0, rows, :]

    ksr, ksi = ksr_ref[...], ksi_ref[...]
    ksp = jnp.concatenate([ksr[0:3], ksi[5:6], ksr[3:4], ksi[6:7], ksr[4:5], ksi[7:8]], axis=0)
    first = lax.broadcasted_iota(jnp.int32, (8, 1), 0) == 0
    nblk = m_refs[0].shape[0]
    fbs = m_refs[0].shape[1] // 2

    def forward(j):
        return [_dot(m_refs[rho][j], z_refs[rho][...]) for rho in range(RADIX)]

    def spectral(j, fwd):
        c = [g[:fbs] for g in fwd]
        sn = [g[fbs:] for g in fwd]
        ar, as_, br, bs = c[0] + c[2], sn[0] + sn[2], c[1] + c[3], sn[1] + sn[3]
        cr, cs, dr, ds = c[0] - c[2], sn[0] - sn[2], c[1] - c[3], sn[1] - sn[3]
        k = [r[j * fbs:(j + 1) * fbs, :] for r in k_refs]
        y1r, y1s = _cmul(ar + br, as_ + bs, k[0], k[1])
        y2r, y2s = _cmul(cr - ds, cs + dr, k[2], k[3])
        y3r, y3s = _cmul(ar - br, bs - as_, k[4], k[5])
        y4r, y4s = _cmul(cr + ds, dr - cs, k[6], k[7])
        pr, ps, qr, qs = y1r + y3r, y1s - y3s, y1r - y3r, y1s + y3s
        rr, rs, tr, ts = y2r + y4r, y2s - y4s, y2r - y4r, y2s + y4s
        wr = [pr + rr, qr + ts, pr - rr, qr - ts]
        ws = [ps + rs, qs - tr, ps - rs, qs + tr]
        if j == 0:
            c0 = [x[0:8] for x in c]
            a0 = [x[0:8] for x in sn]
            y0 = (c0[0] + c0[1] + c0[2] + c0[3]) * ksp[0:1]
            yn = (c0[0] - c0[1] + c0[2] - c0[3]) * ksp[1:2]
            yhr, yhs = _cmul(c0[0] - c0[2], c0[1] - c0[3], ksp[2:3], ksp[3:4])
            xqr = sum(_PHASE[rho][0] * a0[rho] for rho in range(1, RADIX)) + a0[0]
            xqs = sum(_PHASE[rho][1] * a0[rho] for rho in range(1, RADIX))
            xgr = sum(_PHASE[rho][2] * a0[rho] for rho in range(1, RADIX)) + a0[0]
            xgs = sum(_PHASE[rho][3] * a0[rho] for rho in range(1, RADIX))
            yqr, yqs = _cmul(xqr, xqs, ksp[4:5], ksp[5:6])
            ygr, ygs = _cmul(xgr, xgs, ksp[6:7], ksp[7:8])
            turn = [yhr, yhs, -yhr, -yhs]
            for rho in range(RADIX):
                pc, psn, gc, gsn = _PHASE[rho]
                cos0 = y0 + (yn if rho % 2 == 0 else -yn) + turn[rho]
                sin0 = yqr * pc + yqs * psn + ygr * gc + ygs * gsn
                wr[rho] = jnp.concatenate([jnp.where(first, cos0, wr[rho][0:8]), wr[rho][8:]], axis=0)
                ws[rho] = jnp.concatenate([jnp.where(first, sin0, ws[rho][0:8]), ws[rho][8:]], axis=0)
        return [jnp.concatenate([wr[rho], ws[rho]], axis=0).astype(BF16) for rho in range(RADIX)]

    acc = [None] * RADIX
    fwd_next = forward(0)
    for j in range(nblk):
        fwd = fwd_next
        if j + 1 < nblk:
            fwd_next = forward(j + 1)
        w = spectral(j, fwd)
        for rho in range(RADIX):
            part = _dot(mt_refs[rho][j], w[rho])
            acc[rho] = part if acc[rho] is None else acc[rho] + part
    f_out = [acc[rho] + z_refs[rho][...].astype(F32) * fb_ref[0] for rho in range(RADIX)]

    @pl.when(o == 0)
    def _():
        for rho in range(RADIX):
            z_refs[rho][...] = (u_refs[3 * rho + 1][0, rows, :].astype(F32) * f_out[rho]).astype(BF16)

    @pl.when(o == 1)
    def _():
        for rho in range(RADIX):
            if row0 > 0:
                o_refs[rho][0, 0:row0, :] = jnp.zeros((row0, o_refs[rho].shape[2]), BF16)
            o_refs[rho][0, rows, :] = (u_refs[3 * rho + 2][0, rows, :].astype(F32) * f_out[rho]).astype(BF16)


def _hyena_conv(us, mats, spec_re, spec_im, fbias, *, row0, q, block_rows, tc, prev=None):
    bsz, t4, _ = us[0].shape
    nct = C_WIDTH // tc
    u_spec = lambda which: pl.BlockSpec((1, block_rows, tc), lambda b, c, o: (b, 0, which * nct + c))
    m_spec = pl.BlockSpec(mats[0].shape, lambda b, c, o: (0, 0, 0))
    mt_spec = pl.BlockSpec(mats[RADIX].shape, lambda b, c, o: (0, 0, 0))
    p_spec = lambda blk: pl.BlockSpec((q, tc), functools.partial(lambda g, b, c, o: (g, o * nct + c), blk))
    s_spec = pl.BlockSpec((8, tc), lambda b, c, o: (q, o * nct + c))
    in_specs = ([u_spec(w) for _ in range(RADIX) for w in range(3)] + [m_spec] * RADIX + [mt_spec] * RADIX
                + [p_spec(blk) for g in range(4) for blk in (g, 4 + g)] + [s_spec, s_spec]
                + [pl.BlockSpec((1, 1, tc), lambda b, c, o: (o, 0, c))])
    args = ([u for u in us for _ in range(3)] + list(mats) + [spec_re, spec_im] * 4 + [spec_re, spec_im, fbias])
    aliases = {}
    if prev is not None:
        in_specs += [pl.BlockSpec(memory_space=pl.ANY)] * RADIX
        aliases = {len(args) + rho: rho for rho in range(RADIX)}
        args += list(prev)
    o_spec = pl.BlockSpec((1, block_rows, tc), lambda b, c, o: (b, 0, c))
    return pl.pallas_call(
        functools.partial(_hyena_conv_kernel, row0=row0, q=q, aliased=prev is not None),
        grid=(bsz, nct, 2),
        in_specs=in_specs,
        out_specs=[o_spec] * RADIX,
        out_shape=[jax.ShapeDtypeStruct((bsz, t4, C_WIDTH), BF16)] * RADIX,
        scratch_shapes=[pltpu.VMEM((q, tc), BF16)] * RADIX,
        input_output_aliases=aliases,
        compiler_params=_cparams(3),
        name="hyena_long_conv_q%d" % q,
    )(*args)


def _radix4_matrices(n):
    q = n // RADIX
    fbs = min(q, LANE)
    nblk = q // fbs
    f = np.arange(q, dtype=np.int64)[:, None]
    r = np.arange(q, dtype=np.int64)[None, :]
    alt = np.where(np.arange(q) % 2 == 0, 1.0, -1.0)
    mats = []
    for rho in range(RADIX):
        ang = (2.0 * np.pi / (2 * n)) * ((f * (RADIX * r + rho)) % (2 * n)).astype(np.float64)
        cos, sin = np.cos(ang), np.sin(ang)
        sin[0, :] = alt
        mats.append(np.concatenate([cos.reshape(nblk, fbs, q), sin.reshape(nblk, fbs, q)], axis=1).astype(np.float32))
    tr = lambda m: np.ascontiguousarray(np.transpose(m, (0, 2, 1)))
    return tuple(jnp.asarray(m, BF16) for m in mats) + tuple(jnp.asarray(tr(m), BF16) for m in mats)


def _hyena_spectra(n, filt):
    h, q = n // 2, n // RADIX
    cos, sin = _dft_matrix(n)
    f = np.arange(q)
    groups = [f, h + f, np.maximum(n - f, 1) % n, h - f]
    special = [0.5 * cos[0:1], 0.5 * sin[0:1], cos[h:h + 1], cos[q:q + 1], cos[h + q:h + q + 1],
               sin[h:h + 1], sin[q:q + 1], sin[h + q:h + q + 1]]
    rows = np.concatenate([cos[g] for g in groups] + [sin[g] for g in groups] + special
                          + [np.zeros((LANE - 8, n), np.float32)], axis=0)
    nr = 8 * q + LANE
    return _spectrum(jnp.asarray(rows, BF16), filt, 3 * LANE if nr % (3 * LANE) == 0 else nr, 2.0 / (2 * n))


def _rope_tables(seq):
    rows = seq // GRID_W
    row = np.repeat(np.arange(rows, dtype=np.float32), GRID_W)
    col = np.tile(np.arange(GRID_W, dtype=np.float32), rows)
    per_axis = 32
    inv = (ROPE_BASE ** (-np.arange(0, per_axis, 2, dtype=np.float32) / per_axis)).astype(np.float32)
    ang = np.concatenate([row[:, None] * inv, col[:, None] * inv], axis=-1)
    cos = np.concatenate([np.ones((CTX, 32), np.float32), np.cos(ang)], axis=0)
    sin = np.concatenate([np.zeros((CTX, 32), np.float32), np.sin(ang)], axis=0)
    zero = np.zeros_like(cos)
    parts = ([cos, cos, cos, cos], [zero, sin, zero, sin], [-sin, zero, -sin, zero])
    return tuple(jnp.asarray(np.concatenate(p, axis=1), F32) for p in parts)


def _swa_head_perm():
    cols = []
    for j in range(B_KV_HEADS // 2):
        for g in range(B_GROUP):
            for hk in (2 * j, 2 * j + 1):
                h = hk * B_GROUP + g
                cols.extend(range(h * B_HDIM, (h + 1) * B_HDIM))
    return np.asarray(cols, np.int32)


def _mla_weights(w_in, w_q, w_kv):
    c1 = A_Q_RANK + A_KV_RANK
    zpad = jnp.zeros((D, LANE - A_ROPE), F32)
    win = jnp.concatenate([w_in[:, :c1 + A_ROPE], zpad, w_in[:, c1 + A_ROPE:]], axis=1)
    qscale = (A_NOPE + A_ROPE) ** -0.5 * math.log2(math.e)
    wq = w_q.reshape(A_Q_RANK, A_HEADS, A_NOPE + A_ROPE) * qscale
    wq = jnp.concatenate([wq[:, :, :A_NOPE].reshape(A_Q_RANK, -1), wq[:, :, A_NOPE:].reshape(A_Q_RANK, -1)], axis=1)
    return win.astype(BF16), wq.astype(BF16), w_kv.astype(BF16)


def kernel(x, c, ctx, c_ctx, w_mod, b_mod, g_pre, g_post, a_w_in, a_g_q, a_w_q, a_g_kv, a_w_kv, a_w_out, b_w_in, b_sink, b_w_out, c_w_in, c_conv_w, c_conv_b, c_f_w1, c_f_b1, c_f_freq, c_f_w2, c_f_b2, c_f_w3, c_filt_bias, c_w_out):
    bsz, seq, _ = x.shape
    assert ctx.shape[1] == CTX and (CTX + seq) % TP == 0 and seq % GRID_W == 0
    xsrc = (ctx, x)

    pad_rows = (-(bsz + 1)) % 8
    cond = jnp.concatenate([c, c_ctx[None], jnp.zeros((pad_rows, D), F32)], axis=0)
    mod = _modulation(cond, w_mod, b_mod)

    tabs = _rope_tables(seq)

    for layer in range(DEPTH):
        kind, j = layer % 3, layer // 3
        mx = mod[layer, :bsz].reshape(bsz, 3, D)
        mc = jnp.broadcast_to(mod[layer, bsz].reshape(1, 3, D), (bsz, 3, D))
        modl = jnp.stack([mc, mx], axis=0)
        gpre = g_pre[layer][None]
        gpost = g_post[layer][None]
        if kind == 0:
            win, wq, wkv = _mla_weights(a_w_in[j], a_w_q[j], a_w_kv[j])
            q, k, v, gate = _mla_proj(xsrc, modl, gpre, win, a_g_q[j][None], wq, a_g_kv[j][None], wkv, tabs)
            xs = _mla_attn(q, k, v, gate, a_w_out[j].astype(BF16), xsrc, modl, gpost,
                           latents_only=layer == DEPTH - 1)
        elif kind == 1:
            xs, = xsrc
            perm = _swa_head_perm()
            qw, kw = B_HEADS * B_HDIM, B_KV_HEADS * B_HDIM
            w = b_w_in[j]
            win = jnp.concatenate([w[:, :qw][:, perm] * (B_HDIM ** -0.5 * math.log2(math.e)), w[:, qw:qw + 2 * kw],
                                   w[:, qw + 2 * kw:][:, perm]], axis=1).astype(BF16)
            q, k, v, gate = _swa_proj(xs, modl, gpre, win, tabs)
            sink = (b_sink[j].astype(F32) * math.log2(math.e)).reshape(B_KV_HEADS, B_GROUP, 1, 1)
            sink_cols = jnp.broadcast_to(sink, (B_KV_HEADS, B_GROUP, B_BLOCK, LANE)).reshape(B_KV_HEADS, B_GROUP * B_BLOCK, LANE)
            xs = _swa_attn(q, k, v, sink_cols, gate, b_w_out[j][perm, :].astype(BF16), xs, modl, gpost)
        else:
            xs, = xsrc
            parts = _hyena_proj(xs, modl, gpre, c_w_in[j].astype(BF16), c_conv_w[j], c_conv_b[j][None])
            us, gs = parts[:RADIX], parts[RADIX:]
            fargs = (c_f_w1[j], c_f_b1[j], c_f_freq[j], c_f_w2[j], c_f_b2[j], c_f_w3[j])
            fbias = c_filt_bias[j].reshape(2, 1, C_WIDTH)
            os = _hyena_conv(us, _radix4_matrices(seq), *_hyena_spectra(seq, _filters(seq, *fargs)), fbias,
                             row0=CTX // RADIX, q=seq // RADIX, block_rows=(CTX + seq) // RADIX, tc=MXU_W)
            os = _hyena_conv(us, _radix4_matrices(CTX), *_hyena_spectra(CTX, _filters(CTX, *fargs)), fbias,
                             row0=0, q=CTX // RADIX, block_rows=CTX // RADIX, tc=C_WIDTH, prev=os)
            xs = _hyena_out_proj(os, gs, c_w_out[j].astype(BF16), xs, modl, gpost)
        xsrc = (xs,)
    return xs
```

```python
import functools
import math

import numpy as np
import jax
import jax.numpy as jnp
from jax import lax
from jax.experimental import pallas as pl
from jax.experimental.pallas import tpu as pltpu

F32 = jnp.float32
BF16 = jnp.bfloat16

D = 1024
DEPTH = 4
GRID_W = 64
CTX = 256
NORM_EPS = 1e-6
ROPE_BASE = 10000.0
NEG_INF = -1e30

A_HEADS = 8
A_Q_RANK = 512
A_KV_RANK = 256
A_NOPE = 128
A_ROPE = 64
A_VDIM = 128
A_HEAD_PAD = 256

B_HEADS = 16
B_KV_HEADS = 4
B_GROUP = 4
B_HDIM = 64
B_WINDOW = 128
B_BLOCK = 128

C_WIDTH = 1024
C_BANDS = 16
C_EMB = 1 + 2 * C_BANDS
C_FFN = 64
C_MIN_DECAY = math.log(1e-2) / 1.5
C_MAX_DECAY = math.log(1e-2) / 0.3

LANE = 128
MXU_W = 256
TM = 256
TP = 3 * TM
HALO = 8
VMEM_LIMIT = 56 * 1024 * 1024


def _cparams(n_axes):
    return pltpu.CompilerParams(dimension_semantics=("arbitrary",) * n_axes,
                                vmem_limit_bytes=VMEM_LIMIT)


def _rms(x):
    return x * lax.rsqrt(jnp.mean(x * x, axis=-1, keepdims=True) + NORM_EPS)


def _silu(g):
    return g / (1.0 + jnp.exp(-g))


def _dot(a, b):
    return jnp.dot(a, b, preferred_element_type=F32)


def _dot_nt(a, b):
    return lax.dot_general(a, b, (((1,), (1,)), ((), ())), preferred_element_type=F32)


def _rope128(x, a, b, c):
    return x * a + pltpu.roll(x, 32, 1) * b + pltpu.roll(x, LANE - 32, 1) * c


def _modnorm(x, mod_ref, gpre_ref):
    m = mod_ref[0, 0]
    return _rms(x) * gpre_ref[...] * (1.0 + m[1:2]) + m[0:1]


def _mod_kernel(c_ref, w_ref, b_ref, o_ref):
    a = _silu(c_ref[...])
    o_ref[0] = _dot(a.astype(BF16), w_ref[0].astype(BF16)) + b_ref[0]


def _modulation(cond, w_mod, b_mod):
    rows = cond.shape[0]
    return pl.pallas_call(
        _mod_kernel,
        grid=(DEPTH, 3),
        in_specs=[pl.BlockSpec((rows, D), lambda l, j: (0, 0)),
                  pl.BlockSpec((1, D, D), lambda l, j: (l, 0, j)),
                  pl.BlockSpec((1, 1, D), lambda l, j: (l, 0, j))],
        out_specs=pl.BlockSpec((1, rows, D), lambda l, j: (l, 0, j)),
        out_shape=jax.ShapeDtypeStruct((DEPTH, rows, 3 * D), F32),
        compiler_params=_cparams(2),
        name="adaln_modulation",
    )(cond, w_mod, b_mod.reshape(DEPTH, 1, 3 * D))


def _tok_spec(width, rows=TM, skip=0):
    return pl.BlockSpec((1, rows, width), lambda b, t: (b, t + skip, 0))


def _mod_spec(skip=0):
    return pl.BlockSpec((1, 1, 3, D), lambda b, t: (jnp.minimum(t + skip, 1), b, 0, 0))


def _const_spec(shape):
    nd = len(shape)
    return pl.BlockSpec(shape, lambda b, t: (0,) * nd)


def _residual_specs(dual, skip=0):
    if not dual:
        return [_tok_spec(D, TM, skip)]
    return [pl.BlockSpec((1, CTX, D), lambda b, t: (b, 0, 0)),
            pl.BlockSpec((1, TM, D), lambda b, t: (b, jnp.maximum(t - CTX // TM, 0), 0))]


def _wide_specs(dual):
    n = TP // TM
    if not dual:
        return [pl.BlockSpec((1, TM, D), functools.partial(lambda j, b, t: (b, n * t + j, 0), j)) for j in range(n)]
    last = lambda j, b, t: (b, jnp.maximum(n * t + j - CTX // TM, 0), 0)
    return ([pl.BlockSpec((1, CTX, D), lambda b, t: (b, 0, 0))]
            + [pl.BlockSpec((1, TM, D), functools.partial(last, j)) for j in range(n)])


def _wide_modnorm(refs, dual, mod_ref, gpre_ref):
    t = pl.program_id(1)
    blocks = [r[0] for r in refs[1:]] if dual else [r[0] for r in refs]
    if dual:
        blocks[0] = jnp.where(t == 0, refs[0][0], blocks[0])
    x = jnp.concatenate(blocks, axis=0)
    is_ctx = jnp.logical_and(lax.broadcasted_iota(jnp.int32, (TP, 1), 0) < CTX, t == 0)
    mc, mx = mod_ref[0, 0], mod_ref[1, 0]
    scale = jnp.where(is_ctx, mc[1:2], mx[1:2])
    shift = jnp.where(is_ctx, mc[0:1], mx[0:1])
    return _rms(x) * gpre_ref[...] * (1.0 + scale) + shift


def _both_mod_spec():
    return pl.BlockSpec((2, 1, 3, D), lambda b, t: (0, b, 0, 0))


def _resident(shape):
    nd = len(shape)
    return pl.BlockSpec(shape, lambda b, t: (0,) * nd, pipeline_mode=pl.Buffered(1))


def _residual_tile(refs, dual):
    if not dual:
        return refs[0][0]
    return jnp.where(pl.program_id(1) < CTX // TM, refs[0][0], refs[1][0])


def _finish(o, gate_ref, wout_ref, x, mod_ref, gpost_ref):
    a = o.astype(F32) * _silu(gate_ref[0].astype(F32))
    y = _dot(a.astype(BF16), wout_ref[...])
    return x + mod_ref[0, 0][2:3] * (_rms(y) * gpost_ref[...])


def _mla_proj_kernel(*refs, dual):
    nx = TP // TM + (1 if dual else 0)
    (mod_ref, gpre_ref, win_ref, gq_ref, wq_ref, gkv_ref, wkv_ref,
     ra_ref, rb_ref, rc_ref, q_ref, k_ref, v_ref, gate_ref) = refs[nx:]
    h = _wide_modnorm(refs[:nx], dual, mod_ref, gpre_ref)
    p = _dot(h.astype(BF16), win_ref[...])
    c0, c1, c2 = A_Q_RANK, A_Q_RANK + A_KV_RANK, A_Q_RANK + A_KV_RANK + LANE
    gate_ref[0] = p[:, c2:].astype(BF16)
    qn = (_rms(p[:, :c0]) * gq_ref[...]).astype(BF16)
    kvn = (_rms(p[:, c0:c1]) * gkv_ref[...]).astype(BF16)
    ra, rb, rc = ra_ref[...], rb_ref[...], rc_ref[...]
    kr = _rope128(p[:, c1:c2], ra, rb, rc).astype(BF16)
    qa = _dot(qn, wq_ref[...])
    lo = lax.broadcasted_iota(jnp.int32, (1, LANE), 1) < A_ROPE
    nope_w = A_HEADS * A_NOPE
    for pr in range(A_HEADS // 2):
        r = _rope128(qa[:, nope_w + pr * LANE:nope_w + (pr + 1) * LANE], ra, rb, rc)
        q_ref[0, 2 * pr, :, LANE:2 * LANE] = jnp.where(lo, r, 0.0).astype(BF16)
        q_ref[0, 2 * pr + 1, :, LANE:2 * LANE] = jnp.where(lo, pltpu.roll(r, LANE - A_ROPE, 1), 0.0).astype(BF16)
    for hd in range(A_HEADS):
        q_ref[0, hd, :, 0:LANE] = qa[:, hd * A_NOPE:(hd + 1) * A_NOPE].astype(BF16)
        kv = _dot(kvn, wkv_ref[:, hd * (A_NOPE + A_VDIM):(hd + 1) * (A_NOPE + A_VDIM)])
        k_ref[0, hd, :, 0:LANE] = kv[:, 0:A_NOPE].astype(BF16)
        k_ref[0, hd, :, LANE:2 * LANE] = kr
        v_ref[0, hd] = kv[:, A_NOPE:].astype(BF16)


def _mla_proj(xsrc, modl, gpre, win, gq, wq, gkv, wkv, tabs):
    dual = len(xsrc) == 2
    bsz = xsrc[0].shape[0]
    t = tabs[0].shape[0]
    xargs = [xsrc[0]] + [xsrc[1]] * (TP // TM) if dual else [xsrc[0]] * (TP // TM)
    head = lambda w: pl.BlockSpec((1, A_HEADS, TP, w), lambda b, i: (b, 0, i, 0))
    tab = pl.BlockSpec((TP, LANE), lambda b, i: (i, 0))
    return pl.pallas_call(
        functools.partial(_mla_proj_kernel, dual=dual),
        grid=(bsz, t // TP),
        in_specs=[*_wide_specs(dual), _both_mod_spec(), _const_spec((1, D)), _resident(win.shape),
                  _const_spec((1, A_Q_RANK)), _resident(wq.shape),
                  _const_spec((1, A_KV_RANK)), _resident(wkv.shape), tab, tab, tab],
        out_specs=[head(A_HEAD_PAD), head(A_HEAD_PAD), head(A_VDIM), _tok_spec(A_HEADS * A_VDIM, TP)],
        out_shape=[jax.ShapeDtypeStruct((bsz, A_HEADS, t, A_HEAD_PAD), BF16),
                   jax.ShapeDtypeStruct((bsz, A_HEADS, t, A_HEAD_PAD), BF16),
                   jax.ShapeDtypeStruct((bsz, A_HEADS, t, A_VDIM), BF16),
                   jax.ShapeDtypeStruct((bsz, t, A_HEADS * A_VDIM), BF16)],
        compiler_params=_cparams(2),
        name="mla_proj",
    )(*xargs, modl, gpre, win, gq, wq, gkv, wkv, *tabs)


def _mla_attn_kernel(*refs, dual, latents_only):
    nx = 2 if dual else 1
    q_ref, k_ref, v_ref, gate_ref, wout_ref = refs[:5]
    mod_ref, gpost_ref, xo_ref, o_ref = refs[5 + nx:]

    def attend(nk):
        s_next = _dot_nt(q_ref[0, 0], k_ref[0, 0, :nk, :])
        for hd in range(A_HEADS):
            s = s_next
            if hd + 1 < A_HEADS:
                s_next = _dot_nt(q_ref[0, hd + 1], k_ref[0, hd + 1, :nk, :])
            m = jnp.broadcast_to(jnp.max(s, axis=-1, keepdims=True), (TM, LANE))
            p = jnp.concatenate([jnp.exp2(s[:, c:c + LANE] - m) for c in range(0, nk, LANE)], axis=1)
            vx = jnp.concatenate([v_ref[0, hd, :nk, :], jnp.ones((nk, LANE), BF16)], axis=1)
            pv = _dot(p.astype(BF16), vx)
            o_ref[:, hd * A_VDIM:(hd + 1) * A_VDIM] = (pv[:, :A_VDIM] / pv[:, A_VDIM:]).astype(BF16)

    t_all = k_ref.shape[2]
    if latents_only:
        attend(t_all)
    else:
        is_ctx = pl.program_id(1) == 0
        pl.when(is_ctx)(lambda: attend(CTX))
        pl.when(jnp.logical_not(is_ctx))(lambda: attend(t_all))
    x = _residual_tile(refs[5:5 + nx], dual)
    xo_ref[0] = _finish(o_ref[...], gate_ref, wout_ref, x, mod_ref, gpost_ref)


def _mla_attn(q, k, v, gate, wout, xsrc, modl, gpost, latents_only):
    dual = len(xsrc) == 2
    bsz, nh, t, _ = q.shape
    skip = CTX // TM if latents_only else 0
    nt = t // TM - skip
    return pl.pallas_call(
        functools.partial(_mla_attn_kernel, dual=dual, latents_only=latents_only),
        grid=(bsz, nt),
        in_specs=[pl.BlockSpec((1, nh, TM, A_HEAD_PAD), lambda b, i: (b, 0, i + skip, 0)),
                  pl.BlockSpec((1, nh, t, A_HEAD_PAD), lambda b, i: (b, 0, 0, 0)),
                  pl.BlockSpec((1, nh, t, A_VDIM), lambda b, i: (b, 0, 0, 0)),
                  _tok_spec(nh * A_VDIM, skip=skip), _const_spec(wout.shape),
                  *_residual_specs(dual, skip=skip), _mod_spec(skip=skip), _const_spec((1, D))],
        out_specs=_tok_spec(D),
        out_shape=jax.ShapeDtypeStruct((bsz, nt * TM, D), F32),
        scratch_shapes=[pltpu.VMEM((TM, nh * A_VDIM), BF16)],
        input_output_aliases={} if (dual or latents_only) else {5: 0},
        compiler_params=_cparams(2),
        name="mla_attention_out",
    )(q, k, v, gate, wout, *xsrc, modl, gpost)


def _swa_proj_kernel(*refs):
    xrefs = refs[:TP // TM]
    mod_ref, gpre_ref, win_ref, ra_ref, rb_ref, rc_ref, q_ref, k_ref, v_ref, gate_ref = refs[TP // TM:]
    h = _wide_modnorm(xrefs, False, mod_ref, gpre_ref)
    p = _dot(h.astype(BF16), win_ref[...])
    ra, rb, rc = ra_ref[...], rb_ref[...], rc_ref[...]
    qw, kw = B_HEADS * B_HDIM, B_KV_HEADS * B_HDIM
    lane = lax.broadcasted_iota(jnp.int32, (1, LANE), 1)
    lo = lane < B_HDIM
    ones_blk = jnp.ones((TP, LANE), BF16)
    for j in range(B_KV_HEADS // 2):
        for g in range(B_GROUP):
            s = j * B_GROUP + g
            r = _rope128(p[:, s * LANE:(s + 1) * LANE], ra, rb, rc)
            for half in range(2):
                d = ((2 * j + half) * B_GROUP + g) * LANE
                keep = lo if half == 0 else jnp.logical_not(lo)
                q_ref[0, :, d:d + LANE] = jnp.where(keep, r, 0.0).astype(BF16)
        o = qw + j * LANE
        k_ref[0, :, j * LANE:(j + 1) * LANE] = _rope128(p[:, o:o + LANE], ra, rb, rc).astype(BF16)
        v_ref[0, :, 2 * j * LANE:(2 * j + 1) * LANE] = p[:, o + kw:o + kw + LANE].astype(BF16)
        v_ref[0, :, (2 * j + 1) * LANE:(2 * j + 2) * LANE] = ones_blk
    gate_ref[0] = p[:, qw + 2 * kw:].astype(BF16)


def _swa_proj(xs, modl, gpre, win, tabs):
    bsz, t, _ = xs.shape
    qw, kw = B_HEADS * B_HDIM, B_KV_HEADS * B_HDIM
    tab = pl.BlockSpec((TP, LANE), lambda b, i: (i, 0))
    return pl.pallas_call(
        _swa_proj_kernel,
        grid=(bsz, t // TP),
        in_specs=[*_wide_specs(False), _both_mod_spec(), _const_spec((1, D)), _resident(win.shape), tab, tab, tab],
        out_specs=[_tok_spec(2 * qw, TP), _tok_spec(kw, TP), _tok_spec(2 * kw, TP), _tok_spec(qw, TP)],
        out_shape=[jax.ShapeDtypeStruct((bsz, t, 2 * qw), BF16),
                   jax.ShapeDtypeStruct((bsz, t, kw), BF16),
                   jax.ShapeDtypeStruct((bsz, t, 2 * kw), BF16),
                   jax.ShapeDtypeStruct((bsz, t, qw), BF16)],
        compiler_params=_cparams(2),
        name="swa_proj",
    )(*[xs] * (TP // TM), modl, gpre, win, *tabs)


def _swa_attn_kernel(q_ref, k_ref, v_ref, sink_ref, gate_ref, wout_ref, x_ref, mod_ref, gpost_ref, xo_ref, o_ref):
    i = pl.program_id(1)
    n_lat = k_ref.shape[1] - CTX
    band = 3 * B_BLOCK
    lo = lax.broadcasted_iota(jnp.int32, (1, LANE), 1) < B_HDIM

    def run(with_band):
        chains = [(sub, j, half) for sub in range(TM // B_BLOCK) for j in range(B_KV_HEADS // 2) for half in range(2)]
        window, keys = {}, {}

        def block_window(sub):
            if sub not in window:
                li = (i - CTX // TM) * (TM // B_BLOCK) + sub
                start = jnp.clip((li - 1) * B_BLOCK, 0, n_lat - band)
                off = li * B_BLOCK - start
                d0 = (lax.broadcasted_iota(jnp.int32, (B_BLOCK, band), 0)
                      - lax.broadcasted_iota(jnp.int32, (B_BLOCK, band), 1))
                mask = jnp.abs(d0 + off) <= B_WINDOW
                window[sub] = (mask, pl.multiple_of(CTX + start, B_BLOCK))
            return window[sub]

        def block_keys(sub, j):
            if (sub, j) not in keys:
                kcols = slice(j * LANE, (j + 1) * LANE)
                vcols = slice(2 * j * LANE, (2 * j + 2) * LANE)
                kk = k_ref[0, 0:CTX, kcols]
                vv = v_ref[0, 0:CTX, vcols]
                if with_band:
                    kstart = block_window(sub)[1]
                    kk = jnp.concatenate([kk, k_ref[0, pl.ds(kstart, band), kcols]], axis=0)
                    vv = jnp.concatenate([vv, v_ref[0, pl.ds(kstart, band), vcols]], axis=0)
                keys[(sub, j)] = (kk, vv)
            return keys[(sub, j)]

        def scores(chain):
            sub, j, half = chain
            hk = 2 * j + half
            r0 = sub * B_BLOCK
            qs = jnp.concatenate(
                [q_ref[0, r0:r0 + B_BLOCK, (hk * B_GROUP + g) * LANE:(hk * B_GROUP + g + 1) * LANE]
                 for g in range(B_GROUP)], axis=0)
            return _dot_nt(qs, block_keys(sub, j)[0])

        outs = []
        s_next = scores(chains[0])
        for n, (sub, j, half) in enumerate(chains):
            s = s_next
            if n + 1 < len(chains):
                s_next = scores(chains[n + 1])
            sk = sink_ref[2 * j + half]
            if with_band:
                mask = block_window(sub)[0]
                s_b = jnp.concatenate(
                    [jnp.where(mask, s[g * B_BLOCK:(g + 1) * B_BLOCK, CTX:], NEG_INF) for g in range(B_GROUP)],
                    axis=0)
                s = jnp.concatenate([s[:, :CTX], s_b], axis=1)
            m = jnp.maximum(jnp.broadcast_to(jnp.max(s, axis=-1, keepdims=True), sk.shape), sk)
            p = jnp.concatenate([jnp.exp2(s[:, c:c + LANE] - m) for c in range(0, s.shape[1], LANE)], axis=1)
            pv = _dot(p.astype(BF16), block_keys(sub, j)[1])
            l = pv[:, LANE:] + jnp.exp2(sk - m)
            outs.append(pv[:, :LANE] / l)
            if half == 1:
                comb = jnp.where(lo, outs[-2], outs[-1])
                r0 = sub * B_BLOCK
                for g in range(B_GROUP):
                    s_out = j * B_GROUP + g
                    o_ref[r0:r0 + B_BLOCK, s_out * LANE:(s_out + 1) * LANE] = (
                        comb[g * B_BLOCK:(g + 1) * B_BLOCK].astype(BF16))

    is_ctx = i < CTX // TM
    pl.when(is_ctx)(lambda: run(False))
    pl.when(jnp.logical_not(is_ctx))(lambda: run(True))
    xo_ref[0] = _finish(o_ref[...], gate_ref, wout_ref, x_ref[0], mod_ref, gpost_ref)


def _swa_attn(q, k, v, sink_cols, gate, wout, xs, modl, gpost):
    bsz, t, qw2 = q.shape
    kw = k.shape[-1]
    qw = qw2 // 2
    return pl.pallas_call(
        _swa_attn_kernel,
        grid=(bsz, t // TM),
        in_specs=[pl.BlockSpec((1, TM, qw2), lambda b, i: (b, i, 0)),
                  pl.BlockSpec((1, t, kw), lambda b, i: (b, 0, 0)),
                  pl.BlockSpec((1, t, 2 * kw), lambda b, i: (b, 0, 0)),
                  pl.BlockSpec(sink_cols.shape, lambda b, i: (0, 0, 0)),
                  _tok_spec(qw), _const_spec(wout.shape), _tok_spec(D), _mod_spec(), _const_spec((1, D))],
        out_specs=_tok_spec(D),
        out_shape=jax.ShapeDtypeStruct(xs.shape, F32),
        scratch_shapes=[pltpu.VMEM((TM, qw), BF16)],
        input_output_aliases={6: 0},
        compiler_params=_cparams(2),
        name="swa_attention_out",
    )(q, k, v, sink_cols, gate, wout, xs, modl, gpost)


RADIX = 4
QP = TM // RADIX
TILE_ORDER = (3, 1, 2, 0)


def _hyena_proj_kernel(xp_ref, x_ref, xn_ref, mod_ref, gpre_ref, win_ref, cw_ref, cb_ref, perm_ref, *out_refs):
    u_refs, g_refs = out_refs[:RADIX], out_refs[RADIX:]
    t = pl.program_id(1)
    nt = pl.num_programs(1)
    hh = _dot(perm_ref[...], _modnorm(x_ref[0], mod_ref, gpre_ref).astype(BF16))
    h = jnp.concatenate([_modnorm(xp_ref[0], mod_ref, gpre_ref), hh, _modnorm(xn_ref[0], mod_ref, gpre_ref)], axis=0)
    p = _dot(h.astype(BF16), win_ref[...])
    cwid = 3 * C_WIDTH
    u = p[:, :cwid]
    start = {rho: HALO + k * QP for k, rho in enumerate(TILE_ORDER)}
    cls = {rho: u[start[rho]:start[rho] + QP] for rho in range(RADIX)}
    for rho in range(RADIX):
        g_refs[rho][0] = p[start[rho]:start[rho] + QP, cwid:].astype(BF16)
    before0 = pltpu.roll(u[0:HALO + QP], 1, 0)[HALO:]
    after3 = pltpu.roll(u[start[0]:], QP + HALO - 1, 0)[0:QP]
    r = lax.broadcasted_iota(jnp.int32, (QP, 1), 0)
    before0 = jnp.where(jnp.logical_and(r == 0, t <= 1), 0.0, before0)
    after3 = jnp.where(jnp.logical_and(r == QP - 1, jnp.logical_or(t == 0, t == nt - 1)), 0.0, after3)
    cw = cw_ref[...]
    cb = cb_ref[...]
    left = {0: before0, 1: cls[0], 2: cls[1], 3: cls[2]}
    right = {0: cls[1], 1: cls[2], 2: cls[3], 3: after3}
    for rho in range(RADIX):
        u_refs[rho][0] = (cb + left[rho] * cw[0:1] + cls[rho] * cw[1:2] + right[rho] * cw[2:3]).astype(BF16)


def _class_spec(width):
    return pl.BlockSpec((1, QP, width), lambda b, t: (b, t, 0))


def _hyena_proj(xs, modl, gpre, win, conv_w, conv_b):
    bsz, t, _ = xs.shape
    nt = t // TM
    per = TM // HALO
    last = t // HALO - 1
    part = lambda w: jax.ShapeDtypeStruct((bsz, t // RADIX, w), BF16)
    return pl.pallas_call(
        _hyena_proj_kernel,
        grid=(bsz, nt),
        in_specs=[pl.BlockSpec((1, HALO, D), lambda b, i: (b, jnp.maximum(i * per - 1, 0), 0)),
                  _tok_spec(D),
                  pl.BlockSpec((1, HALO, D), lambda b, i: (b, jnp.minimum((i + 1) * per, last), 0)),
                  _mod_spec(), _const_spec((1, D)), _const_spec(win.shape),
                  _const_spec(conv_w.shape), _const_spec(conv_b.shape), _const_spec((TM, TM))],
        out_specs=[_class_spec(3 * C_WIDTH)] * RADIX + [_class_spec(C_WIDTH)] * RADIX,
        out_shape=[part(3 * C_WIDTH)] * RADIX + [part(C_WIDTH)] * RADIX,
        compiler_params=_cparams(2),
        name="hyena_proj",
    )(xs, xs, xs, modl, gpre, win, conv_w, conv_b, _class_perm(TILE_ORDER))


def _class_perm(order):
    r = np.arange(QP)
    src = np.concatenate([RADIX * r + rho for rho in order])
    m = np.zeros((TM, TM), np.float32)
    m[np.arange(TM), src] = 1.0
    return jnp.asarray(m, BF16)


def _hyena_out_kernel(*refs):
    o_refs, g_refs = refs[:RADIX], refs[RADIX:2 * RADIX]
    wout_ref, x_ref, mod_ref, gpost_ref, perm_ref, xo_ref = refs[2 * RADIX:]
    subs = []
    for k in range(TP // TM):
        rows = slice(k * QP, (k + 1) * QP)
        a = jnp.concatenate([o_refs[rho][0, rows, :].astype(F32) * _silu(g_refs[rho][0, rows, :].astype(F32))
                             for rho in range(RADIX)], axis=0)
        subs.append(_dot(perm_ref[...], a.astype(BF16)).astype(BF16))
    y = _dot(jnp.concatenate(subs, axis=0), wout_ref[...])
    is_ctx = jnp.logical_and(lax.broadcasted_iota(jnp.int32, (TP, 1), 0) < CTX, pl.program_id(1) == 0)
    gate = jnp.where(is_ctx, mod_ref[0, 0][2:3], mod_ref[1, 0][2:3])
    xo_ref[0] = x_ref[0] + gate * (_rms(y) * gpost_ref[...])


def _hyena_out_proj(os, gs, wout, xs, modl, gpost):
    bsz, t, _ = xs.shape
    w = os[0].shape[-1]
    cls = pl.BlockSpec((1, TP // RADIX, w), lambda b, i: (b, i, 0))
    return pl.pallas_call(
        _hyena_out_kernel,
        grid=(bsz, t // TP),
        in_specs=[cls] * (2 * RADIX) + [_const_spec(wout.shape), _tok_spec(D, TP), _both_mod_spec(),
                                        _const_spec((1, D)), _const_spec((TM, TM))],
        out_specs=_tok_spec(D, TP),
        out_shape=jax.ShapeDtypeStruct(xs.shape, F32),
        input_output_aliases={2 * RADIX + 1: 0},
        compiler_params=_cparams(2),
        name="hyena_out_proj_residual",
    )(*os, *gs, wout, xs, modl, gpost, _class_perm(range(RADIX)).T)


def _filter_kernel(z_ref, w1_ref, b1_ref, fr_ref, w2_ref, b2_ref, w3_ref, dl_ref, o_ref):
    hp = lax.Precision.HIGHEST
    z = z_ref[...]
    fr = fr_ref[...]
    h = jnp.sin(fr * (jnp.dot(z, w1_ref[...], precision=hp, preferred_element_type=F32) + b1_ref[...]))
    h = jnp.sin(fr * (jnp.dot(h, w2_ref[...], precision=hp, preferred_element_type=F32) + b2_ref[...]))
    h = jnp.dot(h, w3_ref[...], precision=hp, preferred_element_type=F32)
    o_ref[...] = (h * jnp.exp(-z[:, 0:1] * dl_ref[...])).astype(o_ref.dtype)


SPEC_BLOCK = 512


def _filter_columns():
    o, cb, d, cc = np.meshgrid(np.arange(2), np.arange(C_WIDTH // SPEC_BLOCK), np.arange(2), np.arange(SPEC_BLOCK),
                               indexing="ij")
    return (d * 2 * C_WIDTH + o * C_WIDTH + cb * SPEC_BLOCK + cc).reshape(-1)


def _filters(n, w1, b1, fr, w2, b2, w3):
    t = np.linspace(0.0, 1.0, n, dtype=np.float32)[:, None]
    w = ((2.0 * math.pi / n) * np.arange(n, dtype=np.float32))[:, None].astype(np.float32)
    bands = np.linspace(1e-4, C_BANDS - 1, C_BANDS, dtype=np.float32)[None, :]
    z = np.zeros((n, LANE), np.float32)
    z[:, 0:1] = t
    z[:, 1:1 + C_BANDS] = np.cos(bands * w)
    z[:, 1 + C_BANDS:C_EMB] = -np.sin(bands * w)
    deltas = np.abs(np.linspace(C_MIN_DECAY, C_MAX_DECAY, C_WIDTH, dtype=np.float32))
    cols = _filter_columns()
    dl = deltas[cols % C_WIDTH][None, :]
    w3 = w3[:, cols]

    def pad(a, r, c):
        return jnp.zeros((r, c), F32).at[:a.shape[0], :a.shape[1]].set(a)

    tn = min(n, TM)
    nout = 4 * C_WIDTH
    cs = lambda shape: pl.BlockSpec(shape, lambda i: (0, 0))
    return pl.pallas_call(
        _filter_kernel,
        grid=(n // tn,),
        in_specs=[pl.BlockSpec((tn, LANE), lambda i: (i, 0)), cs((LANE, LANE)), cs((1, LANE)), cs((1, LANE)),
                  cs((LANE, LANE)), cs((1, LANE)), cs((LANE, nout)), cs((1, nout))],
        out_specs=pl.BlockSpec((tn, nout), lambda i: (i, 0)),
        out_shape=jax.ShapeDtypeStruct((n, nout), BF16),
        compiler_params=_cparams(1),
        name="hyena_filter_mlp",
    )(jnp.asarray(z), pad(w1, LANE, LANE), pad(b1[None], 1, LANE), pad(fr[None], 1, LANE),
      pad(w2, LANE, LANE), pad(b2[None], 1, LANE), pad(w3, LANE, nout), jnp.asarray(dl))


def _dft_matrix(n):
    f = np.arange(n, dtype=np.int64)[:, None]
    s = np.arange(n, dtype=np.int64)[None, :]
    ang = (2.0 * np.pi / (2 * n)) * ((f * s) % (2 * n)).astype(np.float64)
    cos = np.cos(ang)
    sin = np.sin(ang)
    sin[0, :] = np.where(np.arange(n) % 2 == 0, 1.0, -1.0)
    return cos.astype(np.float32), sin.astype(np.float32)


def _spectrum_kernel(a_ref, b_ref, re_ref, im_ref, *, scale):
    acc = _dot(a_ref[...], b_ref[...])
    fwd, bwd = acc[:, :SPEC_BLOCK], acc[:, SPEC_BLOCK:]
    re_ref[...] = (fwd + bwd) * scale
    im_ref[...] = (bwd - fwd) * scale


def _spectrum(a, b, bm, scale):
    m, k = a.shape
    nb = b.shape[1] // (2 * SPEC_BLOCK)
    assert m % bm == 0
    out = jax.ShapeDtypeStruct((m, nb * SPEC_BLOCK), F32)
    o_spec = pl.BlockSpec((bm, SPEC_BLOCK), lambda i, j: (i, j))
    return pl.pallas_call(
        functools.partial(_spectrum_kernel, scale=scale),
        grid=(m // bm, nb),
        in_specs=[pl.BlockSpec((bm, k), lambda i, j: (i, 0)), pl.BlockSpec((k, 2 * SPEC_BLOCK), lambda i, j: (0, j))],
        out_specs=[o_spec, o_spec],
        out_shape=[out, out],
        compiler_params=_cparams(2),
        name="filter_spectrum",
    )(a, b)


_PHASE = [(math.cos(math.pi * rho / 4), math.sin(math.pi * rho / 4),
           math.cos(3 * math.pi * rho / 4), math.sin(3 * math.pi * rho / 4)) for rho in range(RADIX)]


def _cmul(xr, xs, kr, ki):
    return xr * kr + xs * ki, xs * kr - xr * ki


def _hyena_conv_kernel(*refs, row0, q, aliased):
    nu = 3 * RADIX
    u_refs = refs[:nu]
    m_refs = refs[nu:nu + RADIX]
    mt_refs = refs[nu + RADIX:nu + 2 * RADIX]
    k_refs = refs[nu + 2 * RADIX:nu + 2 * RADIX + 8]
    ksr_ref, ksi_ref, fb_ref = refs[nu + 2 * RADIX + 8:nu + 2 * RADIX + 11]
    rest = refs[nu + 2 * RADIX + 11 + (RADIX if aliased else 0):]
    o_refs, z_refs = rest[:RADIX], rest[RADIX:]
    o = pl.program_id(2)
    rows = slice(row0, row0 + q)

    @pl.when(o == 0)
    def _():
        for rho in range(RADIX):
            z_refs[rho][...] = u_refs[3 * rho][0, rows, :]

    ksr, ksi = ksr_ref[...], ksi_ref[...]
    ksp = jnp.concatenate([ksr[0:3], ksi[5:6], ksr[3:4], ksi[6:7], ksr[4:5], ksi[7:8]], axis=0)
    first = lax.broadcasted_iota(jnp.int32, (8, 1), 0) == 0
    nblk = m_refs[0].shape[0]
    fbs = m_refs[0].shape[1] // 2

    def forward(j):
        return [_dot(m_refs[rho][j], z_refs[rho][...]) for rho in range(RADIX)]

    def spectral(j, fwd):
        c = [g[:fbs] for g in fwd]
        sn = [g[fbs:] for g in fwd]
        ar, as_, br, bs = c[0] + c[2], sn[0] + sn[2], c[1] + c[3], sn[1] + sn[3]
        cr, cs, dr, ds = c[0] - c[2], sn[0] - sn[2], c[1] - c[3], sn[1] - sn[3]
        k = [r[j * fbs:(j + 1) * fbs, :] for r in k_refs]
        y1r, y1s = _cmul(ar + br, as_ + bs, k[0], k[1])
        y2r, y2s = _cmul(cr - ds, cs + dr, k[2], k[3])
        y3r, y3s = _cmul(ar - br, bs - as_, k[4], k[5])
        y4r, y4s = _cmul(cr + ds, dr - cs, k[6], k[7])
        pr, ps, qr, qs = y1r + y3r, y1s - y3s, y1r - y3r, y1s + y3s
        rr, rs, tr, ts = y2r + y4r, y2s - y4s, y2r - y4r, y2s + y4s
        wr = [pr + rr, qr + ts, pr - rr, qr - ts]
        ws = [ps + rs, qs - tr, ps - rs, qs + tr]
        if j == 0:
            c0 = [x[0:8] for x in c]
            a0 = [x[0:8] for x in sn]
            y0 = (c0[0] + c0[1] + c0[2] + c0[3]) * ksp[0:1]
            yn = (c0[0] - c0[1] + c0[2] - c0[3]) * ksp[1:2]
            yhr, yhs = _cmul(c0[0] - c0[2], c0[1] - c0[3], ksp[2:3], ksp[3:4])
            xqr = sum(_PHASE[rho][0] * a0[rho] for rho in range(1, RADIX)) + a0[0]
            xqs = sum(_PHASE[rho][1] * a0[rho] for rho in range(1, RADIX))
            xgr = sum(_PHASE[rho][2] * a0[rho] for rho in range(1, RADIX)) + a0[0]
            xgs = sum(_PHASE[rho][3] * a0[rho] for rho in range(1, RADIX))
            yqr, yqs = _cmul(xqr, xqs, ksp[4:5], ksp[5:6])
            ygr, ygs = _cmul(xgr, xgs, ksp[6:7], ksp[7:8])
            turn = [yhr, yhs, -yhr, -yhs]
            for rho in range(RADIX):
                pc, psn, gc, gsn = _PHASE[rho]
                cos0 = y0 + (yn if rho % 2 == 0 else -yn) + turn[rho]
                sin0 = yqr * pc + yqs * psn + ygr * gc + ygs * gsn
                wr[rho] = jnp.concatenate([jnp.where(first, cos0, wr[rho][0:8]), wr[rho][8:]], axis=0)
                ws[rho] = jnp.concatenate([jnp.where(first, sin0, ws[rho][0:8]), ws[rho][8:]], axis=0)
        return [jnp.concatenate([wr[rho], ws[rho]], axis=0).astype(BF16) for rho in range(RADIX)]

    acc = [None] * RADIX
    fwd_next = forward(0)
    for j in range(nblk):
        fwd = fwd_next
        if j + 1 < nblk:
            fwd_next = forward(j + 1)
        w = spectral(j, fwd)
        for rho in range(RADIX):
            part = _dot(mt_refs[rho][j], w[rho])
            acc[rho] = part if acc[rho] is None else acc[rho] + part
    f_out = [acc[rho] + z_refs[rho][...].astype(F32) * fb_ref[0] for rho in range(RADIX)]

    @pl.when(o == 0)
    def _():
        for rho in range(RADIX):
            z_refs[rho][...] = (u_refs[3 * rho + 1][0, rows, :].astype(F32) * f_out[rho]).astype(BF16)

    @pl.when(o == 1)
    def _():
        for rho in range(RADIX):
            if row0 > 0:
                o_refs[rho][0, 0:row0, :] = jnp.zeros((row0, o_refs[rho].shape[2]), BF16)
            o_refs[rho][0, rows, :] = (u_refs[3 * rho + 2][0, rows, :].astype(F32) * f_out[rho]).astype(BF16)


def _hyena_conv(us, mats, spec_re, spec_im, fbias, *, row0, q, block_rows, tc, prev=None):
    bsz, t4, _ = us[0].shape
    nct = C_WIDTH // tc
    u_spec = lambda which: pl.BlockSpec((1, block_rows, tc), lambda b, c, o: (b, 0, which * nct + c))
    m_spec = pl.BlockSpec(mats[0].shape, lambda b, c, o: (0, 0, 0))
    mt_spec = pl.BlockSpec(mats[RADIX].shape, lambda b, c, o: (0, 0, 0))
    p_spec = lambda blk: pl.BlockSpec((q, tc), functools.partial(lambda g, b, c, o: (g, o * nct + c), blk))
    s_spec = pl.BlockSpec((8, tc), lambda b, c, o: (q, o * nct + c))
    in_specs = ([u_spec(w) for _ in range(RADIX) for w in range(3)] + [m_spec] * RADIX + [mt_spec] * RADIX
                + [p_spec(blk) for g in range(4) for blk in (g, 4 + g)] + [s_spec, s_spec]
                + [pl.BlockSpec((1, 1, tc), lambda b, c, o: (o, 0, c))])
    args = ([u for u in us for _ in range(3)] + list(mats) + [spec_re, spec_im] * 4 + [spec_re, spec_im, fbias])
    aliases = {}
    if prev is not None:
        in_specs += [pl.BlockSpec(memory_space=pl.ANY)] * RADIX
        aliases = {len(args) + rho: rho for rho in range(RADIX)}
        args += list(prev)
    o_spec = pl.BlockSpec((1, block_rows, tc), lambda b, c, o: (b, 0, c))
    return pl.pallas_call(
        functools.partial(_hyena_conv_kernel, row0=row0, q=q, aliased=prev is not None),
        grid=(bsz, nct, 2),
        in_specs=in_specs,
        out_specs=[o_spec] * RADIX,
        out_shape=[jax.ShapeDtypeStruct((bsz, t4, C_WIDTH), BF16)] * RADIX,
        scratch_shapes=[pltpu.VMEM((q, tc), BF16)] * RADIX,
        input_output_aliases=aliases,
        compiler_params=_cparams(3),
        name="hyena_long_conv_q%d" % q,
    )(*args)


def _radix4_matrices(n):
    q = n // RADIX
    fbs = min(q, LANE)
    nblk = q // fbs
    f = np.arange(q, dtype=np.int64)[:, None]
    r = np.arange(q, dtype=np.int64)[None, :]
    alt = np.where(np.arange(q) % 2 == 0, 1.0, -1.0)
    mats = []
    for rho in range(RADIX):
        ang = (2.0 * np.pi / (2 * n)) * ((f * (RADIX * r + rho)) % (2 * n)).astype(np.float64)
        cos, sin = np.cos(ang), np.sin(ang)
        sin[0, :] = alt
        mats.append(np.concatenate([cos.reshape(nblk, fbs, q), sin.reshape(nblk, fbs, q)], axis=1).astype(np.float32))
    tr = lambda m: np.ascontiguousarray(np.transpose(m, (0, 2, 1)))
    return tuple(jnp.asarray(m, BF16) for m in mats) + tuple(jnp.asarray(tr(m), BF16) for m in mats)


def _hyena_spectra(n, filt):
    h, q = n // 2, n // RADIX
    cos, sin = _dft_matrix(n)
    f = np.arange(q)
    groups = [f, h + f, np.maximum(n - f, 1) % n, h - f]
    special = [0.5 * cos[0:1], 0.5 * sin[0:1], cos[h:h + 1], cos[q:q + 1], cos[h + q:h + q + 1],
               sin[h:h + 1], sin[q:q + 1], sin[h + q:h + q + 1]]
    rows = np.concatenate([cos[g] for g in groups] + [sin[g] for g in groups] + special
                          + [np.zeros((LANE - 8, n), np.float32)], axis=0)
    nr = 8 * q + LANE
    return _spectrum(jnp.asarray(rows, BF16), filt, 3 * LANE if nr % (3 * LANE) == 0 else nr, 2.0 / (2 * n))


def _rope_tables(seq):
    rows = seq // GRID_W
    row = np.repeat(np.arange(rows, dtype=np.float32), GRID_W)
    col = np.tile(np.arange(GRID_W, dtype=np.float32), rows)
    per_axis = 32
    inv = (ROPE_BASE ** (-np.arange(0, per_axis, 2, dtype=np.float32) / per_axis)).astype(np.float32)
    ang = np.concatenate([row[:, None] * inv, col[:, None] * inv], axis=-1)
    cos = np.concatenate([np.ones((CTX, 32), np.float32), np.cos(ang)], axis=0)
    sin = np.concatenate([np.zeros((CTX, 32), np.float32), np.sin(ang)], axis=0)
    zero = np.zeros_like(cos)
    parts = ([cos, cos, cos, cos], [zero, sin, zero, sin], [-sin, zero, -sin, zero])
    return tuple(jnp.asarray(np.concatenate(p, axis=1), F32) for p in parts)


def _swa_head_perm():
    cols = []
    for j in range(B_KV_HEADS // 2):
        for g in range(B_GROUP):
            for hk in (2 * j, 2 * j + 1):
                h = hk * B_GROUP + g
                cols.extend(range(h * B_HDIM, (h + 1) * B_HDIM))
    return np.asarray(cols, np.int32)


def _mla_weights(w_in, w_q, w_kv):
    c1 = A_Q_RANK + A_KV_RANK
    zpad = jnp.zeros((D, LANE - A_ROPE), F32)
    win = jnp.concatenate([w_in[:, :c1 + A_ROPE], zpad, w_in[:, c1 + A_ROPE:]], axis=1)
    qscale = (A_NOPE + A_ROPE) ** -0.5 * math.log2(math.e)
    wq = w_q.reshape(A_Q_RANK, A_HEADS, A_NOPE + A_ROPE) * qscale
    wq = jnp.concatenate([wq[:, :, :A_NOPE].reshape(A_Q_RANK, -1), wq[:, :, A_NOPE:].reshape(A_Q_RANK, -1)], axis=1)
    return win.astype(BF16), wq.astype(BF16), w_kv.astype(BF16)


def kernel(x, c, ctx, c_ctx, w_mod, b_mod, g_pre, g_post, a_w_in, a_g_q, a_w_q, a_g_kv, a_w_kv, a_w_out, b_w_in, b_sink, b_w_out, c_w_in, c_conv_w, c_conv_b, c_f_w1, c_f_b1, c_f_freq, c_f_w2, c_f_b2, c_f_w3, c_filt_bias, c_w_out):
    bsz, seq, _ = x.shape
    assert ctx.shape[1] == CTX and (CTX + seq) % TP == 0 and seq % GRID_W == 0
    xsrc = (ctx, x)

    pad_rows = (-(bsz + 1)) % 8
    cond = jnp.concatenate([c, c_ctx[None], jnp.zeros((pad_rows, D), F32)], axis=0)
    mod = _modulation(cond, w_mod, b_mod)

    tabs = _rope_tables(seq)

    for layer in range(DEPTH):
        kind, j = layer % 3, layer // 3
        mx = mod[layer, :bsz].reshape(bsz, 3, D)
        mc = jnp.broadcast_to(mod[layer, bsz].reshape(1, 3, D), (bsz, 3, D))
        modl = jnp.stack([mc, mx], axis=0)
        gpre = g_pre[layer][None]
        gpost = g_post[layer][None]
        if kind == 0:
            win, wq, wkv = _mla_weights(a_w_in[j], a_w_q[j], a_w_kv[j])
            q, k, v, gate = _mla_proj(xsrc, modl, gpre, win, a_g_q[j][None], wq, a_g_kv[j][None], wkv, tabs)
            xs = _mla_attn(q, k, v, gate, a_w_out[j].astype(BF16), xsrc, modl, gpost,
                           latents_only=layer == DEPTH - 1)
        elif kind == 1:
            xs, = xsrc
            perm = _swa_head_perm()
            qw, kw = B_HEADS * B_HDIM, B_KV_HEADS * B_HDIM
            w = b_w_in[j]
            win = jnp.concatenate([w[:, :qw][:, perm] * (B_HDIM ** -0.5 * math.log2(math.e)), w[:, qw:qw + 2 * kw],
                                   w[:, qw + 2 * kw:][:, perm]], axis=1).astype(BF16)
            q, k, v, gate = _swa_proj(xs, modl, gpre, win, tabs)
            sink = (b_sink[j].astype(F32) * math.log2(math.e)).reshape(B_KV_HEADS, B_GROUP, 1, 1)
            sink_cols = jnp.broadcast_to(sink, (B_KV_HEADS, B_GROUP, B_BLOCK, LANE)).reshape(B_KV_HEADS, B_GROUP * B_BLOCK, LANE)
            xs = _swa_attn(q, k, v, sink_cols, gate, b_w_out[j][perm, :].astype(BF16), xs, modl, gpost)
        else:
            xs, = xsrc
            parts = _hyena_proj(xs, modl, gpre, c_w_in[j].astype(BF16), c_conv_w[j], c_conv_b[j][None])
            us, gs = parts[:RADIX], parts[RADIX:]
            fargs = (c_f_w1[j], c_f_b1[j], c_f_freq[j], c_f_w2[j], c_f_b2[j], c_f_w3[j])
            fbias = c_filt_bias[j].reshape(2, 1, C_WIDTH)
            os = _hyena_conv(us, _radix4_matrices(seq), *_hyena_spectra(seq, _filters(seq, *fargs)), fbias,
                             row0=CTX // RADIX, q=seq // RADIX, block_rows=(CTX + seq) // RADIX, tc=MXU_W)
            os = _hyena_conv(us, _radix4_matrices(CTX), *_hyena_spectra(CTX, _filters(CTX, *fargs)), fbias,
                             row0=0, q=CTX // RADIX, block_rows=CTX // RADIX, tc=C_WIDTH, prev=os)
            xs = _hyena_out_proj(os, gs, c_w_out[j].astype(BF16), xs, modl, gpost)
        xsrc = (xs,)
    return xs
```

```python
import functools
import math

import numpy as np
import jax
import jax.numpy as jnp
from jax import lax
from jax.experimental import pallas as pl
from jax.experimental.pallas import tpu as pltpu

F32 = jnp.float32
BF16 = jnp.bfloat16

D = 1024
DEPTH = 4
GRID_W = 64
CTX = 256
NORM_EPS = 1e-6
ROPE_BASE = 10000.0
NEG_INF = -1e30

A_HEADS = 8
A_Q_RANK = 512
A_KV_RANK = 256
A_NOPE = 128
A_ROPE = 64
A_VDIM = 128
A_HEAD_PAD = 256

B_HEADS = 16
B_KV_HEADS = 4
B_GROUP = 4
B_HDIM = 64
B_WINDOW = 128
B_BLOCK = 128

C_WIDTH = 1024
C_BANDS = 16
C_EMB = 1 + 2 * C_BANDS
C_FFN = 64
C_MIN_DECAY = math.log(1e-2) / 1.5
C_MAX_DECAY = math.log(1e-2) / 0.3

LANE = 128
MXU_W = 256
TM = 256
TP = 3 * TM
HALO = 8
VMEM_LIMIT = 56 * 1024 * 1024


def _cparams(n_axes):
    return pltpu.CompilerParams(dimension_semantics=("arbitrary",) * n_axes,
                                vmem_limit_bytes=VMEM_LIMIT)


def _rms(x):
    return x * lax.rsqrt(jnp.mean(x * x, axis=-1, keepdims=True) + NORM_EPS)


def _silu(g):
    return g / (1.0 + jnp.exp(-g))


def _dot(a, b):
    return jnp.dot(a, b, preferred_element_type=F32)


def _dot_nt(a, b):
    return lax.dot_general(a, b, (((1,), (1,)), ((), ())), preferred_element_type=F32)


def _rope128(x, a, b, c):
    return x * a + pltpu.roll(x, 32, 1) * b + pltpu.roll(x, LANE - 32, 1) * c


def _modnorm(x, mod_ref, gpre_ref):
    m = mod_ref[0, 0]
    return _rms(x) * gpre_ref[...] * (1.0 + m[1:2]) + m[0:1]


def _mod_kernel(c_ref, w_ref, b_ref, o_ref):
    a = _silu(c_ref[...])
    o_ref[0] = _dot(a.astype(BF16), w_ref[0].astype(BF16)) + b_ref[0]


def _modulation(cond, w_mod, b_mod):
    rows = cond.shape[0]
    return pl.pallas_call(
        _mod_kernel,
        grid=(DEPTH, 3),
        in_specs=[pl.BlockSpec((rows, D), lambda l, j: (0, 0)),
                  pl.BlockSpec((1, D, D), lambda l, j: (l, 0, j)),
                  pl.BlockSpec((1, 1, D), lambda l, j: (l, 0, j))],
        out_specs=pl.BlockSpec((1, rows, D), lambda l, j: (l, 0, j)),
        out_shape=jax.ShapeDtypeStruct((DEPTH, rows, 3 * D), F32),
        compiler_params=_cparams(2),
        name="adaln_modulation",
    )(cond, w_mod, b_mod.reshape(DEPTH, 1, 3 * D))


def _tok_spec(width, rows=TM, skip=0):
    return pl.BlockSpec((1, rows, width), lambda b, t: (b, t + skip, 0))


def _mod_spec(skip=0):
    return pl.BlockSpec((1, 1, 3, D), lambda b, t: (jnp.minimum(t + skip, 1), b, 0, 0))


def _const_spec(shape):
    nd = len(shape)
    return pl.BlockSpec(shape, lambda b, t: (0,) * nd)


def _residual_specs(dual, skip=0):
    if not dual:
        return [_tok_spec(D, TM, skip)]
    return [pl.BlockSpec((1, CTX, D), lambda b, t: (b, 0, 0)),
            pl.BlockSpec((1, TM, D), lambda b, t: (b, jnp.maximum(t - CTX // TM, 0), 0))]


def _wide_specs(dual):
    n = TP // TM
    if not dual:
        return [pl.BlockSpec((1, TM, D), functools.partial(lambda j, b, t: (b, n * t + j, 0), j)) for j in range(n)]
    last = lambda j, b, t: (b, jnp.maximum(n * t + j - CTX // TM, 0), 0)
    return ([pl.BlockSpec((1, CTX, D), lambda b, t: (b, 0, 0))]
            + [pl.BlockSpec((1, TM, D), functools.partial(last, j)) for j in range(n)])


def _wide_modnorm(refs, dual, mod_ref, gpre_ref):
    t = pl.program_id(1)
    blocks = [r[0] for r in refs[1:]] if dual else [r[0] for r in refs]
    if dual:
        blocks[0] = jnp.where(t == 0, refs[0][0], blocks[0])
    x = jnp.concatenate(blocks, axis=0)
    is_ctx = jnp.logical_and(lax.broadcasted_iota(jnp.int32, (TP, 1), 0) < CTX, t == 0)
    mc, mx = mod_ref[0, 0], mod_ref[1, 0]
    scale = jnp.where(is_ctx, mc[1:2], mx[1:2])
    shift = jnp.where(is_ctx, mc[0:1], mx[0:1])
    return _rms(x) * gpre_ref[...] * (1.0 + scale) + shift


def _both_mod_spec():
    return pl.BlockSpec((2, 1, 3, D), lambda b, t: (0, b, 0, 0))


def _resident(shape):
    nd = len(shape)
    return pl.BlockSpec(shape, lambda b, t: (0,) * nd, pipeline_mode=pl.Buffered(1))


def _residual_tile(refs, dual):
    if not dual:
        return refs[0][0]
    return jnp.where(pl.program_id(1) < CTX // TM, refs[0][0], refs[1][0])


def _finish(o, gate_ref, wout_ref, x, mod_ref, gpost_ref):
    a = o.astype(F32) * _silu(gate_ref[0].astype(F32))
    y = _dot(a.astype(BF16), wout_ref[...])
    return x + mod_ref[0, 0][2:3] * (_rms(y) * gpost_ref[...])


def _mla_proj_kernel(*refs, dual):
    nx = TP // TM + (1 if dual else 0)
    (mod_ref, gpre_ref, win_ref, gq_ref, wq_ref, gkv_ref, wkv_ref,
     ra_ref, rb_ref, rc_ref, q_ref, k_ref, v_ref, gate_ref) = refs[nx:]
    h = _wide_modnorm(refs[:nx], dual, mod_ref, gpre_ref)
    p = _dot(h.astype(BF16), win_ref[...])
    c0, c1, c2 = A_Q_RANK, A_Q_RANK + A_KV_RANK, A_Q_RANK + A_KV_RANK + LANE
    gate_ref[0] = p[:, c2:].astype(BF16)
    qn = (_rms(p[:, :c0]) * gq_ref[...]).astype(BF16)
    kvn = (_rms(p[:, c0:c1]) * gkv_ref[...]).astype(BF16)
    ra, rb, rc = ra_ref[...], rb_ref[...], rc_ref[...]
    kr = _rope128(p[:, c1:c2], ra, rb, rc).astype(BF16)
    qa = _dot(qn, wq_ref[...])
    lo = lax.broadcasted_iota(jnp.int32, (1, LANE), 1) < A_ROPE
    nope_w = A_HEADS * A_NOPE
    for pr in range(A_HEADS // 2):
        r = _rope128(qa[:, nope_w + pr * LANE:nope_w + (pr + 1) * LANE], ra, rb, rc)
        q_ref[0, 2 * pr, :, LANE:2 * LANE] = jnp.where(lo, r, 0.0).astype(BF16)
        q_ref[0, 2 * pr + 1, :, LANE:2 * LANE] = jnp.where(lo, pltpu.roll(r, LANE - A_ROPE, 1), 0.0).astype(BF16)
    for hd in range(A_HEADS):
        q_ref[0, hd, :, 0:LANE] = qa[:, hd * A_NOPE:(hd + 1) * A_NOPE].astype(BF16)
        kv = _dot(kvn, wkv_ref[:, hd * (A_NOPE + A_VDIM):(hd + 1) * (A_NOPE + A_VDIM)])
        k_ref[0, hd, :, 0:LANE] = kv[:, 0:A_NOPE].astype(BF16)
        k_ref[0, hd, :, LANE:2 * LANE] = kr
        v_ref[0, hd] = kv[:, A_NOPE:].astype(BF16)


def _mla_proj(xsrc, modl, gpre, win, gq, wq, gkv, wkv, tabs):
    dual = len(xsrc) == 2
    bsz = xsrc[0].shape[0]
    t = tabs[0].shape[0]
    xargs = [xsrc[0]] + [xsrc[1]] * (TP // TM) if dual else [xsrc[0]] * (TP // TM)
    head = lambda w: pl.BlockSpec((1, A_HEADS, TP, w), lambda b, i: (b, 0, i, 0))
    tab = pl.BlockSpec((TP, LANE), lambda b, i: (i, 0))
    return pl.pallas_call(
        functools.partial(_mla_proj_kernel, dual=dual),
        grid=(bsz, t // TP),
        in_specs=[*_wide_specs(dual), _both_mod_spec(), _const_spec((1, D)), _resident(win.shape),
                  _const_spec((1, A_Q_RANK)), _resident(wq.shape),
                  _const_spec((1, A_KV_RANK)), _resident(wkv.shape), tab, tab, tab],
        out_specs=[head(A_HEAD_PAD), head(A_HEAD_PAD), head(A_VDIM), _tok_spec(A_HEADS * A_VDIM, TP)],
        out_shape=[jax.ShapeDtypeStruct((bsz, A_HEADS, t, A_HEAD_PAD), BF16),
                   jax.ShapeDtypeStruct((bsz, A_HEADS, t, A_HEAD_PAD), BF16),
                   jax.ShapeDtypeStruct((bsz, A_HEADS, t, A_VDIM), BF16),
                   jax.ShapeDtypeStruct((bsz, t, A_HEADS * A_VDIM), BF16)],
        compiler_params=_cparams(2),
        name="mla_proj",
    )(*xargs, modl, gpre, win, gq, wq, gkv, wkv, *tabs)


def _mla_attn_kernel(*refs, dual, latents_only):
    nx = 2 if dual else 1
    q_ref, k_ref, v_ref, gate_ref, wout_ref = refs[:5]
    mod_ref, gpost_ref, xo_ref, o_ref = refs[5 + nx:]

    def attend(nk):
        s_next = _dot_nt(q_ref[0, 0], k_ref[0, 0, :nk, :])
        for hd in range(A_HEADS):
            s = s_next
            if hd + 1 < A_HEADS:
                s_next = _dot_nt(q_ref[0, hd + 1], k_ref[0, hd + 1, :nk, :])
            m = jnp.broadcast_to(jnp.max(s, axis=-1, keepdims=True), (TM, LANE))
            p = jnp.concatenate([jnp.exp2(s[:, c:c + LANE] - m) for c in range(0, nk, LANE)], axis=1)
            vx = jnp.concatenate([v_ref[0, hd, :nk, :], jnp.ones((nk, LANE), BF16)], axis=1)
            pv = _dot(p.astype(BF16), vx)
            o_ref[:, hd * A_VDIM:(hd + 1) * A_VDIM] = (pv[:, :A_VDIM] / pv[:, A_VDIM:]).astype(BF16)

    t_all = k_ref.shape[2]
    if latents_only:
        attend(t_all)
    else:
        is_ctx = pl.program_id(1) == 0
        pl.when(is_ctx)(lambda: attend(CTX))
        pl.when(jnp.logical_not(is_ctx))(lambda: attend(t_all))
    x = _residual_tile(refs[5:5 + nx], dual)
    xo_ref[0] = _finish(o_ref[...], gate_ref, wout_ref, x, mod_ref, gpost_ref)


def _mla_attn(q, k, v, gate, wout, xsrc, modl, gpost, latents_only):
    dual = len(xsrc) == 2
    bsz, nh, t, _ = q.shape
    skip = CTX // TM if latents_only else 0
    nt = t // TM - skip
    return pl.pallas_call(
        functools.partial(_mla_attn_kernel, dual=dual, latents_only=latents_only),
        grid=(bsz, nt),
        in_specs=[pl.BlockSpec((1, nh, TM, A_HEAD_PAD), lambda b, i: (b, 0, i + skip, 0)),
                  pl.BlockSpec((1, nh, t, A_HEAD_PAD), lambda b, i: (b, 0, 0, 0)),
                  pl.BlockSpec((1, nh, t, A_VDIM), lambda b, i: (b, 0, 0, 0)),
                  _tok_spec(nh * A_VDIM, skip=skip), _const_spec(wout.shape),
                  *_residual_specs(dual, skip=skip), _mod_spec(skip=skip), _const_spec((1, D))],
        out_specs=_tok_spec(D),
        out_shape=jax.ShapeDtypeStruct((bsz, nt * TM, D), F32),
        scratch_shapes=[pltpu.VMEM((TM, nh * A_VDIM), BF16)],
        input_output_aliases={} if (dual or latents_only) else {5: 0},
        compiler_params=_cparams(2),
        name="mla_attention_out",
    )(q, k, v, gate, wout, *xsrc, modl, gpost)


def _swa_proj_kernel(*refs):
    xrefs = refs[:TP // TM]
    mod_ref, gpre_ref, win_ref, ra_ref, rb_ref, rc_ref, q_ref, k_ref, v_ref, gate_ref = refs[TP // TM:]
    h = _wide_modnorm(xrefs, False, mod_ref, gpre_ref)
    p = _dot(h.astype(BF16), win_ref[...])
    ra, rb, rc = ra_ref[...], rb_ref[...], rc_ref[...]
    qw, kw = B_HEADS * B_HDIM, B_KV_HEADS * B_HDIM
    lane = lax.broadcasted_iota(jnp.int32, (1, LANE), 1)
    lo = lane < B_HDIM
    ones_blk = jnp.ones((TP, LANE), BF16)
    for j in range(B_KV_HEADS // 2):
        for g in range(B_GROUP):
            s = j * B_GROUP + g
            r = _rope128(p[:, s * LANE:(s + 1) * LANE], ra, rb, rc)
            for half in range(2):
                d = ((2 * j + half) * B_GROUP + g) * LANE
                keep = lo if half == 0 else jnp.logical_not(lo)
                q_ref[0, :, d:d + LANE] = jnp.where(keep, r, 0.0).astype(BF16)
        o = qw + j * LANE
        k_ref[0, :, j * LANE:(j + 1) * LANE] = _rope128(p[:, o:o + LANE], ra, rb, rc).astype(BF16)
        v_ref[0, :, 2 * j * LANE:(2 * j + 1) * LANE] = p[:, o + kw:o + kw + LANE].astype(BF16)
        v_ref[0, :, (2 * j + 1) * LANE:(2 * j + 2) * LANE] = ones_blk
    gate_ref[0] = p[:, qw + 2 * kw:].astype(BF16)


def _swa_proj(xs, modl, gpre, win, tabs):
    bsz, t, _ = xs.shape
    qw, kw = B_HEADS * B_HDIM, B_KV_HEADS * B_HDIM
    tab = pl.BlockSpec((TP, LANE), lambda b, i: (i, 0))
    return pl.pallas_call(
        _swa_proj_kernel,
        grid=(bsz, t // TP),
        in_specs=[*_wide_specs(False), _both_mod_spec(), _const_spec((1, D)), _resident(win.shape), tab, tab, tab],
        out_specs=[_tok_spec(2 * qw, TP), _tok_spec(kw, TP), _tok_spec(2 * kw, TP), _tok_spec(qw, TP)],
        out_shape=[jax.ShapeDtypeStruct((bsz, t, 2 * qw), BF16),
                   jax.ShapeDtypeStruct((bsz, t, kw), BF16),
                   jax.ShapeDtypeStruct((bsz, t, 2 * kw), BF16),
                   jax.ShapeDtypeStruct((bsz, t, qw), BF16)],
        compiler_params=_cparams(2),
        name="swa_proj",
    )(*[xs] * (TP // TM), modl, gpre, win, *tabs)


def _swa_attn_kernel(q_ref, k_ref, v_ref, sink_ref, gate_ref, wout_ref, x_ref, mod_ref, gpost_ref, xo_ref, o_ref):
    i = pl.program_id(1)
    n_lat = k_ref.shape[1] - CTX
    band = 3 * B_BLOCK
    lo = lax.broadcasted_iota(jnp.int32, (1, LANE), 1) < B_HDIM

    def run(with_band):
        chains = [(sub, j, half) for sub in range(TM // B_BLOCK) for j in range(B_KV_HEADS // 2) for half in range(2)]
        window, keys = {}, {}

        def block_window(sub):
            if sub not in window:
                li = (i - CTX // TM) * (TM // B_BLOCK) + sub
                start = jnp.clip((li - 1) * B_BLOCK, 0, n_lat - band)
                off = li * B_BLOCK - start
                d0 = (lax.broadcasted_iota(jnp.int32, (B_BLOCK, band), 0)
                      - lax.broadcasted_iota(jnp.int32, (B_BLOCK, band), 1))
                mask = jnp.abs(d0 + off) <= B_WINDOW
                window[sub] = (mask, pl.multiple_of(CTX + start, B_BLOCK))
            return window[sub]

        def block_keys(sub, j):
            if (sub, j) not in keys:
                kcols = slice(j * LANE, (j + 1) * LANE)
                vcols = slice(2 * j * LANE, (2 * j + 2) * LANE)
                kk = k_ref[0, 0:CTX, kcols]
                vv = v_ref[0, 0:CTX, vcols]
                if with_band:
                    kstart = block_window(sub)[1]
                    kk = jnp.concatenate([kk, k_ref[0, pl.ds(kstart, band), kcols]], axis=0)
                    vv = jnp.concatenate([vv, v_ref[0, pl.ds(kstart, band), vcols]], axis=0)
                keys[(sub, j)] = (kk, vv)
            return keys[(sub, j)]

        def scores(chain):
            sub, j, half = chain
            hk = 2 * j + half
            r0 = sub * B_BLOCK
            qs = jnp.concatenate(
                [q_ref[0, r0:r0 + B_BLOCK, (hk * B_GROUP + g) * LANE:(hk * B_GROUP + g + 1) * LANE]
                 for g in range(B_GROUP)], axis=0)
            return _dot_nt(qs, block_keys(sub, j)[0])

        outs = []
        s_next = scores(chains[0])
        for n, (sub, j, half) in enumerate(chains):
            s = s_next
            if n + 1 < len(chains):
                s_next = scores(chains[n + 1])
            sk = sink_ref[2 * j + half]
            if with_band:
                mask = block_window(sub)[0]
                s_b = jnp.concatenate(
                    [jnp.where(mask, s[g * B_BLOCK:(g + 1) * B_BLOCK, CTX:], NEG_INF) for g in range(B_GROUP)],
                    axis=0)
                s = jnp.concatenate([s[:, :CTX], s_b], axis=1)
            m = jnp.maximum(jnp.broadcast_to(jnp.max(s, axis=-1, keepdims=True), sk.shape), sk)
            p = jnp.concatenate([jnp.exp2(s[:, c:c + LANE] - m) for c in range(0, s.shape[1], LANE)], axis=1)
            pv = _dot(p.astype(BF16), block_keys(sub, j)[1])
            l = pv[:, LANE:] + jnp.exp2(sk - m)
            outs.append(pv[:, :LANE] / l)
            if half == 1:
                comb = jnp.where(lo, outs[-2], outs[-1])
                r0 = sub * B_BLOCK
                for g in range(B_GROUP):
                    s_out = j * B_GROUP + g
                    o_ref[r0:r0 + B_BLOCK, s_out * LANE:(s_out + 1) * LANE] = (
                        comb[g * B_BLOCK:(g + 1) * B_BLOCK].astype(BF16))

    is_ctx = i < CTX // TM
    pl.when(is_ctx)(lambda: run(False))
    pl.when(jnp.logical_not(is_ctx))(lambda: run(True))
    xo_ref[0] = _finish(o_ref[...], gate_ref, wout_ref, x_ref[0], mod_ref, gpost_ref)


def _swa_attn(q, k, v, sink_cols, gate, wout, xs, modl, gpost):
    bsz, t, qw2 = q.shape
    kw = k.shape[-1]
    qw = qw2 // 2
    return pl.pallas_call(
        _swa_attn_kernel,
        grid=(bsz, t // TM),
        in_specs=[pl.BlockSpec((1, TM, qw2), lambda b, i: (b, i, 0)),
                  pl.BlockSpec((1, t, kw), lambda b, i: (b, 0, 0)),
                  pl.BlockSpec((1, t, 2 * kw), lambda b, i: (b, 0, 0)),
                  pl.BlockSpec(sink_cols.shape, lambda b, i: (0, 0, 0)),
                  _tok_spec(qw), _const_spec(wout.shape), _tok_spec(D), _mod_spec(), _const_spec((1, D))],
        out_specs=_tok_spec(D),
        out_shape=jax.ShapeDtypeStruct(xs.shape, F32),
        scratch_shapes=[pltpu.VMEM((TM, qw), BF16)],
        input_output_aliases={6: 0},
        compiler_params=_cparams(2),
        name="swa_attention_out",
    )(q, k, v, sink_cols, gate, wout, xs, modl, gpost)


RADIX = 4
QP = TM // RADIX
TILE_ORDER = (3, 1, 2, 0)


def _hyena_proj_kernel(xp_ref, x_ref, xn_ref, mod_ref, gpre_ref, win_ref, cw_ref, cb_ref, perm_ref, u_ref, g_ref):
    t = pl.program_id(1)
    nt = pl.num_programs(1)
    hh = _dot(perm_ref[...], _modnorm(x_ref[0], mod_ref, gpre_ref).astype(BF16))
    h = jnp.concatenate([_modnorm(xp_ref[0], mod_ref, gpre_ref), hh, _modnorm(xn_ref[0], mod_ref, gpre_ref)], axis=0)
    p = _dot(h.astype(BF16), win_ref[...])
    cwid = 3 * C_WIDTH
    u = p[:, :cwid]
    start = {rho: HALO + k * QP for k, rho in enumerate(TILE_ORDER)}
    cls = {rho: u[start[rho]:start[rho] + QP] for rho in range(RADIX)}
    for rho in range(RADIX):
        g_ref[0, rho] = p[start[rho]:start[rho] + QP, cwid:].astype(BF16)
    before0 = pltpu.roll(u[0:HALO + QP], 1, 0)[HALO:]
    after3 = pltpu.roll(u[start[0]:], QP + HALO - 1, 0)[0:QP]
    r = lax.broadcasted_iota(jnp.int32, (QP, 1), 0)
    before0 = jnp.where(jnp.logical_and(r == 0, t <= 1), 0.0, before0)
    after3 = jnp.where(jnp.logical_and(r == QP - 1, jnp.logical_or(t == 0, t == nt - 1)), 0.0, after3)
    cw = cw_ref[...]
    cb = cb_ref[...]
    left = {0: before0, 1: cls[0], 2: cls[1], 3: cls[2]}
    right = {0: cls[1], 1: cls[2], 2: cls[3], 3: after3}
    for rho in range(RADIX):
        u_ref[0, rho] = (cb + left[rho] * cw[0:1] + cls[rho] * cw[1:2] + right[rho] * cw[2:3]).astype(BF16)


def _class_spec(width, rows=QP):
    return pl.BlockSpec((1, RADIX, rows, width), lambda b, t: (b, 0, t, 0))


def _hyena_proj(xs, modl, gpre, win, conv_w, conv_b):
    bsz, t, _ = xs.shape
    nt = t // TM
    per = TM // HALO
    last = t // HALO - 1
    part = lambda w: jax.ShapeDtypeStruct((bsz, RADIX, t // RADIX, w), BF16)
    return pl.pallas_call(
        _hyena_proj_kernel,
        grid=(bsz, nt),
        in_specs=[pl.BlockSpec((1, HALO, D), lambda b, i: (b, jnp.maximum(i * per - 1, 0), 0)),
                  _tok_spec(D),
                  pl.BlockSpec((1, HALO, D), lambda b, i: (b, jnp.minimum((i + 1) * per, last), 0)),
                  _mod_spec(), _const_spec((1, D)), _const_spec(win.shape),
                  _const_spec(conv_w.shape), _const_spec(conv_b.shape), _const_spec((TM, TM))],
        out_specs=[_class_spec(3 * C_WIDTH), _class_spec(C_WIDTH)],
        out_shape=[part(3 * C_WIDTH), part(C_WIDTH)],
        compiler_params=_cparams(2),
        name="hyena_proj",
    )(xs, xs, xs, modl, gpre, win, conv_w, conv_b, _class_perm(TILE_ORDER))


def _class_perm(order):
    r = np.arange(QP)
    src = np.concatenate([RADIX * r + rho for rho in order])
    m = np.zeros((TM, TM), np.float32)
    m[np.arange(TM), src] = 1.0
    return jnp.asarray(m, BF16)


def _hyena_out_kernel(o_ref, g_ref, wout_ref, x_ref, mod_ref, gpost_ref, perm_ref, xo_ref):
    subs = []
    for k in range(TP // TM):
        rows = slice(k * QP, (k + 1) * QP)
        a = jnp.concatenate([o_ref[0, rho, rows, :].astype(F32) * _silu(g_ref[0, rho, rows, :].astype(F32))
                             for rho in range(RADIX)], axis=0)
        subs.append(_dot(perm_ref[...], a.astype(BF16)).astype(BF16))
    y = _dot(jnp.concatenate(subs, axis=0), wout_ref[...])
    is_ctx = jnp.logical_and(lax.broadcasted_iota(jnp.int32, (TP, 1), 0) < CTX, pl.program_id(1) == 0)
    gate = jnp.where(is_ctx, mod_ref[0, 0][2:3], mod_ref[1, 0][2:3])
    xo_ref[0] = x_ref[0] + gate * (_rms(y) * gpost_ref[...])


def _hyena_out_proj(o, g, wout, xs, modl, gpost):
    bsz, t, _ = xs.shape
    cls = _class_spec(o.shape[-1], TP // RADIX)
    return pl.pallas_call(
        _hyena_out_kernel,
        grid=(bsz, t // TP),
        in_specs=[cls, cls, _const_spec(wout.shape), _tok_spec(D, TP), _both_mod_spec(),
                  _const_spec((1, D)), _const_spec((TM, TM))],
        out_specs=_tok_spec(D, TP),
        out_shape=jax.ShapeDtypeStruct(xs.shape, F32),
        input_output_aliases={3: 0},
        compiler_params=_cparams(2),
        name="hyena_out_proj_residual",
    )(o, g, wout, xs, modl, gpost, _class_perm(range(RADIX)).T)


def _filter_kernel(z_ref, w1_ref, b1_ref, fr_ref, w2_ref, b2_ref, w3_ref, dl_ref, o_ref):
    hp = lax.Precision.HIGHEST
    z = z_ref[...]
    fr = fr_ref[...]
    h = jnp.sin(fr * (jnp.dot(z, w1_ref[...], precision=hp, preferred_element_type=F32) + b1_ref[...]))
    h = jnp.sin(fr * (jnp.dot(h, w2_ref[...], precision=hp, preferred_element_type=F32) + b2_ref[...]))
    h = jnp.dot(h, w3_ref[...], precision=hp, preferred_element_type=F32)
    o_ref[...] = (h * jnp.exp(-z[:, 0:1] * dl_ref[...])).astype(o_ref.dtype)


SPEC_BLOCK = 512


def _filter_columns():
    o, cb, d, cc = np.meshgrid(np.arange(2), np.arange(C_WIDTH // SPEC_BLOCK), np.arange(2), np.arange(SPEC_BLOCK),
                               indexing="ij")
    return (d * 2 * C_WIDTH + o * C_WIDTH + cb * SPEC_BLOCK + cc).reshape(-1)


def _filters(n, w1, b1, fr, w2, b2, w3):
    t = np.linspace(0.0, 1.0, n, dtype=np.float32)[:, None]
    w = ((2.0 * math.pi / n) * np.arange(n, dtype=np.float32))[:, None].astype(np.float32)
    bands = np.linspace(1e-4, C_BANDS - 1, C_BANDS, dtype=np.float32)[None, :]
    z = np.zeros((n, LANE), np.float32)
    z[:, 0:1] = t
    z[:, 1:1 + C_BANDS] = np.cos(bands * w)
    z[:, 1 + C_BANDS:C_EMB] = -np.sin(bands * w)
    deltas = np.abs(np.linspace(C_MIN_DECAY, C_MAX_DECAY, C_WIDTH, dtype=np.float32))
    cols = _filter_columns()
    dl = deltas[cols % C_WIDTH][None, :]
    w3 = w3[:, cols]

    def pad(a, r, c):
        return jnp.zeros((r, c), F32).at[:a.shape[0], :a.shape[1]].set(a)

    tn = min(n, TM)
    nout = 4 * C_WIDTH
    cs = lambda shape: pl.BlockSpec(shape, lambda i: (0, 0))
    return pl.pallas_call(
        _filter_kernel,
        grid=(n // tn,),
        in_specs=[pl.BlockSpec((tn, LANE), lambda i: (i, 0)), cs((LANE, LANE)), cs((1, LANE)), cs((1, LANE)),
                  cs((LANE, LANE)), cs((1, LANE)), cs((LANE, nout)), cs((1, nout))],
        out_specs=pl.BlockSpec((tn, nout), lambda i: (i, 0)),
        out_shape=jax.ShapeDtypeStruct((n, nout), BF16),
        compiler_params=_cparams(1),
        name="hyena_filter_mlp",
    )(jnp.asarray(z), pad(w1, LANE, LANE), pad(b1[None], 1, LANE), pad(fr[None], 1, LANE),
      pad(w2, LANE, LANE), pad(b2[None], 1, LANE), pad(w3, LANE, nout), jnp.asarray(dl))


def _dft_matrix(n):
    f = np.arange(n, dtype=np.int64)[:, None]
    s = np.arange(n, dtype=np.int64)[None, :]
    ang = (2.0 * np.pi / (2 * n)) * ((f * s) % (2 * n)).astype(np.float64)
    cos = np.cos(ang)
    sin = np.sin(ang)
    sin[0, :] = np.where(np.arange(n) % 2 == 0, 1.0, -1.0)
    return cos.astype(np.float32), sin.astype(np.float32)


def _spectrum_kernel(a_ref, b_ref, re_ref, im_ref, *, scale):
    acc = _dot(a_ref[...], b_ref[...])
    fwd, bwd = acc[:, :SPEC_BLOCK], acc[:, SPEC_BLOCK:]
    re_ref[...] = (fwd + bwd) * scale
    im_ref[...] = (bwd - fwd) * scale


def _spectrum(a, b, bm, scale):
    m, k = a.shape
    nb = b.shape[1] // (2 * SPEC_BLOCK)
    assert m % bm == 0
    out = jax.ShapeDtypeStruct((m, nb * SPEC_BLOCK), F32)
    o_spec = pl.BlockSpec((bm, SPEC_BLOCK), lambda i, j: (i, j))
    return pl.pallas_call(
        functools.partial(_spectrum_kernel, scale=scale),
        grid=(m // bm, nb),
        in_specs=[pl.BlockSpec((bm, k), lambda i, j: (i, 0)), pl.BlockSpec((k, 2 * SPEC_BLOCK), lambda i, j: (0, j))],
        out_specs=[o_spec, o_spec],
        out_shape=[out, out],
        compiler_params=_cparams(2),
        name="filter_spectrum",
    )(a, b)


_PHASE = [(math.cos(math.pi * rho / 4), math.sin(math.pi * rho / 4),
           math.cos(3 * math.pi * rho / 4), math.sin(3 * math.pi * rho / 4)) for rho in range(RADIX)]


def _cmul(xr, xs, kr, ki):
    return xr * kr + xs * ki, xs * kr - xr * ki


def _hyena_conv_kernel(*refs, row0, q, aliased):
    nu = 3 * RADIX
    u_refs = refs[:nu]
    m_refs = refs[nu:nu + RADIX]
    mt_refs = refs[nu + RADIX:nu + 2 * RADIX]
    k_refs = refs[nu + 2 * RADIX:nu + 2 * RADIX + 8]
    ksr_ref, ksi_ref, fb_ref = refs[nu + 2 * RADIX + 8:nu + 2 * RADIX + 11]
    rest = refs[nu + 2 * RADIX + 11 + (1 if aliased else 0):]
    o_ref, z_refs = rest[0], rest[1:]
    o = pl.program_id(2)
    rows = slice(row0, row0 + q)

    @pl.when(o == 0)
    def _():
        for rho in range(RADIX):
            z_refs[rho][...] = u_refs[3 * rho][0, 0, rows, :]

    ksr, ksi = ksr_ref[...], ksi_ref[...]
    ksp = jnp.concatenate([ksr[0:3], ksi[5:6], ksr[3:4], ksi[6:7], ksr[4:5], ksi[7:8]], axis=0)
    first = lax.broadcasted_iota(jnp.int32, (8, 1), 0) == 0
    nblk = m_refs[0].shape[0]
    fbs = m_refs[0].shape[1] // 2

    def forward(j):
        return [_dot(m_refs[rho][j], z_refs[rho][...]) for rho in range(RADIX)]

    def spectral(j, fwd):
        c = [g[:fbs] for g in fwd]
        sn = [g[fbs:] for g in fwd]
        ar, as_, br, bs = c[0] + c[2], sn[0] + sn[2], c[1] + c[3], sn[1] + sn[3]
        cr, cs, dr, ds = c[0] - c[2], sn[0] - sn[2], c[1] - c[3], sn[1] - sn[3]
        k = [r[j * fbs:(j + 1) * fbs, :] for r in k_refs]
        y1r, y1s = _cmul(ar + br, as_ + bs, k[0], k[1])
        y2r, y2s = _cmul(cr - ds, cs + dr, k[2], k[3])
        y3r, y3s = _cmul(ar - br, bs - as_, k[4], k[5])
        y4r, y4s = _cmul(cr + ds, dr - cs, k[6], k[7])
        pr, ps, qr, qs = y1r + y3r, y1s - y3s, y1r - y3r, y1s + y3s
        rr, rs, tr, ts = y2r + y4r, y2s - y4s, y2r - y4r, y2s + y4s
        wr = [pr + rr, qr + ts, pr - rr, qr - ts]
        ws = [ps + rs, qs - tr, ps - rs, qs + tr]
        if j == 0:
            c0 = [x[0:8] for x in c]
            a0 = [x[0:8] for x in sn]
            y0 = (c0[0] + c0[1] + c0[2] + c0[3]) * ksp[0:1]
            yn = (c0[0] - c0[1] + c0[2] - c0[3]) * ksp[1:2]
            yhr, yhs = _cmul(c0[0] - c0[2], c0[1] - c0[3], ksp[2:3], ksp[3:4])
            xqr = sum(_PHASE[rho][0] * a0[rho] for rho in range(1, RADIX)) + a0[0]
            xqs = sum(_PHASE[rho][1] * a0[rho] for rho in range(1, RADIX))
            xgr = sum(_PHASE[rho][2] * a0[rho] for rho in range(1, RADIX)) + a0[0]
            xgs = sum(_PHASE[rho][3] * a0[rho] for rho in range(1, RADIX))
            yqr, yqs = _cmul(xqr, xqs, ksp[4:5], ksp[5:6])
            ygr, ygs = _cmul(xgr, xgs, ksp[6:7], ksp[7:8])
            turn = [yhr, yhs, -yhr, -yhs]
            for rho in range(RADIX):
                pc, psn, gc, gsn = _PHASE[rho]
                cos0 = y0 + (yn if rho % 2 == 0 else -yn) + turn[rho]
                sin0 = yqr * pc + yqs * psn + ygr * gc + ygs * gsn
                wr[rho] = jnp.concatenate([jnp.where(first, cos0, wr[rho][0:8]), wr[rho][8:]], axis=0)
                ws[rho] = jnp.concatenate([jnp.where(first, sin0, ws[rho][0:8]), ws[rho][8:]], axis=0)
        return [jnp.concatenate([wr[rho], ws[rho]], axis=0).astype(BF16) for rho in range(RADIX)]

    acc = [None] * RADIX
    fwd_next = forward(0)
    for j in range(nblk):
        fwd = fwd_next
        if j + 1 < nblk:
            fwd_next = forward(j + 1)
        w = spectral(j, fwd)
        for rho in range(RADIX):
            part = _dot(mt_refs[rho][j], w[rho])
            acc[rho] = part if acc[rho] is None else acc[rho] + part
    f_out = [acc[rho] + z_refs[rho][...].astype(F32) * fb_ref[0] for rho in range(RADIX)]

    @pl.when(o == 0)
    def _():
        for rho in range(RADIX):
            z_refs[rho][...] = (u_refs[3 * rho + 1][0, 0, rows, :].astype(F32) * f_out[rho]).astype(BF16)

    @pl.when(o == 1)
    def _():
        for rho in range(RADIX):
            if row0 > 0:
                o_ref[0, rho, 0:row0, :] = jnp.zeros((row0, o_ref.shape[3]), BF16)
            o_ref[0, rho, rows, :] = (u_refs[3 * rho + 2][0, 0, rows, :].astype(F32) * f_out[rho]).astype(BF16)


def _hyena_conv(u, mats, spec_re, spec_im, fbias, *, row0, q, block_rows, tc, prev=None):
    bsz, _, t4, _ = u.shape
    nct = C_WIDTH // tc
    u_spec = lambda rho, which: pl.BlockSpec((1, 1, block_rows, tc),
                                             functools.partial(lambda r, w, b, c, o: (b, r, 0, w * nct + c), rho, which))
    m_spec = pl.BlockSpec(mats[0].shape, lambda b, c, o: (0, 0, 0))
    mt_spec = pl.BlockSpec(mats[RADIX].shape, lambda b, c, o: (0, 0, 0))
    p_spec = lambda blk: pl.BlockSpec((q, tc), functools.partial(lambda g, b, c, o: (g, o * nct + c), blk))
    s_spec = pl.BlockSpec((8, tc), lambda b, c, o: (q, o * nct + c))
    in_specs = ([u_spec(rho, w) for rho in range(RADIX) for w in range(3)] + [m_spec] * RADIX + [mt_spec] * RADIX
                + [p_spec(blk) for g in range(4) for blk in (g, 4 + g)] + [s_spec, s_spec]
                + [pl.BlockSpec((1, 1, tc), lambda b, c, o: (o, 0, c))])
    args = [u] * (3 * RADIX) + list(mats) + [spec_re, spec_im] * 4 + [spec_re, spec_im, fbias]
    aliases = {}
    if prev is not None:
        in_specs.append(pl.BlockSpec(memory_space=pl.ANY))
        aliases = {len(args): 0}
        args.append(prev)
    o_spec = pl.BlockSpec((1, RADIX, block_rows, tc), lambda b, c, o: (b, 0, 0, c))
    return pl.pallas_call(
        functools.partial(_hyena_conv_kernel, row0=row0, q=q, aliased=prev is not None),
        grid=(bsz, nct, 2),
        in_specs=in_specs,
        out_specs=o_spec,
        out_shape=jax.ShapeDtypeStruct((bsz, RADIX, t4, C_WIDTH), BF16),
        scratch_shapes=[pltpu.VMEM((q, tc), BF16)] * RADIX,
        input_output_aliases=aliases,
        compiler_params=_cparams(3),
        name="hyena_long_conv_q%d" % q,
    )(*args)


def _radix4_matrices(n):
    q = n // RADIX
    fbs = min(q, LANE)
    nblk = q // fbs
    f = np.arange(q, dtype=np.int64)[:, None]
    r = np.arange(q, dtype=np.int64)[None, :]
    alt = np.where(np.arange(q) % 2 == 0, 1.0, -1.0)
    mats = []
    for rho in range(RADIX):
        ang = (2.0 * np.pi / (2 * n)) * ((f * (RADIX * r + rho)) % (2 * n)).astype(np.float64)
        cos, sin = np.cos(ang), np.sin(ang)
        sin[0, :] = alt
        mats.append(np.concatenate([cos.reshape(nblk, fbs, q), sin.reshape(nblk, fbs, q)], axis=1).astype(np.float32))
    tr = lambda m: np.ascontiguousarray(np.transpose(m, (0, 2, 1)))
    return tuple(jnp.asarray(m, BF16) for m in mats) + tuple(jnp.asarray(tr(m), BF16) for m in mats)


def _hyena_spectra(n, filt):
    h, q = n // 2, n // RADIX
    cos, sin = _dft_matrix(n)
    f = np.arange(q)
    groups = [f, h + f, np.maximum(n - f, 1) % n, h - f]
    special = [0.5 * cos[0:1], 0.5 * sin[0:1], cos[h:h + 1], cos[q:q + 1], cos[h + q:h + q + 1],
               sin[h:h + 1], sin[q:q + 1], sin[h + q:h + q + 1]]
    rows = np.concatenate([cos[g] for g in groups] + [sin[g] for g in groups] + special
                          + [np.zeros((LANE - 8, n), np.float32)], axis=0)
    nr = 8 * q + LANE
    return _spectrum(jnp.asarray(rows, BF16), filt, 3 * LANE if nr % (3 * LANE) == 0 else nr, 2.0 / (2 * n))


def _rope_tables(seq):
    rows = seq // GRID_W
    row = np.repeat(np.arange(rows, dtype=np.float32), GRID_W)
    col = np.tile(np.arange(GRID_W, dtype=np.float32), rows)
    per_axis = 32
    inv = (ROPE_BASE ** (-np.arange(0, per_axis, 2, dtype=np.float32) / per_axis)).astype(np.float32)
    ang = np.concatenate([row[:, None] * inv, col[:, None] * inv], axis=-1)
    cos = np.concatenate([np.ones((CTX, 32), np.float32), np.cos(ang)], axis=0)
    sin = np.concatenate([np.zeros((CTX, 32), np.float32), np.sin(ang)], axis=0)
    zero = np.zeros_like(cos)
    parts = ([cos, cos, cos, cos], [zero, sin, zero, sin], [-sin, zero, -sin, zero])
    return tuple(jnp.asarray(np.concatenate(p, axis=1), F32) for p in parts)


def _swa_head_perm():
    cols = []
    for j in range(B_KV_HEADS // 2):
        for g in range(B_GROUP):
            for hk in (2 * j, 2 * j + 1):
                h = hk * B_GROUP + g
                cols.extend(range(h * B_HDIM, (h + 1) * B_HDIM))
    return np.asarray(cols, np.int32)


def _mla_weights(w_in, w_q, w_kv):
    c1 = A_Q_RANK + A_KV_RANK
    zpad = jnp.zeros((D, LANE - A_ROPE), F32)
    win = jnp.concatenate([w_in[:, :c1 + A_ROPE], zpad, w_in[:, c1 + A_ROPE:]], axis=1)
    qscale = (A_NOPE + A_ROPE) ** -0.5 * math.log2(math.e)
    wq = w_q.reshape(A_Q_RANK, A_HEADS, A_NOPE + A_ROPE) * qscale
    wq = jnp.concatenate([wq[:, :, :A_NOPE].reshape(A_Q_RANK, -1), wq[:, :, A_NOPE:].reshape(A_Q_RANK, -1)], axis=1)
    return win.astype(BF16), wq.astype(BF16), w_kv.astype(BF16)


def kernel(x, c, ctx, c_ctx, w_mod, b_mod, g_pre, g_post, a_w_in, a_g_q, a_w_q, a_g_kv, a_w_kv, a_w_out, b_w_in, b_sink, b_w_out, c_w_in, c_conv_w, c_conv_b, c_f_w1, c_f_b1, c_f_freq, c_f_w2, c_f_b2, c_f_w3, c_filt_bias, c_w_out):
    bsz, seq, _ = x.shape
    assert ctx.shape[1] == CTX and (CTX + seq) % TP == 0 and seq % GRID_W == 0
    xsrc = (ctx, x)

    pad_rows = (-(bsz + 1)) % 8
    cond = jnp.concatenate([c, c_ctx[None], jnp.zeros((pad_rows, D), F32)], axis=0)
    mod = _modulation(cond, w_mod, b_mod)

    tabs = _rope_tables(seq)

    for layer in range(DEPTH):
        kind, j = layer % 3, layer // 3
        mx = mod[layer, :bsz].reshape(bsz, 3, D)
        mc = jnp.broadcast_to(mod[layer, bsz].reshape(1, 3, D), (bsz, 3, D))
        modl = jnp.stack([mc, mx], axis=0)
        gpre = g_pre[layer][None]
        gpost = g_post[layer][None]
        if kind == 0:
            win, wq, wkv = _mla_weights(a_w_in[j], a_w_q[j], a_w_kv[j])
            q, k, v, gate = _mla_proj(xsrc, modl, gpre, win, a_g_q[j][None], wq, a_g_kv[j][None], wkv, tabs)
            xs = _mla_attn(q, k, v, gate, a_w_out[j].astype(BF16), xsrc, modl, gpost,
                           latents_only=layer == DEPTH - 1)
        elif kind == 1:
            xs, = xsrc
            perm = _swa_head_perm()
            qw, kw = B_HEADS * B_HDIM, B_KV_HEADS * B_HDIM
            w = b_w_in[j]
            win = jnp.concatenate([w[:, :qw][:, perm] * (B_HDIM ** -0.5 * math.log2(math.e)), w[:, qw:qw + 2 * kw],
                                   w[:, qw + 2 * kw:][:, perm]], axis=1).astype(BF16)
            q, k, v, gate = _swa_proj(xs, modl, gpre, win, tabs)
            sink = (b_sink[j].astype(F32) * math.log2(math.e)).reshape(B_KV_HEADS, B_GROUP, 1, 1)
            sink_cols = jnp.broadcast_to(sink, (B_KV_HEADS, B_GROUP, B_BLOCK, LANE)).reshape(B_KV_HEADS, B_GROUP * B_BLOCK, LANE)
            xs = _swa_attn(q, k, v, sink_cols, gate, b_w_out[j][perm, :].astype(BF16), xs, modl, gpost)
        else:
            xs, = xsrc
            us, gs = _hyena_proj(xs, modl, gpre, c_w_in[j].astype(BF16), c_conv_w[j], c_conv_b[j][None])
            fargs = (c_f_w1[j], c_f_b1[j], c_f_freq[j], c_f_w2[j], c_f_b2[j], c_f_w3[j])
            fbias = c_filt_bias[j].reshape(2, 1, C_WIDTH)
            os = _hyena_conv(us, _radix4_matrices(seq), *_hyena_spectra(seq, _filters(seq, *fargs)), fbias,
                             row0=CTX // RADIX, q=seq // RADIX, block_rows=(CTX + seq) // RADIX, tc=MXU_W)
            os = _hyena_conv(us, _radix4_matrices(CTX), *_hyena_spectra(CTX, _filters(CTX, *fargs)), fbias,
                             row0=0, q=CTX // RADIX, block_rows=CTX // RADIX, tc=C_WIDTH, prev=os)
            xs = _hyena_out_proj(os, gs, c_w_out[j].astype(BF16), xs, modl, gpost)
        xsrc = (xs,)
    return xs
```

```python
import functools
import math

import numpy as np
import jax
import jax.numpy as jnp
from jax import lax
from jax.experimental import pallas as pl
from jax.experimental.pallas import tpu as pltpu

F32 = jnp.float32
BF16 = jnp.bfloat16

D = 1024
DEPTH = 4
GRID_W = 64
CTX = 256
NORM_EPS = 1e-6
ROPE_BASE = 10000.0
NEG_INF = -1e30

A_HEADS = 8
A_Q_RANK = 512
A_KV_RANK = 256
A_NOPE = 128
A_ROPE = 64
A_VDIM = 128
A_HEAD_PAD = 256

B_HEADS = 16
B_KV_HEADS = 4
B_GROUP = 4
B_HDIM = 64
B_WINDOW = 128
B_BLOCK = 128

C_WIDTH = 1024
C_BANDS = 16
C_EMB = 1 + 2 * C_BANDS
C_FFN = 64
C_MIN_DECAY = math.log(1e-2) / 1.5
C_MAX_DECAY = math.log(1e-2) / 0.3

LANE = 128
MXU_W = 256
TM = 256
TP = 3 * TM
HALO = 8
VMEM_LIMIT = 56 * 1024 * 1024


def _cparams(n_axes):
    return pltpu.CompilerParams(dimension_semantics=("arbitrary",) * n_axes,
                                vmem_limit_bytes=VMEM_LIMIT)


def _rms(x):
    return x * lax.rsqrt(jnp.mean(x * x, axis=-1, keepdims=True) + NORM_EPS)


def _silu(g):
    return g / (1.0 + jnp.exp(-g))


def _dot(a, b):
    return jnp.dot(a, b, preferred_element_type=F32)


def _dot_nt(a, b):
    return lax.dot_general(a, b, (((1,), (1,)), ((), ())), preferred_element_type=F32)


def _rope128(x, a, b, c):
    return x * a + pltpu.roll(x, 32, 1) * b + pltpu.roll(x, LANE - 32, 1) * c


def _modnorm(x, mod_ref, gpre_ref):
    m = mod_ref[0, 0]
    return _rms(x) * gpre_ref[...] * (1.0 + m[1:2]) + m[0:1]


def _mod_kernel(c_ref, w_ref, b_ref, o_ref):
    a = _silu(c_ref[...])
    o_ref[0] = _dot(a.astype(BF16), w_ref[0].astype(BF16)) + b_ref[0]


def _modulation(cond, w_mod, b_mod):
    rows = cond.shape[0]
    return pl.pallas_call(
        _mod_kernel,
        grid=(DEPTH, 3),
        in_specs=[pl.BlockSpec((rows, D), lambda l, j: (0, 0)),
                  pl.BlockSpec((1, D, D), lambda l, j: (l, 0, j)),
                  pl.BlockSpec((1, 1, D), lambda l, j: (l, 0, j))],
        out_specs=pl.BlockSpec((1, rows, D), lambda l, j: (l, 0, j)),
        out_shape=jax.ShapeDtypeStruct((DEPTH, rows, 3 * D), F32),
        compiler_params=_cparams(2),
        name="adaln_modulation",
    )(cond, w_mod, b_mod.reshape(DEPTH, 1, 3 * D))


def _tok_spec(width, rows=TM, skip=0):
    return pl.BlockSpec((1, rows, width), lambda b, t: (b, t + skip, 0))


def _mod_spec(skip=0):
    return pl.BlockSpec((1, 1, 3, D), lambda b, t: (jnp.minimum(t + skip, 1), b, 0, 0))


def _const_spec(shape):
    nd = len(shape)
    return pl.BlockSpec(shape, lambda b, t: (0,) * nd)


def _residual_specs(dual, skip=0):
    if not dual:
        return [_tok_spec(D, TM, skip)]
    return [pl.BlockSpec((1, CTX, D), lambda b, t: (b, 0, 0)),
            pl.BlockSpec((1, TM, D), lambda b, t: (b, jnp.maximum(t - CTX // TM, 0), 0))]


def _wide_specs(dual):
    n = TP // TM
    if not dual:
        return [pl.BlockSpec((1, TM, D), functools.partial(lambda j, b, t: (b, n * t + j, 0), j)) for j in range(n)]
    last = lambda j, b, t: (b, jnp.maximum(n * t + j - CTX // TM, 0), 0)
    return ([pl.BlockSpec((1, CTX, D), lambda b, t: (b, 0, 0))]
            + [pl.BlockSpec((1, TM, D), functools.partial(last, j)) for j in range(n)])


def _wide_modnorm(refs, dual, mod_ref, gpre_ref):
    t = pl.program_id(1)
    blocks = [r[0] for r in refs[1:]] if dual else [r[0] for r in refs]
    if dual:
        blocks[0] = jnp.where(t == 0, refs[0][0], blocks[0])
    x = jnp.concatenate(blocks, axis=0)
    is_ctx = jnp.logical_and(lax.broadcasted_iota(jnp.int32, (TP, 1), 0) < CTX, t == 0)
    mc, mx = mod_ref[0, 0], mod_ref[1, 0]
    scale = jnp.where(is_ctx, mc[1:2], mx[1:2])
    shift = jnp.where(is_ctx, mc[0:1], mx[0:1])
    return _rms(x) * gpre_ref[...] * (1.0 + scale) + shift


def _both_mod_spec():
    return pl.BlockSpec((2, 1, 3, D), lambda b, t: (0, b, 0, 0))


def _resident(shape):
    nd = len(shape)
    return pl.BlockSpec(shape, lambda b, t: (0,) * nd, pipeline_mode=pl.Buffered(1))


def _residual_tile(refs, dual):
    if not dual:
        return refs[0][0]
    return jnp.where(pl.program_id(1) < CTX // TM, refs[0][0], refs[1][0])


def _finish(o, gate_ref, wout_ref, x, mod_ref, gpost_ref):
    a = o.astype(F32) * _silu(gate_ref[0].astype(F32))
    y = _dot(a.astype(BF16), wout_ref[...])
    return x + mod_ref[0, 0][2:3] * (_rms(y) * gpost_ref[...])


def _mla_proj_kernel(*refs, dual):
    nx = TP // TM + (1 if dual else 0)
    (mod_ref, gpre_ref, win_ref, gq_ref, wq_ref, gkv_ref, wkv_ref,
     ra_ref, rb_ref, rc_ref, q_ref, k_ref, v_ref, gate_ref) = refs[nx:]
    h = _wide_modnorm(refs[:nx], dual, mod_ref, gpre_ref)
    p = _dot(h.astype(BF16), win_ref[...])
    c0, c1, c2 = A_Q_RANK, A_Q_RANK + A_KV_RANK, A_Q_RANK + A_KV_RANK + LANE
    gate_ref[0] = p[:, c2:].astype(BF16)
    qn = (_rms(p[:, :c0]) * gq_ref[...]).astype(BF16)
    kvn = (_rms(p[:, c0:c1]) * gkv_ref[...]).astype(BF16)
    ra, rb, rc = ra_ref[...], rb_ref[...], rc_ref[...]
    kr = _rope128(p[:, c1:c2], ra, rb, rc).astype(BF16)
    qa = _dot(qn, wq_ref[...])
    lo = lax.broadcasted_iota(jnp.int32, (1, LANE), 1) < A_ROPE
    nope_w = A_HEADS * A_NOPE
    for pr in range(A_HEADS // 2):
        r = _rope128(qa[:, nope_w + pr * LANE:nope_w + (pr + 1) * LANE], ra, rb, rc)
        q_ref[0, 2 * pr, :, LANE:2 * LANE] = jnp.where(lo, r, 0.0).astype(BF16)
        q_ref[0, 2 * pr + 1, :, LANE:2 * LANE] = jnp.where(lo, pltpu.roll(r, LANE - A_ROPE, 1), 0.0).astype(BF16)
    for hd in range(A_HEADS):
        q_ref[0, hd, :, 0:LANE] = qa[:, hd * A_NOPE:(hd + 1) * A_NOPE].astype(BF16)
        kv = _dot(kvn, wkv_ref[:, hd * (A_NOPE + A_VDIM):(hd + 1) * (A_NOPE + A_VDIM)])
        k_ref[0, hd, :, 0:LANE] = kv[:, 0:A_NOPE].astype(BF16)
        k_ref[0, hd, :, LANE:2 * LANE] = kr
        v_ref[0, hd] = kv[:, A_NOPE:].astype(BF16)


def _mla_proj(xsrc, modl, gpre, win, gq, wq, gkv, wkv, tabs):
    dual = len(xsrc) == 2
    bsz = xsrc[0].shape[0]
    t = tabs[0].shape[0]
    xargs = [xsrc[0]] + [xsrc[1]] * (TP // TM) if dual else [xsrc[0]] * (TP // TM)
    head = lambda w: pl.BlockSpec((1, A_HEADS, TP, w), lambda b, i: (b, 0, i, 0))
    tab = pl.BlockSpec((TP, LANE), lambda b, i: (i, 0))
    return pl.pallas_call(
        functools.partial(_mla_proj_kernel, dual=dual),
        grid=(bsz, t // TP),
        in_specs=[*_wide_specs(dual), _both_mod_spec(), _const_spec((1, D)), _resident(win.shape),
                  _const_spec((1, A_Q_RANK)), _resident(wq.shape),
                  _const_spec((1, A_KV_RANK)), _resident(wkv.shape), tab, tab, tab],
        out_specs=[head(A_HEAD_PAD), head(A_HEAD_PAD), head(A_VDIM), _tok_spec(A_HEADS * A_VDIM, TP)],
        out_shape=[jax.ShapeDtypeStruct((bsz, A_HEADS, t, A_HEAD_PAD), BF16),
                   jax.ShapeDtypeStruct((bsz, A_HEADS, t, A_HEAD_PAD), BF16),
                   jax.ShapeDtypeStruct((bsz, A_HEADS, t, A_VDIM), BF16),
                   jax.ShapeDtypeStruct((bsz, t, A_HEADS * A_VDIM), BF16)],
        compiler_params=_cparams(2),
        name="mla_proj",
    )(*xargs, modl, gpre, win, gq, wq, gkv, wkv, *tabs)


def _mla_attn_kernel(*refs, dual, latents_only):
    nx = 2 if dual else 1
    q_ref, k_ref, v_ref, gate_ref, wout_ref = refs[:5]
    mod_ref, gpost_ref, xo_ref, o_ref = refs[5 + nx:]

    def attend(nk):
        s_next = _dot_nt(q_ref[0, 0], k_ref[0, 0, :nk, :])
        for hd in range(A_HEADS):
            s = s_next
            if hd + 1 < A_HEADS:
                s_next = _dot_nt(q_ref[0, hd + 1], k_ref[0, hd + 1, :nk, :])
            m = jnp.broadcast_to(jnp.max(s, axis=-1, keepdims=True), (TM, LANE))
            p = jnp.concatenate([jnp.exp2(s[:, c:c + LANE] - m) for c in range(0, nk, LANE)], axis=1)
            vx = jnp.concatenate([v_ref[0, hd, :nk, :], jnp.ones((nk, LANE), BF16)], axis=1)
            pv = _dot(p.astype(BF16), vx)
            o_ref[:, hd * A_VDIM:(hd + 1) * A_VDIM] = (pv[:, :A_VDIM] / pv[:, A_VDIM:]).astype(BF16)

    t_all = k_ref.shape[2]
    if latents_only:
        attend(t_all)
    else:
        is_ctx = pl.program_id(1) == 0
        pl.when(is_ctx)(lambda: attend(CTX))
        pl.when(jnp.logical_not(is_ctx))(lambda: attend(t_all))
    x = _residual_tile(refs[5:5 + nx], dual)
    xo_ref[0] = _finish(o_ref[...], gate_ref, wout_ref, x, mod_ref, gpost_ref)


def _mla_attn(q, k, v, gate, wout, xsrc, modl, gpost, latents_only):
    dual = len(xsrc) == 2
    bsz, nh, t, _ = q.shape
    skip = CTX // TM if latents_only else 0
    nt = t // TM - skip
    return pl.pallas_call(
        functools.partial(_mla_attn_kernel, dual=dual, latents_only=latents_only),
        grid=(bsz, nt),
        in_specs=[pl.BlockSpec((1, nh, TM, A_HEAD_PAD), lambda b, i: (b, 0, i + skip, 0)),
                  pl.BlockSpec((1, nh, t, A_HEAD_PAD), lambda b, i: (b, 0, 0, 0)),
                  pl.BlockSpec((1, nh, t, A_VDIM), lambda b, i: (b, 0, 0, 0)),
                  _tok_spec(nh * A_VDIM, skip=skip), _const_spec(wout.shape),
                  *_residual_specs(dual, skip=skip), _mod_spec(skip=skip), _const_spec((1, D))],
        out_specs=_tok_spec(D),
        out_shape=jax.ShapeDtypeStruct((bsz, nt * TM, D), F32),
        scratch_shapes=[pltpu.VMEM((TM, nh * A_VDIM), BF16)],
        input_output_aliases={} if (dual or latents_only) else {5: 0},
        compiler_params=_cparams(2),
        name="mla_attention_out",
    )(q, k, v, gate, wout, *xsrc, modl, gpost)


def _swa_proj_kernel(*refs):
    xrefs = refs[:TP // TM]
    mod_ref, gpre_ref, win_ref, ra_ref, rb_ref, rc_ref, q_ref, k_ref, v_ref, gate_ref = refs[TP // TM:]
    h = _wide_modnorm(xrefs, False, mod_ref, gpre_ref)
    p = _dot(h.astype(BF16), win_ref[...])
    ra, rb, rc = ra_ref[...], rb_ref[...], rc_ref[...]
    qw, kw = B_HEADS * B_HDIM, B_KV_HEADS * B_HDIM
    lane = lax.broadcasted_iota(jnp.int32, (1, LANE), 1)
    lo = lane < B_HDIM
    ones_blk = jnp.ones((TP, LANE), BF16)
    for j in range(B_KV_HEADS // 2):
        for g in range(B_GROUP):
            s = j * B_GROUP + g
            r = _rope128(p[:, s * LANE:(s + 1) * LANE], ra, rb, rc)
            for half in range(2):
                d = ((2 * j + half) * B_GROUP + g) * LANE
                keep = lo if half == 0 else jnp.logical_not(lo)
                q_ref[0, :, d:d + LANE] = jnp.where(keep, r, 0.0).astype(BF16)
        o = qw + j * LANE
        k_ref[0, :, j * LANE:(j + 1) * LANE] = _rope128(p[:, o:o + LANE], ra, rb, rc).astype(BF16)
        v_ref[0, :, 2 * j * LANE:(2 * j + 1) * LANE] = p[:, o + kw:o + kw + LANE].astype(BF16)
        v_ref[0, :, (2 * j + 1) * LANE:(2 * j + 2) * LANE] = ones_blk
    gate_ref[0] = p[:, qw + 2 * kw:].astype(BF16)


def _swa_proj(xs, modl, gpre, win, tabs):
    bsz, t, _ = xs.shape
    qw, kw = B_HEADS * B_HDIM, B_KV_HEADS * B_HDIM
    tab = pl.BlockSpec((TP, LANE), lambda b, i: (i, 0))
    return pl.pallas_call(
        _swa_proj_kernel,
        grid=(bsz, t // TP),
        in_specs=[*_wide_specs(False), _both_mod_spec(), _const_spec((1, D)), _resident(win.shape), tab, tab, tab],
        out_specs=[_tok_spec(2 * qw, TP), _tok_spec(kw, TP), _tok_spec(2 * kw, TP), _tok_spec(qw, TP)],
        out_shape=[jax.ShapeDtypeStruct((bsz, t, 2 * qw), BF16),
                   jax.ShapeDtypeStruct((bsz, t, kw), BF16),
                   jax.ShapeDtypeStruct((bsz, t, 2 * kw), BF16),
                   jax.ShapeDtypeStruct((bsz, t, qw), BF16)],
        compiler_params=_cparams(2),
        name="swa_proj",
    )(*[xs] * (TP // TM), modl, gpre, win, *tabs)


def _swa_attn_kernel(q_ref, k_ref, v_ref, sink_ref, gate_ref, wout_ref, x_ref, mod_ref, gpost_ref, xo_ref, o_ref):
    i = pl.program_id(1)
    n_lat = k_ref.shape[1] - CTX
    band = 3 * B_BLOCK
    lo = lax.broadcasted_iota(jnp.int32, (1, LANE), 1) < B_HDIM

    def run(with_band):
        chains = [(sub, j, half) for sub in range(TM // B_BLOCK) for j in range(B_KV_HEADS // 2) for half in range(2)]
        window, keys = {}, {}

        def block_window(sub):
            if sub not in window:
                li = (i - CTX // TM) * (TM // B_BLOCK) + sub
                start = jnp.clip((li - 1) * B_BLOCK, 0, n_lat - band)
                off = li * B_BLOCK - start
                d0 = (lax.broadcasted_iota(jnp.int32, (B_BLOCK, band), 0)
                      - lax.broadcasted_iota(jnp.int32, (B_BLOCK, band), 1))
                mask = jnp.abs(d0 + off) <= B_WINDOW
                window[sub] = (mask, pl.multiple_of(CTX + start, B_BLOCK))
            return window[sub]

        def block_keys(sub, j):
            if (sub, j) not in keys:
                kcols = slice(j * LANE, (j + 1) * LANE)
                vcols = slice(2 * j * LANE, (2 * j + 2) * LANE)
                kk = k_ref[0, 0:CTX, kcols]
                vv = v_ref[0, 0:CTX, vcols]
                if with_band:
                    kstart = block_window(sub)[1]
                    kk = jnp.concatenate([kk, k_ref[0, pl.ds(kstart, band), kcols]], axis=0)
                    vv = jnp.concatenate([vv, v_ref[0, pl.ds(kstart, band), vcols]], axis=0)
                keys[(sub, j)] = (kk, vv)
            return keys[(sub, j)]

        def scores(chain):
            sub, j, half = chain
            hk = 2 * j + half
            r0 = sub * B_BLOCK
            qs = jnp.concatenate(
                [q_ref[0, r0:r0 + B_BLOCK, (hk * B_GROUP + g) * LANE:(hk * B_GROUP + g + 1) * LANE]
                 for g in range(B_GROUP)], axis=0)
            return _dot_nt(qs, block_keys(sub, j)[0])

        outs = []
        s_next = scores(chains[0])
        for n, (sub, j, half) in enumerate(chains):
            s = s_next
            if n + 1 < len(chains):
                s_next = scores(chains[n + 1])
            sk = sink_ref[2 * j + half]
            if with_band:
                mask = block_window(sub)[0]
                s_b = jnp.concatenate(
                    [jnp.where(mask, s[g * B_BLOCK:(g + 1) * B_BLOCK, CTX:], NEG_INF) for g in range(B_GROUP)],
                    axis=0)
                s = jnp.concatenate([s[:, :CTX], s_b], axis=1)
            m = jnp.maximum(jnp.broadcast_to(jnp.max(s, axis=-1, keepdims=True), sk.shape), sk)
            p = jnp.concatenate([jnp.exp2(s[:, c:c + LANE] - m) for c in range(0, s.shape[1], LANE)], axis=1)
            pv = _dot(p.astype(BF16), block_keys(sub, j)[1])
            l = pv[:, LANE:] + jnp.exp2(sk - m)
            outs.append(pv[:, :LANE] / l)
            if half == 1:
                comb = jnp.where(lo, outs[-2], outs[-1])
                r0 = sub * B_BLOCK
                for g in range(B_GROUP):
                    s_out = j * B_GROUP + g
                    o_ref[r0:r0 + B_BLOCK, s_out * LANE:(s_out + 1) * LANE] = (
                        comb[g * B_BLOCK:(g + 1) * B_BLOCK].astype(BF16))

    is_ctx = i < CTX // TM
    pl.when(is_ctx)(lambda: run(False))
    pl.when(jnp.logical_not(is_ctx))(lambda: run(True))
    xo_ref[0] = _finish(o_ref[...], gate_ref, wout_ref, x_ref[0], mod_ref, gpost_ref)


def _swa_attn(q, k, v, sink_cols, gate, wout, xs, modl, gpost):
    bsz, t, qw2 = q.shape
    kw = k.shape[-1]
    qw = qw2 // 2
    return pl.pallas_call(
        _swa_attn_kernel,
        grid=(bsz, t // TM),
        in_specs=[pl.BlockSpec((1, TM, qw2), lambda b, i: (b, i, 0)),
                  pl.BlockSpec((1, t, kw), lambda b, i: (b, 0, 0)),
                  pl.BlockSpec((1, t, 2 * kw), lambda b, i: (b, 0, 0)),
                  pl.BlockSpec(sink_cols.shape, lambda b, i: (0, 0, 0)),
                  _tok_spec(qw), _const_spec(wout.shape), _tok_spec(D), _mod_spec(), _const_spec((1, D))],
        out_specs=_tok_spec(D),
        out_shape=jax.ShapeDtypeStruct(xs.shape, F32),
        scratch_shapes=[pltpu.VMEM((TM, qw), BF16)],
        input_output_aliases={6: 0},
        compiler_params=_cparams(2),
        name="swa_attention_out",
    )(q, k, v, sink_cols, gate, wout, xs, modl, gpost)


RADIX = 4
QP = TM // RADIX
TILE_ORDER = (3, 1, 2, 0)


def _hyena_proj_kernel(xp_ref, x_ref, xn_ref, mod_ref, gpre_ref, win_ref, cw_ref, cb_ref, perm_ref, *out_refs):
    u_refs, g_refs = out_refs[:RADIX], out_refs[RADIX:]
    i = pl.program_id(1)
    ntile = pl.num_programs(1) * (TP // TM)
    cwid = 3 * C_WIDTH
    cw = cw_ref[...]
    cb = cb_ref[...]
    r = lax.broadcasted_iota(jnp.int32, (QP, 1), 0)
    mc, mx = mod_ref[0, 0], mod_ref[1, 0]

    def modnorm(x, ctx_rows):
        m = jnp.where(ctx_rows, mc, mx)
        return _rms(x) * gpre_ref[...] * (1.0 + m[1:2]) + m[0:1]

    for k in range(TP // TM):
        t = i * (TP // TM) + k
        is_ctx = t == 0
        x = x_ref[0, k * TM:(k + 1) * TM, :]
        xp = xp_ref[0] if k == 0 else x_ref[0, k * TM - HALO:k * TM, :]
        xn = xn_ref[0] if k == TP // TM - 1 else x_ref[0, (k + 1) * TM:(k + 1) * TM + HALO, :]
        hh = _dot(perm_ref[...], modnorm(x, is_ctx).astype(BF16))
        h = jnp.concatenate([modnorm(xp, is_ctx), hh, modnorm(xn, is_ctx)], axis=0)
        p = _dot(h.astype(BF16), win_ref[...])
        u = p[:, :cwid]
        start = {rho: HALO + j * QP for j, rho in enumerate(TILE_ORDER)}
        cls = {rho: u[start[rho]:start[rho] + QP] for rho in range(RADIX)}
        rows = slice(k * QP, (k + 1) * QP)
        for rho in range(RADIX):
            g_refs[rho][0, rows, :] = p[start[rho]:start[rho] + QP, cwid:].astype(BF16)
        before0 = pltpu.roll(u[0:HALO + QP], 1, 0)[HALO:]
        after3 = pltpu.roll(u[start[0]:], QP + HALO - 1, 0)[0:QP]
        before0 = jnp.where(jnp.logical_and(r == 0, t <= 1), 0.0, before0)
        after3 = jnp.where(jnp.logical_and(r == QP - 1, jnp.logical_or(t == 0, t == ntile - 1)), 0.0, after3)
        left = {0: before0, 1: cls[0], 2: cls[1], 3: cls[2]}
        right = {0: cls[1], 1: cls[2], 2: cls[3], 3: after3}
        for rho in range(RADIX):
            u_refs[rho][0, rows, :] = (cb + left[rho] * cw[0:1] + cls[rho] * cw[1:2] + right[rho] * cw[2:3]).astype(BF16)


def _class_spec(width):
    return pl.BlockSpec((1, TP // RADIX, width), lambda b, t: (b, t, 0))


def _hyena_proj(xs, modl, gpre, win, conv_w, conv_b):
    bsz, t, _ = xs.shape
    per = TP // HALO
    last = t // HALO - 1
    part = lambda w: jax.ShapeDtypeStruct((bsz, t // RADIX, w), BF16)
    return pl.pallas_call(
        _hyena_proj_kernel,
        grid=(bsz, t // TP),
        in_specs=[pl.BlockSpec((1, HALO, D), lambda b, i: (b, jnp.maximum(i * per - 1, 0), 0)),
                  _tok_spec(D, TP),
                  pl.BlockSpec((1, HALO, D), lambda b, i: (b, jnp.minimum((i + 1) * per, last), 0)),
                  _both_mod_spec(), _const_spec((1, D)), _resident(win.shape),
                  _const_spec(conv_w.shape), _const_spec(conv_b.shape), _const_spec((TM, TM))],
        out_specs=[_class_spec(3 * C_WIDTH)] * RADIX + [_class_spec(C_WIDTH)] * RADIX,
        out_shape=[part(3 * C_WIDTH)] * RADIX + [part(C_WIDTH)] * RADIX,
        compiler_params=_cparams(2),
        name="hyena_proj",
    )(xs, xs, xs, modl, gpre, win, conv_w, conv_b, _class_perm(TILE_ORDER))


def _class_perm(order):
    r = np.arange(QP)
    src = np.concatenate([RADIX * r + rho for rho in order])
    m = np.zeros((TM, TM), np.float32)
    m[np.arange(TM), src] = 1.0
    return jnp.asarray(m, BF16)


def _hyena_out_kernel(*refs):
    o_refs, g_refs = refs[:RADIX], refs[RADIX:2 * RADIX]
    wout_ref, x_ref, mod_ref, gpost_ref, perm_ref, xo_ref = refs[2 * RADIX:]
    subs = []
    for k in range(TP // TM):
        rows = slice(k * QP, (k + 1) * QP)
        a = jnp.concatenate([o_refs[rho][0, rows, :].astype(F32) * _silu(g_refs[rho][0, rows, :].astype(F32))
                             for rho in range(RADIX)], axis=0)
        subs.append(_dot(perm_ref[...], a.astype(BF16)).astype(BF16))
    y = _dot(jnp.concatenate(subs, axis=0), wout_ref[...])
    is_ctx = jnp.logical_and(lax.broadcasted_iota(jnp.int32, (TP, 1), 0) < CTX, pl.program_id(1) == 0)
    gate = jnp.where(is_ctx, mod_ref[0, 0][2:3], mod_ref[1, 0][2:3])
    xo_ref[0] = x_ref[0] + gate * (_rms(y) * gpost_ref[...])


def _hyena_out_proj(os, gs, wout, xs, modl, gpost):
    bsz, t, _ = xs.shape
    w = os[0].shape[-1]
    cls = pl.BlockSpec((1, TP // RADIX, w), lambda b, i: (b, i, 0))
    return pl.pallas_call(
        _hyena_out_kernel,
        grid=(bsz, t // TP),
        in_specs=[cls] * (2 * RADIX) + [_const_spec(wout.shape), _tok_spec(D, TP), _both_mod_spec(),
                                        _const_spec((1, D)), _const_spec((TM, TM))],
        out_specs=_tok_spec(D, TP),
        out_shape=jax.ShapeDtypeStruct(xs.shape, F32),
        input_output_aliases={2 * RADIX + 1: 0},
        compiler_params=_cparams(2),
        name="hyena_out_proj_residual",
    )(*os, *gs, wout, xs, modl, gpost, _class_perm(range(RADIX)).T)


def _filter_kernel(z_ref, w1_ref, b1_ref, fr_ref, w2_ref, b2_ref, w3_ref, dl_ref, o_ref):
    hp = lax.Precision.HIGHEST
    z = z_ref[...]
    fr = fr_ref[...]
    h = jnp.sin(fr * (jnp.dot(z, w1_ref[...], precision=hp, preferred_element_type=F32) + b1_ref[...]))
    h = jnp.sin(fr * (jnp.dot(h, w2_ref[...], precision=hp, preferred_element_type=F32) + b2_ref[...]))
    h = jnp.dot(h, w3_ref[...], precision=hp, preferred_element_type=F32)
    o_ref[...] = (h * jnp.exp(-z[:, 0:1] * dl_ref[...])).astype(o_ref.dtype)


SPEC_BLOCK = 512


def _filter_columns():
    o, cb, d, cc = np.meshgrid(np.arange(2), np.arange(C_WIDTH // SPEC_BLOCK), np.arange(2), np.arange(SPEC_BLOCK),
                               indexing="ij")
    return (d * 2 * C_WIDTH + o * C_WIDTH + cb * SPEC_BLOCK + cc).reshape(-1)


def _filters(n, w1, b1, fr, w2, b2, w3):
    t = np.linspace(0.0, 1.0, n, dtype=np.float32)[:, None]
    w = ((2.0 * math.pi / n) * np.arange(n, dtype=np.float32))[:, None].astype(np.float32)
    bands = np.linspace(1e-4, C_BANDS - 1, C_BANDS, dtype=np.float32)[None, :]
    z = np.zeros((n, LANE), np.float32)
    z[:, 0:1] = t
    z[:, 1:1 + C_BANDS] = np.cos(bands * w)
    z[:, 1 + C_BANDS:C_EMB] = -np.sin(bands * w)
    deltas = np.abs(np.linspace(C_MIN_DECAY, C_MAX_DECAY, C_WIDTH, dtype=np.float32))
    cols = _filter_columns()
    dl = deltas[cols % C_WIDTH][None, :]
    w3 = w3[:, cols]

    def pad(a, r, c):
        return jnp.zeros((r, c), F32).at[:a.shape[0], :a.shape[1]].set(a)

    tn = min(n, TM)
    nout = 4 * C_WIDTH
    cs = lambda shape: pl.BlockSpec(shape, lambda i: (0, 0))
    return pl.pallas_call(
        _filter_kernel,
        grid=(n // tn,),
        in_specs=[pl.BlockSpec((tn, LANE), lambda i: (i, 0)), cs((LANE, LANE)), cs((1, LANE)), cs((1, LANE)),
                  cs((LANE, LANE)), cs((1, LANE)), cs((LANE, nout)), cs((1, nout))],
        out_specs=pl.BlockSpec((tn, nout), lambda i: (i, 0)),
        out_shape=jax.ShapeDtypeStruct((n, nout), BF16),
        compiler_params=_cparams(1),
        name="hyena_filter_mlp",
    )(jnp.asarray(z), pad(w1, LANE, LANE), pad(b1[None], 1, LANE), pad(fr[None], 1, LANE),
      pad(w2, LANE, LANE), pad(b2[None], 1, LANE), pad(w3, LANE, nout), jnp.asarray(dl))


def _dft_matrix(n):
    f = np.arange(n, dtype=np.int64)[:, None]
    s = np.arange(n, dtype=np.int64)[None, :]
    ang = (2.0 * np.pi / (2 * n)) * ((f * s) % (2 * n)).astype(np.float64)
    cos = np.cos(ang)
    sin = np.sin(ang)
    sin[0, :] = np.where(np.arange(n) % 2 == 0, 1.0, -1.0)
    return cos.astype(np.float32), sin.astype(np.float32)


def _spectrum_kernel(a_ref, b_ref, re_ref, im_ref, *, scale):
    acc = _dot(a_ref[...], b_ref[...])
    fwd, bwd = acc[:, :SPEC_BLOCK], acc[:, SPEC_BLOCK:]
    re_ref[...] = (fwd + bwd) * scale
    im_ref[...] = (bwd - fwd) * scale


def _spectrum(a, b, bm, scale):
    m, k = a.shape
    nb = b.shape[1] // (2 * SPEC_BLOCK)
    assert m % bm == 0
    out = jax.ShapeDtypeStruct((m, nb * SPEC_BLOCK), F32)
    o_spec = pl.BlockSpec((bm, SPEC_BLOCK), lambda i, j: (i, j))
    return pl.pallas_call(
        functools.partial(_spectrum_kernel, scale=scale),
        grid=(m // bm, nb),
        in_specs=[pl.BlockSpec((bm, k), lambda i, j: (i, 0)), pl.BlockSpec((k, 2 * SPEC_BLOCK), lambda i, j: (0, j))],
        out_specs=[o_spec, o_spec],
        out_shape=[out, out],
        compiler_params=_cparams(2),
        name="filter_spectrum",
    )(a, b)


_PHASE = [(math.cos(math.pi * rho / 4), math.sin(math.pi * rho / 4),
           math.cos(3 * math.pi * rho / 4), math.sin(3 * math.pi * rho / 4)) for rho in range(RADIX)]


def _cmul(xr, xs, kr, ki):
    return xr * kr + xs * ki, xs * kr - xr * ki


def _hyena_conv_kernel(*refs, row0, q, aliased):
    nu = 3 * RADIX
    u_refs = refs[:nu]
    m_refs = refs[nu:nu + RADIX]
    mt_refs = refs[nu + RADIX:nu + 2 * RADIX]
    k_refs = refs[nu + 2 * RADIX:nu + 2 * RADIX + 8]
    ksr_ref, ksi_ref, fb_ref = refs[nu + 2 * RADIX + 8:nu + 2 * RADIX + 11]
    rest = refs[nu + 2 * RADIX + 11 + (RADIX if aliased else 0):]
    o_refs, z_refs = rest[:RADIX], rest[RADIX:]
    o = pl.program_id(2)
    rows = slice(row0, row0 + q)

    @pl.when(o == 0)
    def _():
        for rho in range(RADIX):
            z_refs[rho][...] = u_refs[3 * rho][0, rows, :]

    ksr, ksi = ksr_ref[...], ksi_ref[...]
    ksp = jnp.concatenate([ksr[0:3], ksi[5:6], ksr[3:4], ksi[6:7], ksr[4:5], ksi[7:8]], axis=0)
    first = lax.broadcasted_iota(jnp.int32, (8, 1), 0) == 0
    nblk = m_refs[0].shape[0]
    fbs = m_refs[0].shape[1] // 2

    def forward(j):
        return [_dot(m_refs[rho][j], z_refs[rho][...]) for rho in range(RADIX)]

    def spectral(j, fwd):
        c = [g[:fbs] for g in fwd]
        sn = [g[fbs:] for g in fwd]
        ar, as_, br, bs = c[0] + c[2], sn[0] + sn[2], c[1] + c[3], sn[1] + sn[3]
        cr, cs, dr, ds = c[0] - c[2], sn[0] - sn[2], c[1] - c[3], sn[1] - sn[3]
        k = [r[j * fbs:(j + 1) * fbs, :] for r in k_refs]
        y1r, y1s = _cmul(ar + br, as_ + bs, k[0], k[1])
        y2r, y2s = _cmul(cr - ds, cs + dr, k[2], k[3])
        y3r, y3s = _cmul(ar - br, bs - as_, k[4], k[5])
        y4r, y4s = _cmul(cr + ds, dr - cs, k[6], k[7])
        pr, ps, qr, qs = y1r + y3r, y1s - y3s, y1r - y3r, y1s + y3s
        rr, rs, tr, ts = y2r + y4r, y2s - y4s, y2r - y4r, y2s + y4s
        wr = [pr + rr, qr + ts, pr - rr, qr - ts]
        ws = [ps + rs, qs - tr, ps - rs, qs + tr]
        if j == 0:
            c0 = [x[0:8] for x in c]
            a0 = [x[0:8] for x in sn]
            y0 = (c0[0] + c0[1] + c0[2] + c0[3]) * ksp[0:1]
            yn = (c0[0] - c0[1] + c0[2] - c0[3]) * ksp[1:2]
            yhr, yhs = _cmul(c0[0] - c0[2], c0[1] - c0[3], ksp[2:3], ksp[3:4])
            xqr = sum(_PHASE[rho][0] * a0[rho] for rho in range(1, RADIX)) + a0[0]
            xqs = sum(_PHASE[rho][1] * a0[rho] for rho in range(1, RADIX))
            xgr = sum(_PHASE[rho][2] * a0[rho] for rho in range(1, RADIX)) + a0[0]
            xgs = sum(_PHASE[rho][3] * a0[rho] for rho in range(1, RADIX))
            yqr, yqs = _cmul(xqr, xqs, ksp[4:5], ksp[5:6])
            ygr, ygs = _cmul(xgr, xgs, ksp[6:7], ksp[7:8])
            turn = [yhr, yhs, -yhr, -yhs]
            for rho in range(RADIX):
                pc, psn, gc, gsn = _PHASE[rho]
                cos0 = y0 + (yn if rho % 2 == 0 else -yn) + turn[rho]
                sin0 = yqr * pc + yqs * psn + ygr * gc + ygs * gsn
                wr[rho] = jnp.concatenate([jnp.where(first, cos0, wr[rho][0:8]), wr[rho][8:]], axis=0)
                ws[rho] = jnp.concatenate([jnp.where(first, sin0, ws[rho][0:8]), ws[rho][8:]], axis=0)
        return [jnp.concatenate([wr[rho], ws[rho]], axis=0).astype(BF16) for rho in range(RADIX)]

    acc = [None] * RADIX
    fwd_next = forward(0)
    for j in range(nblk):
        fwd = fwd_next
        if j + 1 < nblk:
            fwd_next = forward(j + 1)
        w = spectral(j, fwd)
        for rho in range(RADIX):
            part = _dot(mt_refs[rho][j], w[rho])
            acc[rho] = part if acc[rho] is None else acc[rho] + part
    f_out = [acc[rho] + z_refs[rho][...].astype(F32) * fb_ref[0] for rho in range(RADIX)]

    @pl.when(o == 0)
    def _():
        for rho in range(RADIX):
            z_refs[rho][...] = (u_refs[3 * rho + 1][0, rows, :].astype(F32) * f_out[rho]).astype(BF16)

    @pl.when(o == 1)
    def _():
        for rho in range(RADIX):
            if row0 > 0:
                o_refs[rho][0, 0:row0, :] = jnp.zeros((row0, o_refs[rho].shape[2]), BF16)
            o_refs[rho][0, rows, :] = (u_refs[3 * rho + 2][0, rows, :].astype(F32) * f_out[rho]).astype(BF16)


def _hyena_conv(us, mats, spec_re, spec_im, fbias, *, row0, q, block_rows, tc, prev=None):
    bsz, t4, _ = us[0].shape
    nct = C_WIDTH // tc
    u_spec = lambda which: pl.BlockSpec((1, block_rows, tc), lambda b, c, o: (b, 0, which * nct + c))
    m_spec = pl.BlockSpec(mats[0].shape, lambda b, c, o: (0, 0, 0))
    mt_spec = pl.BlockSpec(mats[RADIX].shape, lambda b, c, o: (0, 0, 0))
    p_spec = lambda blk: pl.BlockSpec((q, tc), functools.partial(lambda g, b, c, o: (g, o * nct + c), blk))
    s_spec = pl.BlockSpec((8, tc), lambda b, c, o: (q, o * nct + c))
    in_specs = ([u_spec(w) for _ in range(RADIX) for w in range(3)] + [m_spec] * RADIX + [mt_spec] * RADIX
                + [p_spec(blk) for g in range(4) for blk in (g, 4 + g)] + [s_spec, s_spec]
                + [pl.BlockSpec((1, 1, tc), lambda b, c, o: (o, 0, c))])
    args = ([u for u in us for _ in range(3)] + list(mats) + [spec_re, spec_im] * 4 + [spec_re, spec_im, fbias])
    aliases = {}
    if prev is not None:
        in_specs += [pl.BlockSpec(memory_space=pl.ANY)] * RADIX
        aliases = {len(args) + rho: rho for rho in range(RADIX)}
        args += list(prev)
    o_spec = pl.BlockSpec((1, block_rows, tc), lambda b, c, o: (b, 0, c))
    return pl.pallas_call(
        functools.partial(_hyena_conv_kernel, row0=row0, q=q, aliased=prev is not None),
        grid=(bsz, nct, 2),
        in_specs=in_specs,
        out_specs=[o_spec] * RADIX,
        out_shape=[jax.ShapeDtypeStruct((bsz, t4, C_WIDTH), BF16)] * RADIX,
        scratch_shapes=[pltpu.VMEM((q, tc), BF16)] * RADIX,
        input_output_aliases=aliases,
        compiler_params=_cparams(3),
        name="hyena_long_conv_q%d" % q,
    )(*args)


def _radix4_matrices(n):
    q = n // RADIX
    fbs = min(q, LANE)
    nblk = q // fbs
    f = np.arange(q, dtype=np.int64)[:, None]
    r = np.arange(q, dtype=np.int64)[None, :]
    alt = np.where(np.arange(q) % 2 == 0, 1.0, -1.0)
    mats = []
    for rho in range(RADIX):
        ang = (2.0 * np.pi / (2 * n)) * ((f * (RADIX * r + rho)) % (2 * n)).astype(np.float64)
        cos, sin = np.cos(ang), np.sin(ang)
        sin[0, :] = alt
        mats.append(np.concatenate([cos.reshape(nblk, fbs, q), sin.reshape(nblk, fbs, q)], axis=1).astype(np.float32))
    tr = lambda m: np.ascontiguousarray(np.transpose(m, (0, 2, 1)))
    return tuple(jnp.asarray(m, BF16) for m in mats) + tuple(jnp.asarray(tr(m), BF16) for m in mats)


def _hyena_spectra(n, filt):
    h, q = n // 2, n // RADIX
    cos, sin = _dft_matrix(n)
    f = np.arange(q)
    groups = [f, h + f, np.maximum(n - f, 1) % n, h - f]
    special = [0.5 * cos[0:1], 0.5 * sin[0:1], cos[h:h + 1], cos[q:q + 1], cos[h + q:h + q + 1],
               sin[h:h + 1], sin[q:q + 1], sin[h + q:h + q + 1]]
    rows = np.concatenate([cos[g] for g in groups] + [sin[g] for g in groups] + special
                          + [np.zeros((LANE - 8, n), np.float32)], axis=0)
    nr = 8 * q + LANE
    return _spectrum(jnp.asarray(rows, BF16), filt, 3 * LANE if nr % (3 * LANE) == 0 else nr, 2.0 / (2 * n))


def _rope_tables(seq):
    rows = seq // GRID_W
    row = np.repeat(np.arange(rows, dtype=np.float32), GRID_W)
    col = np.tile(np.arange(GRID_W, dtype=np.float32), rows)
    per_axis = 32
    inv = (ROPE_BASE ** (-np.arange(0, per_axis, 2, dtype=np.float32) / per_axis)).astype(np.float32)
    ang = np.concatenate([row[:, None] * inv, col[:, None] * inv], axis=-1)
    cos = np.concatenate([np.ones((CTX, 32), np.float32), np.cos(ang)], axis=0)
    sin = np.concatenate([np.zeros((CTX, 32), np.float32), np.sin(ang)], axis=0)
    zero = np.zeros_like(cos)
    parts = ([cos, cos, cos, cos], [zero, sin, zero, sin], [-sin, zero, -sin, zero])
    return tuple(jnp.asarray(np.concatenate(p, axis=1), F32) for p in parts)


def _swa_head_perm():
    cols = []
    for j in range(B_KV_HEADS // 2):
        for g in range(B_GROUP):
            for hk in (2 * j, 2 * j + 1):
                h = hk * B_GROUP + g
                cols.extend(range(h * B_HDIM, (h + 1) * B_HDIM))
    return np.asarray(cols, np.int32)


def _mla_weights(w_in, w_q, w_kv):
    c1 = A_Q_RANK + A_KV_RANK
    zpad = jnp.zeros((D, LANE - A_ROPE), F32)
    win = jnp.concatenate([w_in[:, :c1 + A_ROPE], zpad, w_in[:, c1 + A_ROPE:]], axis=1)
    qscale = (A_NOPE + A_ROPE) ** -0.5 * math.log2(math.e)
    wq = w_q.reshape(A_Q_RANK, A_HEADS, A_NOPE + A_ROPE) * qscale
    wq = jnp.concatenate([wq[:, :, :A_NOPE].reshape(A_Q_RANK, -1), wq[:, :, A_NOPE:].reshape(A_Q_RANK, -1)], axis=1)
    return win.astype(BF16), wq.astype(BF16), w_kv.astype(BF16)


def kernel(x, c, ctx, c_ctx, w_mod, b_mod, g_pre, g_post, a_w_in, a_g_q, a_w_q, a_g_kv, a_w_kv, a_w_out, b_w_in, b_sink, b_w_out, c_w_in, c_conv_w, c_conv_b, c_f_w1, c_f_b1, c_f_freq, c_f_w2, c_f_b2, c_f_w3, c_filt_bias, c_w_out):
    bsz, seq, _ = x.shape
    assert ctx.shape[1] == CTX and (CTX + seq) % TP == 0 and seq % GRID_W == 0
    xsrc = (ctx, x)

    pad_rows = (-(bsz + 1)) % 8
    cond = jnp.concatenate([c, c_ctx[None], jnp.zeros((pad_rows, D), F32)], axis=0)
    mod = _modulation(cond, w_mod, b_mod)

    tabs = _rope_tables(seq)

    for layer in range(DEPTH):
        kind, j = layer % 3, layer // 3
        mx = mod[layer, :bsz].reshape(bsz, 3, D)
        mc = jnp.broadcast_to(mod[layer, bsz].reshape(1, 3, D), (bsz, 3, D))
        modl = jnp.stack([mc, mx], axis=0)
        gpre = g_pre[layer][None]
        gpost = g_post[layer][None]
        if kind == 0:
            win, wq, wkv = _mla_weights(a_w_in[j], a_w_q[j], a_w_kv[j])
            q, k, v, gate = _mla_proj(xsrc, modl, gpre, win, a_g_q[j][None], wq, a_g_kv[j][None], wkv, tabs)
            xs = _mla_attn(q, k, v, gate, a_w_out[j].astype(BF16), xsrc, modl, gpost,
                           latents_only=layer == DEPTH - 1)
        elif kind == 1:
            xs, = xsrc
            perm = _swa_head_perm()
            qw, kw = B_HEADS * B_HDIM, B_KV_HEADS * B_HDIM
            w = b_w_in[j]
            win = jnp.concatenate([w[:, :qw][:, perm] * (B_HDIM ** -0.5 * math.log2(math.e)), w[:, qw:qw + 2 * kw],
                                   w[:, qw + 2 * kw:][:, perm]], axis=1).astype(BF16)
            q, k, v, gate = _swa_proj(xs, modl, gpre, win, tabs)
            sink = (b_sink[j].astype(F32) * math.log2(math.e)).reshape(B_KV_HEADS, B_GROUP, 1, 1)
            sink_cols = jnp.broadcast_to(sink, (B_KV_HEADS, B_GROUP, B_BLOCK, LANE)).reshape(B_KV_HEADS, B_GROUP * B_BLOCK, LANE)
            xs = _swa_attn(q, k, v, sink_cols, gate, b_w_out[j][perm, :].astype(BF16), xs, modl, gpost)
        else:
            xs, = xsrc
            parts = _hyena_proj(xs, modl, gpre, c_w_in[j].astype(BF16), c_conv_w[j], c_conv_b[j][None])
            us, gs = parts[:RADIX], parts[RADIX:]
            fargs = (c_f_w1[j], c_f_b1[j], c_f_freq[j], c_f_w2[j], c_f_b2[j], c_f_w3[j])
            fbias = c_filt_bias[j].reshape(2, 1, C_WIDTH)
            os = _hyena_conv(us, _radix4_matrices(seq), *_hyena_spectra(seq, _filters(seq, *fargs)), fbias,
                             row0=CTX // RADIX, q=seq // RADIX, block_rows=(CTX + seq) // RADIX, tc=MXU_W)
            os = _hyena_conv(us, _radix4_matrices(CTX), *_hyena_spectra(CTX, _filters(CTX, *fargs)), fbias,
                             row0=0, q=CTX // RADIX, block_rows=CTX // RADIX, tc=C_WIDTH, prev=os)
            xs = _hyena_out_proj(os, gs, c_w_out[j].astype(BF16), xs, modl, gpost)
        xsrc = (xs,)
    return xs
```
